```python
import math
import jax, jax.numpy as jnp
from jax import lax
import numpy as np

D_MODEL = 2048
BATCH = 4
SEQ = 2048
DEPTH = 1
DEC_BATCH = 128
DEC_SEQ = 1
PAST_LEN = 16384
PAGE_SIZE = 128

D_MIX = D_MODEL
D_A = D_MIX // 2
D_B = D_MIX - D_A
CHUNK = 128
HEAD_A = 128
N_HEADS_A = D_A // HEAD_A
GROUP_B = 16
N_GROUPS_B = D_B // GROUP_B
P_STATE = 64
D_IN = 3 * D_A + 2 * D_B
EPS = 1e-6
DT_MIN = 1e-3
DT_MAX = 1e-1

kernel_name = "hymba_gmlp_s5_decode_step"


def rmsnorm(x, g):
    xf = x.astype(jnp.float32)
    r = xf * lax.rsqrt(jnp.mean(xf * xf, axis=-1, keepdims=True) + EPS)
    return (r * g.astype(jnp.float32)).astype(x.dtype)


def layernorm(x, g, b):
    xf = x.astype(jnp.float32)
    mu = jnp.mean(xf, axis=-1, keepdims=True)
    xc = xf - mu
    r = xc * lax.rsqrt(jnp.mean(xc * xc, axis=-1, keepdims=True) + EPS)
    return (r * g.astype(jnp.float32) + b.astype(jnp.float32)).astype(x.dtype)


def adaln(c, w_c, b_c):
    m = jax.nn.silu(c) @ w_c + b_c
    return jnp.split(m, 3, axis=-1)


def chunk_mix(v, w_s, b_s):
    n, L, H, dh = v.shape
    n_chunks = -(-L // CHUNK)
    pad = n_chunks * CHUNK - L
    vp = jnp.pad(v, ((0, 0), (0, pad), (0, 0), (0, 0)))
    vc = vp.reshape(n, n_chunks, CHUNK, H, dh)
    ws = jnp.tril(w_s)
    out = jnp.einsum('hts,bcshd->bcthd', ws, vc) + b_s.T[None, None, :, :, None]
    return out.reshape(n, n_chunks * CHUNK, H, dh)[:, :L]


def s5_discretize(a_re, a_im, log_dt, b_re, b_im):
    a_re = a_re.astype(jnp.float32)
    a_im = a_im.astype(jnp.float32)
    dt = jnp.exp(log_dt.astype(jnp.float32))[:, None]
    mag = jnp.exp(dt * a_re)
    abar_re = mag * jnp.cos(dt * a_im)
    abar_im = mag * jnp.sin(dt * a_im)
    num_re = abar_re - 1.0
    num_im = abar_im
    den = a_re * a_re + a_im * a_im
    coef_re = (num_re * a_re + num_im * a_im) / den
    coef_im = (num_im * a_re - num_re * a_im) / den
    b_re = b_re.astype(jnp.float32)
    b_im = b_im.astype(jnp.float32)
    bbar_re = coef_re[..., None] * b_re - coef_im[..., None] * b_im
    bbar_im = coef_re[..., None] * b_im + coef_im[..., None] * b_re
    return abar_re, abar_im, bbar_re, bbar_im


def _scan_combine(e1, e2):
    a1r, a1i, b1r, b1i = e1
    a2r, a2i, b2r, b2i = e2
    ar = a2r * a1r - a2i * a1i
    ai = a2r * a1i + a2i * a1r
    br = a2r * b1r - a2i * b1i + b2r
    bi = a2r * b1i + a2i * b1r + b2i
    return ar, ai, br, bi


def s5_branch(xb, h0_re, h0_im, a_re, a_im, log_dt, b_re, b_im, c_re, c_im, d_skip, w_glu, b_glu):
    abar_re, abar_im, bbar_re, bbar_im = s5_discretize(a_re, a_im, log_dt, b_re, b_im)
    bu_re = jnp.einsum('nlgc,gpc->nlgp', xb, bbar_re)
    bu_im = jnp.einsum('nlgc,gpc->nlgp', xb, bbar_im)
    bu_re = bu_re.at[:, 0].add(abar_re * h0_re - abar_im * h0_im)
    bu_im = bu_im.at[:, 0].add(abar_re * h0_im + abar_im * h0_re)
    ar = jnp.broadcast_to(abar_re, bu_re.shape)
    ai = jnp.broadcast_to(abar_im, bu_re.shape)
    _, _, h_re, h_im = lax.associative_scan(_scan_combine, (ar, ai, bu_re, bu_im), axis=1)
    d = d_skip.astype(jnp.float32).reshape(N_GROUPS_B, GROUP_B)
    y = (jnp.einsum('nlgp,gcp->nlgc', h_re, c_re.astype(jnp.float32))
         - jnp.einsum('nlgp,gcp->nlgc', h_im, c_im.astype(jnp.float32))
         + d * xb)
    g = jnp.einsum('nlgc,gce->nlge', y, w_glu.astype(jnp.float32)) + b_glu.astype(jnp.float32)
    y = g[..., :GROUP_B] * jax.nn.sigmoid(g[..., GROUP_B:])
    return y, h_re[:, -1], h_im[:, -1]


def hybrid_layer(x, c, h0_re, h0_im, w_c, b_c, g_pre, w_in, ln_v_g, ln_v_b, w_s, b_s,
                 a_re, a_im, log_dt, b_re, b_im, c_re, c_im, d_skip, w_glu, b_glu, w_out, g_post):
    n, L, _ = x.shape
    shift, scale, gate = adaln(c, w_c, b_c)
    h = rmsnorm(x, g_pre) * (1.0 + scale[:, None, :]) + shift[:, None, :]
    proj = h @ w_in
    u_a, v_a, z_a, x_b, z_b = jnp.split(
        proj, [D_A, 2 * D_A, 3 * D_A, 3 * D_A + D_B], axis=-1)
    v_a = layernorm(v_a, ln_v_g, ln_v_b)
    mix = chunk_mix(v_a.reshape(n, L, N_HEADS_A, HEAD_A), w_s, b_s).reshape(n, L, D_A)
    y_a = u_a * mix * jax.nn.silu(z_a)
    xb = x_b.reshape(n, L, N_GROUPS_B, GROUP_B).astype(jnp.float32)
    y_b, h_re, h_im = s5_branch(xb, h0_re, h0_im, a_re, a_im, log_dt, b_re, b_im,
                                c_re, c_im, d_skip, w_glu, b_glu)
    y_b = y_b.reshape(n, L, D_B).astype(x.dtype) * jax.nn.silu(z_b)
    o = jnp.concatenate([y_a, y_b], axis=-1) @ w_out
    out = x + gate[:, None, :] * rmsnorm(o, g_post)
    return out, v_a, h_re, h_im


def setup_inputs(seed: int = 0) -> dict:
    key = jax.random.key(seed)
    ks = jax.random.split(key, 32)
    f32 = jnp.float32
    nrm = lambda k, s: jax.random.normal(k, s, f32)
    x_prompt = nrm(ks[0], (BATCH, SEQ, D_MODEL))
    x_sample = nrm(ks[1], (DEC_BATCH, DEC_SEQ, D_MODEL))
    c_prompt = nrm(ks[2], (BATCH, D_MODEL))
    c_sample = nrm(ks[3], (DEC_BATCH, D_MODEL))
    state_b_re = 0.3 * nrm(ks[4], (DEC_BATCH, N_GROUPS_B, P_STATE))
    state_b_im = 0.3 * nrm(ks[5], (DEC_BATCH, N_GROUPS_B, P_STATE))
    w_c = 0.2 * nrm(ks[6], (D_MODEL, 3 * D_MODEL)) * D_MODEL ** -0.5
    b_c = 0.02 * nrm(ks[7], (3 * D_MODEL,))
    g_pre = 1.0 + 0.05 * nrm(ks[8], (D_MODEL,))
    w_in = nrm(ks[9], (D_MODEL, D_IN)) * D_MODEL ** -0.5
    ln_v_g = 1.0 + 0.05 * nrm(ks[10], (D_A,))
    ln_v_b = 0.02 * nrm(ks[11], (D_A,))
    w_s = nrm(ks[12], (N_HEADS_A, CHUNK, CHUNK)) * CHUNK ** -0.5
    b_s = 1.0 + 0.1 * nrm(ks[13], (N_HEADS_A, CHUNK))
    a_re = -0.5 + 0.01 * nrm(ks[14], (N_GROUPS_B, P_STATE))
    a_im = math.pi * jnp.arange(P_STATE, dtype=f32)[None, :] + 0.01 * nrm(ks[15], (N_GROUPS_B, P_STATE))
    log_dt = jax.random.uniform(ks[16], (N_GROUPS_B,), f32, math.log(DT_MIN), math.log(DT_MAX))
    b_scale = (2.0 * GROUP_B) ** -0.5
    b_re = b_scale * nrm(ks[17], (N_GROUPS_B, P_STATE, GROUP_B))
    b_im = b_scale * nrm(ks[18], (N_GROUPS_B, P_STATE, GROUP_B))
    c_scale = (2.0 * P_STATE) ** -0.5
    c_re = c_scale * nrm(ks[19], (N_GROUPS_B, GROUP_B, P_STATE))
    c_im = c_scale * nrm(ks[20], (N_GROUPS_B, GROUP_B, P_STATE))
    d_skip = nrm(ks[21], (D_B,))
    w_glu = nrm(ks[22], (N_GROUPS_B, GROUP_B, 2 * GROUP_B)) * GROUP_B ** -0.5
    b_glu = 0.02 * nrm(ks[23], (N_GROUPS_B, 2 * GROUP_B))
    w_out = nrm(ks[24], (D_MIX, D_MODEL)) * D_MIX ** -0.5
    g_post = 1.0 + 0.05 * nrm(ks[25], (D_MODEL,))
    return {"x_prompt": x_prompt, "x_sample": x_sample, "c_prompt": c_prompt, "c_sample": c_sample,
            "state_b_re": state_b_re, "state_b_im": state_b_im,
            "w_c": w_c, "b_c": b_c, "g_pre": g_pre, "w_in": w_in, "ln_v_g": ln_v_g, "ln_v_b": ln_v_b,
            "w_s": w_s, "b_s": b_s, "a_re": a_re, "a_im": a_im, "log_dt": log_dt,
            "b_re": b_re, "b_im": b_im, "c_re": c_re, "c_im": c_im, "d_skip": d_skip,
            "w_glu": w_glu, "b_glu": b_glu, "w_out": w_out, "g_post": g_post}


def reference(x_prompt, x_sample, c_prompt, c_sample, state_b_re, state_b_im,
              w_c, b_c, g_pre, w_in, ln_v_g, ln_v_b, w_s, b_s, a_re, a_im, log_dt,
              b_re, b_im, c_re, c_im, d_skip, w_glu, b_glu, w_out, g_post):
    y_prompt = x_prompt
    y_sample = x_sample
    for layer in range(DEPTH):
        h0_re_p = jnp.zeros((x_prompt.shape[0], N_GROUPS_B, P_STATE), jnp.float32)
        h0_im_p = jnp.zeros((x_prompt.shape[0], N_GROUPS_B, P_STATE), jnp.float32)
        y_prompt, _, hp_re, hp_im = hybrid_layer(
            y_prompt, c_prompt, h0_re_p, h0_im_p, w_c, b_c, g_pre, w_in, ln_v_g, ln_v_b, w_s, b_s,
            a_re, a_im, log_dt, b_re, b_im, c_re, c_im, d_skip, w_glu, b_glu, w_out, g_post)
        y_sample, v_s, hs_re, hs_im = hybrid_layer(
            y_sample, c_sample, state_b_re.astype(jnp.float32), state_b_im.astype(jnp.float32),
            w_c, b_c, g_pre, w_in, ln_v_g, ln_v_b, w_s, b_s,
            a_re, a_im, log_dt, b_re, b_im, c_re, c_im, d_skip, w_glu, b_glu, w_out, g_post)
    return (y_prompt, y_sample, v_s, hp_re, hp_im, hs_re, hs_im)
```

```python
import functools

import jax
import jax.numpy as jnp
from jax import lax
from jax.experimental import pallas as pl
from jax.experimental.pallas import tpu as pltpu

F32 = jnp.float32
BF16 = jnp.bfloat16

EPS = 1e-6
D_MODEL = 2048
D_A = 1024
D_B = 1024
D_IN = 3 * D_A + 2 * D_B
CHUNK = 128
HEAD_A = 128
N_HEADS_A = D_A // HEAD_A
GROUP_B = 16
N_GROUPS_B = D_B // GROUP_B
P_STATE = 64

LANES = 128
SUBLANES = 8
SLAB_GROUPS = LANES // GROUP_B
N_SLABS = N_GROUPS_B // SLAB_GROUPS
SLAB_STATES = SLAB_GROUPS * P_STATE
N_STREAMS = SUBLANES
HALF_SLABS = N_SLABS // 2
STREAM_PITCH = CHUNK + SUBLANES
VMEM_LIMIT = 62 * 1024 * 1024


def _silu(x):
    return x * jax.nn.sigmoid(x)


def _rms(x, g):
    return x * lax.rsqrt(jnp.mean(x * x, axis=-1, keepdims=True) + EPS) * g


def _layernorm(x, g, b):
    mu = jnp.mean(x, axis=-1, keepdims=True)
    xc = x - mu
    return xc * lax.rsqrt(jnp.mean(xc * xc, axis=-1, keepdims=True) + EPS) * g + b


def _dot(a, b):
    return jnp.dot(a, b, preferred_element_type=F32)


def _discretize(a_re, a_im, log_dt):
    dt = jnp.exp(log_dt)
    mag = jnp.exp(dt * a_re)
    abar_re = mag * jnp.cos(dt * a_im)
    abar_im = mag * jnp.sin(dt * a_im)
    return abar_re, abar_im


def _prep_kernel(a_re_rep, a_im_rep, ldt_rep, bt_re, bt_im, ct_re, ct_im, wglu, a_re8, a_im8, ldt8,
                 bmat_ref, cblk_ref, gmat_ref, lam_re_ref, lam_im_ref):
    lr, li = _discretize(a_re8[...], a_im8[...], ldt8[...])
    lam_re_ref[...] = lr
    lam_im_ref[...] = li

    a_re = a_re_rep[...]
    a_im = a_im_rep[...]
    abar_re, abar_im = _discretize(a_re, a_im, ldt_rep[...])
    num_re = abar_re - 1.0
    num_im = abar_im
    den = a_re * a_re + a_im * a_im
    coef_re = (num_re * a_re + num_im * a_im) / den
    coef_im = (num_im * a_re - num_re * a_im) / den
    b_re = bt_re[...]
    b_im = bt_im[...]
    bbar_re = (coef_re * b_re - coef_im * b_im).astype(BF16)
    bbar_im = (coef_re * b_im + coef_im * b_re).astype(BF16)

    def rep_matrix(k, n, period, offset=0):
        row = lax.broadcasted_iota(jnp.int32, (k, n), 0)
        col = lax.broadcasted_iota(jnp.int32, (k, n), 1)
        return jnp.where((col & (period - 1)) + offset == row, 1.0, 0.0).astype(BF16)

    def block_mask(m, n, row_shift, col_shift):
        row = lax.broadcasted_iota(jnp.int32, (m, n), 0)
        col = lax.broadcasted_iota(jnp.int32, (m, n), 1)
        return (row >> row_shift) == (col >> col_shift)

    rep_state = rep_matrix(P_STATE, SLAB_STATES, P_STATE)
    rep_chan = rep_matrix(GROUP_B, LANES, GROUP_B)
    rep_val = rep_matrix(2 * GROUP_B, LANES, GROUP_B)
    rep_gate = rep_matrix(2 * GROUP_B, LANES, GROUP_B, GROUP_B)
    mask_b = block_mask(LANES, SLAB_STATES, 4, 6)
    mask_c = block_mask(SLAB_STATES, LANES, 6, 4)
    mask_g = block_mask(LANES, LANES, 4, 4)

    for j in range(N_SLABS):
        rows = slice(j * LANES, (j + 1) * LANES)
        bmat_ref[j, :, 0:SLAB_STATES] = jnp.where(
            mask_b, _dot(bbar_re[rows], rep_state), 0.0).astype(BF16)
        bmat_ref[j, :, SLAB_STATES:2 * SLAB_STATES] = jnp.where(
            mask_b, _dot(bbar_im[rows], rep_state), 0.0).astype(BF16)

        srows = slice(j * SLAB_STATES, (j + 1) * SLAB_STATES)
        c_re = ct_re[srows, :].astype(BF16)
        c_im = ct_im[srows, :].astype(BF16)
        cblk_ref[j, 0:SLAB_STATES, :] = jnp.where(mask_c, _dot(c_re, rep_chan), 0.0).astype(BF16)
        cblk_ref[j, SLAB_STATES:2 * SLAB_STATES, :] = jnp.where(
            mask_c, -_dot(c_im, rep_chan), 0.0).astype(BF16)

        w = wglu[rows, :].astype(BF16)
        gmat_ref[j, :, 0:LANES] = jnp.where(mask_g, _dot(w, rep_val), 0.0).astype(BF16)
        gmat_ref[j, :, LANES:2 * LANES] = jnp.where(mask_g, _dot(w, rep_gate), 0.0).astype(BF16)


def _s5_prep(a_re, a_im, log_dt, b_re, b_im, c_re, c_im, w_glu):
    g, p, c = N_GROUPS_B, P_STATE, GROUP_B
    rep = lambda v: jnp.repeat(v, c, axis=0)
    ldt_gp = jnp.broadcast_to(log_dt[:, None], (g, p))
    args = (
        rep(a_re), rep(a_im), rep(ldt_gp),
        b_re.transpose(0, 2, 1).reshape(g * c, p), b_im.transpose(0, 2, 1).reshape(g * c, p),
        c_re.transpose(0, 2, 1).reshape(g * p, c), c_im.transpose(0, 2, 1).reshape(g * p, c),
        w_glu.reshape(g * c, 2 * c),
        a_re.reshape(N_SLABS, SLAB_STATES), a_im.reshape(N_SLABS, SLAB_STATES),
        ldt_gp.reshape(N_SLABS, SLAB_STATES),
    )
    return pl.pallas_call(
        _prep_kernel,
        out_shape=(
            jax.ShapeDtypeStruct((N_SLABS, LANES, 2 * SLAB_STATES), BF16),
            jax.ShapeDtypeStruct((N_SLABS, 2 * SLAB_STATES, LANES), BF16),
            jax.ShapeDtypeStruct((N_SLABS, LANES, 2 * LANES), BF16),
            jax.ShapeDtypeStruct((N_SLABS, SLAB_STATES), F32),
            jax.ShapeDtypeStruct((N_SLABS, SLAB_STATES), F32),
        ),
        name="s5_prep",
    )(*args)


ADALN_TILE = 512


def _adaln_kernel(c_ref, w_ref, b_ref, o_ref):
    s = _silu(c_ref[...]).astype(BF16)
    o_ref[...] = _dot(s, w_ref[...].astype(BF16)) + b_ref[...]


def _adaln(c_all, w_c, b_c):
    n = c_all.shape[0]
    return pl.pallas_call(
        _adaln_kernel,
        grid=(3 * D_MODEL // ADALN_TILE,),
        in_specs=[
            pl.BlockSpec((n, D_MODEL), lambda j: (0, 0)),
            pl.BlockSpec((D_MODEL, ADALN_TILE), lambda j: (0, j)),
            pl.BlockSpec((1, ADALN_TILE), lambda j: (0, j)),
        ],
        out_specs=pl.BlockSpec((n, ADALN_TILE), lambda j: (0, j)),
        out_shape=jax.ShapeDtypeStruct((n, 3 * D_MODEL), F32),
        name="adaln",
    )(c_all, w_c, b_c.reshape(1, -1))


def _mixer_kernel(x_ref, mod_ref, gpre_ref, w_in_ref, lng_ref, lnb_ref, ws_ref, bst_ref,
                  lam_re_ref, lam_im_ref, bmat_ref, cblk_ref, gmat_ref, dskip_ref, bval_ref, bgate_ref,
                  oin_ref, state_ref,
                  h_sc, a_sc, b_sc, c_sc, v_sc, bu_sc, hh_sc, y2_sc, st_sc, *, n_seq):
    step = pl.program_id(0)
    rows = n_seq * CHUNK

    @pl.when(step == 0)
    def _():
        st_sc[...] = jnp.zeros_like(st_sc)

    for b in range(n_seq):
        r = _rms(x_ref[b], gpre_ref[...])
        shift = mod_ref[b:b + 1, 0:D_MODEL]
        scale = mod_ref[b:b + 1, D_MODEL:2 * D_MODEL]
        h_sc[b * CHUNK:(b + 1) * CHUNK, :] = (r * (1.0 + scale) + shift).astype(BF16)

    h = h_sc[...]
    v = _dot(h, w_in_ref[:, D_A:2 * D_A])
    v_sc[...] = _layernorm(v, lng_ref[...], lnb_ref[...]).astype(BF16)
    a_sc[...] = _dot(h, w_in_ref[:, 0:D_A])
    a_sc[...] = a_sc[...] * _silu(_dot(h, w_in_ref[:, 2 * D_A:3 * D_A]))
    tril = (lax.broadcasted_iota(jnp.int32, (CHUNK, CHUNK), 0)
            >= lax.broadcasted_iota(jnp.int32, (CHUNK, CHUNK), 1))
    for hd in range(N_HEADS_A):
        cols = slice(hd * HEAD_A, (hd + 1) * HEAD_A)
        w_t = jnp.where(tril, ws_ref[hd], 0.0).astype(BF16)
        v_h = jnp.concatenate([v_sc[b * CHUNK:(b + 1) * CHUNK, cols] for b in range(n_seq)], axis=1)
        mix = _dot(w_t, v_h) + bst_ref[:, hd:hd + 1]
        for b in range(n_seq):
            rws = slice(b * CHUNK, (b + 1) * CHUNK)
            oin_ref[rws, cols] = (a_sc[rws, cols] * mix[:, b * HEAD_A:(b + 1) * HEAD_A]).astype(BF16)

    b_sc[...] = _dot(h, w_in_ref[:, 3 * D_A:3 * D_A + D_B])
    v_sc[...] = b_sc[...].astype(BF16)
    a_sc[...] = _silu(_dot(h, w_in_ref[:, 3 * D_A + D_B:D_IN]))

    first_half = lax.broadcasted_iota(jnp.int32, (N_STREAMS, LANES), 0) < n_seq
    for k in range(HALF_SLABS):
        for half in range(2):
            j = half * HALF_SLABS + k
            bu = _dot(v_sc[:, j * LANES:(j + 1) * LANES], bmat_ref[j])
            for b in range(n_seq):
                s = half * n_seq + b
                for l in range(2 * SLAB_STATES // LANES):
                    bu_sc[l, s * STREAM_PITCH:s * STREAM_PITCH + CHUNK, :] = (
                        bu[b * CHUNK:(b + 1) * CHUNK, l * LANES:(l + 1) * LANES])

        n_l = SLAB_STATES // LANES
        lam_r = [jnp.where(first_half,
                           lam_re_ref[k:k + 1, l * LANES:(l + 1) * LANES],
                           lam_re_ref[HALF_SLABS + k:HALF_SLABS + k + 1, l * LANES:(l + 1) * LANES])
                 for l in range(n_l)]
        lam_i = [jnp.where(first_half,
                           lam_im_ref[k:k + 1, l * LANES:(l + 1) * LANES],
                           lam_im_ref[HALF_SLABS + k:HALF_SLABS + k + 1, l * LANES:(l + 1) * LANES])
                 for l in range(n_l)]

        def scan_step(t, carry, lam_r=lam_r, lam_i=lam_i, n_l=n_l):
            hr, hi = carry
            new_r, new_i = [], []
            for l in range(n_l):
                br = bu_sc[l, pl.ds(t, N_STREAMS, stride=STREAM_PITCH), :]
                bi = bu_sc[n_l + l, pl.ds(t, N_STREAMS, stride=STREAM_PITCH), :]
                new_r.append(lam_r[l] * hr[l] - lam_i[l] * hi[l] + br)
                new_i.append(lam_r[l] * hi[l] + lam_i[l] * hr[l] + bi)
            row = pl.multiple_of(t * N_STREAMS, N_STREAMS)
            hh_sc[pl.ds(row, N_STREAMS), :] = jnp.concatenate(new_r + new_i, axis=1)
            return tuple(new_r), tuple(new_i)

        hr0 = tuple(st_sc[k, :, l * LANES:(l + 1) * LANES] for l in range(n_l))
        hi0 = tuple(st_sc[k, :, (n_l + l) * LANES:(n_l + l + 1) * LANES] for l in range(n_l))
        hr, hi = lax.fori_loop(0, CHUNK, scan_step, (hr0, hi0), unroll=4)
        st_sc[k] = jnp.concatenate(list(hr) + list(hi), axis=1)

        c_both = jnp.concatenate([cblk_ref[k], cblk_ref[HALF_SLABS + k]], axis=1)
        y2 = _dot(hh_sc[...].astype(BF16), c_both)
        y2_sc[0] = y2[:, 0:LANES]
        y2_sc[1] = y2[:, LANES:2 * LANES]
        for half in range(2):
            j = half * HALF_SLABS + k
            for b in range(n_seq):
                s = half * n_seq + b
                c_sc[b * CHUNK:(b + 1) * CHUNK, j * LANES:(j + 1) * LANES] = (
                    y2_sc[half, pl.ds(s, CHUNK, stride=N_STREAMS), :])

    state_ref[...] = st_sc[...]

    for j in range(N_SLABS):
        cols = slice(j * LANES, (j + 1) * LANES)
        y = c_sc[:, cols] + dskip_ref[:, cols] * b_sc[:, cols]
        g = _dot(y.astype(BF16), gmat_ref[j])
        val = g[:, 0:LANES] + bval_ref[:, cols]
        gate = g[:, LANES:2 * LANES] + bgate_ref[:, cols]
        oin_ref[:, D_A + j * LANES:D_A + (j + 1) * LANES] = (
            val * jax.nn.sigmoid(gate) * a_sc[:, cols]).astype(BF16)
    del rows


def _const_spec(shape):
    zeros = (0,) * len(shape)
    return pl.BlockSpec(shape, lambda i: zeros, pipeline_mode=pl.Buffered(1))


def _mixer(x, mod, g_pre, w_in_bf, ln_g, ln_b, w_s, bst, lam_re, lam_im, bmat, cblk, gmat,
           d_skip, b_val, b_gate):
    n_seq, seq, _ = x.shape
    assert 2 * n_seq == N_STREAMS and seq % CHUNK == 0
    rows = n_seq * CHUNK
    n_steps = seq // CHUNK
    consts = (mod, g_pre, w_in_bf, ln_g, ln_b, w_s, bst, lam_re, lam_im, bmat, cblk, gmat,
              d_skip, b_val, b_gate)
    return pl.pallas_call(
        functools.partial(_mixer_kernel, n_seq=n_seq),
        grid=(n_steps,),
        in_specs=[pl.BlockSpec((n_seq, CHUNK, D_MODEL), lambda i: (0, i, 0))]
        + [_const_spec(c.shape) for c in consts],
        out_specs=(
            pl.BlockSpec((rows, D_A + D_B), lambda i: (i, 0)),
            pl.BlockSpec((HALF_SLABS, N_STREAMS, 2 * SLAB_STATES), lambda i: (0, 0, 0)),
        ),
        out_shape=(
            jax.ShapeDtypeStruct((n_steps * rows, D_A + D_B), BF16),
            jax.ShapeDtypeStruct((HALF_SLABS, N_STREAMS, 2 * SLAB_STATES), F32),
        ),
        scratch_shapes=[
            pltpu.VMEM((rows, D_MODEL), BF16),
            pltpu.VMEM((rows, D_A), F32),
            pltpu.VMEM((rows, D_A), F32),
            pltpu.VMEM((rows, D_A), F32),
            pltpu.VMEM((rows, D_A), BF16),
            pltpu.VMEM((2 * SLAB_STATES // LANES, N_STREAMS * STREAM_PITCH, LANES), F32),
            pltpu.VMEM((N_STREAMS * CHUNK, 2 * SLAB_STATES), F32),
            pltpu.VMEM((2, N_STREAMS * CHUNK, LANES), F32),
            pltpu.VMEM((HALF_SLABS, N_STREAMS, 2 * SLAB_STATES), F32),
        ],
        compiler_params=pltpu.CompilerParams(
            dimension_semantics=("arbitrary",), vmem_limit_bytes=VMEM_LIMIT),
        name="mixer",
    )(x, *consts)


def _outproj_kernel(oin_ref, x_ref, gate_ref, gpost_ref, w_out_ref, y_ref, *, n_seq):
    o = _dot(oin_ref[...], w_out_ref[...])
    r = _rms(o, gpost_ref[...])
    for b in range(n_seq):
        y_ref[b] = x_ref[b] + gate_ref[b:b + 1, :] * r[b * CHUNK:(b + 1) * CHUNK, :]


def _outproj(oin, x, gate, g_post, w_out_bf):
    n_seq, seq, _ = x.shape
    rows = n_seq * CHUNK
    return pl.pallas_call(
        functools.partial(_outproj_kernel, n_seq=n_seq),
        grid=(seq // CHUNK,),
        in_specs=[
            pl.BlockSpec((rows, D_A + D_B), lambda i: (i, 0)),
            pl.BlockSpec((n_seq, CHUNK, D_MODEL), lambda i: (0, i, 0)),
            _const_spec(gate.shape), _const_spec(g_post.shape), _const_spec(w_out_bf.shape),
        ],
        out_specs=pl.BlockSpec((n_seq, CHUNK, D_MODEL), lambda i: (0, i, 0)),
        out_shape=jax.ShapeDtypeStruct(x.shape, F32),
        compiler_params=pltpu.CompilerParams(
            dimension_semantics=("arbitrary",), vmem_limit_bytes=VMEM_LIMIT),
        name="outproj",
    )(oin, x, gate, g_post, w_out_bf)


def _sample_kernel(x_ref, mod_ref, gpre_ref, w_in_ref, lng_ref, lnb_ref, ws_ref, bst_ref,
                   lam_re_ref, lam_im_ref, bmat_ref, cblk_ref, gmat_ref, dskip_ref, bval_ref, bgate_ref,
                   h0_re_ref, h0_im_ref, gpost_ref, w_out_ref,
                   y_ref, v_ref, hs_re_ref, hs_im_ref, oin_sc):
    x = x_ref[...]
    shift = mod_ref[:, 0:D_MODEL]
    scale = mod_ref[:, D_MODEL:2 * D_MODEL]
    gate = mod_ref[:, 2 * D_MODEL:3 * D_MODEL]
    h = (_rms(x, gpre_ref[...]) * (1.0 + scale) + shift).astype(BF16)

    v = _layernorm(_dot(h, w_in_ref[:, D_A:2 * D_A]), lng_ref[...], lnb_ref[...])
    v_ref[...] = v
    u = _dot(h, w_in_ref[:, 0:D_A])
    z = _dot(h, w_in_ref[:, 2 * D_A:3 * D_A])
    for hd in range(N_HEADS_A):
        cols = slice(hd * HEAD_A, (hd + 1) * HEAD_A)
        mix = ws_ref[hd, 0:1, 0:1] * v[:, cols] + bst_ref[0:1, hd:hd + 1]
        oin_sc[:, cols] = (u[:, cols] * mix * _silu(z[:, cols])).astype(BF16)

    xb = _dot(h, w_in_ref[:, 3 * D_A:3 * D_A + D_B])
    zb = _silu(_dot(h, w_in_ref[:, 3 * D_A + D_B:D_IN]))
    xb_bf = xb.astype(BF16)
    for j in range(N_SLABS):
        cols = slice(j * LANES, (j + 1) * LANES)
        st = slice(j * SLAB_STATES, (j + 1) * SLAB_STATES)
        bu = _dot(xb_bf[:, cols], bmat_ref[j])
        lr = lam_re_ref[j:j + 1, :]
        li = lam_im_ref[j:j + 1, :]
        h0r = h0_re_ref[:, st]
        h0i = h0_im_ref[:, st]
        hr = lr * h0r - li * h0i + bu[:, 0:SLAB_STATES]
        hi = lr * h0i + li * h0r + bu[:, SLAB_STATES:2 * SLAB_STATES]
        hs_re_ref[:, st] = hr
        hs_im_ref[:, st] = hi
        y = _dot(jnp.concatenate([hr, hi], axis=1).astype(BF16), cblk_ref[j])
        y = y + dskip_ref[:, cols] * xb[:, cols]
        g = _dot(y.astype(BF16), gmat_ref[j])
        val = g[:, 0:LANES] + bval_ref[:, cols]
        gt = g[:, LANES:2 * LANES] + bgate_ref[:, cols]
        oin_sc[:, D_A + j * LANES:D_A + (j + 1) * LANES] = (
            val * jax.nn.sigmoid(gt) * zb[:, cols]).astype(BF16)

    o = _dot(oin_sc[...], w_out_ref[...])
    y_ref[...] = x + gate * _rms(o, gpost_ref[...])


def _sample(x, mod, g_pre, w_in_bf, ln_g, ln_b, w_s, bst, lam_re, lam_im, bmat, cblk, gmat,
            d_skip, b_val, b_gate, h0_re, h0_im, g_post, w_out_bf):
    n = x.shape[0]
    n_state = N_GROUPS_B * P_STATE
    return pl.pallas_call(
        _sample_kernel,
        out_shape=(
            jax.ShapeDtypeStruct((n, D_MODEL), F32),
            jax.ShapeDtypeStruct((n, D_A), F32),
            jax.ShapeDtypeStruct((n, n_state), F32),
            jax.ShapeDtypeStruct((n, n_state), F32),
        ),
        scratch_shapes=[pltpu.VMEM((n, D_A + D_B), BF16)],
        compiler_params=pltpu.CompilerParams(vmem_limit_bytes=VMEM_LIMIT),
        name="sample",
    )(x, mod, g_pre, w_in_bf, ln_g, ln_b, w_s, bst, lam_re, lam_im, bmat, cblk, gmat,
      d_skip, b_val, b_gate, h0_re, h0_im, g_post, w_out_bf)


def kernel(x_prompt, x_sample, c_prompt, c_sample, state_b_re, state_b_im, w_c, b_c, g_pre, w_in,
           ln_v_g, ln_v_b, w_s, b_s, a_re, a_im, log_dt, b_re, b_im, c_re, c_im, d_skip, w_glu, b_glu,
           w_out, g_post):
    n_p = x_prompt.shape[0]
    n_s = x_sample.shape[0]
    assert x_sample.shape[1] == 1

    bmat, cblk, gmat, lam_re, lam_im = _s5_prep(a_re, a_im, log_dt, b_re, b_im, c_re, c_im, w_glu)

    c_all = jnp.concatenate([c_prompt, c_sample], axis=0)
    pad = (-c_all.shape[0]) % SUBLANES
    mod = _adaln(jnp.pad(c_all, ((0, pad), (0, 0))), w_c, b_c)
    mod_p = mod[:n_p]
    mod_s = mod[n_p:n_p + n_s]

    row = lambda v: v.reshape(1, -1)
    w_in_bf = w_in.astype(BF16)
    w_out_bf = w_out.astype(BF16)
    shared = (row(g_pre), w_in_bf, row(ln_v_g), row(ln_v_b), w_s, b_s.T, lam_re, lam_im, bmat, cblk,
              gmat, row(d_skip), row(b_glu[:, :GROUP_B]), row(b_glu[:, GROUP_B:]))

    oin, state = _mixer(x_prompt, mod_p, *shared)
    y_prompt = _outproj(oin, x_prompt, mod_p[:, 2 * D_MODEL:], row(g_post), w_out_bf)

    st = state.reshape(HALF_SLABS, 2, n_p, 2, SLAB_GROUPS, P_STATE).transpose(3, 2, 1, 0, 4, 5)
    st = st.reshape(2, n_p, N_GROUPS_B, P_STATE)

    y_s, v_s, hs_re, hs_im = _sample(
        x_sample.reshape(n_s, D_MODEL), mod_s, *shared,
        state_b_re.reshape(n_s, -1), state_b_im.reshape(n_s, -1), row(g_post), w_out_bf)

    return (y_prompt, y_s.reshape(n_s, 1, D_MODEL), v_s.reshape(n_s, 1, D_A), st[0], st[1],
            hs_re.reshape(n_s, N_GROUPS_B, P_STATE), hs_im.reshape(n_s, N_GROUPS_B, P_STATE))
```

```python
import functools

import jax
import jax.numpy as jnp
from jax import lax
from jax.experimental import pallas as pl
from jax.experimental.pallas import tpu as pltpu

F32 = jnp.float32
BF16 = jnp.bfloat16

EPS = 1e-6
D_MODEL = 2048
D_A = 1024
D_B = 1024
D_IN = 3 * D_A + 2 * D_B
CHUNK = 128
HEAD_A = 128
N_HEADS_A = D_A // HEAD_A
GROUP_B = 16
N_GROUPS_B = D_B // GROUP_B
P_STATE = 64

LANES = 128
SUBLANES = 8
SLAB_GROUPS = LANES // GROUP_B
N_SLABS = N_GROUPS_B // SLAB_GROUPS
SLAB_STATES = SLAB_GROUPS * P_STATE
SLAB_TILES = SLAB_STATES // LANES
N_STREAMS = SUBLANES
HALF_SLABS = N_SLABS // 2
STREAM_PITCH = CHUNK + SUBLANES
SCAN_PIECES = 4
PIECE = CHUNK // SCAN_PIECES
VMEM_LIMIT = 62 * 1024 * 1024


def _silu(x):
    return x * jax.nn.sigmoid(x)


def _rms(x, g):
    return x * lax.rsqrt(jnp.mean(x * x, axis=-1, keepdims=True) + EPS) * g


def _layernorm(x, g, b):
    mu = jnp.mean(x, axis=-1, keepdims=True)
    xc = x - mu
    return xc * lax.rsqrt(jnp.mean(xc * xc, axis=-1, keepdims=True) + EPS) * g + b


def _dot(a, b):
    return jnp.dot(a, b, preferred_element_type=F32)


def _discretize(a_re, a_im, log_dt):
    dt = jnp.exp(log_dt)
    mag = jnp.exp(dt * a_re)
    abar_re = mag * jnp.cos(dt * a_im)
    abar_im = mag * jnp.sin(dt * a_im)
    return abar_re, abar_im


def _prep_kernel(a_re_rep, a_im_rep, ldt_rep, bt_re, bt_im, ct_re, ct_im, wglu, a_re8, a_im8, ldt8,
                 bmat_ref, cboth_ref, gmat_ref, lam_re_ref, lam_im_ref):
    lr, li = _discretize(a_re8[...], a_im8[...], ldt8[...])
    lam_re_ref[...] = lr
    lam_im_ref[...] = li

    a_re = a_re_rep[...]
    a_im = a_im_rep[...]
    abar_re, abar_im = _discretize(a_re, a_im, ldt_rep[...])
    num_re = abar_re - 1.0
    num_im = abar_im
    den = a_re * a_re + a_im * a_im
    coef_re = (num_re * a_re + num_im * a_im) / den
    coef_im = (num_im * a_re - num_re * a_im) / den
    b_re = bt_re[...]
    b_im = bt_im[...]
    bbar_re = (coef_re * b_re - coef_im * b_im).astype(BF16)
    bbar_im = (coef_re * b_im + coef_im * b_re).astype(BF16)

    def rep_matrix(k, n, period, offset=0):
        row = lax.broadcasted_iota(jnp.int32, (k, n), 0)
        col = lax.broadcasted_iota(jnp.int32, (k, n), 1)
        return jnp.where((col & (period - 1)) + offset == row, 1.0, 0.0).astype(BF16)

    def block_mask(m, n, row_shift, col_shift):
        row = lax.broadcasted_iota(jnp.int32, (m, n), 0)
        col = lax.broadcasted_iota(jnp.int32, (m, n), 1)
        return (row >> row_shift) == (col >> col_shift)

    rep_state = rep_matrix(P_STATE, SLAB_STATES, P_STATE)
    rep_chan = rep_matrix(GROUP_B, LANES, GROUP_B)
    rep_val = rep_matrix(2 * GROUP_B, LANES, GROUP_B)
    rep_gate = rep_matrix(2 * GROUP_B, LANES, GROUP_B, GROUP_B)
    mask_b = block_mask(LANES, SLAB_STATES, 4, 6)
    mask_c = block_mask(SLAB_STATES, LANES, 6, 4)
    mask_g = block_mask(LANES, LANES, 4, 4)

    for j in range(N_SLABS):
        rows = slice(j * LANES, (j + 1) * LANES)
        bmat_ref[j, :, 0:SLAB_STATES] = jnp.where(
            mask_b, _dot(bbar_re[rows], rep_state), 0.0).astype(BF16)
        bmat_ref[j, :, SLAB_STATES:2 * SLAB_STATES] = jnp.where(
            mask_b, _dot(bbar_im[rows], rep_state), 0.0).astype(BF16)

        k, half = j % HALF_SLABS, j // HALF_SLABS
        ccols = slice(half * LANES, (half + 1) * LANES)
        srows = slice(j * SLAB_STATES, (j + 1) * SLAB_STATES)
        c_re = ct_re[srows, :].astype(BF16)
        c_im = ct_im[srows, :].astype(BF16)
        cboth_ref[k, 0:SLAB_STATES, ccols] = jnp.where(mask_c, _dot(c_re, rep_chan), 0.0).astype(BF16)
        cboth_ref[k, SLAB_STATES:2 * SLAB_STATES, ccols] = jnp.where(
            mask_c, -_dot(c_im, rep_chan), 0.0).astype(BF16)

        w = wglu[rows, :].astype(BF16)
        gmat_ref[j, :, 0:LANES] = jnp.where(mask_g, _dot(w, rep_val), 0.0).astype(BF16)
        gmat_ref[j, :, LANES:2 * LANES] = jnp.where(mask_g, _dot(w, rep_gate), 0.0).astype(BF16)


def _s5_prep(a_re, a_im, log_dt, b_re, b_im, c_re, c_im, w_glu):
    g, p, c = N_GROUPS_B, P_STATE, GROUP_B
    rep = lambda v: jnp.repeat(v, c, axis=0)
    ldt_gp = jnp.broadcast_to(log_dt[:, None], (g, p))
    args = (
        rep(a_re), rep(a_im), rep(ldt_gp),
        b_re.transpose(0, 2, 1).reshape(g * c, p), b_im.transpose(0, 2, 1).reshape(g * c, p),
        c_re.transpose(0, 2, 1).reshape(g * p, c), c_im.transpose(0, 2, 1).reshape(g * p, c),
        w_glu.reshape(g * c, 2 * c),
        a_re.reshape(N_SLABS, SLAB_STATES), a_im.reshape(N_SLABS, SLAB_STATES),
        ldt_gp.reshape(N_SLABS, SLAB_STATES),
    )
    return pl.pallas_call(
        _prep_kernel,
        out_shape=(
            jax.ShapeDtypeStruct((N_SLABS, LANES, 2 * SLAB_STATES), BF16),
            jax.ShapeDtypeStruct((HALF_SLABS, 2 * SLAB_STATES, 2 * LANES), BF16),
            jax.ShapeDtypeStruct((N_SLABS, LANES, 2 * LANES), BF16),
            jax.ShapeDtypeStruct((N_SLABS, SLAB_STATES), F32),
            jax.ShapeDtypeStruct((N_SLABS, SLAB_STATES), F32),
        ),
        name="s5_prep",
    )(*args)


ADALN_TILE = 1024


def _adaln_kernel(c_ref, w_ref, b_ref, o_ref):
    s = _silu(c_ref[...]).astype(BF16)
    o_ref[...] = _dot(s, w_ref[...].astype(BF16)) + b_ref[...]


def _adaln(c_all, w_c, b_c):
    n = c_all.shape[0]
    return pl.pallas_call(
        _adaln_kernel,
        grid=(3 * D_MODEL // ADALN_TILE,),
        in_specs=[
            pl.BlockSpec((n, D_MODEL), lambda j: (0, 0)),
            pl.BlockSpec((D_MODEL, ADALN_TILE), lambda j: (0, j)),
            pl.BlockSpec((1, ADALN_TILE), lambda j: (0, j)),
        ],
        out_specs=pl.BlockSpec((n, ADALN_TILE), lambda j: (0, j)),
        out_shape=jax.ShapeDtypeStruct((n, 3 * D_MODEL), F32),
        compiler_params=pltpu.CompilerParams(vmem_limit_bytes=VMEM_LIMIT),
        name="adaln",
    )(c_all, w_c, b_c.reshape(1, -1))


def _mixer_kernel(x_ref, mod_ref, gpre_ref, w_in_ref, lng_ref, lnb_ref, ws_ref, bst_ref,
                  lam_re_ref, lam_im_ref, bmat_ref, cboth_ref, gmat_ref, dskip_ref, bval_ref, bgate_ref,
                  oin_ref, state_ref,
                  h_sc, a_sc, b_sc, c_sc, v_sc, bu_sc, hh_sc, y2_sc, st_sc, *, n_seq, mod_row0):
    step = pl.program_id(0)

    @pl.when(step == 0)
    def _():
        st_sc[...] = jnp.zeros_like(st_sc)

    for b in range(n_seq):
        r = _rms(x_ref[b], gpre_ref[...])
        m = mod_row0 + b
        shift = mod_ref[m:m + 1, 0:D_MODEL]
        scale = mod_ref[m:m + 1, D_MODEL:2 * D_MODEL]
        h_sc[b * CHUNK:(b + 1) * CHUNK, :] = (r * (1.0 + scale) + shift).astype(BF16)

    h = h_sc[...]
    v = _dot(h, w_in_ref[:, D_A:2 * D_A])
    v_sc[...] = _layernorm(v, lng_ref[...], lnb_ref[...]).astype(BF16)
    a_sc[...] = _dot(h, w_in_ref[:, 0:D_A])
    a_sc[...] = a_sc[...] * _silu(_dot(h, w_in_ref[:, 2 * D_A:3 * D_A]))
    tril = (lax.broadcasted_iota(jnp.int32, (CHUNK, CHUNK), 0)
            >= lax.broadcasted_iota(jnp.int32, (CHUNK, CHUNK), 1))
    for hd in range(N_HEADS_A):
        cols = slice(hd * HEAD_A, (hd + 1) * HEAD_A)
        w_t = jnp.where(tril, ws_ref[hd], 0.0).astype(BF16)
        v_h = jnp.concatenate([v_sc[b * CHUNK:(b + 1) * CHUNK, cols] for b in range(n_seq)], axis=1)
        mix = _dot(w_t, v_h) + bst_ref[:, hd:hd + 1]
        for b in range(n_seq):
            rws = slice(b * CHUNK, (b + 1) * CHUNK)
            oin_ref[rws, cols] = (a_sc[rws, cols] * mix[:, b * HEAD_A:(b + 1) * HEAD_A]).astype(BF16)

    b_sc[...] = _dot(h, w_in_ref[:, 3 * D_A:3 * D_A + D_B])
    v_sc[...] = b_sc[...].astype(BF16)
    a_sc[...] = _silu(_dot(h, w_in_ref[:, 3 * D_A + D_B:D_IN]))

    first_half = lax.broadcasted_iota(jnp.int32, (N_STREAMS, LANES), 0) < n_seq

    def bu_piece(k, piece):
        t0 = piece * PIECE
        for half in range(2):
            j = half * HALF_SLABS + k
            lhs = jnp.concatenate(
                [v_sc[b * CHUNK + t0:b * CHUNK + t0 + PIECE, j * LANES:(j + 1) * LANES]
                 for b in range(n_seq)], axis=0)
            bu = _dot(lhs, bmat_ref[j])
            for b in range(n_seq):
                r0 = (half * n_seq + b) * STREAM_PITCH + t0
                for l in range(2 * SLAB_TILES):
                    bu_sc[l, r0:r0 + PIECE, :] = bu[b * PIECE:(b + 1) * PIECE, l * LANES:(l + 1) * LANES]

    def scan_piece(k, piece, hr, hi, lam_r, lam_i):
        for t in range(piece * PIECE, (piece + 1) * PIECE):
            new_r, new_i = [], []
            for l in range(SLAB_TILES):
                br = bu_sc[l, pl.ds(t, N_STREAMS, stride=STREAM_PITCH), :]
                bi = bu_sc[SLAB_TILES + l, pl.ds(t, N_STREAMS, stride=STREAM_PITCH), :]
                new_r.append(lam_r[l] * hr[l] - lam_i[l] * hi[l] + br)
                new_i.append(lam_r[l] * hi[l] + lam_i[l] * hr[l] + bi)
            hr, hi = new_r, new_i
            hh_sc[t * N_STREAMS:(t + 1) * N_STREAMS, :] = jnp.concatenate(hr + hi, axis=1)
        return hr, hi

    def c_piece(k, piece):
        t0 = piece * PIECE
        rows = slice(t0 * N_STREAMS, (t0 + PIECE) * N_STREAMS)
        y2 = _dot(hh_sc[rows, :].astype(BF16), cboth_ref[k])
        y2_sc[0, rows, :] = y2[:, 0:LANES]
        y2_sc[1, rows, :] = y2[:, LANES:2 * LANES]
        for half in range(2):
            j = half * HALF_SLABS + k
            for b in range(n_seq):
                s = half * n_seq + b
                c_sc[b * CHUNK + t0:b * CHUNK + t0 + PIECE, j * LANES:(j + 1) * LANES] = (
                    y2_sc[half, pl.ds(t0 * N_STREAMS + s, PIECE, stride=N_STREAMS), :])

    for piece in range(SCAN_PIECES):
        bu_piece(0, piece)
    for k in range(HALF_SLABS):
        tiles = [slice(l * LANES, (l + 1) * LANES) for l in range(SLAB_TILES)]
        lam_r = [jnp.where(first_half, lam_re_ref[k:k + 1, t], lam_re_ref[HALF_SLABS + k:HALF_SLABS + k + 1, t])
                 for t in tiles]
        lam_i = [jnp.where(first_half, lam_im_ref[k:k + 1, t], lam_im_ref[HALF_SLABS + k:HALF_SLABS + k + 1, t])
                 for t in tiles]
        hr = [st_sc[k, :, l * LANES:(l + 1) * LANES] for l in range(SLAB_TILES)]
        hi = [st_sc[k, :, (SLAB_TILES + l) * LANES:(SLAB_TILES + l + 1) * LANES] for l in range(SLAB_TILES)]
        for piece in range(SCAN_PIECES):
            hr, hi = scan_piece(k, piece, hr, hi, lam_r, lam_i)
            c_piece(k, piece)
            if k + 1 < HALF_SLABS:
                bu_piece(k + 1, piece)
        st_sc[k] = jnp.concatenate(hr + hi, axis=1)

    state_ref[...] = st_sc[...]

    for j in range(N_SLABS):
        cols = slice(j * LANES, (j + 1) * LANES)
        y = c_sc[:, cols] + dskip_ref[:, cols] * b_sc[:, cols]
        g = _dot(y.astype(BF16), gmat_ref[j])
        val = g[:, 0:LANES] + bval_ref[:, cols]
        gate = g[:, LANES:2 * LANES] + bgate_ref[:, cols]
        oin_ref[:, D_A + j * LANES:D_A + (j + 1) * LANES] = (
            val * jax.nn.sigmoid(gate) * a_sc[:, cols]).astype(BF16)


def _const_spec(shape):
    zeros = (0,) * len(shape)
    return pl.BlockSpec(shape, lambda i: zeros, pipeline_mode=pl.Buffered(1))


def _mixer(x, mod, mod_row0, g_pre, w_in_bf, ln_g, ln_b, w_s, bst, lam_re, lam_im, bmat, cboth, gmat,
           d_skip, b_val, b_gate):
    n_seq, seq, _ = x.shape
    assert 2 * n_seq == N_STREAMS and seq % CHUNK == 0
    rows = n_seq * CHUNK
    n_steps = seq // CHUNK
    consts = (mod, g_pre, w_in_bf, ln_g, ln_b, w_s, bst, lam_re, lam_im, bmat, cboth, gmat,
              d_skip, b_val, b_gate)
    return pl.pallas_call(
        functools.partial(_mixer_kernel, n_seq=n_seq, mod_row0=mod_row0),
        grid=(n_steps,),
        in_specs=[pl.BlockSpec((n_seq, CHUNK, D_MODEL), lambda i: (0, i, 0))]
        + [_const_spec(c.shape) for c in consts],
        out_specs=(
            pl.BlockSpec((rows, D_A + D_B), lambda i: (i, 0)),
            pl.BlockSpec((HALF_SLABS, N_STREAMS, 2 * SLAB_STATES), lambda i: (0, 0, 0)),
        ),
        out_shape=(
            jax.ShapeDtypeStruct((n_steps * rows, D_A + D_B), BF16),
            jax.ShapeDtypeStruct((HALF_SLABS, N_STREAMS, 2 * SLAB_STATES), F32),
        ),
        scratch_shapes=[
            pltpu.VMEM((rows, D_MODEL), BF16),
            pltpu.VMEM((rows, D_A), F32),
            pltpu.VMEM((rows, D_A), F32),
            pltpu.VMEM((rows, D_A), F32),
            pltpu.VMEM((rows, D_A), BF16),
            pltpu.VMEM((2 * SLAB_TILES, N_STREAMS * STREAM_PITCH, LANES), F32),
            pltpu.VMEM((N_STREAMS * CHUNK, 2 * SLAB_STATES), F32),
            pltpu.VMEM((2, N_STREAMS * CHUNK, LANES), F32),
            pltpu.VMEM((HALF_SLABS, N_STREAMS, 2 * SLAB_STATES), F32),
        ],
        compiler_params=pltpu.CompilerParams(
            dimension_semantics=("arbitrary",), vmem_limit_bytes=VMEM_LIMIT),
        name="mixer",
    )(x, *consts)


def _outproj_kernel(oin_ref, x_ref, mod_ref, gpost_ref, w_out_ref, y_ref, *, n_seq, mod_row0):
    o = _dot(oin_ref[...], w_out_ref[...])
    r = _rms(o, gpost_ref[...])
    for b in range(n_seq):
        m = mod_row0 + b
        gate = mod_ref[m:m + 1, 2 * D_MODEL:3 * D_MODEL]
        y_ref[b] = x_ref[b] + gate * r[b * CHUNK:(b + 1) * CHUNK, :]


def _outproj(oin, x, mod, mod_row0, g_post, w_out_bf):
    n_seq, seq, _ = x.shape
    rows = n_seq * CHUNK
    return pl.pallas_call(
        functools.partial(_outproj_kernel, n_seq=n_seq, mod_row0=mod_row0),
        grid=(seq // CHUNK,),
        in_specs=[
            pl.BlockSpec((rows, D_A + D_B), lambda i: (i, 0)),
            pl.BlockSpec((n_seq, CHUNK, D_MODEL), lambda i: (0, i, 0)),
            _const_spec(mod.shape), _const_spec(g_post.shape), _const_spec(w_out_bf.shape),
        ],
        out_specs=pl.BlockSpec((n_seq, CHUNK, D_MODEL), lambda i: (0, i, 0)),
        out_shape=jax.ShapeDtypeStruct(x.shape, F32),
        compiler_params=pltpu.CompilerParams(
            dimension_semantics=("arbitrary",), vmem_limit_bytes=VMEM_LIMIT),
        name="outproj",
    )(oin, x, mod, g_post, w_out_bf)


def _sample_kernel(x_ref, mod_ref, gpre_ref, w_in_ref, lng_ref, lnb_ref, ws_ref, bst_ref,
                   lam_re_ref, lam_im_ref, bmat_ref, cboth_ref, gmat_ref, dskip_ref, bval_ref, bgate_ref,
                   h0_re_ref, h0_im_ref, gpost_ref, w_out_ref,
                   y_ref, v_ref, hs_re_ref, hs_im_ref, oin_sc):
    n = x_ref.shape[0]
    x = x_ref[...]
    shift = mod_ref[0:n, 0:D_MODEL]
    scale = mod_ref[0:n, D_MODEL:2 * D_MODEL]
    gate = mod_ref[0:n, 2 * D_MODEL:3 * D_MODEL]
    h = (_rms(x, gpre_ref[...]) * (1.0 + scale) + shift).astype(BF16)

    v = _layernorm(_dot(h, w_in_ref[:, D_A:2 * D_A]), lng_ref[...], lnb_ref[...])
    v_ref[...] = v
    u = _dot(h, w_in_ref[:, 0:D_A])
    z = _dot(h, w_in_ref[:, 2 * D_A:3 * D_A])
    for hd in range(N_HEADS_A):
        cols = slice(hd * HEAD_A, (hd + 1) * HEAD_A)
        mix = ws_ref[hd, 0:1, 0:1] * v[:, cols] + bst_ref[0:1, hd:hd + 1]
        oin_sc[:, cols] = (u[:, cols] * mix * _silu(z[:, cols])).astype(BF16)

    xb = _dot(h, w_in_ref[:, 3 * D_A:3 * D_A + D_B])
    zb = _silu(_dot(h, w_in_ref[:, 3 * D_A + D_B:D_IN]))
    xb_bf = xb.astype(BF16)
    for j in range(N_SLABS):
        cols = slice(j * LANES, (j + 1) * LANES)
        st = slice(j * SLAB_STATES, (j + 1) * SLAB_STATES)
        bu = _dot(xb_bf[:, cols], bmat_ref[j])
        lr = lam_re_ref[j:j + 1, :]
        li = lam_im_ref[j:j + 1, :]
        h0r = h0_re_ref[:, st]
        h0i = h0_im_ref[:, st]
        hr = lr * h0r - li * h0i + bu[:, 0:SLAB_STATES]
        hi = lr * h0i + li * h0r + bu[:, SLAB_STATES:2 * SLAB_STATES]
        hs_re_ref[:, st] = hr
        hs_im_ref[:, st] = hi
        k, half = j % HALF_SLABS, j // HALF_SLABS
        c_j = cboth_ref[k, :, half * LANES:(half + 1) * LANES]
        y = _dot(jnp.concatenate([hr, hi], axis=1).astype(BF16), c_j)
        y = y + dskip_ref[:, cols] * xb[:, cols]
        g = _dot(y.astype(BF16), gmat_ref[j])
        val = g[:, 0:LANES] + bval_ref[:, cols]
        gt = g[:, LANES:2 * LANES] + bgate_ref[:, cols]
        oin_sc[:, D_A + j * LANES:D_A + (j + 1) * LANES] = (
            val * jax.nn.sigmoid(gt) * zb[:, cols]).astype(BF16)

    o = _dot(oin_sc[...], w_out_ref[...])
    y_ref[...] = x + gate * _rms(o, gpost_ref[...])


def _sample(x, mod, g_pre, w_in_bf, ln_g, ln_b, w_s, bst, lam_re, lam_im, bmat, cboth, gmat,
            d_skip, b_val, b_gate, h0_re, h0_im, g_post, w_out_bf):
    n = x.shape[0]
    n_state = N_GROUPS_B * P_STATE
    return pl.pallas_call(
        _sample_kernel,
        out_shape=(
            jax.ShapeDtypeStruct((n, D_MODEL), F32),
            jax.ShapeDtypeStruct((n, D_A), F32),
            jax.ShapeDtypeStruct((n, n_state), F32),
            jax.ShapeDtypeStruct((n, n_state), F32),
        ),
        scratch_shapes=[pltpu.VMEM((n, D_A + D_B), BF16)],
        compiler_params=pltpu.CompilerParams(vmem_limit_bytes=VMEM_LIMIT),
        name="sample",
    )(x, mod, g_pre, w_in_bf, ln_g, ln_b, w_s, bst, lam_re, lam_im, bmat, cboth, gmat,
      d_skip, b_val, b_gate, h0_re, h0_im, g_post, w_out_bf)


def kernel(x_prompt, x_sample, c_prompt, c_sample, state_b_re, state_b_im, w_c, b_c, g_pre, w_in,
           ln_v_g, ln_v_b, w_s, b_s, a_re, a_im, log_dt, b_re, b_im, c_re, c_im, d_skip, w_glu, b_glu,
           w_out, g_post):
    n_p = x_prompt.shape[0]
    n_s = x_sample.shape[0]
    assert x_sample.shape[1] == 1 and n_s % SUBLANES == 0

    bmat, cboth, gmat, lam_re, lam_im = _s5_prep(a_re, a_im, log_dt, b_re, b_im, c_re, c_im, w_glu)

    c_all = jnp.concatenate([c_sample, c_prompt], axis=0)
    pad = (-c_all.shape[0]) % SUBLANES
    mod = _adaln(jnp.pad(c_all, ((0, pad), (0, 0))), w_c, b_c)

    row = lambda v: v.reshape(1, -1)
    w_in_bf = w_in.astype(BF16)
    w_out_bf = w_out.astype(BF16)
    shared = (row(g_pre), w_in_bf, row(ln_v_g), row(ln_v_b), w_s, b_s.T, lam_re, lam_im, bmat, cboth,
              gmat, row(d_skip), row(b_glu[:, :GROUP_B]), row(b_glu[:, GROUP_B:]))

    oin, state = _mixer(x_prompt, mod, n_s, *shared)
    y_prompt = _outproj(oin, x_prompt, mod, n_s, row(g_post), w_out_bf)

    st = state.reshape(HALF_SLABS, 2, n_p, 2, SLAB_GROUPS, P_STATE).transpose(3, 2, 1, 0, 4, 5)
    st = st.reshape(2, n_p, N_GROUPS_B, P_STATE)

    y_s, v_s, hs_re, hs_im = _sample(
        x_sample.reshape(n_s, D_MODEL), mod, *shared,
        state_b_re.reshape(n_s, -1), state_b_im.reshape(n_s, -1), row(g_post), w_out_bf)

    return (y_prompt, y_s.reshape(n_s, 1, D_MODEL), v_s.reshape(n_s, 1, D_A), st[0], st[1],
            hs_re.reshape(n_s, N_GROUPS_B, P_STATE), hs_im.reshape(n_s, N_GROUPS_B, P_STATE))
```

```python
import functools

import jax
import jax.numpy as jnp
from jax import lax
from jax.experimental import pallas as pl
from jax.experimental.pallas import tpu as pltpu

F32 = jnp.float32
BF16 = jnp.bfloat16

EPS = 1e-6
D_MODEL = 2048
D_A = 1024
D_B = 1024
D_IN = 3 * D_A + 2 * D_B
CHUNK = 128
HEAD_A = 128
N_HEADS_A = D_A // HEAD_A
GROUP_B = 16
N_GROUPS_B = D_B // GROUP_B
P_STATE = 64

LANES = 128
SUBLANES = 8
SLAB_GROUPS = LANES // GROUP_B
N_SLABS = N_GROUPS_B // SLAB_GROUPS
SLAB_STATES = SLAB_GROUPS * P_STATE
SLAB_TILES = SLAB_STATES // LANES
N_STREAMS = SUBLANES
HALF_SLABS = N_SLABS // 2
STREAM_PITCH = CHUNK + SUBLANES
SCAN_PIECES = 4
PIECE = CHUNK // SCAN_PIECES
VMEM_LIMIT = 62 * 1024 * 1024


def _silu(x):
    return x * jax.nn.sigmoid(x)


def _rms(x, g):
    return x * lax.rsqrt(jnp.mean(x * x, axis=-1, keepdims=True) + EPS) * g


def _layernorm(x, g, b):
    mu = jnp.mean(x, axis=-1, keepdims=True)
    xc = x - mu
    return xc * lax.rsqrt(jnp.mean(xc * xc, axis=-1, keepdims=True) + EPS) * g + b


def _dot(a, b):
    return jnp.dot(a, b, preferred_element_type=F32)


def _discretize(a_re, a_im, log_dt):
    dt = jnp.exp(log_dt)
    mag = jnp.exp(dt * a_re)
    abar_re = mag * jnp.cos(dt * a_im)
    abar_im = mag * jnp.sin(dt * a_im)
    return abar_re, abar_im


def _prep_kernel(a_re_rep, a_im_rep, ldt_rep, bt_re, bt_im, ct_re, ct_im, wglu, a_re8, a_im8, ldt8,
                 bmat_ref, cboth_ref, gmat_ref, lam_re_ref, lam_im_ref):
    lr, li = _discretize(a_re8[...], a_im8[...], ldt8[...])
    lam_re_ref[...] = lr
    lam_im_ref[...] = li

    a_re = a_re_rep[...]
    a_im = a_im_rep[...]
    abar_re, abar_im = _discretize(a_re, a_im, ldt_rep[...])
    num_re = abar_re - 1.0
    num_im = abar_im
    den = a_re * a_re + a_im * a_im
    coef_re = (num_re * a_re + num_im * a_im) / den
    coef_im = (num_im * a_re - num_re * a_im) / den
    b_re = bt_re[...]
    b_im = bt_im[...]
    bbar_re = (coef_re * b_re - coef_im * b_im).astype(BF16)
    bbar_im = (coef_re * b_im + coef_im * b_re).astype(BF16)

    def rep_matrix(k, n, period, offset=0):
        row = lax.broadcasted_iota(jnp.int32, (k, n), 0)
        col = lax.broadcasted_iota(jnp.int32, (k, n), 1)
        return jnp.where((col & (period - 1)) + offset == row, 1.0, 0.0).astype(BF16)

    def block_mask(m, n, row_shift, col_shift):
        row = lax.broadcasted_iota(jnp.int32, (m, n), 0)
        col = lax.broadcasted_iota(jnp.int32, (m, n), 1)
        return (row >> row_shift) == (col >> col_shift)

    rep_state = rep_matrix(P_STATE, SLAB_STATES, P_STATE)
    rep_chan = rep_matrix(GROUP_B, LANES, GROUP_B)
    rep_val = rep_matrix(2 * GROUP_B, LANES, GROUP_B)
    rep_gate = rep_matrix(2 * GROUP_B, LANES, GROUP_B, GROUP_B)
    mask_b = block_mask(LANES, SLAB_STATES, 4, 6)
    mask_c = block_mask(SLAB_STATES, LANES, 6, 4)
    mask_g = block_mask(LANES, LANES, 4, 4)

    for j in range(N_SLABS):
        rows = slice(j * LANES, (j + 1) * LANES)
        bmat_ref[j, :, 0:SLAB_STATES] = jnp.where(
            mask_b, _dot(bbar_re[rows], rep_state), 0.0).astype(BF16)
        bmat_ref[j, :, SLAB_STATES:2 * SLAB_STATES] = jnp.where(
            mask_b, _dot(bbar_im[rows], rep_state), 0.0).astype(BF16)

        k, half = j % HALF_SLABS, j // HALF_SLABS
        ccols = slice(half * LANES, (half + 1) * LANES)
        srows = slice(j * SLAB_STATES, (j + 1) * SLAB_STATES)
        c_re = ct_re[srows, :].astype(BF16)
        c_im = ct_im[srows, :].astype(BF16)
        cboth_ref[k, 0:SLAB_STATES, ccols] = jnp.where(mask_c, _dot(c_re, rep_chan), 0.0).astype(BF16)
        cboth_ref[k, SLAB_STATES:2 * SLAB_STATES, ccols] = jnp.where(
            mask_c, -_dot(c_im, rep_chan), 0.0).astype(BF16)

        w = wglu[rows, :].astype(BF16)
        gmat_ref[j, :, 0:LANES] = jnp.where(mask_g, _dot(w, rep_val), 0.0).astype(BF16)
        gmat_ref[j, :, LANES:2 * LANES] = jnp.where(mask_g, _dot(w, rep_gate), 0.0).astype(BF16)


def _s5_prep(a_re, a_im, log_dt, b_re, b_im, c_re, c_im, w_glu):
    g, p, c = N_GROUPS_B, P_STATE, GROUP_B
    rep = lambda v: jnp.repeat(v, c, axis=0)
    ldt_gp = jnp.broadcast_to(log_dt[:, None], (g, p))
    args = (
        rep(a_re), rep(a_im), rep(ldt_gp),
        b_re.transpose(0, 2, 1).reshape(g * c, p), b_im.transpose(0, 2, 1).reshape(g * c, p),
        c_re.transpose(0, 2, 1).reshape(g * p, c), c_im.transpose(0, 2, 1).reshape(g * p, c),
        w_glu.reshape(g * c, 2 * c),
        a_re.reshape(N_SLABS, SLAB_STATES), a_im.reshape(N_SLABS, SLAB_STATES),
        ldt_gp.reshape(N_SLABS, SLAB_STATES),
    )
    return pl.pallas_call(
        _prep_kernel,
        out_shape=(
            jax.ShapeDtypeStruct((N_SLABS, LANES, 2 * SLAB_STATES), BF16),
            jax.ShapeDtypeStruct((HALF_SLABS, 2 * SLAB_STATES, 2 * LANES), BF16),
            jax.ShapeDtypeStruct((N_SLABS, LANES, 2 * LANES), BF16),
            jax.ShapeDtypeStruct((N_SLABS, SLAB_STATES), F32),
            jax.ShapeDtypeStruct((N_SLABS, SLAB_STATES), F32),
        ),
        name="s5_prep",
    )(*args)


ADALN_TILE = 1024


def _adaln_kernel(c_ref, w_ref, b_ref, o_ref):
    s = _silu(c_ref[...]).astype(BF16)
    o_ref[...] = _dot(s, w_ref[...].astype(BF16)) + b_ref[...]


def _adaln(c_all, w_c, b_c):
    n = c_all.shape[0]
    return pl.pallas_call(
        _adaln_kernel,
        grid=(3 * D_MODEL // ADALN_TILE,),
        in_specs=[
            pl.BlockSpec((n, D_MODEL), lambda j: (0, 0)),
            pl.BlockSpec((D_MODEL, ADALN_TILE), lambda j: (0, j)),
            pl.BlockSpec((1, ADALN_TILE), lambda j: (0, j)),
        ],
        out_specs=pl.BlockSpec((n, ADALN_TILE), lambda j: (0, j)),
        out_shape=jax.ShapeDtypeStruct((n, 3 * D_MODEL), F32),
        compiler_params=pltpu.CompilerParams(vmem_limit_bytes=VMEM_LIMIT),
        name="adaln",
    )(c_all, w_c, b_c.reshape(1, -1))


def _mixer_kernel(x_ref, mod_ref, gpre_ref, w_in_ref, lng_ref, lnb_ref, ws_ref, bst_ref,
                  lam_re_ref, lam_im_ref, bmat_ref, cboth_ref, gmat_ref, dskip_ref, bval_ref, bgate_ref,
                  oin_ref, state_ref,
                  h_sc, a_sc, b_sc, c_sc, v_sc, xb_sc, bu_sc, hh_sc, y2_sc, st_sc, *, n_seq, mod_row0):
    step = pl.program_id(0)

    @pl.when(step == 0)
    def _():
        st_sc[...] = jnp.zeros_like(st_sc)

    for b in range(n_seq):
        r = _rms(x_ref[b], gpre_ref[...])
        m = mod_row0 + b
        shift = mod_ref[m:m + 1, 0:D_MODEL]
        scale = mod_ref[m:m + 1, D_MODEL:2 * D_MODEL]
        h_sc[b * CHUNK:(b + 1) * CHUNK, :] = (r * (1.0 + scale) + shift).astype(BF16)

    h = h_sc[...]
    w_u, w_v, w_z, w_xb, w_zb = 0, D_A, 2 * D_A, 3 * D_A, 3 * D_A + D_B

    b_sc[...] = _dot(h, w_in_ref[:, w_xb:w_xb + D_B])
    xb_sc[...] = b_sc[...].astype(BF16)
    c_sc[...] = _dot(h, w_in_ref[:, w_v:w_v + D_A])
    v_sc[...] = _layernorm(c_sc[...], lng_ref[...], lnb_ref[...]).astype(BF16)

    tril = (lax.broadcasted_iota(jnp.int32, (CHUNK, CHUNK), 0)
            >= lax.broadcasted_iota(jnp.int32, (CHUNK, CHUNK), 1))
    n_fill = HALF_SLABS * SCAN_PIECES // 4
    fill_cols = D_A // n_fill

    def fill_u(q):
        cols = slice(q * fill_cols, (q + 1) * fill_cols)
        a_sc[:, cols] = _dot(h, w_in_ref[:, w_u + q * fill_cols:w_u + (q + 1) * fill_cols])

    def fill_z(q):
        cols = slice(q * fill_cols, (q + 1) * fill_cols)
        z = _dot(h, w_in_ref[:, w_z + q * fill_cols:w_z + (q + 1) * fill_cols])
        a_sc[:, cols] = a_sc[:, cols] * _silu(z)

    def fill_mix(q):
        heads_per_fill = N_HEADS_A // n_fill
        for hd in range(q * heads_per_fill, (q + 1) * heads_per_fill):
            cols = slice(hd * HEAD_A, (hd + 1) * HEAD_A)
            w_t = jnp.where(tril, ws_ref[hd], 0.0).astype(BF16)
            v_h = jnp.concatenate([v_sc[b * CHUNK:(b + 1) * CHUNK, cols] for b in range(n_seq)], axis=1)
            mix = _dot(w_t, v_h) + bst_ref[:, hd:hd + 1]
            for b in range(n_seq):
                rws = slice(b * CHUNK, (b + 1) * CHUNK)
                oin_ref[rws, cols] = (a_sc[rws, cols] * mix[:, b * HEAD_A:(b + 1) * HEAD_A]).astype(BF16)

    def fill_zb(q):
        cols = slice(q * fill_cols, (q + 1) * fill_cols)
        a_sc[:, cols] = _silu(_dot(h, w_in_ref[:, w_zb + q * fill_cols:w_zb + (q + 1) * fill_cols]))

    fillers = [functools.partial(f, q) for f in (fill_u, fill_z, fill_mix, fill_zb) for q in range(n_fill)]

    first_half = lax.broadcasted_iota(jnp.int32, (N_STREAMS, LANES), 0) < n_seq

    def bu_piece(k, piece):
        t0 = piece * PIECE
        for half in range(2):
            j = half * HALF_SLABS + k
            lhs = jnp.concatenate(
                [xb_sc[b * CHUNK + t0:b * CHUNK + t0 + PIECE, j * LANES:(j + 1) * LANES]
                 for b in range(n_seq)], axis=0)
            bu = _dot(lhs, bmat_ref[j])
            for b in range(n_seq):
                r0 = (half * n_seq + b) * STREAM_PITCH + t0
                for l in range(2 * SLAB_TILES):
                    bu_sc[l, r0:r0 + PIECE, :] = bu[b * PIECE:(b + 1) * PIECE, l * LANES:(l + 1) * LANES]

    def scan_piece(k, piece, hr, hi, lam_r, lam_i):
        for t in range(piece * PIECE, (piece + 1) * PIECE):
            new_r, new_i = [], []
            for l in range(SLAB_TILES):
                br = bu_sc[l, pl.ds(t, N_STREAMS, stride=STREAM_PITCH), :]
                bi = bu_sc[SLAB_TILES + l, pl.ds(t, N_STREAMS, stride=STREAM_PITCH), :]
                new_r.append(lam_r[l] * hr[l] - lam_i[l] * hi[l] + br)
                new_i.append(lam_r[l] * hi[l] + lam_i[l] * hr[l] + bi)
            hr, hi = new_r, new_i
            hh_sc[t * N_STREAMS:(t + 1) * N_STREAMS, :] = jnp.concatenate(hr + hi, axis=1)
        return hr, hi

    def c_piece(k, piece):
        t0 = piece * PIECE
        rows = slice(t0 * N_STREAMS, (t0 + PIECE) * N_STREAMS)
        y2 = _dot(hh_sc[rows, :].astype(BF16), cboth_ref[k])
        y2_sc[0, rows, :] = y2[:, 0:LANES]
        y2_sc[1, rows, :] = y2[:, LANES:2 * LANES]
        for half in range(2):
            j = half * HALF_SLABS + k
            for b in range(n_seq):
                s = half * n_seq + b
                c_sc[b * CHUNK + t0:b * CHUNK + t0 + PIECE, j * LANES:(j + 1) * LANES] = (
                    y2_sc[half, pl.ds(t0 * N_STREAMS + s, PIECE, stride=N_STREAMS), :])

    for piece in range(SCAN_PIECES):
        bu_piece(0, piece)
    for k in range(HALF_SLABS):
        tiles = [slice(l * LANES, (l + 1) * LANES) for l in range(SLAB_TILES)]
        lam_r = [jnp.where(first_half, lam_re_ref[k:k + 1, t], lam_re_ref[HALF_SLABS + k:HALF_SLABS + k + 1, t])
                 for t in tiles]
        lam_i = [jnp.where(first_half, lam_im_ref[k:k + 1, t], lam_im_ref[HALF_SLABS + k:HALF_SLABS + k + 1, t])
                 for t in tiles]
        hr = [st_sc[k, :, l * LANES:(l + 1) * LANES] for l in range(SLAB_TILES)]
        hi = [st_sc[k, :, (SLAB_TILES + l) * LANES:(SLAB_TILES + l + 1) * LANES] for l in range(SLAB_TILES)]
        for piece in range(SCAN_PIECES):
            hr, hi = scan_piece(k, piece, hr, hi, lam_r, lam_i)
            c_piece(k, piece)
            if k + 1 < HALF_SLABS:
                bu_piece(k + 1, piece)
            fillers[k * SCAN_PIECES + piece]()
        st_sc[k] = jnp.concatenate(hr + hi, axis=1)

    state_ref[...] = st_sc[...]

    for j in range(N_SLABS):
        cols = slice(j * LANES, (j + 1) * LANES)
        y = c_sc[:, cols] + dskip_ref[:, cols] * b_sc[:, cols]
        g = _dot(y.astype(BF16), gmat_ref[j])
        val = g[:, 0:LANES] + bval_ref[:, cols]
        gate = g[:, LANES:2 * LANES] + bgate_ref[:, cols]
        oin_ref[:, D_A + j * LANES:D_A + (j + 1) * LANES] = (
            val * jax.nn.sigmoid(gate) * a_sc[:, cols]).astype(BF16)


def _const_spec(shape):
    zeros = (0,) * len(shape)
    return pl.BlockSpec(shape, lambda i: zeros, pipeline_mode=pl.Buffered(1))


def _mixer(x, mod, mod_row0, g_pre, w_in_bf, ln_g, ln_b, w_s, bst, lam_re, lam_im, bmat, cboth, gmat,
           d_skip, b_val, b_gate):
    n_seq, seq, _ = x.shape
    assert 2 * n_seq == N_STREAMS and seq % CHUNK == 0
    rows = n_seq * CHUNK
    n_steps = seq // CHUNK
    consts = (mod, g_pre, w_in_bf, ln_g, ln_b, w_s, bst, lam_re, lam_im, bmat, cboth, gmat,
              d_skip, b_val, b_gate)
    return pl.pallas_call(
        functools.partial(_mixer_kernel, n_seq=n_seq, mod_row0=mod_row0),
        grid=(n_steps,),
        in_specs=[pl.BlockSpec((n_seq, CHUNK, D_MODEL), lambda i: (0, i, 0))]
        + [_const_spec(c.shape) for c in consts],
        out_specs=(
            pl.BlockSpec((rows, D_A + D_B), lambda i: (i, 0)),
            pl.BlockSpec((HALF_SLABS, N_STREAMS, 2 * SLAB_STATES), lambda i: (0, 0, 0)),
        ),
        out_shape=(
            jax.ShapeDtypeStruct((n_steps * rows, D_A + D_B), BF16),
            jax.ShapeDtypeStruct((HALF_SLABS, N_STREAMS, 2 * SLAB_STATES), F32),
        ),
        scratch_shapes=[
            pltpu.VMEM((rows, D_MODEL), BF16),
            pltpu.VMEM((rows, D_A), F32),
            pltpu.VMEM((rows, D_A), F32),
            pltpu.VMEM((rows, D_A), F32),
            pltpu.VMEM((rows, D_A), BF16),
            pltpu.VMEM((rows, D_B), BF16),
            pltpu.VMEM((2 * SLAB_TILES, N_STREAMS * STREAM_PITCH, LANES), F32),
            pltpu.VMEM((N_STREAMS * CHUNK, 2 * SLAB_STATES), F32),
            pltpu.VMEM((2, N_STREAMS * CHUNK, LANES), F32),
            pltpu.VMEM((HALF_SLABS, N_STREAMS, 2 * SLAB_STATES), F32),
        ],
        compiler_params=pltpu.CompilerParams(
            dimension_semantics=("arbitrary",), vmem_limit_bytes=VMEM_LIMIT),
        name="mixer",
    )(x, *consts)


def _outproj_kernel(oin_ref, x_ref, mod_ref, gpost_ref, w_out_ref, y_ref, *, n_seq, mod_row0):
    for b in range(n_seq):
        o = _dot(oin_ref[b * CHUNK:(b + 1) * CHUNK, :], w_out_ref[...])
        m = mod_row0 + b
        gate = mod_ref[m:m + 1, 2 * D_MODEL:3 * D_MODEL]
        y_ref[b] = x_ref[b] + gate * _rms(o, gpost_ref[...])


def _outproj(oin, x, mod, mod_row0, g_post, w_out_bf):
    n_seq, seq, _ = x.shape
    rows = n_seq * CHUNK
    return pl.pallas_call(
        functools.partial(_outproj_kernel, n_seq=n_seq, mod_row0=mod_row0),
        grid=(seq // CHUNK,),
        in_specs=[
            pl.BlockSpec((rows, D_A + D_B), lambda i: (i, 0)),
            pl.BlockSpec((n_seq, CHUNK, D_MODEL), lambda i: (0, i, 0)),
            _const_spec(mod.shape), _const_spec(g_post.shape), _const_spec(w_out_bf.shape),
        ],
        out_specs=pl.BlockSpec((n_seq, CHUNK, D_MODEL), lambda i: (0, i, 0)),
        out_shape=jax.ShapeDtypeStruct(x.shape, F32),
        compiler_params=pltpu.CompilerParams(
            dimension_semantics=("arbitrary",), vmem_limit_bytes=VMEM_LIMIT),
        name="outproj",
    )(oin, x, mod, g_post, w_out_bf)


def _sample_kernel(x_ref, mod_ref, gpre_ref, w_in_ref, lng_ref, lnb_ref, ws_ref, bst_ref,
                   lam_re_ref, lam_im_ref, bmat_ref, cboth_ref, gmat_ref, dskip_ref, bval_ref, bgate_ref,
                   h0_re_ref, h0_im_ref, gpost_ref, w_out_ref,
                   y_ref, v_ref, hs_re_ref, hs_im_ref, oin_sc):
    n = x_ref.shape[0]
    x = x_ref[...]
    shift = mod_ref[0:n, 0:D_MODEL]
    scale = mod_ref[0:n, D_MODEL:2 * D_MODEL]
    gate = mod_ref[0:n, 2 * D_MODEL:3 * D_MODEL]
    h = (_rms(x, gpre_ref[...]) * (1.0 + scale) + shift).astype(BF16)

    v = _layernorm(_dot(h, w_in_ref[:, D_A:2 * D_A]), lng_ref[...], lnb_ref[...])
    v_ref[...] = v
    u = _dot(h, w_in_ref[:, 0:D_A])
    z = _dot(h, w_in_ref[:, 2 * D_A:3 * D_A])
    for hd in range(N_HEADS_A):
        cols = slice(hd * HEAD_A, (hd + 1) * HEAD_A)
        mix = ws_ref[hd, 0:1, 0:1] * v[:, cols] + bst_ref[0:1, hd:hd + 1]
        oin_sc[:, cols] = (u[:, cols] * mix * _silu(z[:, cols])).astype(BF16)

    xb = _dot(h, w_in_ref[:, 3 * D_A:3 * D_A + D_B])
    zb = _silu(_dot(h, w_in_ref[:, 3 * D_A + D_B:D_IN]))
    xb_bf = xb.astype(BF16)
    for j in range(N_SLABS):
        cols = slice(j * LANES, (j + 1) * LANES)
        st = slice(j * SLAB_STATES, (j + 1) * SLAB_STATES)
        bu = _dot(xb_bf[:, cols], bmat_ref[j])
        lr = lam_re_ref[j:j + 1, :]
        li = lam_im_ref[j:j + 1, :]
        h0r = h0_re_ref[:, st]
        h0i = h0_im_ref[:, st]
        hr = lr * h0r - li * h0i + bu[:, 0:SLAB_STATES]
        hi = lr * h0i + li * h0r + bu[:, SLAB_STATES:2 * SLAB_STATES]
        hs_re_ref[:, st] = hr
        hs_im_ref[:, st] = hi
        k, half = j % HALF_SLABS, j // HALF_SLABS
        c_j = cboth_ref[k, :, half * LANES:(half + 1) * LANES]
        y = _dot(jnp.concatenate([hr, hi], axis=1).astype(BF16), c_j)
        y = y + dskip_ref[:, cols] * xb[:, cols]
        g = _dot(y.astype(BF16), gmat_ref[j])
        val = g[:, 0:LANES] + bval_ref[:, cols]
        gt = g[:, LANES:2 * LANES] + bgate_ref[:, cols]
        oin_sc[:, D_A + j * LANES:D_A + (j + 1) * LANES] = (
            val * jax.nn.sigmoid(gt) * zb[:, cols]).astype(BF16)

    o = _dot(oin_sc[...], w_out_ref[...])
    y_ref[...] = x + gate * _rms(o, gpost_ref[...])


def _sample(x, mod, g_pre, w_in_bf, ln_g, ln_b, w_s, bst, lam_re, lam_im, bmat, cboth, gmat,
            d_skip, b_val, b_gate, h0_re, h0_im, g_post, w_out_bf):
    n = x.shape[0]
    n_state = N_GROUPS_B * P_STATE
    return pl.pallas_call(
        _sample_kernel,
        out_shape=(
            jax.ShapeDtypeStruct((n, D_MODEL), F32),
            jax.ShapeDtypeStruct((n, D_A), F32),
            jax.ShapeDtypeStruct((n, n_state), F32),
            jax.ShapeDtypeStruct((n, n_state), F32),
        ),
        scratch_shapes=[pltpu.VMEM((n, D_A + D_B), BF16)],
        compiler_params=pltpu.CompilerParams(vmem_limit_bytes=VMEM_LIMIT),
        name="sample",
    )(x, mod, g_pre, w_in_bf, ln_g, ln_b, w_s, bst, lam_re, lam_im, bmat, cboth, gmat,
      d_skip, b_val, b_gate, h0_re, h0_im, g_post, w_out_bf)


def kernel(x_prompt, x_sample, c_prompt, c_sample, state_b_re, state_b_im, w_c, b_c, g_pre, w_in,
           ln_v_g, ln_v_b, w_s, b_s, a_re, a_im, log_dt, b_re, b_im, c_re, c_im, d_skip, w_glu, b_glu,
           w_out, g_post):
    n_p = x_prompt.shape[0]
    n_s = x_sample.shape[0]
    assert x_sample.shape[1] == 1 and n_s % SUBLANES == 0

    bmat, cboth, gmat, lam_re, lam_im = _s5_prep(a_re, a_im, log_dt, b_re, b_im, c_re, c_im, w_glu)

    c_all = jnp.concatenate([c_sample, c_prompt], axis=0)
    pad = (-c_all.shape[0]) % SUBLANES
    mod = _adaln(jnp.pad(c_all, ((0, pad), (0, 0))), w_c, b_c)

    row = lambda v: v.reshape(1, -1)
    w_in_bf = w_in.astype(BF16)
    w_out_bf = w_out.astype(BF16)
    shared = (row(g_pre), w_in_bf, row(ln_v_g), row(ln_v_b), w_s, b_s.T, lam_re, lam_im, bmat, cboth,
              gmat, row(d_skip), row(b_glu[:, :GROUP_B]), row(b_glu[:, GROUP_B:]))

    oin, state = _mixer(x_prompt, mod, n_s, *shared)
    y_prompt = _outproj(oin, x_prompt, mod, n_s, row(g_post), w_out_bf)

    st = state.reshape(HALF_SLABS, 2, n_p, 2, SLAB_GROUPS, P_STATE).transpose(3, 2, 1, 0, 4, 5)
    st = st.reshape(2, n_p, N_GROUPS_B, P_STATE)

    y_s, v_s, hs_re, hs_im = _sample(
        x_sample.reshape(n_s, D_MODEL), mod, *shared,
        state_b_re.reshape(n_s, -1), state_b_im.reshape(n_s, -1), row(g_post), w_out_bf)

    return (y_prompt, y_s.reshape(n_s, 1, D_MODEL), v_s.reshape(n_s, 1, D_A), st[0], st[1],
            hs_re.reshape(n_s, N_GROUPS_B, P_STATE), hs_im.reshape(n_s, N_GROUPS_B, P_STATE))
```

```python
import functools

import jax
import jax.numpy as jnp
from jax import lax
from jax.experimental import pallas as pl
from jax.experimental.pallas import tpu as pltpu

F32 = jnp.float32
BF16 = jnp.bfloat16

EPS = 1e-6
D_MODEL = 2048
D_A = 1024
D_B = 1024
D_IN = 3 * D_A + 2 * D_B
CHUNK = 128
HEAD_A = 128
N_HEADS_A = D_A // HEAD_A
GROUP_B = 16
N_GROUPS_B = D_B // GROUP_B
P_STATE = 64

LANES = 128
SUBLANES = 8
SLAB_GROUPS = LANES // GROUP_B
N_SLABS = N_GROUPS_B // SLAB_GROUPS
SLAB_STATES = SLAB_GROUPS * P_STATE
SLAB_TILES = SLAB_STATES // LANES
N_STREAMS = SUBLANES
HALF_SLABS = N_SLABS // 2
STREAM_PITCH = CHUNK + SUBLANES
SCAN_PIECES = 4
PIECE = CHUNK // SCAN_PIECES
VMEM_LIMIT = 62 * 1024 * 1024


def _silu(x):
    return x * jax.nn.sigmoid(x)


def _rms(x, g):
    return x * lax.rsqrt(jnp.mean(x * x, axis=-1, keepdims=True) + EPS) * g


def _layernorm(x, g, b):
    mu = jnp.mean(x, axis=-1, keepdims=True)
    xc = x - mu
    return xc * lax.rsqrt(jnp.mean(xc * xc, axis=-1, keepdims=True) + EPS) * g + b


def _dot(a, b):
    return jnp.dot(a, b, preferred_element_type=F32)


def _discretize(a_re, a_im, log_dt):
    dt = jnp.exp(log_dt)
    mag = jnp.exp(dt * a_re)
    abar_re = mag * jnp.cos(dt * a_im)
    abar_im = mag * jnp.sin(dt * a_im)
    return abar_re, abar_im


def _nt_dot(a, b):
    return lax.dot_general(a, b, (((1,), (1,)), ((), ())), preferred_element_type=F32)


def _bf16_terms(x):
    hi = x.astype(BF16)
    r1 = x - hi.astype(F32)
    mid = r1.astype(BF16)
    lo = (r1 - mid.astype(F32)).astype(BF16)
    return hi, mid, lo


def _prep_kernel(a8_ref, b_re_ref, b_im_ref, c_re_ref, c_im_ref, wglu_ref, bglu_ref, bs_ref,
                 bmat_ref, cboth_ref, gmat_ref, lam_re_ref, lam_im_ref, bglu_row_ref, bst_ref):
    a_re = a8_ref[0:N_SLABS, :]
    a_im = a8_ref[N_SLABS:2 * N_SLABS, :]
    abar_re, abar_im = _discretize(a_re, a_im, a8_ref[2 * N_SLABS:3 * N_SLABS, :])
    lam_re_ref[...] = abar_re
    lam_im_ref[...] = abar_im
    num_re = abar_re - 1.0
    num_im = abar_im
    den = a_re * a_re + a_im * a_im
    coef_re = (num_re * a_re + num_im * a_im) / den
    coef_im = (num_im * a_re - num_re * a_im) / den

    def rep_matrix(k, n, period, offset=0):
        row = lax.broadcasted_iota(jnp.int32, (k, n), 0)
        col = lax.broadcasted_iota(jnp.int32, (k, n), 1)
        return jnp.where((col & (period - 1)) + offset == row, 1.0, 0.0).astype(BF16)

    def rep_rows(m, k, period):
        row = lax.broadcasted_iota(jnp.int32, (m, k), 0)
        col = lax.broadcasted_iota(jnp.int32, (m, k), 1)
        return jnp.where((row & (period - 1)) == col, 1.0, 0.0).astype(BF16)

    def block_mask(m, n, row_shift, col_shift):
        row = lax.broadcasted_iota(jnp.int32, (m, n), 0)
        col = lax.broadcasted_iota(jnp.int32, (m, n), 1)
        return (row >> row_shift) == (col >> col_shift)

    rows_chan = rep_rows(LANES, GROUP_B, GROUP_B)
    rows_state = rep_rows(SLAB_STATES, P_STATE, P_STATE)
    rep_val = rep_matrix(2 * GROUP_B, LANES, GROUP_B)
    rep_gate = rep_matrix(2 * GROUP_B, LANES, GROUP_B, GROUP_B)
    mask_b = block_mask(LANES, SLAB_STATES, 4, 6)
    mask_c = block_mask(SLAB_STATES, LANES, 6, 4)
    mask_g = block_mask(LANES, LANES, 4, 4)

    def transposed_copy(rep, x):
        return sum(_nt_dot(rep, term) for term in _bf16_terms(x))

    for j in range(N_SLABS):
        srows = slice(j * SLAB_STATES, (j + 1) * SLAB_STATES)
        bt_re = transposed_copy(rows_chan, b_re_ref[srows, :])
        bt_im = transposed_copy(rows_chan, b_im_ref[srows, :])
        cr = coef_re[j:j + 1, :]
        ci = coef_im[j:j + 1, :]
        bmat_ref[j, :, 0:SLAB_STATES] = jnp.where(mask_b, cr * bt_re - ci * bt_im, 0.0).astype(BF16)
        bmat_ref[j, :, SLAB_STATES:2 * SLAB_STATES] = jnp.where(
            mask_b, cr * bt_im + ci * bt_re, 0.0).astype(BF16)

        k, half = j % HALF_SLABS, j // HALF_SLABS
        ccols = slice(half * LANES, (half + 1) * LANES)
        crows = slice(j * LANES, (j + 1) * LANES)
        ct_re = _nt_dot(rows_state, c_re_ref[crows, :].astype(BF16))
        ct_im = _nt_dot(rows_state, c_im_ref[crows, :].astype(BF16))
        cboth_ref[k, 0:SLAB_STATES, ccols] = jnp.where(mask_c, ct_re, 0.0).astype(BF16)
        cboth_ref[k, SLAB_STATES:2 * SLAB_STATES, ccols] = jnp.where(mask_c, -ct_im, 0.0).astype(BF16)

        w = wglu_ref[crows, :].astype(BF16)
        gmat_ref[j, :, 0:LANES] = jnp.where(mask_g, _dot(w, rep_val), 0.0).astype(BF16)
        gmat_ref[j, :, LANES:2 * LANES] = jnp.where(mask_g, _dot(w, rep_gate), 0.0).astype(BF16)

    mask_row = block_mask(N_GROUPS_B, D_B, 0, 4)
    for i, rep in enumerate((rep_matrix(2 * GROUP_B, D_B, GROUP_B),
                             rep_matrix(2 * GROUP_B, D_B, GROUP_B, GROUP_B))):
        spread = sum(_dot(term, rep) for term in _bf16_terms(bglu_ref[...]))
        bglu_row_ref[i:i + 1, :] = jnp.sum(jnp.where(mask_row, spread, 0.0), axis=0, keepdims=True)

    eye = rep_rows(CHUNK, CHUNK, CHUNK)
    bst_ref[...] = transposed_copy(eye, bs_ref[...])


def _s5_prep(a_re, a_im, log_dt, b_re, b_im, c_re, c_im, w_glu, b_glu, b_s):
    g, p, c = N_GROUPS_B, P_STATE, GROUP_B
    a8 = jnp.stack([a_re, a_im, jnp.broadcast_to(log_dt[:, None], (g, p))]).reshape(3 * N_SLABS, SLAB_STATES)
    args = (a8, b_re.reshape(g * p, c), b_im.reshape(g * p, c), c_re.reshape(g * c, p),
            c_im.reshape(g * c, p), w_glu.reshape(g * c, 2 * c), b_glu, b_s)
    return pl.pallas_call(
        _prep_kernel,
        out_shape=(
            jax.ShapeDtypeStruct((N_SLABS, LANES, 2 * SLAB_STATES), BF16),
            jax.ShapeDtypeStruct((HALF_SLABS, 2 * SLAB_STATES, 2 * LANES), BF16),
            jax.ShapeDtypeStruct((N_SLABS, LANES, 2 * LANES), BF16),
            jax.ShapeDtypeStruct((N_SLABS, SLAB_STATES), F32),
            jax.ShapeDtypeStruct((N_SLABS, SLAB_STATES), F32),
            jax.ShapeDtypeStruct((2, D_B), F32),
            jax.ShapeDtypeStruct((CHUNK, N_HEADS_A), F32),
        ),
        name="s5_prep",
    )(*args)


ADALN_TILE = 1024


def _adaln_kernel(c_ref, w_ref, b_ref, o_ref):
    s = _silu(c_ref[...]).astype(BF16)
    o_ref[...] = _dot(s, w_ref[...].astype(BF16)) + b_ref[...]


def _adaln(c_all, w_c, b_c):
    n = c_all.shape[0]
    return pl.pallas_call(
        _adaln_kernel,
        grid=(3 * D_MODEL // ADALN_TILE,),
        in_specs=[
            pl.BlockSpec((n, D_MODEL), lambda j: (0, 0)),
            pl.BlockSpec((D_MODEL, ADALN_TILE), lambda j: (0, j)),
            pl.BlockSpec((1, ADALN_TILE), lambda j: (0, j)),
        ],
        out_specs=pl.BlockSpec((n, ADALN_TILE), lambda j: (0, j)),
        out_shape=jax.ShapeDtypeStruct((n, 3 * D_MODEL), F32),
        compiler_params=pltpu.CompilerParams(vmem_limit_bytes=VMEM_LIMIT),
        name="adaln",
    )(c_all, w_c, b_c.reshape(1, -1))


def _mixer_kernel(x_ref, mod_ref, gpre_ref, w_in_ref, lng_ref, lnb_ref, ws_ref, bst_ref,
                  lam_re_ref, lam_im_ref, bmat_ref, cboth_ref, gmat_ref, dskip_ref, bglu_ref,
                  oin_ref, state_ref,
                  h_sc, a_sc, b_sc, c_sc, v_sc, xb_sc, bu_sc, hh_sc, y2_sc, st_sc, *, n_seq, mod_row0):
    step = pl.program_id(0)

    @pl.when(step == 0)
    def _():
        st_sc[...] = jnp.zeros_like(st_sc)

    w_u, w_v, w_z, w_xb, w_zb = 0, D_A, 2 * D_A, 3 * D_A, 3 * D_A + D_B

    for b in range(n_seq):
        r = _rms(x_ref[b], gpre_ref[...])
        m = mod_row0 + b
        shift = mod_ref[m:m + 1, 0:D_MODEL]
        scale = mod_ref[m:m + 1, D_MODEL:2 * D_MODEL]
        rws = slice(b * CHUNK, (b + 1) * CHUNK)
        h_sc[rws, :] = (r * (1.0 + scale) + shift).astype(BF16)
        b_sc[rws, :] = _dot(h_sc[rws, :], w_in_ref[:, w_xb:w_xb + D_B])
        xb_sc[rws, :] = b_sc[rws, :].astype(BF16)

    h = h_sc[...]
    c_sc[...] = _dot(h, w_in_ref[:, w_v:w_v + D_A])
    v_sc[...] = _layernorm(c_sc[...], lng_ref[...], lnb_ref[...]).astype(BF16)

    tril = (lax.broadcasted_iota(jnp.int32, (CHUNK, CHUNK), 0)
            >= lax.broadcasted_iota(jnp.int32, (CHUNK, CHUNK), 1))
    n_fill = HALF_SLABS * SCAN_PIECES // 4
    fill_cols = D_A // n_fill

    def fill_u(q):
        cols = slice(q * fill_cols, (q + 1) * fill_cols)
        a_sc[:, cols] = _dot(h, w_in_ref[:, w_u + q * fill_cols:w_u + (q + 1) * fill_cols])

    def fill_z(q):
        cols = slice(q * fill_cols, (q + 1) * fill_cols)
        z = _dot(h, w_in_ref[:, w_z + q * fill_cols:w_z + (q + 1) * fill_cols])
        a_sc[:, cols] = a_sc[:, cols] * _silu(z)

    def fill_mix(q):
        heads_per_fill = N_HEADS_A // n_fill
        for hd in range(q * heads_per_fill, (q + 1) * heads_per_fill):
            cols = slice(hd * HEAD_A, (hd + 1) * HEAD_A)
            w_t = jnp.where(tril, ws_ref[hd], 0.0).astype(BF16)
            v_h = jnp.concatenate([v_sc[b * CHUNK:(b + 1) * CHUNK, cols] for b in range(n_seq)], axis=1)
            mix = _dot(w_t, v_h) + bst_ref[:, hd:hd + 1]
            for b in range(n_seq):
                rws = slice(b * CHUNK, (b + 1) * CHUNK)
                oin_ref[rws, cols] = (a_sc[rws, cols] * mix[:, b * HEAD_A:(b + 1) * HEAD_A]).astype(BF16)

    def fill_zb(q):
        cols = slice(q * fill_cols, (q + 1) * fill_cols)
        a_sc[:, cols] = _silu(_dot(h, w_in_ref[:, w_zb + q * fill_cols:w_zb + (q + 1) * fill_cols]))

    fillers = [functools.partial(f, q) for f in (fill_u, fill_z, fill_mix, fill_zb) for q in range(n_fill)]

    first_half = lax.broadcasted_iota(jnp.int32, (N_STREAMS, LANES), 0) < n_seq

    def bu_piece(k, piece):
        t0 = piece * PIECE
        for half in range(2):
            j = half * HALF_SLABS + k
            lhs = jnp.concatenate(
                [xb_sc[b * CHUNK + t0:b * CHUNK + t0 + PIECE, j * LANES:(j + 1) * LANES]
                 for b in range(n_seq)], axis=0)
            bu = _dot(lhs, bmat_ref[j])
            for b in range(n_seq):
                r0 = (half * n_seq + b) * STREAM_PITCH + t0
                for l in range(2 * SLAB_TILES):
                    bu_sc[l, r0:r0 + PIECE, :] = bu[b * PIECE:(b + 1) * PIECE, l * LANES:(l + 1) * LANES]

    def scan_piece(k, piece, hr, hi, lam_r, lam_i):
        for t in range(piece * PIECE, (piece + 1) * PIECE):
            new_r, new_i = [], []
            for l in range(SLAB_TILES):
                br = bu_sc[l, pl.ds(t, N_STREAMS, stride=STREAM_PITCH), :]
                bi = bu_sc[SLAB_TILES + l, pl.ds(t, N_STREAMS, stride=STREAM_PITCH), :]
                new_r.append(lam_r[l] * hr[l] - lam_i[l] * hi[l] + br)
                new_i.append(lam_r[l] * hi[l] + lam_i[l] * hr[l] + bi)
            hr, hi = new_r, new_i
            hh_sc[t * N_STREAMS:(t + 1) * N_STREAMS, :] = jnp.concatenate(hr + hi, axis=1)
        return hr, hi

    def c_piece(k, piece):
        t0 = piece * PIECE
        rows = slice(t0 * N_STREAMS, (t0 + PIECE) * N_STREAMS)
        y2 = _dot(hh_sc[rows, :].astype(BF16), cboth_ref[k])
        y2_sc[0, rows, :] = y2[:, 0:LANES]
        y2_sc[1, rows, :] = y2[:, LANES:2 * LANES]
        for half in range(2):
            j = half * HALF_SLABS + k
            for b in range(n_seq):
                s = half * n_seq + b
                c_sc[b * CHUNK + t0:b * CHUNK + t0 + PIECE, j * LANES:(j + 1) * LANES] = (
                    y2_sc[half, pl.ds(t0 * N_STREAMS + s, PIECE, stride=N_STREAMS), :])

    for piece in range(SCAN_PIECES):
        bu_piece(0, piece)
    for k in range(HALF_SLABS):
        tiles = [slice(l * LANES, (l + 1) * LANES) for l in range(SLAB_TILES)]
        lam_r = [jnp.where(first_half, lam_re_ref[k:k + 1, t], lam_re_ref[HALF_SLABS + k:HALF_SLABS + k + 1, t])
                 for t in tiles]
        lam_i = [jnp.where(first_half, lam_im_ref[k:k + 1, t], lam_im_ref[HALF_SLABS + k:HALF_SLABS + k + 1, t])
                 for t in tiles]
        hr = [st_sc[k, :, l * LANES:(l + 1) * LANES] for l in range(SLAB_TILES)]
        hi = [st_sc[k, :, (SLAB_TILES + l) * LANES:(SLAB_TILES + l + 1) * LANES] for l in range(SLAB_TILES)]
        for piece in range(SCAN_PIECES):
            hr, hi = scan_piece(k, piece, hr, hi, lam_r, lam_i)
            c_piece(k, piece)
            if k + 1 < HALF_SLABS:
                bu_piece(k + 1, piece)
            fillers[k * SCAN_PIECES + piece]()
        st_sc[k] = jnp.concatenate(hr + hi, axis=1)

    for k in range(HALF_SLABS):
        for half in range(2):
            j = half * HALF_SLABS + k
            for b in range(n_seq):
                s = half * n_seq + b
                for ri in range(2):
                    state_ref[ri, b, j:j + 1, :] = st_sc[k, s:s + 1, ri * SLAB_STATES:(ri + 1) * SLAB_STATES]

    for j in range(N_SLABS):
        cols = slice(j * LANES, (j + 1) * LANES)
        y = c_sc[:, cols] + dskip_ref[:, cols] * b_sc[:, cols]
        g = _dot(y.astype(BF16), gmat_ref[j])
        val = g[:, 0:LANES] + bglu_ref[0:1, cols]
        gate = g[:, LANES:2 * LANES] + bglu_ref[1:2, cols]
        oin_ref[:, D_A + j * LANES:D_A + (j + 1) * LANES] = (
            val * jax.nn.sigmoid(gate) * a_sc[:, cols]).astype(BF16)


def _const_spec(shape):
    zeros = (0,) * len(shape)
    return pl.BlockSpec(shape, lambda i: zeros, pipeline_mode=pl.Buffered(1))


def _mixer(x, mod, mod_row0, g_pre, w_in_bf, ln_g, ln_b, w_s, bst, lam_re, lam_im, bmat, cboth, gmat,
           d_skip, b_glu_rows):
    n_seq, seq, _ = x.shape
    assert 2 * n_seq == N_STREAMS and seq % CHUNK == 0
    rows = n_seq * CHUNK
    n_steps = seq // CHUNK
    consts = (mod, g_pre, w_in_bf, ln_g, ln_b, w_s, bst, lam_re, lam_im, bmat, cboth, gmat,
              d_skip, b_glu_rows)
    return pl.pallas_call(
        functools.partial(_mixer_kernel, n_seq=n_seq, mod_row0=mod_row0),
        grid=(n_steps,),
        in_specs=[pl.BlockSpec((n_seq, CHUNK, D_MODEL), lambda i: (0, i, 0))]
        + [_const_spec(c.shape) for c in consts],
        out_specs=(
            pl.BlockSpec((rows, D_A + D_B), lambda i: (i, 0)),
            pl.BlockSpec((2, n_seq, N_SLABS, SLAB_STATES), lambda i: (0, 0, 0, 0)),
        ),
        out_shape=(
            jax.ShapeDtypeStruct((n_steps * rows, D_A + D_B), BF16),
            jax.ShapeDtypeStruct((2, n_seq, N_SLABS, SLAB_STATES), F32),
        ),
        scratch_shapes=[
            pltpu.VMEM((rows, D_MODEL), BF16),
            pltpu.VMEM((rows, D_A), F32),
            pltpu.VMEM((rows, D_A), F32),
            pltpu.VMEM((rows, D_A), F32),
            pltpu.VMEM((rows, D_A), BF16),
            pltpu.VMEM((rows, D_B), BF16),
            pltpu.VMEM((2 * SLAB_TILES, N_STREAMS * STREAM_PITCH, LANES), F32),
            pltpu.VMEM((N_STREAMS * CHUNK, 2 * SLAB_STATES), F32),
            pltpu.VMEM((2, N_STREAMS * CHUNK, LANES), F32),
            pltpu.VMEM((HALF_SLABS, N_STREAMS, 2 * SLAB_STATES), F32),
        ],
        compiler_params=pltpu.CompilerParams(
            dimension_semantics=("arbitrary",), vmem_limit_bytes=VMEM_LIMIT),
        name="mixer",
    )(x, *consts)


def _outproj_kernel(oin_ref, x_ref, mod_ref, gpost_ref, w_out_ref, y_ref, *, n_seq, mod_row0):
    r = _rms(_dot(oin_ref[...], w_out_ref[...]), gpost_ref[...])
    for b in range(n_seq):
        m = mod_row0 + b
        gate = mod_ref[m:m + 1, 2 * D_MODEL:3 * D_MODEL]
        y_ref[b] = x_ref[b] + gate * r[b * CHUNK:(b + 1) * CHUNK, :]


def _outproj(oin, x, mod, mod_row0, g_post, w_out_bf):
    n_seq, seq, _ = x.shape
    rows = n_seq * CHUNK
    return pl.pallas_call(
        functools.partial(_outproj_kernel, n_seq=n_seq, mod_row0=mod_row0),
        grid=(seq // CHUNK,),
        in_specs=[
            pl.BlockSpec((rows, D_A + D_B), lambda i: (i, 0)),
            pl.BlockSpec((n_seq, CHUNK, D_MODEL), lambda i: (0, i, 0)),
            _const_spec(mod.shape), _const_spec(g_post.shape), _const_spec(w_out_bf.shape),
        ],
        out_specs=pl.BlockSpec((n_seq, CHUNK, D_MODEL), lambda i: (0, i, 0)),
        out_shape=jax.ShapeDtypeStruct(x.shape, F32),
        compiler_params=pltpu.CompilerParams(
            dimension_semantics=("arbitrary",), vmem_limit_bytes=VMEM_LIMIT),
        name="outproj",
    )(oin, x, mod, g_post, w_out_bf)


def _sample_kernel(x_ref, mod_ref, gpre_ref, w_in_ref, lng_ref, lnb_ref, ws_ref, bst_ref,
                   lam_re_ref, lam_im_ref, bmat_ref, cboth_ref, gmat_ref, dskip_ref, bglu_ref,
                   h0_re_ref, h0_im_ref, gpost_ref, w_out_ref,
                   y_ref, v_ref, hs_re_ref, hs_im_ref, oin_sc):
    n = x_ref.shape[0]
    x = x_ref[...]
    shift = mod_ref[0:n, 0:D_MODEL]
    scale = mod_ref[0:n, D_MODEL:2 * D_MODEL]
    gate = mod_ref[0:n, 2 * D_MODEL:3 * D_MODEL]
    h = (_rms(x, gpre_ref[...]) * (1.0 + scale) + shift).astype(BF16)

    v = _layernorm(_dot(h, w_in_ref[:, D_A:2 * D_A]), lng_ref[...], lnb_ref[...])
    v_ref[...] = v
    u = _dot(h, w_in_ref[:, 0:D_A])
    z = _dot(h, w_in_ref[:, 2 * D_A:3 * D_A])
    for hd in range(N_HEADS_A):
        cols = slice(hd * HEAD_A, (hd + 1) * HEAD_A)
        mix = ws_ref[hd, 0:1, 0:1] * v[:, cols] + bst_ref[0:1, hd:hd + 1]
        oin_sc[:, cols] = (u[:, cols] * mix * _silu(z[:, cols])).astype(BF16)

    xb = _dot(h, w_in_ref[:, 3 * D_A:3 * D_A + D_B])
    zb = _silu(_dot(h, w_in_ref[:, 3 * D_A + D_B:D_IN]))
    xb_bf = xb.astype(BF16)
    for j in range(N_SLABS):
        cols = slice(j * LANES, (j + 1) * LANES)
        st = slice(j * SLAB_STATES, (j + 1) * SLAB_STATES)
        bu = _dot(xb_bf[:, cols], bmat_ref[j])
        lr = lam_re_ref[j:j + 1, :]
        li = lam_im_ref[j:j + 1, :]
        h0r = h0_re_ref[:, st]
        h0i = h0_im_ref[:, st]
        hr = lr * h0r - li * h0i + bu[:, 0:SLAB_STATES]
        hi = lr * h0i + li * h0r + bu[:, SLAB_STATES:2 * SLAB_STATES]
        hs_re_ref[:, st] = hr
        hs_im_ref[:, st] = hi
        k, half = j % HALF_SLABS, j // HALF_SLABS
        c_j = cboth_ref[k, :, half * LANES:(half + 1) * LANES]
        y = _dot(jnp.concatenate([hr, hi], axis=1).astype(BF16), c_j)
        y = y + dskip_ref[:, cols] * xb[:, cols]
        g = _dot(y.astype(BF16), gmat_ref[j])
        val = g[:, 0:LANES] + bglu_ref[0:1, cols]
        gt = g[:, LANES:2 * LANES] + bglu_ref[1:2, cols]
        oin_sc[:, D_A + j * LANES:D_A + (j + 1) * LANES] = (
            val * jax.nn.sigmoid(gt) * zb[:, cols]).astype(BF16)

    o = _dot(oin_sc[...], w_out_ref[...])
    y_ref[...] = x + gate * _rms(o, gpost_ref[...])


def _sample(x, mod, g_pre, w_in_bf, ln_g, ln_b, w_s, bst, lam_re, lam_im, bmat, cboth, gmat,
            d_skip, b_glu_rows, h0_re, h0_im, g_post, w_out_bf):
    n = x.shape[0]
    n_state = N_GROUPS_B * P_STATE
    return pl.pallas_call(
        _sample_kernel,
        out_shape=(
            jax.ShapeDtypeStruct((n, D_MODEL), F32),
            jax.ShapeDtypeStruct((n, D_A), F32),
            jax.ShapeDtypeStruct((n, n_state), F32),
            jax.ShapeDtypeStruct((n, n_state), F32),
        ),
        scratch_shapes=[pltpu.VMEM((n, D_A + D_B), BF16)],
        compiler_params=pltpu.CompilerParams(vmem_limit_bytes=VMEM_LIMIT),
        name="sample",
    )(x, mod, g_pre, w_in_bf, ln_g, ln_b, w_s, bst, lam_re, lam_im, bmat, cboth, gmat,
      d_skip, b_glu_rows, h0_re, h0_im, g_post, w_out_bf)


def kernel(x_prompt, x_sample, c_prompt, c_sample, state_b_re, state_b_im, w_c, b_c, g_pre, w_in,
           ln_v_g, ln_v_b, w_s, b_s, a_re, a_im, log_dt, b_re, b_im, c_re, c_im, d_skip, w_glu, b_glu,
           w_out, g_post):
    n_p = x_prompt.shape[0]
    n_s = x_sample.shape[0]
    assert x_sample.shape[1] == 1 and n_s % SUBLANES == 0

    bmat, cboth, gmat, lam_re, lam_im, b_glu_rows, bst = _s5_prep(
        a_re, a_im, log_dt, b_re, b_im, c_re, c_im, w_glu, b_glu, b_s)

    c_all = jnp.concatenate([c_sample, c_prompt], axis=0)
    pad = (-c_all.shape[0]) % SUBLANES
    mod = _adaln(jnp.pad(c_all, ((0, pad), (0, 0))), w_c, b_c)

    row = lambda v: v.reshape(1, -1)
    w_in_bf = w_in.astype(BF16)
    w_out_bf = w_out.astype(BF16)
    shared = (row(g_pre), w_in_bf, row(ln_v_g), row(ln_v_b), w_s, bst, lam_re, lam_im, bmat, cboth,
              gmat, row(d_skip), b_glu_rows)

    oin, state = _mixer(x_prompt, mod, n_s, *shared)
    y_prompt = _outproj(oin, x_prompt, mod, n_s, row(g_post), w_out_bf)
    st = state.reshape(2, n_p, N_GROUPS_B, P_STATE)

    y_s, v_s, hs_re, hs_im = _sample(
        x_sample.reshape(n_s, D_MODEL), mod, *shared,
        state_b_re.reshape(n_s, -1), state_b_im.reshape(n_s, -1), row(g_post), w_out_bf)

    return (y_prompt, y_s.reshape(n_s, 1, D_MODEL), v_s.reshape(n_s, 1, D_A), st[0], st[1],
            hs_re.reshape(n_s, N_GROUPS_B, P_STATE), hs_im.reshape(n_s, N_GROUPS_B, P_STATE))
```

```python
import functools

import jax
import jax.numpy as jnp
from jax import lax
from jax.experimental import pallas as pl
from jax.experimental.pallas import tpu as pltpu

F32 = jnp.float32
BF16 = jnp.bfloat16

EPS = 1e-6
D_MODEL = 2048
D_A = 1024
D_B = 1024
D_IN = 3 * D_A + 2 * D_B
CHUNK = 128
HEAD_A = 128
N_HEADS_A = D_A // HEAD_A
GROUP_B = 16
N_GROUPS_B = D_B // GROUP_B
P_STATE = 64

LANES = 128
SUBLANES = 8
SLAB_GROUPS = LANES // GROUP_B
N_SLABS = N_GROUPS_B // SLAB_GROUPS
SLAB_STATES = SLAB_GROUPS * P_STATE
SLAB_TILES = SLAB_STATES // LANES
N_STREAMS = SUBLANES
HALF_SLABS = N_SLABS // 2
STREAM_PITCH = CHUNK + SUBLANES
SCAN_PIECES = 4
PIECE = CHUNK // SCAN_PIECES
VMEM_LIMIT = 62 * 1024 * 1024


def _silu(x):
    return x * jax.nn.sigmoid(x)


def _rms(x, g):
    return x * lax.rsqrt(jnp.mean(x * x, axis=-1, keepdims=True) + EPS) * g


def _layernorm(x, g, b):
    mu = jnp.mean(x, axis=-1, keepdims=True)
    xc = x - mu
    return xc * lax.rsqrt(jnp.mean(xc * xc, axis=-1, keepdims=True) + EPS) * g + b


def _dot(a, b):
    return jnp.dot(a, b, preferred_element_type=F32)


def _discretize(a_re, a_im, log_dt):
    dt = jnp.exp(log_dt)
    mag = jnp.exp(dt * a_re)
    abar_re = mag * jnp.cos(dt * a_im)
    abar_im = mag * jnp.sin(dt * a_im)
    return abar_re, abar_im


def _nt_dot(a, b):
    return lax.dot_general(a, b, (((1,), (1,)), ((), ())), preferred_element_type=F32)


def _bf16_terms(x):
    hi = x.astype(BF16)
    r1 = x - hi.astype(F32)
    mid = r1.astype(BF16)
    lo = (r1 - mid.astype(F32)).astype(BF16)
    return hi, mid, lo


def _prep_kernel(a8_ref, bt_re_ref, bt_im_ref, c_re_ref, c_im_ref, wglu_ref, bglu_t_ref, bs_ref,
                 bmat_ref, cboth_ref, gmat_ref, lam_re_ref, lam_im_ref, bglu_row_ref, bst_ref):
    a_re = a8_ref[0:N_SLABS, :]
    a_im = a8_ref[N_SLABS:2 * N_SLABS, :]
    abar_re, abar_im = _discretize(a_re, a_im, a8_ref[2 * N_SLABS:3 * N_SLABS, :])
    lam_re_ref[...] = abar_re
    lam_im_ref[...] = abar_im
    num_re = abar_re - 1.0
    num_im = abar_im
    den = a_re * a_re + a_im * a_im
    coef_re = (num_re * a_re + num_im * a_im) / den
    coef_im = (num_im * a_re - num_re * a_im) / den

    def rep_matrix(k, n, period, offset=0):
        row = lax.broadcasted_iota(jnp.int32, (k, n), 0)
        col = lax.broadcasted_iota(jnp.int32, (k, n), 1)
        return jnp.where((col & (period - 1)) + offset == row, 1.0, 0.0).astype(BF16)

    def rep_rows(m, k, period):
        row = lax.broadcasted_iota(jnp.int32, (m, k), 0)
        col = lax.broadcasted_iota(jnp.int32, (m, k), 1)
        return jnp.where((row & (period - 1)) == col, 1.0, 0.0).astype(BF16)

    def block_mask(m, n, row_shift, col_shift):
        row = lax.broadcasted_iota(jnp.int32, (m, n), 0)
        col = lax.broadcasted_iota(jnp.int32, (m, n), 1)
        return (row >> row_shift) == (col >> col_shift)

    rep_state = rep_matrix(P_STATE, SLAB_STATES, P_STATE)
    rows_state = rep_rows(SLAB_STATES, P_STATE, P_STATE)
    rep_val = rep_matrix(2 * GROUP_B, LANES, GROUP_B)
    rep_gate = rep_matrix(2 * GROUP_B, LANES, GROUP_B, GROUP_B)
    mask_b = block_mask(LANES, SLAB_STATES, 4, 6)
    mask_c = block_mask(SLAB_STATES, LANES, 6, 4)
    mask_g = block_mask(LANES, LANES, 4, 4)

    def spread(x, rep):
        return sum(_dot(term, rep) for term in _bf16_terms(x))

    for j in range(N_SLABS):
        crows = slice(j * LANES, (j + 1) * LANES)
        bt_re = spread(bt_re_ref[crows, :], rep_state)
        bt_im = spread(bt_im_ref[crows, :], rep_state)
        cr = coef_re[j:j + 1, :]
        ci = coef_im[j:j + 1, :]
        bmat_ref[j, :, 0:SLAB_STATES] = jnp.where(mask_b, cr * bt_re - ci * bt_im, 0.0).astype(BF16)
        bmat_ref[j, :, SLAB_STATES:2 * SLAB_STATES] = jnp.where(
            mask_b, cr * bt_im + ci * bt_re, 0.0).astype(BF16)

        k, half = j % HALF_SLABS, j // HALF_SLABS
        ccols = slice(half * LANES, (half + 1) * LANES)
        ct_re = _nt_dot(rows_state, c_re_ref[crows, :].astype(BF16))
        ct_im = _nt_dot(rows_state, c_im_ref[crows, :].astype(BF16))
        cboth_ref[k, 0:SLAB_STATES, ccols] = jnp.where(mask_c, ct_re, 0.0).astype(BF16)
        cboth_ref[k, SLAB_STATES:2 * SLAB_STATES, ccols] = jnp.where(mask_c, -ct_im, 0.0).astype(BF16)

        w = wglu_ref[crows, :].astype(BF16)
        gmat_ref[j, :, 0:LANES] = jnp.where(mask_g, _dot(w, rep_val), 0.0).astype(BF16)
        gmat_ref[j, :, LANES:2 * LANES] = jnp.where(mask_g, _dot(w, rep_gate), 0.0).astype(BF16)

    row = lax.broadcasted_iota(jnp.int32, (2 * GROUP_B, D_B), 0)
    col = lax.broadcasted_iota(jnp.int32, (2 * GROUP_B, D_B), 1)
    by_group = spread(bglu_t_ref[...], jnp.where(block_mask(N_GROUPS_B, D_B, 0, 4), 1.0, 0.0).astype(BF16))
    for i in range(2):
        mine = row == (col & (GROUP_B - 1)) + i * GROUP_B
        bglu_row_ref[i:i + 1, :] = jnp.sum(jnp.where(mine, by_group, 0.0), axis=0, keepdims=True)

    eye = rep_rows(CHUNK, CHUNK, CHUNK)
    bst_ref[...] = sum(_nt_dot(eye, term) for term in _bf16_terms(bs_ref[...]))


def _s5_prep(a_re, a_im, log_dt, b_re, b_im, c_re, c_im, w_glu, b_glu, b_s):
    g, p, c = N_GROUPS_B, P_STATE, GROUP_B
    a8 = jnp.stack([a_re, a_im, jnp.broadcast_to(log_dt[:, None], (g, p))]).reshape(3 * N_SLABS, SLAB_STATES)
    args = (a8, b_re.transpose(0, 2, 1).reshape(g * c, p), b_im.transpose(0, 2, 1).reshape(g * c, p),
            c_re.reshape(g * c, p), c_im.reshape(g * c, p), w_glu.reshape(g * c, 2 * c), b_glu.T, b_s)
    return pl.pallas_call(
        _prep_kernel,
        out_shape=(
            jax.ShapeDtypeStruct((N_SLABS, LANES, 2 * SLAB_STATES), BF16),
            jax.ShapeDtypeStruct((HALF_SLABS, 2 * SLAB_STATES, 2 * LANES), BF16),
            jax.ShapeDtypeStruct((N_SLABS, LANES, 2 * LANES), BF16),
            jax.ShapeDtypeStruct((N_SLABS, SLAB_STATES), F32),
            jax.ShapeDtypeStruct((N_SLABS, SLAB_STATES), F32),
            jax.ShapeDtypeStruct((2, D_B), F32),
            jax.ShapeDtypeStruct((CHUNK, N_HEADS_A), F32),
        ),
        name="s5_prep",
    )(*args)


ADALN_TILE = 1024


def _adaln_kernel(c_ref, w_ref, b_ref, o_ref):
    s = _silu(c_ref[...]).astype(BF16)
    o_ref[...] = _dot(s, w_ref[...].astype(BF16)) + b_ref[...]


def _adaln(c_all, w_c, b_c):
    n = c_all.shape[0]
    return pl.pallas_call(
        _adaln_kernel,
        grid=(3 * D_MODEL // ADALN_TILE,),
        in_specs=[
            pl.BlockSpec((n, D_MODEL), lambda j: (0, 0)),
            pl.BlockSpec((D_MODEL, ADALN_TILE), lambda j: (0, j)),
            pl.BlockSpec((1, ADALN_TILE), lambda j: (0, j)),
        ],
        out_specs=pl.BlockSpec((n, ADALN_TILE), lambda j: (0, j)),
        out_shape=jax.ShapeDtypeStruct((n, 3 * D_MODEL), F32),
        compiler_params=pltpu.CompilerParams(vmem_limit_bytes=VMEM_LIMIT),
        name="adaln",
    )(c_all, w_c, b_c.reshape(1, -1))


def _mixer_kernel(x_ref, mod_ref, gpre_ref, w_in_ref, lng_ref, lnb_ref, ws_ref, bst_ref,
                  lam_re_ref, lam_im_ref, bmat_ref, cboth_ref, gmat_ref, dskip_ref, bglu_ref,
                  oin_ref, state_ref,
                  h_sc, a_sc, b_sc, c_sc, v_sc, xb_sc, bu_sc, hh_sc, y2_sc, st_sc, *, n_seq, mod_row0):
    step = pl.program_id(0)

    @pl.when(step == 0)
    def _():
        st_sc[...] = jnp.zeros_like(st_sc)

    w_u, w_v, w_z, w_xb, w_zb = 0, D_A, 2 * D_A, 3 * D_A, 3 * D_A + D_B

    for b in range(n_seq):
        r = _rms(x_ref[b], gpre_ref[...])
        m = mod_row0 + b
        shift = mod_ref[m:m + 1, 0:D_MODEL]
        scale = mod_ref[m:m + 1, D_MODEL:2 * D_MODEL]
        rws = slice(b * CHUNK, (b + 1) * CHUNK)
        h_sc[rws, :] = (r * (1.0 + scale) + shift).astype(BF16)
        b_sc[rws, :] = _dot(h_sc[rws, :], w_in_ref[:, w_xb:w_xb + D_B])
        xb_sc[rws, :] = b_sc[rws, :].astype(BF16)

    h = h_sc[...]
    c_sc[...] = _dot(h, w_in_ref[:, w_v:w_v + D_A])
    v_sc[...] = _layernorm(c_sc[...], lng_ref[...], lnb_ref[...]).astype(BF16)

    tril = (lax.broadcasted_iota(jnp.int32, (CHUNK, CHUNK), 0)
            >= lax.broadcasted_iota(jnp.int32, (CHUNK, CHUNK), 1))
    n_fill = HALF_SLABS * SCAN_PIECES // 4
    fill_cols = D_A // n_fill

    def fill_u(q):
        cols = slice(q * fill_cols, (q + 1) * fill_cols)
        a_sc[:, cols] = _dot(h, w_in_ref[:, w_u + q * fill_cols:w_u + (q + 1) * fill_cols])

    def fill_z(q):
        cols = slice(q * fill_cols, (q + 1) * fill_cols)
        z = _dot(h, w_in_ref[:, w_z + q * fill_cols:w_z + (q + 1) * fill_cols])
        a_sc[:, cols] = a_sc[:, cols] * _silu(z)

    def fill_mix(q):
        heads_per_fill = N_HEADS_A // n_fill
        for hd in range(q * heads_per_fill, (q + 1) * heads_per_fill):
            cols = slice(hd * HEAD_A, (hd + 1) * HEAD_A)
            w_t = jnp.where(tril, ws_ref[hd], 0.0).astype(BF16)
            v_h = jnp.concatenate([v_sc[b * CHUNK:(b + 1) * CHUNK, cols] for b in range(n_seq)], axis=1)
            mix = _dot(w_t, v_h) + bst_ref[:, hd:hd + 1]
            for b in range(n_seq):
                rws = slice(b * CHUNK, (b + 1) * CHUNK)
                oin_ref[rws, cols] = (a_sc[rws, cols] * mix[:, b * HEAD_A:(b + 1) * HEAD_A]).astype(BF16)

    def fill_zb(q):
        cols = slice(q * fill_cols, (q + 1) * fill_cols)
        a_sc[:, cols] = _silu(_dot(h, w_in_ref[:, w_zb + q * fill_cols:w_zb + (q + 1) * fill_cols]))

    fillers = [functools.partial(f, q) for f in (fill_u, fill_z, fill_mix, fill_zb) for q in range(n_fill)]

    first_half = lax.broadcasted_iota(jnp.int32, (N_STREAMS, LANES), 0) < n_seq

    def bu_piece(k, piece):
        t0 = piece * PIECE
        for half in range(2):
            j = half * HALF_SLABS + k
            lhs = jnp.concatenate(
                [xb_sc[b * CHUNK + t0:b * CHUNK + t0 + PIECE, j * LANES:(j + 1) * LANES]
                 for b in range(n_seq)], axis=0)
            bu = _dot(lhs, bmat_ref[j])
            for b in range(n_seq):
                r0 = (half * n_seq + b) * STREAM_PITCH + t0
                for l in range(2 * SLAB_TILES):
                    bu_sc[l, r0:r0 + PIECE, :] = bu[b * PIECE:(b + 1) * PIECE, l * LANES:(l + 1) * LANES]

    def scan_piece(k, piece, hr, hi, lam_r, lam_i):
        for t in range(piece * PIECE, (piece + 1) * PIECE):
            new_r, new_i = [], []
            for l in range(SLAB_TILES):
                br = bu_sc[l, pl.ds(t, N_STREAMS, stride=STREAM_PITCH), :]
                bi = bu_sc[SLAB_TILES + l, pl.ds(t, N_STREAMS, stride=STREAM_PITCH), :]
                new_r.append(lam_r[l] * hr[l] - lam_i[l] * hi[l] + br)
                new_i.append(lam_r[l] * hi[l] + lam_i[l] * hr[l] + bi)
            hr, hi = new_r, new_i
            hh_sc[t * N_STREAMS:(t + 1) * N_STREAMS, :] = jnp.concatenate(hr + hi, axis=1)
        return hr, hi

    def c_piece(k, piece):
        t0 = piece * PIECE
        rows = slice(t0 * N_STREAMS, (t0 + PIECE) * N_STREAMS)
        y2 = _dot(hh_sc[rows, :].astype(BF16), cboth_ref[k])
        y2_sc[0, rows, :] = y2[:, 0:LANES]
        y2_sc[1, rows, :] = y2[:, LANES:2 * LANES]
        for half in range(2):
            j = half * HALF_SLABS + k
            for b in range(n_seq):
                s = half * n_seq + b
                c_sc[b * CHUNK + t0:b * CHUNK + t0 + PIECE, j * LANES:(j + 1) * LANES] = (
                    y2_sc[half, pl.ds(t0 * N_STREAMS + s, PIECE, stride=N_STREAMS), :])

    for piece in range(SCAN_PIECES):
        bu_piece(0, piece)
    for k in range(HALF_SLABS):
        tiles = [slice(l * LANES, (l + 1) * LANES) for l in range(SLAB_TILES)]
        lam_r = [jnp.where(first_half, lam_re_ref[k:k + 1, t], lam_re_ref[HALF_SLABS + k:HALF_SLABS + k + 1, t])
                 for t in tiles]
        lam_i = [jnp.where(first_half, lam_im_ref[k:k + 1, t], lam_im_ref[HALF_SLABS + k:HALF_SLABS + k + 1, t])
                 for t in tiles]
        hr = [st_sc[k, :, l * LANES:(l + 1) * LANES] for l in range(SLAB_TILES)]
        hi = [st_sc[k, :, (SLAB_TILES + l) * LANES:(SLAB_TILES + l + 1) * LANES] for l in range(SLAB_TILES)]
        for piece in range(SCAN_PIECES):
            hr, hi = scan_piece(k, piece, hr, hi, lam_r, lam_i)
            c_piece(k, piece)
            if k + 1 < HALF_SLABS:
                bu_piece(k + 1, piece)
            fillers[k * SCAN_PIECES + piece]()
        st_sc[k] = jnp.concatenate(hr + hi, axis=1)

    for k in range(HALF_SLABS):
        for half in range(2):
            j = half * HALF_SLABS + k
            for b in range(n_seq):
                s = half * n_seq + b
                for ri in range(2):
                    state_ref[ri, b, j:j + 1, :] = st_sc[k, s:s + 1, ri * SLAB_STATES:(ri + 1) * SLAB_STATES]

    for j in range(N_SLABS):
        cols = slice(j * LANES, (j + 1) * LANES)
        y = c_sc[:, cols] + dskip_ref[:, cols] * b_sc[:, cols]
        g = _dot(y.astype(BF16), gmat_ref[j])
        val = g[:, 0:LANES] + bglu_ref[0:1, cols]
        gate = g[:, LANES:2 * LANES] + bglu_ref[1:2, cols]
        oin_ref[:, D_A + j * LANES:D_A + (j + 1) * LANES] = (
            val * jax.nn.sigmoid(gate) * a_sc[:, cols]).astype(BF16)


def _const_spec(shape):
    zeros = (0,) * len(shape)
    return pl.BlockSpec(shape, lambda i: zeros, pipeline_mode=pl.Buffered(1))


def _mixer(x, mod, mod_row0, g_pre, w_in_bf, ln_g, ln_b, w_s, bst, lam_re, lam_im, bmat, cboth, gmat,
           d_skip, b_glu_rows):
    n_seq, seq, _ = x.shape
    assert 2 * n_seq == N_STREAMS and seq % CHUNK == 0
    rows = n_seq * CHUNK
    n_steps = seq // CHUNK
    consts = (mod, g_pre, w_in_bf, ln_g, ln_b, w_s, bst, lam_re, lam_im, bmat, cboth, gmat,
              d_skip, b_glu_rows)
    return pl.pallas_call(
        functools.partial(_mixer_kernel, n_seq=n_seq, mod_row0=mod_row0),
        grid=(n_steps,),
        in_specs=[pl.BlockSpec((n_seq, CHUNK, D_MODEL), lambda i: (0, i, 0))]
        + [_const_spec(c.shape) for c in consts],
        out_specs=(
            pl.BlockSpec((rows, D_A + D_B), lambda i: (i, 0)),
            pl.BlockSpec((2, n_seq, N_SLABS, SLAB_STATES), lambda i: (0, 0, 0, 0)),
        ),
        out_shape=(
            jax.ShapeDtypeStruct((n_steps * rows, D_A + D_B), BF16),
            jax.ShapeDtypeStruct((2, n_seq, N_SLABS, SLAB_STATES), F32),
        ),
        scratch_shapes=[
            pltpu.VMEM((rows, D_MODEL), BF16),
            pltpu.VMEM((rows, D_A), F32),
            pltpu.VMEM((rows, D_A), F32),
            pltpu.VMEM((rows, D_A), F32),
            pltpu.VMEM((rows, D_A), BF16),
            pltpu.VMEM((rows, D_B), BF16),
            pltpu.VMEM((2 * SLAB_TILES, N_STREAMS * STREAM_PITCH, LANES), F32),
            pltpu.VMEM((N_STREAMS * CHUNK, 2 * SLAB_STATES), F32),
            pltpu.VMEM((2, N_STREAMS * CHUNK, LANES), F32),
            pltpu.VMEM((HALF_SLABS, N_STREAMS, 2 * SLAB_STATES), F32),
        ],
        compiler_params=pltpu.CompilerParams(
            dimension_semantics=("arbitrary",), vmem_limit_bytes=VMEM_LIMIT),
        name="mixer",
    )(x, *consts)


def _outproj_kernel(oin_ref, x_ref, mod_ref, gpost_ref, w_out_ref, y_ref, *, n_seq, mod_row0):
    r = _rms(_dot(oin_ref[...], w_out_ref[...]), gpost_ref[...])
    for b in range(n_seq):
        m = mod_row0 + b
        gate = mod_ref[m:m + 1, 2 * D_MODEL:3 * D_MODEL]
        y_ref[b] = x_ref[b] + gate * r[b * CHUNK:(b + 1) * CHUNK, :]


def _outproj(oin, x, mod, mod_row0, g_post, w_out_bf):
    n_seq, seq, _ = x.shape
    rows = n_seq * CHUNK
    return pl.pallas_call(
        functools.partial(_outproj_kernel, n_seq=n_seq, mod_row0=mod_row0),
        grid=(seq // CHUNK,),
        in_specs=[
            pl.BlockSpec((rows, D_A + D_B), lambda i: (i, 0)),
            pl.BlockSpec((n_seq, CHUNK, D_MODEL), lambda i: (0, i, 0)),
            _const_spec(mod.shape), _const_spec(g_post.shape), _const_spec(w_out_bf.shape),
        ],
        out_specs=pl.BlockSpec((n_seq, CHUNK, D_MODEL), lambda i: (0, i, 0)),
        out_shape=jax.ShapeDtypeStruct(x.shape, F32),
        compiler_params=pltpu.CompilerParams(
            dimension_semantics=("arbitrary",), vmem_limit_bytes=VMEM_LIMIT),
        name="outproj",
    )(oin, x, mod, g_post, w_out_bf)


SAMPLE_SLAB = 1024
N_IN_SLABS = D_IN // SAMPLE_SLAB
N_OUT_SLABS = D_MODEL // SAMPLE_SLAB


def _sample_kernel(x2_ref, mod_ref, gpre_ref, w_in_ref, lng_ref, lnb_ref, ws_ref, bst_ref,
                   lam_re_ref, lam_im_ref, bmat_ref, cboth_ref, gmat_ref, dskip_ref, bglu_ref,
                   h0_re_ref, h0_im_ref, gpost_ref, w_out_ref,
                   y2_ref, v2_ref, hs_re_ref, hs_im_ref, x_sc, h_sc, p_sc, oin_sc, o_sc):
    c = pl.program_id(0)
    n = x_sc.shape[0]
    x_tiles = D_MODEL // LANES
    v_tiles = D_A // LANES

    @pl.when(c == 0)
    def _():
        for q in range(x_tiles):
            x_sc[:, q * LANES:(q + 1) * LANES] = x2_ref[pl.ds(q, n, stride=x_tiles), :]
        shift = mod_ref[0:n, 0:D_MODEL]
        scale = mod_ref[0:n, D_MODEL:2 * D_MODEL]
        h_sc[...] = (_rms(x_sc[...], gpre_ref[...]) * (1.0 + scale) + shift).astype(BF16)

    @pl.when(c < N_IN_SLABS)
    def _():
        p_sc[c] = _dot(h_sc[...], w_in_ref[...])

    @pl.when(c == N_IN_SLABS - 1)
    def _():
        v = _layernorm(p_sc[1], lng_ref[...], lnb_ref[...])
        for q in range(v_tiles):
            v2_ref[pl.ds(q, n, stride=v_tiles), :] = v[:, q * LANES:(q + 1) * LANES]
        for hd in range(N_HEADS_A):
            cols = slice(hd * HEAD_A, (hd + 1) * HEAD_A)
            mix = ws_ref[hd, 0:1, 0:1] * v[:, cols] + bst_ref[0:1, hd:hd + 1]
            oin_sc[:, cols] = (p_sc[0, :, cols] * mix * _silu(p_sc[2, :, cols])).astype(BF16)

        for j in range(N_SLABS):
            cols = slice(j * LANES, (j + 1) * LANES)
            st = slice(j * SLAB_STATES, (j + 1) * SLAB_STATES)
            xb = p_sc[3, :, cols]
            bu = _dot(xb.astype(BF16), bmat_ref[j])
            lr = lam_re_ref[j:j + 1, :]
            li = lam_im_ref[j:j + 1, :]
            h0r = h0_re_ref[:, st]
            h0i = h0_im_ref[:, st]
            hr = lr * h0r - li * h0i + bu[:, 0:SLAB_STATES]
            hi = lr * h0i + li * h0r + bu[:, SLAB_STATES:2 * SLAB_STATES]
            hs_re_ref[:, st] = hr
            hs_im_ref[:, st] = hi
            k, half = j % HALF_SLABS, j // HALF_SLABS
            c_j = cboth_ref[k, :, half * LANES:(half + 1) * LANES]
            y = _dot(jnp.concatenate([hr, hi], axis=1).astype(BF16), c_j)
            y = y + dskip_ref[:, cols] * xb
            g = _dot(y.astype(BF16), gmat_ref[j])
            val = g[:, 0:LANES] + bglu_ref[0:1, cols]
            gt = g[:, LANES:2 * LANES] + bglu_ref[1:2, cols]
            oin_sc[:, D_A + j * LANES:D_A + (j + 1) * LANES] = (
                val * jax.nn.sigmoid(gt) * _silu(p_sc[4, :, cols])).astype(BF16)

    @pl.when(c >= N_IN_SLABS)
    def _():
        o_sc[c - N_IN_SLABS] = _dot(oin_sc[...], w_out_ref[...])

    @pl.when(c == N_IN_SLABS + N_OUT_SLABS - 1)
    def _():
        o = jnp.concatenate([o_sc[i] for i in range(N_OUT_SLABS)], axis=1)
        gate = mod_ref[0:n, 2 * D_MODEL:3 * D_MODEL]
        y = x_sc[...] + gate * _rms(o, gpost_ref[...])
        for q in range(x_tiles):
            y2_ref[pl.ds(q, n, stride=x_tiles), :] = y[:, q * LANES:(q + 1) * LANES]


def _sample(x2, mod, g_pre, w_in_bf, ln_g, ln_b, w_s, bst, lam_re, lam_im, bmat, cboth, gmat,
            d_skip, b_glu_rows, h0_re, h0_im, g_post, w_out_bf):
    n = h0_re.shape[0]
    n_state = N_GROUPS_B * P_STATE
    in_slab = pl.BlockSpec((D_MODEL, SAMPLE_SLAB), lambda c: (0, jnp.minimum(c, N_IN_SLABS - 1)))
    out_slab = pl.BlockSpec((D_A + D_B, SAMPLE_SLAB), lambda c: (0, jnp.maximum(c - N_IN_SLABS, 0)))
    operands = (x2, mod, g_pre, w_in_bf, ln_g, ln_b, w_s, bst, lam_re, lam_im, bmat, cboth, gmat,
                d_skip, b_glu_rows, h0_re, h0_im, g_post, w_out_bf)
    in_specs = [in_slab if a is w_in_bf else out_slab if a is w_out_bf else _const_spec(a.shape)
                for a in operands]
    out_shapes = ((n * D_MODEL // LANES, LANES), (n * D_A // LANES, LANES), (n, n_state), (n, n_state))
    return pl.pallas_call(
        _sample_kernel,
        grid=(N_IN_SLABS + N_OUT_SLABS,),
        in_specs=in_specs,
        out_specs=tuple(pl.BlockSpec(shp, lambda c: (0, 0)) for shp in out_shapes),
        out_shape=tuple(jax.ShapeDtypeStruct(shp, F32) for shp in out_shapes),
        scratch_shapes=[
            pltpu.VMEM((n, D_MODEL), F32),
            pltpu.VMEM((n, D_MODEL), BF16),
            pltpu.VMEM((N_IN_SLABS, n, SAMPLE_SLAB), F32),
            pltpu.VMEM((n, D_A + D_B), BF16),
            pltpu.VMEM((N_OUT_SLABS, n, SAMPLE_SLAB), F32),
        ],
        compiler_params=pltpu.CompilerParams(
            dimension_semantics=("arbitrary",), vmem_limit_bytes=VMEM_LIMIT),
        name="sample",
    )(*operands)


def kernel(x_prompt, x_sample, c_prompt, c_sample, state_b_re, state_b_im, w_c, b_c, g_pre, w_in,
           ln_v_g, ln_v_b, w_s, b_s, a_re, a_im, log_dt, b_re, b_im, c_re, c_im, d_skip, w_glu, b_glu,
           w_out, g_post):
    n_p = x_prompt.shape[0]
    n_s = x_sample.shape[0]
    assert x_sample.shape[1] == 1 and n_s % SUBLANES == 0

    bmat, cboth, gmat, lam_re, lam_im, b_glu_rows, bst = _s5_prep(
        a_re, a_im, log_dt, b_re, b_im, c_re, c_im, w_glu, b_glu, b_s)

    c_all = jnp.concatenate([c_sample, c_prompt], axis=0)
    pad = (-c_all.shape[0]) % SUBLANES
    mod = _adaln(jnp.pad(c_all, ((0, pad), (0, 0))), w_c, b_c)

    row = lambda v: v.reshape(1, -1)
    w_in_bf = w_in.astype(BF16)
    w_out_bf = w_out.astype(BF16)
    shared = (row(g_pre), w_in_bf, row(ln_v_g), row(ln_v_b), w_s, bst, lam_re, lam_im, bmat, cboth,
              gmat, row(d_skip), b_glu_rows)

    oin, state = _mixer(x_prompt, mod, n_s, *shared)
    y_prompt = _outproj(oin, x_prompt, mod, n_s, row(g_post), w_out_bf)
    st = state.reshape(2, n_p, N_GROUPS_B, P_STATE)

    y_s, v_s, hs_re, hs_im = _sample(
        x_sample.reshape(n_s * D_MODEL // LANES, LANES), mod, *shared,
        state_b_re.reshape(n_s, -1), state_b_im.reshape(n_s, -1), row(g_post), w_out_bf)

    return (y_prompt, y_s.reshape(n_s, 1, D_MODEL), v_s.reshape(n_s, 1, D_A), st[0], st[1],
            hs_re.reshape(n_s, N_GROUPS_B, P_STATE), hs_im.reshape(n_s, N_GROUPS_B, P_STATE))
```

```python
import functools

import jax
import jax.numpy as jnp
from jax import lax
from jax.experimental import pallas as pl
from jax.experimental.pallas import tpu as pltpu

F32 = jnp.float32
BF16 = jnp.bfloat16

EPS = 1e-6
D_MODEL = 2048
D_A = 1024
D_B = 1024
D_IN = 3 * D_A + 2 * D_B
CHUNK = 128
HEAD_A = 128
N_HEADS_A = D_A // HEAD_A
GROUP_B = 16
N_GROUPS_B = D_B // GROUP_B
P_STATE = 64

LANES = 128
SUBLANES = 8
SLAB_GROUPS = LANES // GROUP_B
N_SLABS = N_GROUPS_B // SLAB_GROUPS
SLAB_STATES = SLAB_GROUPS * P_STATE
SLAB_TILES = SLAB_STATES // LANES
N_STREAMS = SUBLANES
HALF_SLABS = N_SLABS // 2
STREAM_PITCH = CHUNK + SUBLANES
SCAN_PIECES = 4
PIECE = CHUNK // SCAN_PIECES
VMEM_LIMIT = 62 * 1024 * 1024


def _silu(x):
    return x * jax.nn.sigmoid(x)


def _rms(x, g):
    return x * lax.rsqrt(jnp.mean(x * x, axis=-1, keepdims=True) + EPS) * g


def _layernorm(x, g, b):
    mu = jnp.mean(x, axis=-1, keepdims=True)
    xc = x - mu
    return xc * lax.rsqrt(jnp.mean(xc * xc, axis=-1, keepdims=True) + EPS) * g + b


def _dot(a, b):
    return jnp.dot(a, b, preferred_element_type=F32)


def _discretize(a_re, a_im, log_dt):
    dt = jnp.exp(log_dt)
    mag = jnp.exp(dt * a_re)
    abar_re = mag * jnp.cos(dt * a_im)
    abar_im = mag * jnp.sin(dt * a_im)
    return abar_re, abar_im


def _nt_dot(a, b):
    return lax.dot_general(a, b, (((1,), (1,)), ((), ())), preferred_element_type=F32)


def _bf16_terms(x):
    hi = x.astype(BF16)
    r1 = x - hi.astype(F32)
    mid = r1.astype(BF16)
    lo = (r1 - mid.astype(F32)).astype(BF16)
    return hi, mid, lo


def _prep_kernel(a8_ref, bt_re_ref, bt_im_ref, c_re_ref, c_im_ref, wglu_ref, bglu_t_ref, bs_ref,
                 bmat_ref, cboth_ref, gmat_ref, lam_re_ref, lam_im_ref, bglu_row_ref, bst_ref):
    a_re = a8_ref[0:N_SLABS, :]
    a_im = a8_ref[N_SLABS:2 * N_SLABS, :]
    abar_re, abar_im = _discretize(a_re, a_im, a8_ref[2 * N_SLABS:3 * N_SLABS, :])
    lam_re_ref[...] = abar_re
    lam_im_ref[...] = abar_im
    num_re = abar_re - 1.0
    num_im = abar_im
    den = a_re * a_re + a_im * a_im
    coef_re = (num_re * a_re + num_im * a_im) / den
    coef_im = (num_im * a_re - num_re * a_im) / den

    def rep_matrix(k, n, period, offset=0):
        row = lax.broadcasted_iota(jnp.int32, (k, n), 0)
        col = lax.broadcasted_iota(jnp.int32, (k, n), 1)
        return jnp.where((col & (period - 1)) + offset == row, 1.0, 0.0).astype(BF16)

    def rep_rows(m, k, period):
        row = lax.broadcasted_iota(jnp.int32, (m, k), 0)
        col = lax.broadcasted_iota(jnp.int32, (m, k), 1)
        return jnp.where((row & (period - 1)) == col, 1.0, 0.0).astype(BF16)

    def block_mask(m, n, row_shift, col_shift):
        row = lax.broadcasted_iota(jnp.int32, (m, n), 0)
        col = lax.broadcasted_iota(jnp.int32, (m, n), 1)
        return (row >> row_shift) == (col >> col_shift)

    rep_state = rep_matrix(P_STATE, SLAB_STATES, P_STATE)
    rows_state = rep_rows(SLAB_STATES, P_STATE, P_STATE)
    rep_val = rep_matrix(2 * GROUP_B, LANES, GROUP_B)
    rep_gate = rep_matrix(2 * GROUP_B, LANES, GROUP_B, GROUP_B)
    mask_b = block_mask(LANES, SLAB_STATES, 4, 6)
    mask_c = block_mask(SLAB_STATES, LANES, 6, 4)
    mask_g = block_mask(LANES, LANES, 4, 4)

    def spread(x, rep):
        return sum(_dot(term, rep) for term in _bf16_terms(x))

    for j in range(N_SLABS):
        crows = slice(j * LANES, (j + 1) * LANES)
        bt_re = spread(bt_re_ref[crows, :], rep_state)
        bt_im = spread(bt_im_ref[crows, :], rep_state)
        cr = coef_re[j:j + 1, :]
        ci = coef_im[j:j + 1, :]
        bmat_ref[j, :, 0:SLAB_STATES] = jnp.where(mask_b, cr * bt_re - ci * bt_im, 0.0).astype(BF16)
        bmat_ref[j, :, SLAB_STATES:2 * SLAB_STATES] = jnp.where(
            mask_b, cr * bt_im + ci * bt_re, 0.0).astype(BF16)

        k, half = j % HALF_SLABS, j // HALF_SLABS
        ccols = slice(half * LANES, (half + 1) * LANES)
        ct_re = _nt_dot(rows_state, c_re_ref[crows, :].astype(BF16))
        ct_im = _nt_dot(rows_state, c_im_ref[crows, :].astype(BF16))
        cboth_ref[k, 0:SLAB_STATES, ccols] = jnp.where(mask_c, ct_re, 0.0).astype(BF16)
        cboth_ref[k, SLAB_STATES:2 * SLAB_STATES, ccols] = jnp.where(mask_c, -ct_im, 0.0).astype(BF16)

        w = wglu_ref[crows, :].astype(BF16)
        gmat_ref[j, :, 0:LANES] = jnp.where(mask_g, _dot(w, rep_val), 0.0).astype(BF16)
        gmat_ref[j, :, LANES:2 * LANES] = jnp.where(mask_g, _dot(w, rep_gate), 0.0).astype(BF16)

    row = lax.broadcasted_iota(jnp.int32, (2 * GROUP_B, D_B), 0)
    col = lax.broadcasted_iota(jnp.int32, (2 * GROUP_B, D_B), 1)
    by_group = spread(bglu_t_ref[...], jnp.where(block_mask(N_GROUPS_B, D_B, 0, 4), 1.0, 0.0).astype(BF16))
    for i in range(2):
        mine = row == (col & (GROUP_B - 1)) + i * GROUP_B
        bglu_row_ref[i:i + 1, :] = jnp.sum(jnp.where(mine, by_group, 0.0), axis=0, keepdims=True)

    eye = rep_rows(CHUNK, CHUNK, CHUNK)
    bst_ref[...] = sum(_nt_dot(eye, term) for term in _bf16_terms(bs_ref[...]))


def _s5_prep(a_re, a_im, log_dt, b_re, b_im, c_re, c_im, w_glu, b_glu, b_s):
    g, p, c = N_GROUPS_B, P_STATE, GROUP_B
    a8 = jnp.stack([a_re, a_im, jnp.broadcast_to(log_dt[:, None], (g, p))]).reshape(3 * N_SLABS, SLAB_STATES)
    args = (a8, b_re.transpose(0, 2, 1).reshape(g * c, p), b_im.transpose(0, 2, 1).reshape(g * c, p),
            c_re.reshape(g * c, p), c_im.reshape(g * c, p), w_glu.reshape(g * c, 2 * c), b_glu.T, b_s)
    return pl.pallas_call(
        _prep_kernel,
        out_shape=(
            jax.ShapeDtypeStruct((N_SLABS, LANES, 2 * SLAB_STATES), BF16),
            jax.ShapeDtypeStruct((HALF_SLABS, 2 * SLAB_STATES, 2 * LANES), BF16),
            jax.ShapeDtypeStruct((N_SLABS, LANES, 2 * LANES), BF16),
            jax.ShapeDtypeStruct((N_SLABS, SLAB_STATES), F32),
            jax.ShapeDtypeStruct((N_SLABS, SLAB_STATES), F32),
            jax.ShapeDtypeStruct((2, D_B), F32),
            jax.ShapeDtypeStruct((CHUNK, N_HEADS_A), F32),
        ),
        name="s5_prep",
    )(*args)


ADALN_TILE = 1024


ADALN_PROMPT_ROWS = 2 * SUBLANES


def _adaln_kernel(cs_ref, cp_ref, w_ref, b_ref, os_ref, op_ref):
    w = w_ref[...].astype(BF16)
    os_ref[...] = _dot(_silu(cs_ref[...]).astype(BF16), w) + b_ref[...]
    n_p = cp_ref.shape[0]
    cp = jnp.concatenate([cp_ref[...], jnp.zeros((ADALN_PROMPT_ROWS - n_p, D_MODEL), F32)], axis=0)
    op_ref[...] = _dot(_silu(cp).astype(BF16), w) + b_ref[...]


def _adaln(c_sample, c_prompt, w_c, b_c):
    n_s, n_p = c_sample.shape[0], c_prompt.shape[0]
    assert n_p <= ADALN_PROMPT_ROWS
    return pl.pallas_call(
        _adaln_kernel,
        grid=(3 * D_MODEL // ADALN_TILE,),
        in_specs=[
            pl.BlockSpec((n_s, D_MODEL), lambda j: (0, 0)),
            pl.BlockSpec((n_p, D_MODEL), lambda j: (0, 0)),
            pl.BlockSpec((D_MODEL, ADALN_TILE), lambda j: (0, j)),
            pl.BlockSpec((1, ADALN_TILE), lambda j: (0, j)),
        ],
        out_specs=(pl.BlockSpec((n_s, ADALN_TILE), lambda j: (0, j)),
                   pl.BlockSpec((ADALN_PROMPT_ROWS, ADALN_TILE), lambda j: (0, j))),
        out_shape=(jax.ShapeDtypeStruct((n_s, 3 * D_MODEL), F32),
                   jax.ShapeDtypeStruct((ADALN_PROMPT_ROWS, 3 * D_MODEL), F32)),
        compiler_params=pltpu.CompilerParams(vmem_limit_bytes=VMEM_LIMIT),
        name="adaln",
    )(c_sample, c_prompt, w_c, b_c.reshape(1, -1))


def _mixer_kernel(x_ref, mod_ref, gpre_ref, w_in_ref, lng_ref, lnb_ref, ws_ref, bst_ref,
                  lam_re_ref, lam_im_ref, bmat_ref, cboth_ref, gmat_ref, dskip_ref, bglu_ref, wout_ref,
                  oin_ref, st_re_ref, st_im_ref, wout_bf_ref,
                  h_sc, a_sc, b_sc, c_sc, v_sc, xb_sc, bu_sc, y2_sc, st_sc, *, n_seq):
    step = pl.program_id(0)

    @pl.when(step == 0)
    def _():
        st_sc[...] = jnp.zeros_like(st_sc)

    wout_bf_ref[...] = wout_ref[...].astype(BF16)

    w_u, w_v, w_z, w_xb, w_zb = 0, D_A, 2 * D_A, 3 * D_A, 3 * D_A + D_B

    for b in range(n_seq):
        r = _rms(x_ref[b], gpre_ref[...])
        shift = mod_ref[b:b + 1, 0:D_MODEL]
        scale = mod_ref[b:b + 1, D_MODEL:2 * D_MODEL]
        rws = slice(b * CHUNK, (b + 1) * CHUNK)
        h_sc[rws, :] = (r * (1.0 + scale) + shift).astype(BF16)
        b_sc[rws, :] = _dot(h_sc[rws, :], w_in_ref[:, w_xb:w_xb + D_B])
        xb_sc[rws, :] = b_sc[rws, :].astype(BF16)

    h = h_sc[...]
    c_sc[...] = _dot(h, w_in_ref[:, w_v:w_v + D_A])
    v_sc[...] = _layernorm(c_sc[...], lng_ref[...], lnb_ref[...]).astype(BF16)

    tril = (lax.broadcasted_iota(jnp.int32, (CHUNK, CHUNK), 0)
            >= lax.broadcasted_iota(jnp.int32, (CHUNK, CHUNK), 1))
    n_fill = HALF_SLABS * SCAN_PIECES // 4
    fill_cols = D_A // n_fill

    def fill_u(q):
        cols = slice(q * fill_cols, (q + 1) * fill_cols)
        a_sc[:, cols] = _dot(h, w_in_ref[:, w_u + q * fill_cols:w_u + (q + 1) * fill_cols])

    def fill_z(q):
        cols = slice(q * fill_cols, (q + 1) * fill_cols)
        z = _dot(h, w_in_ref[:, w_z + q * fill_cols:w_z + (q + 1) * fill_cols])
        a_sc[:, cols] = a_sc[:, cols] * _silu(z)

    def fill_mix(q):
        heads_per_fill = N_HEADS_A // n_fill
        for hd in range(q * heads_per_fill, (q + 1) * heads_per_fill):
            cols = slice(hd * HEAD_A, (hd + 1) * HEAD_A)
            w_t = jnp.where(tril, ws_ref[hd], 0.0).astype(BF16)
            v_h = jnp.concatenate([v_sc[b * CHUNK:(b + 1) * CHUNK, cols] for b in range(n_seq)], axis=1)
            mix = _dot(w_t, v_h) + bst_ref[:, hd:hd + 1]
            for b in range(n_seq):
                rws = slice(b * CHUNK, (b + 1) * CHUNK)
                oin_ref[rws, cols] = (a_sc[rws, cols] * mix[:, b * HEAD_A:(b + 1) * HEAD_A]).astype(BF16)

    def fill_zb(q):
        cols = slice(q * fill_cols, (q + 1) * fill_cols)
        a_sc[:, cols] = _silu(_dot(h, w_in_ref[:, w_zb + q * fill_cols:w_zb + (q + 1) * fill_cols]))

    fillers = [functools.partial(f, q) for f in (fill_u, fill_z, fill_mix, fill_zb) for q in range(n_fill)]

    first_half = lax.broadcasted_iota(jnp.int32, (N_STREAMS, LANES), 0) < n_seq

    def bu_piece(k, piece):
        t0 = piece * PIECE
        for half in range(2):
            j = half * HALF_SLABS + k
            lhs = jnp.concatenate(
                [xb_sc[b * CHUNK + t0:b * CHUNK + t0 + PIECE, j * LANES:(j + 1) * LANES]
                 for b in range(n_seq)], axis=0)
            bu = _dot(lhs, bmat_ref[j])
            for b in range(n_seq):
                r0 = (half * n_seq + b) * STREAM_PITCH + t0
                for l in range(2 * SLAB_TILES):
                    bu_sc[l, r0:r0 + PIECE, :] = bu[b * PIECE:(b + 1) * PIECE, l * LANES:(l + 1) * LANES]

    def scan_piece(k, piece, hr, hi, lam_r, lam_i):
        rows = []
        for t in range(piece * PIECE, (piece + 1) * PIECE):
            new_r, new_i = [], []
            for l in range(SLAB_TILES):
                br = bu_sc[l, pl.ds(t, N_STREAMS, stride=STREAM_PITCH), :]
                bi = bu_sc[SLAB_TILES + l, pl.ds(t, N_STREAMS, stride=STREAM_PITCH), :]
                new_r.append(lam_r[l] * hr[l] - lam_i[l] * hi[l] + br)
                new_i.append(lam_r[l] * hi[l] + lam_i[l] * hr[l] + bi)
            hr, hi = new_r, new_i
            rows.append(jnp.concatenate(hr + hi, axis=1))
        return hr, hi, jnp.concatenate(rows, axis=0)

    def c_piece(k, piece, hh):
        t0 = piece * PIECE
        rows = slice(t0 * N_STREAMS, (t0 + PIECE) * N_STREAMS)
        y2 = _dot(hh.astype(BF16), cboth_ref[k])
        y2_sc[0, rows, :] = y2[:, 0:LANES]
        y2_sc[1, rows, :] = y2[:, LANES:2 * LANES]
        for half in range(2):
            j = half * HALF_SLABS + k
            for b in range(n_seq):
                s = half * n_seq + b
                c_sc[b * CHUNK + t0:b * CHUNK + t0 + PIECE, j * LANES:(j + 1) * LANES] = (
                    y2_sc[half, pl.ds(t0 * N_STREAMS + s, PIECE, stride=N_STREAMS), :])

    for piece in range(SCAN_PIECES):
        bu_piece(0, piece)
    for k in range(HALF_SLABS):
        tiles = [slice(l * LANES, (l + 1) * LANES) for l in range(SLAB_TILES)]
        lam_r = [jnp.where(first_half, lam_re_ref[k:k + 1, t], lam_re_ref[HALF_SLABS + k:HALF_SLABS + k + 1, t])
                 for t in tiles]
        lam_i = [jnp.where(first_half, lam_im_ref[k:k + 1, t], lam_im_ref[HALF_SLABS + k:HALF_SLABS + k + 1, t])
                 for t in tiles]
        hr = [st_sc[k, :, l * LANES:(l + 1) * LANES] for l in range(SLAB_TILES)]
        hi = [st_sc[k, :, (SLAB_TILES + l) * LANES:(SLAB_TILES + l + 1) * LANES] for l in range(SLAB_TILES)]
        for piece in range(SCAN_PIECES):
            hr, hi, hh = scan_piece(k, piece, hr, hi, lam_r, lam_i)
            c_piece(k, piece, hh)
            if k + 1 < HALF_SLABS:
                bu_piece(k + 1, piece)
            fillers[k * SCAN_PIECES + piece]()
        st_sc[k] = jnp.concatenate(hr + hi, axis=1)

    for k in range(HALF_SLABS):
        for half in range(2):
            j = half * HALF_SLABS + k
            for b in range(n_seq):
                s = half * n_seq + b
                st_re_ref[b, j:j + 1, :] = st_sc[k, s:s + 1, 0:SLAB_STATES]
                st_im_ref[b, j:j + 1, :] = st_sc[k, s:s + 1, SLAB_STATES:2 * SLAB_STATES]

    for j in range(N_SLABS):
        cols = slice(j * LANES, (j + 1) * LANES)
        y = c_sc[:, cols] + dskip_ref[:, cols] * b_sc[:, cols]
        g = _dot(y.astype(BF16), gmat_ref[j])
        val = g[:, 0:LANES] + bglu_ref[0:1, cols]
        gate = g[:, LANES:2 * LANES] + bglu_ref[1:2, cols]
        oin_ref[:, D_A + j * LANES:D_A + (j + 1) * LANES] = (
            val * jax.nn.sigmoid(gate) * a_sc[:, cols]).astype(BF16)


def _const_spec(shape):
    zeros = (0,) * len(shape)
    return pl.BlockSpec(shape, lambda i: zeros, pipeline_mode=pl.Buffered(1))


def _mixer(x, mod, g_pre, w_in_bf, ln_g, ln_b, w_s, bst, lam_re, lam_im, bmat, cboth, gmat,
           d_skip, b_glu_rows, w_out):
    n_seq, seq, _ = x.shape
    assert 2 * n_seq == N_STREAMS and seq % CHUNK == 0
    rows = n_seq * CHUNK
    n_steps = seq // CHUNK
    wout_rows = w_out.shape[0] // n_steps
    assert wout_rows * n_steps == w_out.shape[0] and wout_rows % (2 * SUBLANES) == 0
    state_shape = (n_seq, N_SLABS, SLAB_STATES)
    consts = (mod, g_pre, w_in_bf, ln_g, ln_b, w_s, bst, lam_re, lam_im, bmat, cboth, gmat,
              d_skip, b_glu_rows)
    return pl.pallas_call(
        functools.partial(_mixer_kernel, n_seq=n_seq),
        grid=(n_steps,),
        in_specs=[pl.BlockSpec((n_seq, CHUNK, D_MODEL), lambda i: (0, i, 0))]
        + [_const_spec(c.shape) for c in consts]
        + [pl.BlockSpec((wout_rows, w_out.shape[1]), lambda i: (i, 0))],
        out_specs=(
            pl.BlockSpec((rows, D_A + D_B), lambda i: (i, 0)),
            pl.BlockSpec(state_shape, lambda i: (0, 0, 0)),
            pl.BlockSpec(state_shape, lambda i: (0, 0, 0)),
            pl.BlockSpec((wout_rows, w_out.shape[1]), lambda i: (i, 0)),
        ),
        out_shape=(
            jax.ShapeDtypeStruct((n_steps * rows, D_A + D_B), BF16),
            jax.ShapeDtypeStruct(state_shape, F32),
            jax.ShapeDtypeStruct(state_shape, F32),
            jax.ShapeDtypeStruct(w_out.shape, BF16),
        ),
        scratch_shapes=[
            pltpu.VMEM((rows, D_MODEL), BF16),
            pltpu.VMEM((rows, D_A), F32),
            pltpu.VMEM((rows, D_A), F32),
            pltpu.VMEM((rows, D_A), F32),
            pltpu.VMEM((rows, D_A), BF16),
            pltpu.VMEM((rows, D_B), BF16),
            pltpu.VMEM((2 * SLAB_TILES, N_STREAMS * STREAM_PITCH, LANES), F32),
            pltpu.VMEM((2, N_STREAMS * CHUNK, LANES), F32),
            pltpu.VMEM((HALF_SLABS, N_STREAMS, 2 * SLAB_STATES), F32),
        ],
        compiler_params=pltpu.CompilerParams(
            dimension_semantics=("arbitrary",), vmem_limit_bytes=VMEM_LIMIT),
        name="mixer",
    )(x, *consts, w_out)


def _outproj_kernel(oin_ref, x_ref, mod_ref, gpost_ref, w_out_ref, y_ref, *, n_seq):
    r = _rms(_dot(oin_ref[...], w_out_ref[...]), gpost_ref[...])
    for b in range(n_seq):
        gate = mod_ref[b:b + 1, 2 * D_MODEL:3 * D_MODEL]
        y_ref[b] = x_ref[b] + gate * r[b * CHUNK:(b + 1) * CHUNK, :]


def _outproj(oin, x, mod, g_post, w_out_bf):
    n_seq, seq, _ = x.shape
    rows = n_seq * CHUNK
    return pl.pallas_call(
        functools.partial(_outproj_kernel, n_seq=n_seq),
        grid=(seq // CHUNK,),
        in_specs=[
            pl.BlockSpec((rows, D_A + D_B), lambda i: (i, 0)),
            pl.BlockSpec((n_seq, CHUNK, D_MODEL), lambda i: (0, i, 0)),
            _const_spec(mod.shape), _const_spec(g_post.shape), _const_spec(w_out_bf.shape),
        ],
        out_specs=pl.BlockSpec((n_seq, CHUNK, D_MODEL), lambda i: (0, i, 0)),
        out_shape=jax.ShapeDtypeStruct(x.shape, F32),
        compiler_params=pltpu.CompilerParams(
            dimension_semantics=("arbitrary",), vmem_limit_bytes=VMEM_LIMIT),
        name="outproj",
    )(oin, x, mod, g_post, w_out_bf)


SAMPLE_SLAB = 1024
N_IN_SLABS = D_IN // SAMPLE_SLAB
N_OUT_SLABS = D_MODEL // SAMPLE_SLAB


def _sample_kernel(x2_ref, mod_ref, gpre_ref, w_in_ref, lng_ref, lnb_ref, ws_ref, bst_ref,
                   lam_re_ref, lam_im_ref, bmat_ref, cboth_ref, gmat_ref, dskip_ref, bglu_ref,
                   h0_re_ref, h0_im_ref, gpost_ref, w_out_ref,
                   y2_ref, v2_ref, hs_re_ref, hs_im_ref, x_sc, h_sc, p_sc, g_sc, oin_sc, o_sc):
    c = pl.program_id(0)
    n = x_sc.shape[0]
    x_tiles = D_MODEL // LANES
    v_tiles = D_A // LANES

    @pl.when(c == 0)
    def _():
        for q in range(x_tiles):
            x_sc[:, q * LANES:(q + 1) * LANES] = x2_ref[pl.ds(q, n, stride=x_tiles), :]
        shift = mod_ref[0:n, 0:D_MODEL]
        scale = mod_ref[0:n, D_MODEL:2 * D_MODEL]
        h_sc[...] = (_rms(x_sc[...], gpre_ref[...]) * (1.0 + scale) + shift).astype(BF16)

    @pl.when(c < N_IN_SLABS)
    def _():
        p_sc[c] = _dot(h_sc[...], w_in_ref[...])

    @pl.when(c == 0)
    def _():
        for j in range(N_SLABS):
            cols = slice(j * LANES, (j + 1) * LANES)
            st = slice(j * SLAB_STATES, (j + 1) * SLAB_STATES)
            xb = p_sc[0, :, cols]
            bu = _dot(xb.astype(BF16), bmat_ref[j])
            lr = lam_re_ref[j:j + 1, :]
            li = lam_im_ref[j:j + 1, :]
            h0r = h0_re_ref[:, st]
            h0i = h0_im_ref[:, st]
            hr = lr * h0r - li * h0i + bu[:, 0:SLAB_STATES]
            hi = lr * h0i + li * h0r + bu[:, SLAB_STATES:2 * SLAB_STATES]
            hs_re_ref[:, st] = hr
            hs_im_ref[:, st] = hi
            k, half = j % HALF_SLABS, j // HALF_SLABS
            c_j = cboth_ref[k, :, half * LANES:(half + 1) * LANES]
            y = _dot(jnp.concatenate([hr, hi], axis=1).astype(BF16), c_j)
            y = y + dskip_ref[:, cols] * xb
            g = _dot(y.astype(BF16), gmat_ref[j])
            val = g[:, 0:LANES] + bglu_ref[0:1, cols]
            gt = g[:, LANES:2 * LANES] + bglu_ref[1:2, cols]
            g_sc[:, cols] = val * jax.nn.sigmoid(gt)

    @pl.when(c == 3)
    def _():
        v = _layernorm(p_sc[2], lng_ref[...], lnb_ref[...])
        for q in range(v_tiles):
            v2_ref[pl.ds(q, n, stride=v_tiles), :] = v[:, q * LANES:(q + 1) * LANES]
        for hd in range(N_HEADS_A):
            cols = slice(hd * HEAD_A, (hd + 1) * HEAD_A)
            mix = ws_ref[hd, 0:1, 0:1] * v[:, cols] + bst_ref[0:1, hd:hd + 1]
            oin_sc[:, cols] = (p_sc[1, :, cols] * mix * _silu(p_sc[3, :, cols])).astype(BF16)

    @pl.when(c == N_IN_SLABS - 1)
    def _():
        oin_sc[:, D_A:D_A + D_B] = (g_sc[...] * _silu(p_sc[N_IN_SLABS - 1])).astype(BF16)

    @pl.when(c >= N_IN_SLABS)
    def _():
        o_sc[c - N_IN_SLABS] = _dot(oin_sc[...], w_out_ref[...])

    @pl.when(c == N_IN_SLABS + N_OUT_SLABS - 1)
    def _():
        o = jnp.concatenate([o_sc[i] for i in range(N_OUT_SLABS)], axis=1)
        gate = mod_ref[0:n, 2 * D_MODEL:3 * D_MODEL]
        y = x_sc[...] + gate * _rms(o, gpost_ref[...])
        for q in range(x_tiles):
            y2_ref[pl.ds(q, n, stride=x_tiles), :] = y[:, q * LANES:(q + 1) * LANES]


def _sample(x2, mod, g_pre, w_in_bf, ln_g, ln_b, w_s, bst, lam_re, lam_im, bmat, cboth, gmat,
            d_skip, b_glu_rows, h0_re, h0_im, g_post, w_out_bf):
    n = h0_re.shape[0]
    n_state = N_GROUPS_B * P_STATE
    in_slab = pl.BlockSpec((D_MODEL, SAMPLE_SLAB),
                           lambda c: (0, jnp.where(c < 4, (c + 3) % 4, N_IN_SLABS - 1)))
    out_slab = pl.BlockSpec((D_A + D_B, SAMPLE_SLAB), lambda c: (0, jnp.maximum(c - N_IN_SLABS, 0)))
    operands = (x2, mod, g_pre, w_in_bf, ln_g, ln_b, w_s, bst, lam_re, lam_im, bmat, cboth, gmat,
                d_skip, b_glu_rows, h0_re, h0_im, g_post, w_out_bf)
    in_specs = [in_slab if a is w_in_bf else out_slab if a is w_out_bf else _const_spec(a.shape)
                for a in operands]
    out_shapes = ((n * D_MODEL // LANES, LANES), (n * D_A // LANES, LANES), (n, n_state), (n, n_state))
    return pl.pallas_call(
        _sample_kernel,
        grid=(N_IN_SLABS + N_OUT_SLABS,),
        in_specs=in_specs,
        out_specs=tuple(pl.BlockSpec(shp, lambda c: (0, 0)) for shp in out_shapes),
        out_shape=tuple(jax.ShapeDtypeStruct(shp, F32) for shp in out_shapes),
        scratch_shapes=[
            pltpu.VMEM((n, D_MODEL), F32),
            pltpu.VMEM((n, D_MODEL), BF16),
            pltpu.VMEM((N_IN_SLABS, n, SAMPLE_SLAB), F32),
            pltpu.VMEM((n, D_B), F32),
            pltpu.VMEM((n, D_A + D_B), BF16),
            pltpu.VMEM((N_OUT_SLABS, n, SAMPLE_SLAB), F32),
        ],
        compiler_params=pltpu.CompilerParams(
            dimension_semantics=("arbitrary",), vmem_limit_bytes=VMEM_LIMIT),
        name="sample",
    )(*operands)


def kernel(x_prompt, x_sample, c_prompt, c_sample, state_b_re, state_b_im, w_c, b_c, g_pre, w_in,
           ln_v_g, ln_v_b, w_s, b_s, a_re, a_im, log_dt, b_re, b_im, c_re, c_im, d_skip, w_glu, b_glu,
           w_out, g_post):
    n_p = x_prompt.shape[0]
    n_s = x_sample.shape[0]
    assert x_sample.shape[1] == 1 and n_s % SUBLANES == 0

    bmat, cboth, gmat, lam_re, lam_im, b_glu_rows, bst = _s5_prep(
        a_re, a_im, log_dt, b_re, b_im, c_re, c_im, w_glu, b_glu, b_s)

    mod_s, mod_p = _adaln(c_sample, c_prompt, w_c, b_c)

    row = lambda v: v.reshape(1, -1)
    w_in_bf = w_in.astype(BF16)
    shared = (row(g_pre), w_in_bf, row(ln_v_g), row(ln_v_b), w_s, bst, lam_re, lam_im, bmat, cboth,
              gmat, row(d_skip), b_glu_rows)

    oin, st_re, st_im, w_out_bf = _mixer(x_prompt, mod_p, *shared, w_out)
    y_prompt = _outproj(oin, x_prompt, mod_p, row(g_post), w_out_bf)

    y_s, v_s, hs_re, hs_im = _sample(
        x_sample.reshape(n_s * D_MODEL // LANES, LANES), mod_s, *shared,
        state_b_re.reshape(n_s, -1), state_b_im.reshape(n_s, -1), row(g_post), w_out_bf)

    return (y_prompt, y_s.reshape(n_s, 1, D_MODEL), v_s.reshape(n_s, 1, D_A),
            st_re.reshape(n_p, N_GROUPS_B, P_STATE), st_im.reshape(n_p, N_GROUPS_B, P_STATE),
            hs_re.reshape(n_s, N_GROUPS_B, P_STATE), hs_im.reshape(n_s, N_GROUPS_B, P_STATE))
```

```python
import functools

import jax
import jax.numpy as jnp
from jax import lax
from jax.experimental import pallas as pl
from jax.experimental.pallas import tpu as pltpu

F32 = jnp.float32
BF16 = jnp.bfloat16

EPS = 1e-6
D_MODEL = 2048
D_A = 1024
D_B = 1024
D_IN = 3 * D_A + 2 * D_B
CHUNK = 128
HEAD_A = 128
N_HEADS_A = D_A // HEAD_A
GROUP_B = 16
N_GROUPS_B = D_B // GROUP_B
P_STATE = 64

LANES = 128
SUBLANES = 8
SLAB_GROUPS = LANES // GROUP_B
N_SLABS = N_GROUPS_B // SLAB_GROUPS
SLAB_STATES = SLAB_GROUPS * P_STATE
SLAB_TILES = SLAB_STATES // LANES
N_STREAMS = SUBLANES
HALF_SLABS = N_SLABS // 2
STREAM_PITCH = CHUNK + SUBLANES
SCAN_PIECES = 4
PIECE = CHUNK // SCAN_PIECES
VMEM_LIMIT = 62 * 1024 * 1024


def _silu(x):
    return x * jax.nn.sigmoid(x)


def _rms(x, g):
    return x * lax.rsqrt(jnp.mean(x * x, axis=-1, keepdims=True) + EPS) * g


def _layernorm(x, g, b):
    mu = jnp.mean(x, axis=-1, keepdims=True)
    xc = x - mu
    return xc * lax.rsqrt(jnp.mean(xc * xc, axis=-1, keepdims=True) + EPS) * g + b


def _dot(a, b):
    return jnp.dot(a, b, preferred_element_type=F32)


def _discretize(a_re, a_im, log_dt):
    dt = jnp.exp(log_dt)
    mag = jnp.exp(dt * a_re)
    abar_re = mag * jnp.cos(dt * a_im)
    abar_im = mag * jnp.sin(dt * a_im)
    return abar_re, abar_im


def _nt_dot(a, b):
    return lax.dot_general(a, b, (((1,), (1,)), ((), ())), preferred_element_type=F32)


def _bf16_terms(x):
    hi = x.astype(BF16)
    r1 = x - hi.astype(F32)
    mid = r1.astype(BF16)
    lo = (r1 - mid.astype(F32)).astype(BF16)
    return hi, mid, lo


def _prep_kernel(a8_ref, bt_re_ref, bt_im_ref, c_re_ref, c_im_ref, wglu_ref, bglu_t_ref, bs_ref,
                 bmat_ref, cboth_ref, gmat_ref, lam_re_ref, lam_im_ref, bglu_row_ref, bst_ref):
    a_re = a8_ref[0:N_SLABS, :]
    a_im = a8_ref[N_SLABS:2 * N_SLABS, :]
    abar_re, abar_im = _discretize(a_re, a_im, a8_ref[2 * N_SLABS:3 * N_SLABS, :])
    lam_re_ref[...] = abar_re
    lam_im_ref[...] = abar_im
    num_re = abar_re - 1.0
    num_im = abar_im
    den = a_re * a_re + a_im * a_im
    coef_re = (num_re * a_re + num_im * a_im) / den
    coef_im = (num_im * a_re - num_re * a_im) / den

    def rep_matrix(k, n, period, offset=0):
        row = lax.broadcasted_iota(jnp.int32, (k, n), 0)
        col = lax.broadcasted_iota(jnp.int32, (k, n), 1)
        return jnp.where((col & (period - 1)) + offset == row, 1.0, 0.0).astype(BF16)

    def rep_rows(m, k, period):
        row = lax.broadcasted_iota(jnp.int32, (m, k), 0)
        col = lax.broadcasted_iota(jnp.int32, (m, k), 1)
        return jnp.where((row & (period - 1)) == col, 1.0, 0.0).astype(BF16)

    def block_mask(m, n, row_shift, col_shift):
        row = lax.broadcasted_iota(jnp.int32, (m, n), 0)
        col = lax.broadcasted_iota(jnp.int32, (m, n), 1)
        return (row >> row_shift) == (col >> col_shift)

    rep_state = rep_matrix(P_STATE, SLAB_STATES, P_STATE)
    rows_state = rep_rows(SLAB_STATES, P_STATE, P_STATE)
    rep_val = rep_matrix(2 * GROUP_B, LANES, GROUP_B)
    rep_gate = rep_matrix(2 * GROUP_B, LANES, GROUP_B, GROUP_B)
    mask_b = block_mask(LANES, SLAB_STATES, 4, 6)
    mask_c = block_mask(SLAB_STATES, LANES, 6, 4)
    mask_g = block_mask(LANES, LANES, 4, 4)

    def spread(x, rep):
        return sum(_dot(term, rep) for term in _bf16_terms(x))

    for j in range(N_SLABS):
        crows = slice(j * LANES, (j + 1) * LANES)
        bt_re = spread(bt_re_ref[crows, :], rep_state)
        bt_im = spread(bt_im_ref[crows, :], rep_state)
        cr = coef_re[j:j + 1, :]
        ci = coef_im[j:j + 1, :]
        bmat_ref[j, :, 0:SLAB_STATES] = jnp.where(mask_b, cr * bt_re - ci * bt_im, 0.0).astype(BF16)
        bmat_ref[j, :, SLAB_STATES:2 * SLAB_STATES] = jnp.where(
            mask_b, cr * bt_im + ci * bt_re, 0.0).astype(BF16)

        k, half = j % HALF_SLABS, j // HALF_SLABS
        ccols = slice(half * LANES, (half + 1) * LANES)
        ct_re = _nt_dot(rows_state, c_re_ref[crows, :].astype(BF16))
        ct_im = _nt_dot(rows_state, c_im_ref[crows, :].astype(BF16))
        cboth_ref[k, 0:SLAB_STATES, ccols] = jnp.where(mask_c, ct_re, 0.0).astype(BF16)
        cboth_ref[k, SLAB_STATES:2 * SLAB_STATES, ccols] = jnp.where(mask_c, -ct_im, 0.0).astype(BF16)

        w = wglu_ref[crows, :].astype(BF16)
        gmat_ref[j, :, 0:LANES] = jnp.where(mask_g, _dot(w, rep_val), 0.0).astype(BF16)
        gmat_ref[j, :, LANES:2 * LANES] = jnp.where(mask_g, _dot(w, rep_gate), 0.0).astype(BF16)

    row = lax.broadcasted_iota(jnp.int32, (2 * GROUP_B, D_B), 0)
    col = lax.broadcasted_iota(jnp.int32, (2 * GROUP_B, D_B), 1)
    by_group = spread(bglu_t_ref[...], jnp.where(block_mask(N_GROUPS_B, D_B, 0, 4), 1.0, 0.0).astype(BF16))
    for i in range(2):
        mine = row == (col & (GROUP_B - 1)) + i * GROUP_B
        bglu_row_ref[i:i + 1, :] = jnp.sum(jnp.where(mine, by_group, 0.0), axis=0, keepdims=True)

    eye = rep_rows(CHUNK, CHUNK, CHUNK)
    bst_ref[...] = sum(_nt_dot(eye, term) for term in _bf16_terms(bs_ref[...]))


def _s5_prep(a_re, a_im, log_dt, b_re, b_im, c_re, c_im, w_glu, b_glu, b_s):
    g, p, c = N_GROUPS_B, P_STATE, GROUP_B
    a8 = jnp.stack([a_re, a_im, jnp.broadcast_to(log_dt[:, None], (g, p))]).reshape(3 * N_SLABS, SLAB_STATES)
    args = (a8, b_re.transpose(0, 2, 1).reshape(g * c, p), b_im.transpose(0, 2, 1).reshape(g * c, p),
            c_re.reshape(g * c, p), c_im.reshape(g * c, p), w_glu.reshape(g * c, 2 * c), b_glu.T, b_s)
    return pl.pallas_call(
        _prep_kernel,
        out_shape=(
            jax.ShapeDtypeStruct((N_SLABS, LANES, 2 * SLAB_STATES), BF16),
            jax.ShapeDtypeStruct((HALF_SLABS, 2 * SLAB_STATES, 2 * LANES), BF16),
            jax.ShapeDtypeStruct((N_SLABS, LANES, 2 * LANES), BF16),
            jax.ShapeDtypeStruct((N_SLABS, SLAB_STATES), F32),
            jax.ShapeDtypeStruct((N_SLABS, SLAB_STATES), F32),
            jax.ShapeDtypeStruct((2, D_B), F32),
            jax.ShapeDtypeStruct((CHUNK, N_HEADS_A), F32),
        ),
        name="s5_prep",
    )(*args)


ADALN_ROWS = 256
ADALN_PROMPT_ROWS = 2 * SUBLANES


def _adaln_kernel(cs_ref, cp_ref, w_ref, b_ref, w_in_ref, os_ref, op_ref, w_in_bf_ref):
    k = pl.program_id(0)

    @pl.when(k == 0)
    def _():
        os_ref[...] = jnp.broadcast_to(b_ref[...], os_ref.shape)
        op_ref[...] = jnp.broadcast_to(b_ref[...], op_ref.shape)

    w_in_bf_ref[...] = w_in_ref[...].astype(BF16)

    cols = pl.ds(pl.multiple_of(k * ADALN_ROWS, ADALN_ROWS), ADALN_ROWS)
    w = w_ref[...].astype(BF16)
    os_ref[...] += _dot(_silu(cs_ref[:, cols]).astype(BF16), w)
    n_p = cp_ref.shape[0]
    cp = jnp.concatenate([cp_ref[:, cols], jnp.zeros((ADALN_PROMPT_ROWS - n_p, ADALN_ROWS), F32)], axis=0)
    op_ref[...] += _dot(_silu(cp).astype(BF16), w)


def _adaln(c_sample, c_prompt, w_c, b_c, w_in):
    n_s, n_p = c_sample.shape[0], c_prompt.shape[0]
    assert n_p <= ADALN_PROMPT_ROWS and D_MODEL % ADALN_ROWS == 0
    return pl.pallas_call(
        _adaln_kernel,
        grid=(D_MODEL // ADALN_ROWS,),
        in_specs=[
            pl.BlockSpec((n_s, D_MODEL), lambda k: (0, 0)),
            pl.BlockSpec((n_p, D_MODEL), lambda k: (0, 0)),
            pl.BlockSpec((ADALN_ROWS, 3 * D_MODEL), lambda k: (k, 0)),
            pl.BlockSpec((1, 3 * D_MODEL), lambda k: (0, 0)),
            pl.BlockSpec((ADALN_ROWS, D_IN), lambda k: (k, 0)),
        ],
        out_specs=(pl.BlockSpec((n_s, 3 * D_MODEL), lambda k: (0, 0)),
                   pl.BlockSpec((ADALN_PROMPT_ROWS, 3 * D_MODEL), lambda k: (0, 0)),
                   pl.BlockSpec((ADALN_ROWS, D_IN), lambda k: (k, 0))),
        out_shape=(jax.ShapeDtypeStruct((n_s, 3 * D_MODEL), F32),
                   jax.ShapeDtypeStruct((ADALN_PROMPT_ROWS, 3 * D_MODEL), F32),
                   jax.ShapeDtypeStruct((D_MODEL, D_IN), BF16)),
        compiler_params=pltpu.CompilerParams(
            dimension_semantics=("arbitrary",), vmem_limit_bytes=VMEM_LIMIT),
        name="adaln",
    )(c_sample, c_prompt, w_c, b_c.reshape(1, -1), w_in)


def _mixer_kernel(x_ref, mod_ref, gpre_ref, w_in_ref, lng_ref, lnb_ref, ws_ref, bst_ref,
                  lam_re_ref, lam_im_ref, bmat_ref, cboth_ref, gmat_ref, dskip_ref, bglu_ref, wout_ref,
                  oin_ref, st_re_ref, st_im_ref, wout_bf_ref,
                  h_sc, a_sc, b_sc, c_sc, v_sc, xb_sc, bu_sc, y2_sc, st_sc, *, n_seq):
    step = pl.program_id(0)

    @pl.when(step == 0)
    def _():
        st_sc[...] = jnp.zeros_like(st_sc)

    wout_bf_ref[...] = wout_ref[...].astype(BF16)

    w_u, w_v, w_z, w_xb, w_zb = 0, D_A, 2 * D_A, 3 * D_A, 3 * D_A + D_B

    for b in range(n_seq):
        r = _rms(x_ref[b], gpre_ref[...])
        shift = mod_ref[b:b + 1, 0:D_MODEL]
        scale = mod_ref[b:b + 1, D_MODEL:2 * D_MODEL]
        rws = slice(b * CHUNK, (b + 1) * CHUNK)
        h_sc[rws, :] = (r * (1.0 + scale) + shift).astype(BF16)
        b_sc[rws, :] = _dot(h_sc[rws, :], w_in_ref[:, w_xb:w_xb + D_B])
        xb_sc[rws, :] = b_sc[rws, :].astype(BF16)

    h = h_sc[...]
    c_sc[...] = _dot(h, w_in_ref[:, w_v:w_v + D_A])
    v_sc[...] = _layernorm(c_sc[...], lng_ref[...], lnb_ref[...]).astype(BF16)

    tril = (lax.broadcasted_iota(jnp.int32, (CHUNK, CHUNK), 0)
            >= lax.broadcasted_iota(jnp.int32, (CHUNK, CHUNK), 1))
    n_fill = HALF_SLABS * SCAN_PIECES // 4
    fill_cols = D_A // n_fill

    def fill_u(q):
        cols = slice(q * fill_cols, (q + 1) * fill_cols)
        a_sc[:, cols] = _dot(h, w_in_ref[:, w_u + q * fill_cols:w_u + (q + 1) * fill_cols])

    def fill_z(q):
        cols = slice(q * fill_cols, (q + 1) * fill_cols)
        z = _dot(h, w_in_ref[:, w_z + q * fill_cols:w_z + (q + 1) * fill_cols])
        a_sc[:, cols] = a_sc[:, cols] * _silu(z)

    def fill_mix(q):
        heads_per_fill = N_HEADS_A // n_fill
        for hd in range(q * heads_per_fill, (q + 1) * heads_per_fill):
            cols = slice(hd * HEAD_A, (hd + 1) * HEAD_A)
            w_t = jnp.where(tril, ws_ref[hd], 0.0).astype(BF16)
            v_h = jnp.concatenate([v_sc[b * CHUNK:(b + 1) * CHUNK, cols] for b in range(n_seq)], axis=1)
            mix = _dot(w_t, v_h) + bst_ref[:, hd:hd + 1]
            for b in range(n_seq):
                rws = slice(b * CHUNK, (b + 1) * CHUNK)
                oin_ref[rws, cols] = (a_sc[rws, cols] * mix[:, b * HEAD_A:(b + 1) * HEAD_A]).astype(BF16)

    def fill_zb(q):
        cols = slice(q * fill_cols, (q + 1) * fill_cols)
        a_sc[:, cols] = _silu(_dot(h, w_in_ref[:, w_zb + q * fill_cols:w_zb + (q + 1) * fill_cols]))

    fillers = [functools.partial(f, q) for f in (fill_u, fill_z, fill_mix, fill_zb) for q in range(n_fill)]

    first_half = lax.broadcasted_iota(jnp.int32, (N_STREAMS, LANES), 0) < n_seq

    def bu_piece(k, piece):
        t0 = piece * PIECE
        for half in range(2):
            j = half * HALF_SLABS + k
            lhs = jnp.concatenate(
                [xb_sc[b * CHUNK + t0:b * CHUNK + t0 + PIECE, j * LANES:(j + 1) * LANES]
                 for b in range(n_seq)], axis=0)
            bu = _dot(lhs, bmat_ref[j])
            for b in range(n_seq):
                r0 = (half * n_seq + b) * STREAM_PITCH + t0
                for l in range(2 * SLAB_TILES):
                    bu_sc[l, r0:r0 + PIECE, :] = bu[b * PIECE:(b + 1) * PIECE, l * LANES:(l + 1) * LANES]

    def scan_piece(k, piece, hr, hi, lam_r, lam_i):
        rows = []
        for t in range(piece * PIECE, (piece + 1) * PIECE):
            new_r, new_i = [], []
            for l in range(SLAB_TILES):
                br = bu_sc[l, pl.ds(t, N_STREAMS, stride=STREAM_PITCH), :]
                bi = bu_sc[SLAB_TILES + l, pl.ds(t, N_STREAMS, stride=STREAM_PITCH), :]
                new_r.append(lam_r[l] * hr[l] - lam_i[l] * hi[l] + br)
                new_i.append(lam_r[l] * hi[l] + lam_i[l] * hr[l] + bi)
            hr, hi = new_r, new_i
            rows.append(jnp.concatenate(hr + hi, axis=1))
        return hr, hi, jnp.concatenate(rows, axis=0)

    def c_piece(k, piece, hh):
        t0 = piece * PIECE
        rows = slice(t0 * N_STREAMS, (t0 + PIECE) * N_STREAMS)
        y2 = _dot(hh.astype(BF16), cboth_ref[k])
        y2_sc[0, rows, :] = y2[:, 0:LANES]
        y2_sc[1, rows, :] = y2[:, LANES:2 * LANES]
        for half in range(2):
            j = half * HALF_SLABS + k
            for b in range(n_seq):
                s = half * n_seq + b
                c_sc[b * CHUNK + t0:b * CHUNK + t0 + PIECE, j * LANES:(j + 1) * LANES] = (
                    y2_sc[half, pl.ds(t0 * N_STREAMS + s, PIECE, stride=N_STREAMS), :])

    for piece in range(SCAN_PIECES):
        bu_piece(0, piece)
    for k in range(HALF_SLABS):
        tiles = [slice(l * LANES, (l + 1) * LANES) for l in range(SLAB_TILES)]
        lam_r = [jnp.where(first_half, lam_re_ref[k:k + 1, t], lam_re_ref[HALF_SLABS + k:HALF_SLABS + k + 1, t])
                 for t in tiles]
        lam_i = [jnp.where(first_half, lam_im_ref[k:k + 1, t], lam_im_ref[HALF_SLABS + k:HALF_SLABS + k + 1, t])
                 for t in tiles]
        hr = [st_sc[k, :, l * LANES:(l + 1) * LANES] for l in range(SLAB_TILES)]
        hi = [st_sc[k, :, (SLAB_TILES + l) * LANES:(SLAB_TILES + l + 1) * LANES] for l in range(SLAB_TILES)]
        for piece in range(SCAN_PIECES):
            hr, hi, hh = scan_piece(k, piece, hr, hi, lam_r, lam_i)
            c_piece(k, piece, hh)
            if k + 1 < HALF_SLABS:
                bu_piece(k + 1, piece)
            fillers[k * SCAN_PIECES + piece]()
        st_sc[k] = jnp.concatenate(hr + hi, axis=1)

    for k in range(HALF_SLABS):
        for half in range(2):
            j = half * HALF_SLABS + k
            for b in range(n_seq):
                s = half * n_seq + b
                st_re_ref[b, j:j + 1, :] = st_sc[k, s:s + 1, 0:SLAB_STATES]
                st_im_ref[b, j:j + 1, :] = st_sc[k, s:s + 1, SLAB_STATES:2 * SLAB_STATES]

    for j in range(N_SLABS):
        cols = slice(j * LANES, (j + 1) * LANES)
        y = c_sc[:, cols] + dskip_ref[:, cols] * b_sc[:, cols]
        g = _dot(y.astype(BF16), gmat_ref[j])
        val = g[:, 0:LANES] + bglu_ref[0:1, cols]
        gate = g[:, LANES:2 * LANES] + bglu_ref[1:2, cols]
        oin_ref[:, D_A + j * LANES:D_A + (j + 1) * LANES] = (
            val * jax.nn.sigmoid(gate) * a_sc[:, cols]).astype(BF16)


def _const_spec(shape):
    zeros = (0,) * len(shape)
    return pl.BlockSpec(shape, lambda i: zeros, pipeline_mode=pl.Buffered(1))


def _mixer(x, mod, g_pre, w_in_bf, ln_g, ln_b, w_s, bst, lam_re, lam_im, bmat, cboth, gmat,
           d_skip, b_glu_rows, w_out):
    n_seq, seq, _ = x.shape
    assert 2 * n_seq == N_STREAMS and seq % CHUNK == 0
    rows = n_seq * CHUNK
    n_steps = seq // CHUNK
    wout_rows = w_out.shape[0] // n_steps
    assert wout_rows * n_steps == w_out.shape[0] and wout_rows % (2 * SUBLANES) == 0
    state_shape = (n_seq, N_SLABS, SLAB_STATES)
    consts = (mod, g_pre, w_in_bf, ln_g, ln_b, w_s, bst, lam_re, lam_im, bmat, cboth, gmat,
              d_skip, b_glu_rows)
    return pl.pallas_call(
        functools.partial(_mixer_kernel, n_seq=n_seq),
        grid=(n_steps,),
        in_specs=[pl.BlockSpec((n_seq, CHUNK, D_MODEL), lambda i: (0, i, 0))]
        + [_const_spec(c.shape) for c in consts]
        + [pl.BlockSpec((wout_rows, w_out.shape[1]), lambda i: (i, 0))],
        out_specs=(
            pl.BlockSpec((rows, D_A + D_B), lambda i: (i, 0)),
            pl.BlockSpec(state_shape, lambda i: (0, 0, 0)),
            pl.BlockSpec(state_shape, lambda i: (0, 0, 0)),
            pl.BlockSpec((wout_rows, w_out.shape[1]), lambda i: (i, 0)),
        ),
        out_shape=(
            jax.ShapeDtypeStruct((n_steps * rows, D_A + D_B), BF16),
            jax.ShapeDtypeStruct(state_shape, F32),
            jax.ShapeDtypeStruct(state_shape, F32),
            jax.ShapeDtypeStruct(w_out.shape, BF16),
        ),
        scratch_shapes=[
            pltpu.VMEM((rows, D_MODEL), BF16),
            pltpu.VMEM((rows, D_A), F32),
            pltpu.VMEM((rows, D_A), F32),
            pltpu.VMEM((rows, D_A), F32),
            pltpu.VMEM((rows, D_A), BF16),
            pltpu.VMEM((rows, D_B), BF16),
            pltpu.VMEM((2 * SLAB_TILES, N_STREAMS * STREAM_PITCH, LANES), F32),
            pltpu.VMEM((2, N_STREAMS * CHUNK, LANES), F32),
            pltpu.VMEM((HALF_SLABS, N_STREAMS, 2 * SLAB_STATES), F32),
        ],
        compiler_params=pltpu.CompilerParams(
            dimension_semantics=("arbitrary",), vmem_limit_bytes=VMEM_LIMIT),
        name="mixer",
    )(x, *consts, w_out)


def _outproj_kernel(oin_ref, x_ref, mod_ref, gpost_ref, w_out_ref, y_ref, *, n_seq):
    r = _rms(_dot(oin_ref[...], w_out_ref[...]), gpost_ref[...])
    for b in range(n_seq):
        gate = mod_ref[b:b + 1, 2 * D_MODEL:3 * D_MODEL]
        y_ref[b] = x_ref[b] + gate * r[b * CHUNK:(b + 1) * CHUNK, :]


def _outproj(oin, x, mod, g_post, w_out_bf):
    n_seq, seq, _ = x.shape
    rows = n_seq * CHUNK
    return pl.pallas_call(
        functools.partial(_outproj_kernel, n_seq=n_seq),
        grid=(seq // CHUNK,),
        in_specs=[
            pl.BlockSpec((rows, D_A + D_B), lambda i: (i, 0)),
            pl.BlockSpec((n_seq, CHUNK, D_MODEL), lambda i: (0, i, 0)),
            _const_spec(mod.shape), _const_spec(g_post.shape), _const_spec(w_out_bf.shape),
        ],
        out_specs=pl.BlockSpec((n_seq, CHUNK, D_MODEL), lambda i: (0, i, 0)),
        out_shape=jax.ShapeDtypeStruct(x.shape, F32),
        compiler_params=pltpu.CompilerParams(
            dimension_semantics=("arbitrary",), vmem_limit_bytes=VMEM_LIMIT),
        name="outproj",
    )(oin, x, mod, g_post, w_out_bf)


SAMPLE_SLAB = 1024
N_IN_SLABS = D_IN // SAMPLE_SLAB
N_OUT_SLABS = D_MODEL // SAMPLE_SLAB


def _sample_kernel(x2_ref, mod_ref, gpre_ref, w_in_ref, lng_ref, lnb_ref, ws_ref, bst_ref,
                   lam_re_ref, lam_im_ref, bmat_ref, cboth_ref, gmat_ref, dskip_ref, bglu_ref,
                   h0_re_ref, h0_im_ref, gpost_ref, w_out_ref,
                   y2_ref, v2_ref, hs_re_ref, hs_im_ref, x_sc, h_sc, p_sc, g_sc, oin_sc, o_sc):
    c = pl.program_id(0)
    n = x_sc.shape[0]
    x_tiles = D_MODEL // LANES
    v_tiles = D_A // LANES

    @pl.when(c == 0)
    def _():
        for q in range(x_tiles):
            x_sc[:, q * LANES:(q + 1) * LANES] = x2_ref[pl.ds(q, n, stride=x_tiles), :]
        shift = mod_ref[0:n, 0:D_MODEL]
        scale = mod_ref[0:n, D_MODEL:2 * D_MODEL]
        h_sc[...] = (_rms(x_sc[...], gpre_ref[...]) * (1.0 + scale) + shift).astype(BF16)

    @pl.when(c < N_IN_SLABS)
    def _():
        p_sc[c] = _dot(h_sc[...], w_in_ref[...])

    @pl.when(c == 0)
    def _():
        for j in range(N_SLABS):
            cols = slice(j * LANES, (j + 1) * LANES)
            st = slice(j * SLAB_STATES, (j + 1) * SLAB_STATES)
            xb = p_sc[0, :, cols]
            bu = _dot(xb.astype(BF16), bmat_ref[j])
            lr = lam_re_ref[j:j + 1, :]
            li = lam_im_ref[j:j + 1, :]
            h0r = h0_re_ref[:, st]
            h0i = h0_im_ref[:, st]
            hr = lr * h0r - li * h0i + bu[:, 0:SLAB_STATES]
            hi = lr * h0i + li * h0r + bu[:, SLAB_STATES:2 * SLAB_STATES]
            hs_re_ref[:, st] = hr
            hs_im_ref[:, st] = hi
            k, half = j % HALF_SLABS, j // HALF_SLABS
            c_j = cboth_ref[k, :, half * LANES:(half + 1) * LANES]
            y = _dot(jnp.concatenate([hr, hi], axis=1).astype(BF16), c_j)
            y = y + dskip_ref[:, cols] * xb
            g = _dot(y.astype(BF16), gmat_ref[j])
            val = g[:, 0:LANES] + bglu_ref[0:1, cols]
            gt = g[:, LANES:2 * LANES] + bglu_ref[1:2, cols]
            g_sc[:, cols] = val * jax.nn.sigmoid(gt)

    @pl.when(c == 3)
    def _():
        v = _layernorm(p_sc[2], lng_ref[...], lnb_ref[...])
        for q in range(v_tiles):
            v2_ref[pl.ds(q, n, stride=v_tiles), :] = v[:, q * LANES:(q + 1) * LANES]
        for hd in range(N_HEADS_A):
            cols = slice(hd * HEAD_A, (hd + 1) * HEAD_A)
            mix = ws_ref[hd, 0:1, 0:1] * v[:, cols] + bst_ref[0:1, hd:hd + 1]
            oin_sc[:, cols] = (p_sc[1, :, cols] * mix * _silu(p_sc[3, :, cols])).astype(BF16)

    @pl.when(c == N_IN_SLABS - 1)
    def _():
        oin_sc[:, D_A:D_A + D_B] = (g_sc[...] * _silu(p_sc[N_IN_SLABS - 1])).astype(BF16)

    @pl.when(c >= N_IN_SLABS)
    def _():
        o_sc[c - N_IN_SLABS] = _dot(oin_sc[...], w_out_ref[...])

    @pl.when(c == N_IN_SLABS + N_OUT_SLABS - 1)
    def _():
        o = jnp.concatenate([o_sc[i] for i in range(N_OUT_SLABS)], axis=1)
        gate = mod_ref[0:n, 2 * D_MODEL:3 * D_MODEL]
        y = x_sc[...] + gate * _rms(o, gpost_ref[...])
        for q in range(x_tiles):
            y2_ref[pl.ds(q, n, stride=x_tiles), :] = y[:, q * LANES:(q + 1) * LANES]


def _sample(x2, mod, g_pre, w_in_bf, ln_g, ln_b, w_s, bst, lam_re, lam_im, bmat, cboth, gmat,
            d_skip, b_glu_rows, h0_re, h0_im, g_post, w_out_bf):
    n = h0_re.shape[0]
    n_state = N_GROUPS_B * P_STATE
    in_slab = pl.BlockSpec((D_MODEL, SAMPLE_SLAB),
                           lambda c: (0, jnp.where(c < 4, (c + 3) % 4, N_IN_SLABS - 1)))
    out_slab = pl.BlockSpec((D_A + D_B, SAMPLE_SLAB), lambda c: (0, jnp.maximum(c - N_IN_SLABS, 0)))
    operands = (x2, mod, g_pre, w_in_bf, ln_g, ln_b, w_s, bst, lam_re, lam_im, bmat, cboth, gmat,
                d_skip, b_glu_rows, h0_re, h0_im, g_post, w_out_bf)
    in_specs = [in_slab if a is w_in_bf else out_slab if a is w_out_bf else _const_spec(a.shape)
                for a in operands]
    out_shapes = ((n * D_MODEL // LANES, LANES), (n * D_A // LANES, LANES), (n, n_state), (n, n_state))
    return pl.pallas_call(
        _sample_kernel,
        grid=(N_IN_SLABS + N_OUT_SLABS,),
        in_specs=in_specs,
        out_specs=tuple(pl.BlockSpec(shp, lambda c: (0, 0)) for shp in out_shapes),
        out_shape=tuple(jax.ShapeDtypeStruct(shp, F32) for shp in out_shapes),
        scratch_shapes=[
            pltpu.VMEM((n, D_MODEL), F32),
            pltpu.VMEM((n, D_MODEL), BF16),
            pltpu.VMEM((N_IN_SLABS, n, SAMPLE_SLAB), F32),
            pltpu.VMEM((n, D_B), F32),
            pltpu.VMEM((n, D_A + D_B), BF16),
            pltpu.VMEM((N_OUT_SLABS, n, SAMPLE_SLAB), F32),
        ],
        compiler_params=pltpu.CompilerParams(
            dimension_semantics=("arbitrary",), vmem_limit_bytes=VMEM_LIMIT),
        name="sample",
    )(*operands)


def kernel(x_prompt, x_sample, c_prompt, c_sample, state_b_re, state_b_im, w_c, b_c, g_pre, w_in,
           ln_v_g, ln_v_b, w_s, b_s, a_re, a_im, log_dt, b_re, b_im, c_re, c_im, d_skip, w_glu, b_glu,
           w_out, g_post):
    n_p = x_prompt.shape[0]
    n_s = x_sample.shape[0]
    assert x_sample.shape[1] == 1 and n_s % SUBLANES == 0

    bmat, cboth, gmat, lam_re, lam_im, b_glu_rows, bst = _s5_prep(
        a_re, a_im, log_dt, b_re, b_im, c_re, c_im, w_glu, b_glu, b_s)

    mod_s, mod_p, w_in_bf = _adaln(c_sample, c_prompt, w_c, b_c, w_in)

    row = lambda v: v.reshape(1, -1)
    shared = (row(g_pre), w_in_bf, row(ln_v_g), row(ln_v_b), w_s, bst, lam_re, lam_im, bmat, cboth,
              gmat, row(d_skip), b_glu_rows)

    oin, st_re, st_im, w_out_bf = _mixer(x_prompt, mod_p, *shared, w_out)
    y_prompt = _outproj(oin, x_prompt, mod_p, row(g_post), w_out_bf)

    y_s, v_s, hs_re, hs_im = _sample(
        x_sample.reshape(n_s * D_MODEL // LANES, LANES), mod_s, *shared,
        state_b_re.reshape(n_s, -1), state_b_im.reshape(n_s, -1), row(g_post), w_out_bf)

    return (y_prompt, y_s.reshape(n_s, 1, D_MODEL), v_s.reshape(n_s, 1, D_A),
            st_re.reshape(n_p, N_GROUPS_B, P_STATE), st_im.reshape(n_p, N_GROUPS_B, P_STATE),
            hs_re.reshape(n_s, N_GROUPS_B, P_STATE), hs_im.reshape(n_s, N_GROUPS_B, P_STATE))
```

```python
import functools

import jax
import jax.numpy as jnp
from jax import lax
from jax.experimental import pallas as pl
from jax.experimental.pallas import tpu as pltpu

F32 = jnp.float32
BF16 = jnp.bfloat16

EPS = 1e-6
D_MODEL = 2048
D_A = 1024
D_B = 1024
D_IN = 3 * D_A + 2 * D_B
CHUNK = 128
HEAD_A = 128
N_HEADS_A = D_A // HEAD_A
GROUP_B = 16
N_GROUPS_B = D_B // GROUP_B
P_STATE = 64

LANES = 128
SUBLANES = 8
SLAB_GROUPS = LANES // GROUP_B
N_SLABS = N_GROUPS_B // SLAB_GROUPS
SLAB_STATES = SLAB_GROUPS * P_STATE
SLAB_TILES = SLAB_STATES // LANES
N_STREAMS = SUBLANES
HALF_SLABS = N_SLABS // 2
STREAM_PITCH = CHUNK + SUBLANES
SCAN_PIECES = 4
PIECE = CHUNK // SCAN_PIECES
VMEM_LIMIT = 62 * 1024 * 1024
SAMPLE_PROJ_COLS = 512


def _silu(x):
    return x * jax.nn.sigmoid(x)


def _rms(x, g):
    return x * lax.rsqrt(jnp.mean(x * x, axis=-1, keepdims=True) + EPS) * g


def _layernorm(x, g, b):
    mu = jnp.mean(x, axis=-1, keepdims=True)
    xc = x - mu
    return xc * lax.rsqrt(jnp.mean(xc * xc, axis=-1, keepdims=True) + EPS) * g + b


def _dot(a, b):
    return jnp.dot(a, b, preferred_element_type=F32)


def _discretize(a_re, a_im, log_dt):
    dt = jnp.exp(log_dt)
    mag = jnp.exp(dt * a_re)
    abar_re = mag * jnp.cos(dt * a_im)
    abar_im = mag * jnp.sin(dt * a_im)
    return abar_re, abar_im


def _nt_dot(a, b):
    return lax.dot_general(a, b, (((1,), (1,)), ((), ())), preferred_element_type=F32)


def _bf16_terms(x):
    hi = x.astype(BF16)
    r1 = x - hi.astype(F32)
    mid = r1.astype(BF16)
    lo = (r1 - mid.astype(F32)).astype(BF16)
    return hi, mid, lo


def _prep_kernel(a8_ref, bt_re_ref, bt_im_ref, c_re_ref, c_im_ref, wglu_ref, bglu_t_ref, bs_ref,
                 bmat_ref, cboth_ref, gmat_ref, lam_re_ref, lam_im_ref, bglu_row_ref, bst_ref):
    a_re = a8_ref[0:N_SLABS, :]
    a_im = a8_ref[N_SLABS:2 * N_SLABS, :]
    abar_re, abar_im = _discretize(a_re, a_im, a8_ref[2 * N_SLABS:3 * N_SLABS, :])
    lam_re_ref[...] = abar_re
    lam_im_ref[...] = abar_im
    num_re = abar_re - 1.0
    num_im = abar_im
    den = a_re * a_re + a_im * a_im
    coef_re = (num_re * a_re + num_im * a_im) / den
    coef_im = (num_im * a_re - num_re * a_im) / den

    def rep_matrix(k, n, period, offset=0):
        row = lax.broadcasted_iota(jnp.int32, (k, n), 0)
        col = lax.broadcasted_iota(jnp.int32, (k, n), 1)
        return jnp.where((col & (period - 1)) + offset == row, 1.0, 0.0).astype(BF16)

    def rep_rows(m, k, period):
        row = lax.broadcasted_iota(jnp.int32, (m, k), 0)
        col = lax.broadcasted_iota(jnp.int32, (m, k), 1)
        return jnp.where((row & (period - 1)) == col, 1.0, 0.0).astype(BF16)

    def block_mask(m, n, row_shift, col_shift):
        row = lax.broadcasted_iota(jnp.int32, (m, n), 0)
        col = lax.broadcasted_iota(jnp.int32, (m, n), 1)
        return (row >> row_shift) == (col >> col_shift)

    rep_state = rep_matrix(P_STATE, SLAB_STATES, P_STATE)
    rows_state = rep_rows(SLAB_STATES, P_STATE, P_STATE)
    rep_val = rep_matrix(2 * GROUP_B, LANES, GROUP_B)
    rep_gate = rep_matrix(2 * GROUP_B, LANES, GROUP_B, GROUP_B)
    mask_b = block_mask(LANES, SLAB_STATES, 4, 6)
    mask_c = block_mask(SLAB_STATES, LANES, 6, 4)
    mask_g = block_mask(LANES, LANES, 4, 4)

    def spread(x, rep):
        return sum(_dot(term, rep) for term in _bf16_terms(x))

    for j in range(N_SLABS):
        crows = slice(j * LANES, (j + 1) * LANES)
        bt_re = spread(bt_re_ref[crows, :], rep_state)
        bt_im = spread(bt_im_ref[crows, :], rep_state)
        cr = coef_re[j:j + 1, :]
        ci = coef_im[j:j + 1, :]
        bmat_ref[j, :, 0:SLAB_STATES] = jnp.where(mask_b, cr * bt_re - ci * bt_im, 0.0).astype(BF16)
        bmat_ref[j, :, SLAB_STATES:2 * SLAB_STATES] = jnp.where(
            mask_b, cr * bt_im + ci * bt_re, 0.0).astype(BF16)

        k, half = j % HALF_SLABS, j // HALF_SLABS
        ccols = slice(half * LANES, (half + 1) * LANES)
        ct_re = _nt_dot(rows_state, c_re_ref[crows, :].astype(BF16))
        ct_im = _nt_dot(rows_state, c_im_ref[crows, :].astype(BF16))
        cboth_ref[k, 0:SLAB_STATES, ccols] = jnp.where(mask_c, ct_re, 0.0).astype(BF16)
        cboth_ref[k, SLAB_STATES:2 * SLAB_STATES, ccols] = jnp.where(mask_c, -ct_im, 0.0).astype(BF16)

        w = wglu_ref[crows, :].astype(BF16)
        gmat_ref[j, :, 0:LANES] = jnp.where(mask_g, _dot(w, rep_val), 0.0).astype(BF16)
        gmat_ref[j, :, LANES:2 * LANES] = jnp.where(mask_g, _dot(w, rep_gate), 0.0).astype(BF16)

    row = lax.broadcasted_iota(jnp.int32, (2 * GROUP_B, D_B), 0)
    col = lax.broadcasted_iota(jnp.int32, (2 * GROUP_B, D_B), 1)
    by_group = spread(bglu_t_ref[...], jnp.where(block_mask(N_GROUPS_B, D_B, 0, 4), 1.0, 0.0).astype(BF16))
    for i in range(2):
        mine = row == (col & (GROUP_B - 1)) + i * GROUP_B
        bglu_row_ref[i:i + 1, :] = jnp.sum(jnp.where(mine, by_group, 0.0), axis=0, keepdims=True)

    eye = rep_rows(CHUNK, CHUNK, CHUNK)
    bst_ref[...] = sum(_nt_dot(eye, term) for term in _bf16_terms(bs_ref[...]))


def _s5_prep(a_re, a_im, log_dt, b_re, b_im, c_re, c_im, w_glu, b_glu, b_s):
    g, p, c = N_GROUPS_B, P_STATE, GROUP_B
    a8 = jnp.stack([a_re, a_im, jnp.broadcast_to(log_dt[:, None], (g, p))]).reshape(3 * N_SLABS, SLAB_STATES)
    args = (a8, b_re.transpose(0, 2, 1).reshape(g * c, p), b_im.transpose(0, 2, 1).reshape(g * c, p),
            c_re.reshape(g * c, p), c_im.reshape(g * c, p), w_glu.reshape(g * c, 2 * c), b_glu.T, b_s)
    return pl.pallas_call(
        _prep_kernel,
        out_shape=(
            jax.ShapeDtypeStruct((N_SLABS, LANES, 2 * SLAB_STATES), BF16),
            jax.ShapeDtypeStruct((HALF_SLABS, 2 * SLAB_STATES, 2 * LANES), BF16),
            jax.ShapeDtypeStruct((N_SLABS, LANES, 2 * LANES), BF16),
            jax.ShapeDtypeStruct((N_SLABS, SLAB_STATES), F32),
            jax.ShapeDtypeStruct((N_SLABS, SLAB_STATES), F32),
            jax.ShapeDtypeStruct((2, D_B), F32),
            jax.ShapeDtypeStruct((CHUNK, N_HEADS_A), F32),
        ),
        name="s5_prep",
    )(*args)


ADALN_ROWS = 256
ADALN_PROMPT_ROWS = 2 * SUBLANES


def _adaln_kernel(cs_ref, cp_ref, w_ref, b_ref, w_in_ref, x2_ref, gpre_ref,
                  os_ref, op_ref, w_in_bf_ref, hs_ref):
    k = pl.program_id(0)

    @pl.when(k == 0)
    def _():
        os_ref[...] = jnp.broadcast_to(b_ref[...], os_ref.shape)
        op_ref[...] = jnp.broadcast_to(b_ref[...], op_ref.shape)

    w_in_bf_ref[...] = w_in_ref[...].astype(BF16)

    cols = pl.ds(pl.multiple_of(k * ADALN_ROWS, ADALN_ROWS), ADALN_ROWS)
    w = w_ref[...].astype(BF16)
    os_ref[...] += _dot(_silu(cs_ref[:, cols]).astype(BF16), w)
    n_p = cp_ref.shape[0]
    cp = jnp.concatenate([cp_ref[:, cols], jnp.zeros((ADALN_PROMPT_ROWS - n_p, ADALN_ROWS), F32)], axis=0)
    op_ref[...] += _dot(_silu(cp).astype(BF16), w)

    @pl.when(k == pl.num_programs(0) - 1)
    def _():
        x_tiles = D_MODEL // LANES
        n = hs_ref.shape[0]
        x = jnp.concatenate([x2_ref[pl.ds(q, n, stride=x_tiles), :] for q in range(x_tiles)], axis=1)
        shift = os_ref[:, 0:D_MODEL]
        scale = os_ref[:, D_MODEL:2 * D_MODEL]
        hs_ref[...] = (_rms(x, gpre_ref[...]) * (1.0 + scale) + shift).astype(BF16)


def _adaln(c_sample, c_prompt, w_c, b_c, w_in, x2_sample, g_pre):
    n_s, n_p = c_sample.shape[0], c_prompt.shape[0]
    assert n_p <= ADALN_PROMPT_ROWS and D_MODEL % ADALN_ROWS == 0
    return pl.pallas_call(
        _adaln_kernel,
        grid=(D_MODEL // ADALN_ROWS,),
        in_specs=[
            pl.BlockSpec((n_s, D_MODEL), lambda k: (0, 0)),
            pl.BlockSpec((n_p, D_MODEL), lambda k: (0, 0)),
            pl.BlockSpec((ADALN_ROWS, 3 * D_MODEL), lambda k: (k, 0)),
            pl.BlockSpec((1, 3 * D_MODEL), lambda k: (0, 0)),
            pl.BlockSpec((ADALN_ROWS, D_IN), lambda k: (k, 0)),
            pl.BlockSpec(x2_sample.shape, lambda k: (0, 0)),
            pl.BlockSpec(g_pre.shape, lambda k: (0, 0)),
        ],
        out_specs=(pl.BlockSpec((n_s, 3 * D_MODEL), lambda k: (0, 0)),
                   pl.BlockSpec((ADALN_PROMPT_ROWS, 3 * D_MODEL), lambda k: (0, 0)),
                   pl.BlockSpec((ADALN_ROWS, D_IN), lambda k: (k, 0)),
                   pl.BlockSpec((n_s, D_MODEL), lambda k: (0, 0))),
        out_shape=(jax.ShapeDtypeStruct((n_s, 3 * D_MODEL), F32),
                   jax.ShapeDtypeStruct((ADALN_PROMPT_ROWS, 3 * D_MODEL), F32),
                   jax.ShapeDtypeStruct((D_MODEL, D_IN), BF16),
                   jax.ShapeDtypeStruct((n_s, D_MODEL), BF16)),
        compiler_params=pltpu.CompilerParams(
            dimension_semantics=("arbitrary",), vmem_limit_bytes=VMEM_LIMIT),
        name="adaln",
    )(c_sample, c_prompt, w_c, b_c.reshape(1, -1), w_in, x2_sample, g_pre)


def _mixer_kernel(x_ref, mod_ref, gpre_ref, w_in_ref, lng_ref, lnb_ref, ws_ref, bst_ref,
                  lam_re_ref, lam_im_ref, bmat_ref, cboth_ref, gmat_ref, dskip_ref, bglu_ref, hs_ref, wout_ref,
                  oin_ref, st_re_ref, st_im_ref, wout_bf_ref, ps_ref,
                  h_sc, a_sc, b_sc, c_sc, v_sc, xb_sc, bu_sc, y2_sc, st_sc, *, n_seq):
    step = pl.program_id(0)

    @pl.when(step == 0)
    def _():
        st_sc[...] = jnp.zeros_like(st_sc)

    wout_bf_ref[...] = wout_ref[...].astype(BF16)

    @pl.when(step < D_IN // SAMPLE_PROJ_COLS)
    def _():
        cols = pl.ds(pl.multiple_of(step * SAMPLE_PROJ_COLS, SAMPLE_PROJ_COLS), SAMPLE_PROJ_COLS)
        ps_ref[...] = _dot(hs_ref[...], w_in_ref[:, cols])

    w_u, w_v, w_z, w_xb, w_zb = 0, D_A, 2 * D_A, 3 * D_A, 3 * D_A + D_B

    for b in range(n_seq):
        r = _rms(x_ref[b], gpre_ref[...])
        shift = mod_ref[b:b + 1, 0:D_MODEL]
        scale = mod_ref[b:b + 1, D_MODEL:2 * D_MODEL]
        rws = slice(b * CHUNK, (b + 1) * CHUNK)
        h_sc[rws, :] = (r * (1.0 + scale) + shift).astype(BF16)
        b_sc[rws, :] = _dot(h_sc[rws, :], w_in_ref[:, w_xb:w_xb + D_B])
        xb_sc[rws, :] = b_sc[rws, :].astype(BF16)

    h = h_sc[...]

    tril = (lax.broadcasted_iota(jnp.int32, (CHUNK, CHUNK), 0)
            >= lax.broadcasted_iota(jnp.int32, (CHUNK, CHUNK), 1))
    n_fill = HALF_SLABS * SCAN_PIECES // 4
    fill_cols = D_A // n_fill

    def fill_u(q):
        cols = slice(q * fill_cols, (q + 1) * fill_cols)
        a_sc[:, cols] = _dot(h, w_in_ref[:, w_u + q * fill_cols:w_u + (q + 1) * fill_cols])

    def fill_z(q):
        cols = slice(q * fill_cols, (q + 1) * fill_cols)
        z = _dot(h, w_in_ref[:, w_z + q * fill_cols:w_z + (q + 1) * fill_cols])
        a_sc[:, cols] = a_sc[:, cols] * _silu(z)

    def fill_mix(q):
        heads_per_fill = N_HEADS_A // n_fill
        for hd in range(q * heads_per_fill, (q + 1) * heads_per_fill):
            cols = slice(hd * HEAD_A, (hd + 1) * HEAD_A)
            w_t = jnp.where(tril, ws_ref[hd], 0.0).astype(BF16)
            v_h = jnp.concatenate([v_sc[b * CHUNK:(b + 1) * CHUNK, cols] for b in range(n_seq)], axis=1)
            mix = _dot(w_t, v_h) + bst_ref[:, hd:hd + 1]
            for b in range(n_seq):
                rws = slice(b * CHUNK, (b + 1) * CHUNK)
                oin_ref[rws, cols] = (a_sc[rws, cols] * mix[:, b * HEAD_A:(b + 1) * HEAD_A]).astype(BF16)

    def fill_zb(q):
        cols = slice(q * fill_cols, (q + 1) * fill_cols)
        a_sc[:, cols] = _silu(_dot(h, w_in_ref[:, w_zb + q * fill_cols:w_zb + (q + 1) * fill_cols]))

    fillers = [functools.partial(f, q) for f, q in (
        (fill_u, 0), (fill_z, 0), (fill_u, 1), (fill_mix, 0), (fill_z, 1), (fill_u, 2), (fill_z, 2), (fill_mix, 1),
        (fill_u, 3), (fill_z, 3), (fill_zb, 0), (fill_mix, 2), (fill_zb, 1), (fill_zb, 2), (fill_mix, 3), (fill_zb, 3))]
    assert len(fillers) == HALF_SLABS * SCAN_PIECES

    first_half = lax.broadcasted_iota(jnp.int32, (N_STREAMS, LANES), 0) < n_seq

    def bu_piece(k, piece):
        t0 = piece * PIECE
        for half in range(2):
            j = half * HALF_SLABS + k
            lhs = jnp.concatenate(
                [xb_sc[b * CHUNK + t0:b * CHUNK + t0 + PIECE, j * LANES:(j + 1) * LANES]
                 for b in range(n_seq)], axis=0)
            bu = _dot(lhs, bmat_ref[j])
            for b in range(n_seq):
                r0 = (half * n_seq + b) * STREAM_PITCH + t0
                for l in range(2 * SLAB_TILES):
                    bu_sc[l, r0:r0 + PIECE, :] = bu[b * PIECE:(b + 1) * PIECE, l * LANES:(l + 1) * LANES]

    def scan_piece(k, piece, hr, hi, lam_r, lam_i):
        rows = []
        for t in range(piece * PIECE, (piece + 1) * PIECE):
            new_r, new_i = [], []
            for l in range(SLAB_TILES):
                br = bu_sc[l, pl.ds(t, N_STREAMS, stride=STREAM_PITCH), :]
                bi = bu_sc[SLAB_TILES + l, pl.ds(t, N_STREAMS, stride=STREAM_PITCH), :]
                new_r.append(lam_r[l] * hr[l] - lam_i[l] * hi[l] + br)
                new_i.append(lam_r[l] * hi[l] + lam_i[l] * hr[l] + bi)
            hr, hi = new_r, new_i
            rows.append(jnp.concatenate(hr + hi, axis=1))
        return hr, hi, jnp.concatenate(rows, axis=0)

    def c_piece(k, piece, hh):
        t0 = piece * PIECE
        rows = slice(t0 * N_STREAMS, (t0 + PIECE) * N_STREAMS)
        y2 = _dot(hh.astype(BF16), cboth_ref[k])
        y2_sc[0, rows, :] = y2[:, 0:LANES]
        y2_sc[1, rows, :] = y2[:, LANES:2 * LANES]
        for half in range(2):
            j = half * HALF_SLABS + k
            for b in range(n_seq):
                s = half * n_seq + b
                c_sc[b * CHUNK + t0:b * CHUNK + t0 + PIECE, j * LANES:(j + 1) * LANES] = (
                    y2_sc[half, pl.ds(t0 * N_STREAMS + s, PIECE, stride=N_STREAMS), :])

    for piece in range(SCAN_PIECES):
        bu_piece(0, piece)
    c_sc[...] = _dot(h, w_in_ref[:, w_v:w_v + D_A])
    v_sc[...] = _layernorm(c_sc[...], lng_ref[...], lnb_ref[...]).astype(BF16)
    for k in range(HALF_SLABS):
        tiles = [slice(l * LANES, (l + 1) * LANES) for l in range(SLAB_TILES)]
        lam_r = [jnp.where(first_half, lam_re_ref[k:k + 1, t], lam_re_ref[HALF_SLABS + k:HALF_SLABS + k + 1, t])
                 for t in tiles]
        lam_i = [jnp.where(first_half, lam_im_ref[k:k + 1, t], lam_im_ref[HALF_SLABS + k:HALF_SLABS + k + 1, t])
                 for t in tiles]
        hr = [st_sc[k, :, l * LANES:(l + 1) * LANES] for l in range(SLAB_TILES)]
        hi = [st_sc[k, :, (SLAB_TILES + l) * LANES:(SLAB_TILES + l + 1) * LANES] for l in range(SLAB_TILES)]
        for piece in range(SCAN_PIECES):
            hr, hi, hh = scan_piece(k, piece, hr, hi, lam_r, lam_i)
            c_piece(k, piece, hh)
            if k + 1 < HALF_SLABS:
                bu_piece(k + 1, piece)
            fillers[k * SCAN_PIECES + piece]()
        st_sc[k] = jnp.concatenate(hr + hi, axis=1)

    for k in range(HALF_SLABS):
        for half in range(2):
            j = half * HALF_SLABS + k
            for b in range(n_seq):
                s = half * n_seq + b
                st_re_ref[b, j:j + 1, :] = st_sc[k, s:s + 1, 0:SLAB_STATES]
                st_im_ref[b, j:j + 1, :] = st_sc[k, s:s + 1, SLAB_STATES:2 * SLAB_STATES]

    for j in range(N_SLABS):
        cols = slice(j * LANES, (j + 1) * LANES)
        y = c_sc[:, cols] + dskip_ref[:, cols] * b_sc[:, cols]
        g = _dot(y.astype(BF16), gmat_ref[j])
        val = g[:, 0:LANES] + bglu_ref[0:1, cols]
        gate = g[:, LANES:2 * LANES] + bglu_ref[1:2, cols]
        oin_ref[:, D_A + j * LANES:D_A + (j + 1) * LANES] = (
            val * jax.nn.sigmoid(gate) * a_sc[:, cols]).astype(BF16)


def _const_spec(shape):
    zeros = (0,) * len(shape)
    return pl.BlockSpec(shape, lambda i: zeros, pipeline_mode=pl.Buffered(1))


def _mixer(x, mod, g_pre, w_in_bf, ln_g, ln_b, w_s, bst, lam_re, lam_im, bmat, cboth, gmat,
           d_skip, b_glu_rows, h_sample, w_out):
    n_seq, seq, _ = x.shape
    assert 2 * n_seq == N_STREAMS and seq % CHUNK == 0
    rows = n_seq * CHUNK
    n_steps = seq // CHUNK
    wout_rows = w_out.shape[0] // n_steps
    assert wout_rows * n_steps == w_out.shape[0] and wout_rows % (2 * SUBLANES) == 0
    state_shape = (n_seq, N_SLABS, SLAB_STATES)
    n_sample = h_sample.shape[0]
    n_proj = D_IN // SAMPLE_PROJ_COLS
    assert n_proj <= n_steps
    consts = (mod, g_pre, w_in_bf, ln_g, ln_b, w_s, bst, lam_re, lam_im, bmat, cboth, gmat,
              d_skip, b_glu_rows, h_sample)
    return pl.pallas_call(
        functools.partial(_mixer_kernel, n_seq=n_seq),
        grid=(n_steps,),
        in_specs=[pl.BlockSpec((n_seq, CHUNK, D_MODEL), lambda i: (0, i, 0))]
        + [_const_spec(c.shape) for c in consts]
        + [pl.BlockSpec((wout_rows, w_out.shape[1]), lambda i: (i, 0))],
        out_specs=(
            pl.BlockSpec((rows, D_A + D_B), lambda i: (i, 0)),
            pl.BlockSpec(state_shape, lambda i: (0, 0, 0)),
            pl.BlockSpec(state_shape, lambda i: (0, 0, 0)),
            pl.BlockSpec((wout_rows, w_out.shape[1]), lambda i: (i, 0)),
            pl.BlockSpec((n_sample, SAMPLE_PROJ_COLS), lambda i: (0, jnp.minimum(i, n_proj - 1))),
        ),
        out_shape=(
            jax.ShapeDtypeStruct((n_steps * rows, D_A + D_B), BF16),
            jax.ShapeDtypeStruct(state_shape, F32),
            jax.ShapeDtypeStruct(state_shape, F32),
            jax.ShapeDtypeStruct(w_out.shape, BF16),
            jax.ShapeDtypeStruct((n_sample, D_IN), F32),
        ),
        scratch_shapes=[
            pltpu.VMEM((rows, D_MODEL), BF16),
            pltpu.VMEM((rows, D_A), F32),
            pltpu.VMEM((rows, D_A), F32),
            pltpu.VMEM((rows, D_A), F32),
            pltpu.VMEM((rows, D_A), BF16),
            pltpu.VMEM((rows, D_B), BF16),
            pltpu.VMEM((2 * SLAB_TILES, N_STREAMS * STREAM_PITCH, LANES), F32),
            pltpu.VMEM((2, N_STREAMS * CHUNK, LANES), F32),
            pltpu.VMEM((HALF_SLABS, N_STREAMS, 2 * SLAB_STATES), F32),
        ],
        compiler_params=pltpu.CompilerParams(
            dimension_semantics=("arbitrary",), vmem_limit_bytes=VMEM_LIMIT),
        name="mixer",
    )(x, *consts, w_out)


def _outproj_kernel(oin_ref, x_ref, mod_ref, gpost_ref, w_out_ref, y_ref, *, n_seq):
    r = _rms(_dot(oin_ref[...], w_out_ref[...]), gpost_ref[...])
    for b in range(n_seq):
        gate = mod_ref[b:b + 1, 2 * D_MODEL:3 * D_MODEL]
        y_ref[b] = x_ref[b] + gate * r[b * CHUNK:(b + 1) * CHUNK, :]


def _outproj(oin, x, mod, g_post, w_out_bf):
    n_seq, seq, _ = x.shape
    rows = n_seq * CHUNK
    return pl.pallas_call(
        functools.partial(_outproj_kernel, n_seq=n_seq),
        grid=(seq // CHUNK,),
        in_specs=[
            pl.BlockSpec((rows, D_A + D_B), lambda i: (i, 0)),
            pl.BlockSpec((n_seq, CHUNK, D_MODEL), lambda i: (0, i, 0)),
            _const_spec(mod.shape), _const_spec(g_post.shape), _const_spec(w_out_bf.shape),
        ],
        out_specs=pl.BlockSpec((n_seq, CHUNK, D_MODEL), lambda i: (0, i, 0)),
        out_shape=jax.ShapeDtypeStruct(x.shape, F32),
        compiler_params=pltpu.CompilerParams(
            dimension_semantics=("arbitrary",), vmem_limit_bytes=VMEM_LIMIT),
        name="outproj",
    )(oin, x, mod, g_post, w_out_bf)


def _sample_kernel(x2_ref, mod_ref, proj_ref, lng_ref, lnb_ref, ws_ref, bst_ref,
                   lam_re_ref, lam_im_ref, bmat_ref, cboth_ref, gmat_ref, dskip_ref, bglu_ref,
                   h0_re_ref, h0_im_ref, gpost_ref, w_out_ref,
                   y2_ref, v2_ref, hs_re_ref, hs_im_ref, oin_sc):
    n = proj_ref.shape[0]
    x_tiles = D_MODEL // LANES
    v_tiles = D_A // LANES
    w_u, w_v, w_z, w_xb, w_zb = 0, D_A, 2 * D_A, 3 * D_A, 3 * D_A + D_B

    v = _layernorm(proj_ref[:, w_v:w_v + D_A], lng_ref[...], lnb_ref[...])
    for q in range(v_tiles):
        v2_ref[pl.ds(q, n, stride=v_tiles), :] = v[:, q * LANES:(q + 1) * LANES]
    for hd in range(N_HEADS_A):
        cols = slice(hd * HEAD_A, (hd + 1) * HEAD_A)
        mix = ws_ref[hd, 0:1, 0:1] * v[:, cols] + bst_ref[0:1, hd:hd + 1]
        u = proj_ref[:, w_u + hd * HEAD_A:w_u + (hd + 1) * HEAD_A]
        z = proj_ref[:, w_z + hd * HEAD_A:w_z + (hd + 1) * HEAD_A]
        oin_sc[:, cols] = (u * mix * _silu(z)).astype(BF16)

    for j in range(N_SLABS):
        cols = slice(j * LANES, (j + 1) * LANES)
        st = slice(j * SLAB_STATES, (j + 1) * SLAB_STATES)
        xb = proj_ref[:, w_xb + j * LANES:w_xb + (j + 1) * LANES]
        zb = proj_ref[:, w_zb + j * LANES:w_zb + (j + 1) * LANES]
        bu = _dot(xb.astype(BF16), bmat_ref[j])
        lr = lam_re_ref[j:j + 1, :]
        li = lam_im_ref[j:j + 1, :]
        h0r = h0_re_ref[:, st]
        h0i = h0_im_ref[:, st]
        hr = lr * h0r - li * h0i + bu[:, 0:SLAB_STATES]
        hi = lr * h0i + li * h0r + bu[:, SLAB_STATES:2 * SLAB_STATES]
        hs_re_ref[:, st] = hr
        hs_im_ref[:, st] = hi
        k, half = j % HALF_SLABS, j // HALF_SLABS
        c_j = cboth_ref[k, :, half * LANES:(half + 1) * LANES]
        y = _dot(jnp.concatenate([hr, hi], axis=1).astype(BF16), c_j)
        y = y + dskip_ref[:, cols] * xb
        g = _dot(y.astype(BF16), gmat_ref[j])
        val = g[:, 0:LANES] + bglu_ref[0:1, cols]
        gt = g[:, LANES:2 * LANES] + bglu_ref[1:2, cols]
        oin_sc[:, D_A + j * LANES:D_A + (j + 1) * LANES] = (
            val * jax.nn.sigmoid(gt) * _silu(zb)).astype(BF16)

    o = _dot(oin_sc[...], w_out_ref[...])
    gate = mod_ref[:, 2 * D_MODEL:3 * D_MODEL]
    r = gate * _rms(o, gpost_ref[...])
    for q in range(x_tiles):
        cols = slice(q * LANES, (q + 1) * LANES)
        y2_ref[pl.ds(q, n, stride=x_tiles), :] = x2_ref[pl.ds(q, n, stride=x_tiles), :] + r[:, cols]


def _sample(x2, mod, proj, ln_g, ln_b, w_s, bst, lam_re, lam_im, bmat, cboth, gmat,
            d_skip, b_glu_rows, h0_re, h0_im, g_post, w_out_bf):
    n = proj.shape[0]
    n_state = N_GROUPS_B * P_STATE
    out_shapes = ((n * D_MODEL // LANES, LANES), (n * D_A // LANES, LANES), (n, n_state), (n, n_state))
    return pl.pallas_call(
        _sample_kernel,
        out_shape=tuple(jax.ShapeDtypeStruct(shp, F32) for shp in out_shapes),
        scratch_shapes=[pltpu.VMEM((n, D_A + D_B), BF16)],
        compiler_params=pltpu.CompilerParams(vmem_limit_bytes=VMEM_LIMIT),
        name="sample",
    )(x2, mod, proj, ln_g, ln_b, w_s, bst, lam_re, lam_im, bmat, cboth, gmat,
      d_skip, b_glu_rows, h0_re, h0_im, g_post, w_out_bf)


def kernel(x_prompt, x_sample, c_prompt, c_sample, state_b_re, state_b_im, w_c, b_c, g_pre, w_in,
           ln_v_g, ln_v_b, w_s, b_s, a_re, a_im, log_dt, b_re, b_im, c_re, c_im, d_skip, w_glu, b_glu,
           w_out, g_post):
    n_p = x_prompt.shape[0]
    n_s = x_sample.shape[0]
    assert x_sample.shape[1] == 1 and n_s % SUBLANES == 0

    bmat, cboth, gmat, lam_re, lam_im, b_glu_rows, bst = _s5_prep(
        a_re, a_im, log_dt, b_re, b_im, c_re, c_im, w_glu, b_glu, b_s)

    row = lambda v: v.reshape(1, -1)
    x2_sample = x_sample.reshape(n_s * D_MODEL // LANES, LANES)
    mod_s, mod_p, w_in_bf, h_sample = _adaln(c_sample, c_prompt, w_c, b_c, w_in, x2_sample, row(g_pre))

    s5 = (w_s, bst, lam_re, lam_im, bmat, cboth, gmat, row(d_skip), b_glu_rows)
    oin, st_re, st_im, w_out_bf, proj_s = _mixer(
        x_prompt, mod_p, row(g_pre), w_in_bf, row(ln_v_g), row(ln_v_b), *s5, h_sample, w_out)
    y_prompt = _outproj(oin, x_prompt, mod_p, row(g_post), w_out_bf)

    y_s, v_s, hs_re, hs_im = _sample(
        x2_sample, mod_s, proj_s, row(ln_v_g), row(ln_v_b), *s5,
        state_b_re.reshape(n_s, -1), state_b_im.reshape(n_s, -1), row(g_post), w_out_bf)

    return (y_prompt, y_s.reshape(n_s, 1, D_MODEL), v_s.reshape(n_s, 1, D_A),
            st_re.reshape(n_p, N_GROUPS_B, P_STATE), st_im.reshape(n_p, N_GROUPS_B, P_STATE),
            hs_re.reshape(n_s, N_GROUPS_B, P_STATE), hs_im.reshape(n_s, N_GROUPS_B, P_STATE))
```

```python
import functools

import jax
import jax.numpy as jnp
from jax import lax
from jax.experimental import pallas as pl
from jax.experimental.pallas import tpu as pltpu

F32 = jnp.float32
BF16 = jnp.bfloat16

EPS = 1e-6
D_MODEL = 2048
D_A = 1024
D_B = 1024
D_IN = 3 * D_A + 2 * D_B
CHUNK = 128
HEAD_A = 128
N_HEADS_A = D_A // HEAD_A
GROUP_B = 16
N_GROUPS_B = D_B // GROUP_B
P_STATE = 64

LANES = 128
SUBLANES = 8
SLAB_GROUPS = LANES // GROUP_B
N_SLABS = N_GROUPS_B // SLAB_GROUPS
SLAB_STATES = SLAB_GROUPS * P_STATE
SLAB_TILES = SLAB_STATES // LANES
N_STREAMS = SUBLANES
HALF_SLABS = N_SLABS // 2
STREAM_PITCH = CHUNK + SUBLANES
SCAN_PIECES = 4
PIECE = CHUNK // SCAN_PIECES
VMEM_LIMIT = 62 * 1024 * 1024
SAMPLE_PROJ_COLS = 512


def _silu(x):
    return x * jax.nn.sigmoid(x)


def _rms(x, g):
    return x * lax.rsqrt(jnp.mean(x * x, axis=-1, keepdims=True) + EPS) * g


def _layernorm(x, g, b):
    mu = jnp.mean(x, axis=-1, keepdims=True)
    xc = x - mu
    return xc * lax.rsqrt(jnp.mean(xc * xc, axis=-1, keepdims=True) + EPS) * g + b


def _dot(a, b):
    return jnp.dot(a, b, preferred_element_type=F32)


def _discretize(a_re, a_im, log_dt):
    dt = jnp.exp(log_dt)
    mag = jnp.exp(dt * a_re)
    abar_re = mag * jnp.cos(dt * a_im)
    abar_im = mag * jnp.sin(dt * a_im)
    return abar_re, abar_im


def _nt_dot(a, b):
    return lax.dot_general(a, b, (((1,), (1,)), ((), ())), preferred_element_type=F32)


def _bf16_terms(x):
    hi = x.astype(BF16)
    r1 = x - hi.astype(F32)
    mid = r1.astype(BF16)
    lo = (r1 - mid.astype(F32)).astype(BF16)
    return hi, mid, lo


def _prep_kernel(a8_ref, bt_re_ref, bt_im_ref, c_re_ref, c_im_ref, wglu_ref, bglu_t_ref, bs_ref,
                 bmat_ref, cboth_ref, gmat_ref, lam_re_ref, lam_im_ref, bglu_row_ref, bst_ref):
    a_re = a8_ref[0:N_SLABS, :]
    a_im = a8_ref[N_SLABS:2 * N_SLABS, :]
    abar_re, abar_im = _discretize(a_re, a_im, a8_ref[2 * N_SLABS:3 * N_SLABS, :])
    lam_re_ref[...] = abar_re
    lam_im_ref[...] = abar_im
    num_re = abar_re - 1.0
    num_im = abar_im
    den = a_re * a_re + a_im * a_im
    coef_re = (num_re * a_re + num_im * a_im) / den
    coef_im = (num_im * a_re - num_re * a_im) / den

    def rep_matrix(k, n, period, offset=0):
        row = lax.broadcasted_iota(jnp.int32, (k, n), 0)
        col = lax.broadcasted_iota(jnp.int32, (k, n), 1)
        return jnp.where((col & (period - 1)) + offset == row, 1.0, 0.0).astype(BF16)

    def rep_rows(m, k, period):
        row = lax.broadcasted_iota(jnp.int32, (m, k), 0)
        col = lax.broadcasted_iota(jnp.int32, (m, k), 1)
        return jnp.where((row & (period - 1)) == col, 1.0, 0.0).astype(BF16)

    def block_mask(m, n, row_shift, col_shift):
        row = lax.broadcasted_iota(jnp.int32, (m, n), 0)
        col = lax.broadcasted_iota(jnp.int32, (m, n), 1)
        return (row >> row_shift) == (col >> col_shift)

    rep_state = rep_matrix(P_STATE, SLAB_STATES, P_STATE)
    rows_state = rep_rows(SLAB_STATES, P_STATE, P_STATE)
    rep_val = rep_matrix(2 * GROUP_B, LANES, GROUP_B)
    rep_gate = rep_matrix(2 * GROUP_B, LANES, GROUP_B, GROUP_B)
    mask_b = block_mask(LANES, SLAB_STATES, 4, 6)
    mask_c = block_mask(SLAB_STATES, LANES, 6, 4)
    mask_g = block_mask(LANES, LANES, 4, 4)

    def spread(x, rep):
        return sum(_dot(term, rep) for term in _bf16_terms(x))

    for j in range(N_SLABS):
        crows = slice(j * LANES, (j + 1) * LANES)
        bt_re = spread(bt_re_ref[crows, :], rep_state)
        bt_im = spread(bt_im_ref[crows, :], rep_state)
        cr = coef_re[j:j + 1, :]
        ci = coef_im[j:j + 1, :]
        bmat_ref[j, :, 0:SLAB_STATES] = jnp.where(mask_b, cr * bt_re - ci * bt_im, 0.0).astype(BF16)
        bmat_ref[j, :, SLAB_STATES:2 * SLAB_STATES] = jnp.where(
            mask_b, cr * bt_im + ci * bt_re, 0.0).astype(BF16)

        k, half = j % HALF_SLABS, j // HALF_SLABS
        ccols = slice(half * LANES, (half + 1) * LANES)
        ct_re = _nt_dot(rows_state, c_re_ref[crows, :].astype(BF16))
        ct_im = _nt_dot(rows_state, c_im_ref[crows, :].astype(BF16))
        cboth_ref[k, 0:SLAB_STATES, ccols] = jnp.where(mask_c, ct_re, 0.0).astype(BF16)
        cboth_ref[k, SLAB_STATES:2 * SLAB_STATES, ccols] = jnp.where(mask_c, -ct_im, 0.0).astype(BF16)

        w = wglu_ref[crows, :].astype(BF16)
        gmat_ref[j, :, 0:LANES] = jnp.where(mask_g, _dot(w, rep_val), 0.0).astype(BF16)
        gmat_ref[j, :, LANES:2 * LANES] = jnp.where(mask_g, _dot(w, rep_gate), 0.0).astype(BF16)

    row = lax.broadcasted_iota(jnp.int32, (2 * GROUP_B, D_B), 0)
    col = lax.broadcasted_iota(jnp.int32, (2 * GROUP_B, D_B), 1)
    by_group = spread(bglu_t_ref[...], jnp.where(block_mask(N_GROUPS_B, D_B, 0, 4), 1.0, 0.0).astype(BF16))
    for i in range(2):
        mine = row == (col & (GROUP_B - 1)) + i * GROUP_B
        bglu_row_ref[i:i + 1, :] = jnp.sum(jnp.where(mine, by_group, 0.0), axis=0, keepdims=True)

    eye = rep_rows(CHUNK, CHUNK, CHUNK)
    bst_ref[...] = sum(_nt_dot(eye, term) for term in _bf16_terms(bs_ref[...]))


def _s5_prep(a_re, a_im, log_dt, b_re, b_im, c_re, c_im, w_glu, b_glu, b_s):
    g, p, c = N_GROUPS_B, P_STATE, GROUP_B
    a8 = jnp.stack([a_re, a_im, jnp.broadcast_to(log_dt[:, None], (g, p))]).reshape(3 * N_SLABS, SLAB_STATES)
    args = (a8, b_re.transpose(0, 2, 1).reshape(g * c, p), b_im.transpose(0, 2, 1).reshape(g * c, p),
            c_re.reshape(g * c, p), c_im.reshape(g * c, p), w_glu.reshape(g * c, 2 * c), b_glu.T, b_s)
    return pl.pallas_call(
        _prep_kernel,
        out_shape=(
            jax.ShapeDtypeStruct((N_SLABS, LANES, 2 * SLAB_STATES), BF16),
            jax.ShapeDtypeStruct((HALF_SLABS, 2 * SLAB_STATES, 2 * LANES), BF16),
            jax.ShapeDtypeStruct((N_SLABS, LANES, 2 * LANES), BF16),
            jax.ShapeDtypeStruct((N_SLABS, SLAB_STATES), F32),
            jax.ShapeDtypeStruct((N_SLABS, SLAB_STATES), F32),
            jax.ShapeDtypeStruct((2, D_B), F32),
            jax.ShapeDtypeStruct((CHUNK, N_HEADS_A), F32),
        ),
        name="s5_prep",
    )(*args)


ADALN_ROWS = 256
ADALN_PROMPT_ROWS = 2 * SUBLANES


def _adaln_kernel(cs_ref, cp_ref, w_ref, b_ref, w_in_ref, x2_ref, gpre_ref,
                  os_ref, op_ref, w_in_bf_ref, hs_ref):
    k = pl.program_id(0)

    @pl.when(k == 0)
    def _():
        os_ref[...] = jnp.broadcast_to(b_ref[...], os_ref.shape)
        op_ref[...] = jnp.broadcast_to(b_ref[...], op_ref.shape)

    w_in_bf_ref[...] = w_in_ref[...].astype(BF16)

    cols = pl.ds(pl.multiple_of(k * ADALN_ROWS, ADALN_ROWS), ADALN_ROWS)
    w = w_ref[...].astype(BF16)
    os_ref[...] += _dot(_silu(cs_ref[:, cols]).astype(BF16), w)
    n_p = cp_ref.shape[0]
    cp = jnp.concatenate([cp_ref[:, cols], jnp.zeros((ADALN_PROMPT_ROWS - n_p, ADALN_ROWS), F32)], axis=0)
    op_ref[...] += _dot(_silu(cp).astype(BF16), w)

    @pl.when(k == pl.num_programs(0) - 1)
    def _():
        x_tiles = D_MODEL // LANES
        n = hs_ref.shape[0]
        x = jnp.concatenate([x2_ref[pl.ds(q, n, stride=x_tiles), :] for q in range(x_tiles)], axis=1)
        shift = os_ref[:, 0:D_MODEL]
        scale = os_ref[:, D_MODEL:2 * D_MODEL]
        hs_ref[...] = (_rms(x, gpre_ref[...]) * (1.0 + scale) + shift).astype(BF16)


def _adaln(c_sample, c_prompt, w_c, b_c, w_in, x2_sample, g_pre):
    n_s, n_p = c_sample.shape[0], c_prompt.shape[0]
    assert n_p <= ADALN_PROMPT_ROWS and D_MODEL % ADALN_ROWS == 0
    return pl.pallas_call(
        _adaln_kernel,
        grid=(D_MODEL // ADALN_ROWS,),
        in_specs=[
            pl.BlockSpec((n_s, D_MODEL), lambda k: (0, 0)),
            pl.BlockSpec((n_p, D_MODEL), lambda k: (0, 0)),
            pl.BlockSpec((ADALN_ROWS, 3 * D_MODEL), lambda k: (k, 0)),
            pl.BlockSpec((1, 3 * D_MODEL), lambda k: (0, 0)),
            pl.BlockSpec((ADALN_ROWS, D_IN), lambda k: (k, 0)),
            pl.BlockSpec(x2_sample.shape, lambda k: (0, 0)),
            pl.BlockSpec(g_pre.shape, lambda k: (0, 0)),
        ],
        out_specs=(pl.BlockSpec((n_s, 3 * D_MODEL), lambda k: (0, 0)),
                   pl.BlockSpec((ADALN_PROMPT_ROWS, 3 * D_MODEL), lambda k: (0, 0)),
                   pl.BlockSpec((ADALN_ROWS, D_IN), lambda k: (k, 0)),
                   pl.BlockSpec((n_s, D_MODEL), lambda k: (0, 0))),
        out_shape=(jax.ShapeDtypeStruct((n_s, 3 * D_MODEL), F32),
                   jax.ShapeDtypeStruct((ADALN_PROMPT_ROWS, 3 * D_MODEL), F32),
                   jax.ShapeDtypeStruct((D_MODEL, D_IN), BF16),
                   jax.ShapeDtypeStruct((n_s, D_MODEL), BF16)),
        compiler_params=pltpu.CompilerParams(
            dimension_semantics=("arbitrary",), vmem_limit_bytes=VMEM_LIMIT),
        name="adaln",
    )(c_sample, c_prompt, w_c, b_c.reshape(1, -1), w_in, x2_sample, g_pre)


def _mixer_kernel(x_ref, mod_ref, gpre_ref, w_in_ref, lng_ref, lnb_ref, ws_ref, bst_ref,
                  lam_re_ref, lam_im_ref, bmat_ref, cboth_ref, gmat_ref, dskip_ref, bglu_ref, hs_ref, wout_ref,
                  oin_ref, st_re_ref, st_im_ref, wout_bf_ref, ps_ref,
                  h_sc, a_sc, b_sc, c_sc, v_sc, xb_sc, bu_sc, y2_sc, st_sc, *, n_seq):
    step = pl.program_id(0)

    @pl.when(step == 0)
    def _():
        st_sc[...] = jnp.zeros_like(st_sc)

    wout_bf_ref[...] = wout_ref[...].astype(BF16)

    @pl.when(step < D_IN // SAMPLE_PROJ_COLS)
    def _():
        cols = pl.ds(pl.multiple_of(step * SAMPLE_PROJ_COLS, SAMPLE_PROJ_COLS), SAMPLE_PROJ_COLS)
        ps_ref[...] = _dot(hs_ref[...], w_in_ref[:, cols])

    w_u, w_v, w_z, w_xb, w_zb = 0, D_A, 2 * D_A, 3 * D_A, 3 * D_A + D_B

    for b in range(n_seq):
        r = _rms(x_ref[b], gpre_ref[...])
        shift = mod_ref[b:b + 1, 0:D_MODEL]
        scale = mod_ref[b:b + 1, D_MODEL:2 * D_MODEL]
        rws = slice(b * CHUNK, (b + 1) * CHUNK)
        h_sc[rws, :] = (r * (1.0 + scale) + shift).astype(BF16)
        b_sc[rws, :] = _dot(h_sc[rws, :], w_in_ref[:, w_xb:w_xb + D_B])
        xb_sc[rws, :] = b_sc[rws, :].astype(BF16)

    h = h_sc[...]

    tril = (lax.broadcasted_iota(jnp.int32, (CHUNK, CHUNK), 0)
            >= lax.broadcasted_iota(jnp.int32, (CHUNK, CHUNK), 1))
    n_fill = HALF_SLABS * SCAN_PIECES // 4
    fill_cols = D_A // n_fill

    def fill_u(q):
        cols = slice(q * fill_cols, (q + 1) * fill_cols)
        a_sc[:, cols] = _dot(h, w_in_ref[:, w_u + q * fill_cols:w_u + (q + 1) * fill_cols])

    def fill_z(q):
        cols = slice(q * fill_cols, (q + 1) * fill_cols)
        z = _dot(h, w_in_ref[:, w_z + q * fill_cols:w_z + (q + 1) * fill_cols])
        a_sc[:, cols] = a_sc[:, cols] * _silu(z)

    def fill_mix(q):
        heads_per_fill = N_HEADS_A // n_fill
        for hd in range(q * heads_per_fill, (q + 1) * heads_per_fill):
            cols = slice(hd * HEAD_A, (hd + 1) * HEAD_A)
            w_t = jnp.where(tril, ws_ref[hd], 0.0).astype(BF16)
            v_h = jnp.concatenate([v_sc[b * CHUNK:(b + 1) * CHUNK, cols] for b in range(n_seq)], axis=1)
            mix = _dot(w_t, v_h) + bst_ref[:, hd:hd + 1]
            for b in range(n_seq):
                rws = slice(b * CHUNK, (b + 1) * CHUNK)
                oin_ref[rws, cols] = (a_sc[rws, cols] * mix[:, b * HEAD_A:(b + 1) * HEAD_A]).astype(BF16)

    def fill_zb(q):
        cols = slice(q * fill_cols, (q + 1) * fill_cols)
        a_sc[:, cols] = _silu(_dot(h, w_in_ref[:, w_zb + q * fill_cols:w_zb + (q + 1) * fill_cols]))

    fillers = [functools.partial(f, q) for f in (fill_u, fill_z, fill_mix, fill_zb) for q in range(n_fill)]

    first_half = lax.broadcasted_iota(jnp.int32, (N_STREAMS, LANES), 0) < n_seq

    def bu_piece(k, piece):
        t0 = piece * PIECE
        for half in range(2):
            j = half * HALF_SLABS + k
            lhs = jnp.concatenate(
                [xb_sc[b * CHUNK + t0:b * CHUNK + t0 + PIECE, j * LANES:(j + 1) * LANES]
                 for b in range(n_seq)], axis=0)
            bu = _dot(lhs, bmat_ref[j])
            for b in range(n_seq):
                r0 = (half * n_seq + b) * STREAM_PITCH + t0
                for l in range(2 * SLAB_TILES):
                    bu_sc[l, r0:r0 + PIECE, :] = bu[b * PIECE:(b + 1) * PIECE, l * LANES:(l + 1) * LANES]

    def scan_piece(k, piece, hr, hi, lam_r, lam_i):
        rows = []
        for t in range(piece * PIECE, (piece + 1) * PIECE):
            new_r, new_i = [], []
            for l in range(SLAB_TILES):
                br = bu_sc[l, pl.ds(t, N_STREAMS, stride=STREAM_PITCH), :]
                bi = bu_sc[SLAB_TILES + l, pl.ds(t, N_STREAMS, stride=STREAM_PITCH), :]
                new_r.append(lam_r[l] * hr[l] - lam_i[l] * hi[l] + br)
                new_i.append(lam_r[l] * hi[l] + lam_i[l] * hr[l] + bi)
            hr, hi = new_r, new_i
            rows.append(jnp.concatenate(hr + hi, axis=1))
        return hr, hi, jnp.concatenate(rows, axis=0)

    def c_piece(k, piece, hh):
        t0 = piece * PIECE
        rows = slice(t0 * N_STREAMS, (t0 + PIECE) * N_STREAMS)
        y2 = _dot(hh.astype(BF16), cboth_ref[k])
        y2_sc[0, rows, :] = y2[:, 0:LANES]
        y2_sc[1, rows, :] = y2[:, LANES:2 * LANES]
        for half in range(2):
            j = half * HALF_SLABS + k
            for b in range(n_seq):
                s = half * n_seq + b
                c_sc[b * CHUNK + t0:b * CHUNK + t0 + PIECE, j * LANES:(j + 1) * LANES] = (
                    y2_sc[half, pl.ds(t0 * N_STREAMS + s, PIECE, stride=N_STREAMS), :])

    for piece in range(SCAN_PIECES):
        bu_piece(0, piece)
    c_sc[...] = _dot(h, w_in_ref[:, w_v:w_v + D_A])
    v_sc[...] = _layernorm(c_sc[...], lng_ref[...], lnb_ref[...]).astype(BF16)
    for k in range(HALF_SLABS):
        tiles = [slice(l * LANES, (l + 1) * LANES) for l in range(SLAB_TILES)]
        lam_r = [jnp.where(first_half, lam_re_ref[k:k + 1, t], lam_re_ref[HALF_SLABS + k:HALF_SLABS + k + 1, t])
                 for t in tiles]
        lam_i = [jnp.where(first_half, lam_im_ref[k:k + 1, t], lam_im_ref[HALF_SLABS + k:HALF_SLABS + k + 1, t])
                 for t in tiles]
        hr = [st_sc[k, :, l * LANES:(l + 1) * LANES] for l in range(SLAB_TILES)]
        hi = [st_sc[k, :, (SLAB_TILES + l) * LANES:(SLAB_TILES + l + 1) * LANES] for l in range(SLAB_TILES)]
        for piece in range(SCAN_PIECES):
            hr, hi, hh = scan_piece(k, piece, hr, hi, lam_r, lam_i)
            c_piece(k, piece, hh)
            if k + 1 < HALF_SLABS:
                bu_piece(k + 1, piece)
            fillers[k * SCAN_PIECES + piece]()
        st_sc[k] = jnp.concatenate(hr + hi, axis=1)

    for k in range(HALF_SLABS):
        for half in range(2):
            j = half * HALF_SLABS + k
            for b in range(n_seq):
                s = half * n_seq + b
                st_re_ref[b, j:j + 1, :] = st_sc[k, s:s + 1, 0:SLAB_STATES]
                st_im_ref[b, j:j + 1, :] = st_sc[k, s:s + 1, SLAB_STATES:2 * SLAB_STATES]

    for j in range(N_SLABS):
        cols = slice(j * LANES, (j + 1) * LANES)
        y = c_sc[:, cols] + dskip_ref[:, cols] * b_sc[:, cols]
        g = _dot(y.astype(BF16), gmat_ref[j])
        val = g[:, 0:LANES] + bglu_ref[0:1, cols]
        gate = g[:, LANES:2 * LANES] + bglu_ref[1:2, cols]
        oin_ref[:, D_A + j * LANES:D_A + (j + 1) * LANES] = (
            val * jax.nn.sigmoid(gate) * a_sc[:, cols]).astype(BF16)


def _const_spec(shape):
    zeros = (0,) * len(shape)
    return pl.BlockSpec(shape, lambda i: zeros, pipeline_mode=pl.Buffered(1))


def _mixer(x, mod, g_pre, w_in_bf, ln_g, ln_b, w_s, bst, lam_re, lam_im, bmat, cboth, gmat,
           d_skip, b_glu_rows, h_sample, w_out):
    n_seq, seq, _ = x.shape
    assert 2 * n_seq == N_STREAMS and seq % CHUNK == 0
    rows = n_seq * CHUNK
    n_steps = seq // CHUNK
    wout_rows = w_out.shape[0] // n_steps
    assert wout_rows * n_steps == w_out.shape[0] and wout_rows % (2 * SUBLANES) == 0
    state_shape = (n_seq, N_SLABS, SLAB_STATES)
    n_sample = h_sample.shape[0]
    n_proj = D_IN // SAMPLE_PROJ_COLS
    assert n_proj <= n_steps
    consts = (mod, g_pre, w_in_bf, ln_g, ln_b, w_s, bst, lam_re, lam_im, bmat, cboth, gmat,
              d_skip, b_glu_rows, h_sample)
    return pl.pallas_call(
        functools.partial(_mixer_kernel, n_seq=n_seq),
        grid=(n_steps,),
        in_specs=[pl.BlockSpec((n_seq, CHUNK, D_MODEL), lambda i: (0, i, 0))]
        + [_const_spec(c.shape) for c in consts]
        + [pl.BlockSpec((wout_rows, w_out.shape[1]), lambda i: (i, 0))],
        out_specs=(
            pl.BlockSpec((rows, D_A + D_B), lambda i: (i, 0)),
            pl.BlockSpec(state_shape, lambda i: (0, 0, 0)),
            pl.BlockSpec(state_shape, lambda i: (0, 0, 0)),
            pl.BlockSpec((wout_rows, w_out.shape[1]), lambda i: (i, 0)),
            pl.BlockSpec((n_sample, SAMPLE_PROJ_COLS), lambda i: (0, jnp.minimum(i, n_proj - 1))),
        ),
        out_shape=(
            jax.ShapeDtypeStruct((n_steps * rows, D_A + D_B), BF16),
            jax.ShapeDtypeStruct(state_shape, F32),
            jax.ShapeDtypeStruct(state_shape, F32),
            jax.ShapeDtypeStruct(w_out.shape, BF16),
            jax.ShapeDtypeStruct((n_sample, D_IN), F32),
        ),
        scratch_shapes=[
            pltpu.VMEM((rows, D_MODEL), BF16),
            pltpu.VMEM((rows, D_A), F32),
            pltpu.VMEM((rows, D_A), F32),
            pltpu.VMEM((rows, D_A), F32),
            pltpu.VMEM((rows, D_A), BF16),
            pltpu.VMEM((rows, D_B), BF16),
            pltpu.VMEM((2 * SLAB_TILES, N_STREAMS * STREAM_PITCH, LANES), F32),
            pltpu.VMEM((2, N_STREAMS * CHUNK, LANES), F32),
            pltpu.VMEM((HALF_SLABS, N_STREAMS, 2 * SLAB_STATES), F32),
        ],
        compiler_params=pltpu.CompilerParams(
            dimension_semantics=("arbitrary",), vmem_limit_bytes=VMEM_LIMIT),
        name="mixer",
    )(x, *consts, w_out)


def _outproj_kernel(oin_ref, x_ref, mod_ref, gpost_ref, w_out_ref, y_ref, *, n_seq):
    r = _rms(_dot(oin_ref[...], w_out_ref[...]), gpost_ref[...])
    for b in range(n_seq):
        gate = mod_ref[b:b + 1, 2 * D_MODEL:3 * D_MODEL]
        y_ref[b] = x_ref[b] + gate * r[b * CHUNK:(b + 1) * CHUNK, :]


def _outproj(oin, x, mod, g_post, w_out_bf):
    n_seq, seq, _ = x.shape
    rows = n_seq * CHUNK
    return pl.pallas_call(
        functools.partial(_outproj_kernel, n_seq=n_seq),
        grid=(seq // CHUNK,),
        in_specs=[
            pl.BlockSpec((rows, D_A + D_B), lambda i: (i, 0)),
            pl.BlockSpec((n_seq, CHUNK, D_MODEL), lambda i: (0, i, 0)),
            _const_spec(mod.shape), _const_spec(g_post.shape), _const_spec(w_out_bf.shape),
        ],
        out_specs=pl.BlockSpec((n_seq, CHUNK, D_MODEL), lambda i: (0, i, 0)),
        out_shape=jax.ShapeDtypeStruct(x.shape, F32),
        compiler_params=pltpu.CompilerParams(
            dimension_semantics=("arbitrary",), vmem_limit_bytes=VMEM_LIMIT),
        name="outproj",
    )(oin, x, mod, g_post, w_out_bf)


def _sample_kernel(x2_ref, mod_ref, proj_ref, lng_ref, lnb_ref, ws_ref, bst_ref,
                   lam_re_ref, lam_im_ref, bmat_ref, cboth_ref, gmat_ref, dskip_ref, bglu_ref,
                   h0_re_ref, h0_im_ref, gpost_ref, w_out_ref,
                   y2_ref, v2_ref, hs_re_ref, hs_im_ref, oin_sc):
    n = proj_ref.shape[0]
    x_tiles = D_MODEL // LANES
    v_tiles = D_A // LANES
    w_u, w_v, w_z, w_xb, w_zb = 0, D_A, 2 * D_A, 3 * D_A, 3 * D_A + D_B

    v = _layernorm(proj_ref[:, w_v:w_v + D_A], lng_ref[...], lnb_ref[...])
    for q in range(v_tiles):
        v2_ref[pl.ds(q, n, stride=v_tiles), :] = v[:, q * LANES:(q + 1) * LANES]
    for hd in range(N_HEADS_A):
        cols = slice(hd * HEAD_A, (hd + 1) * HEAD_A)
        mix = ws_ref[hd, 0:1, 0:1] * v[:, cols] + bst_ref[0:1, hd:hd + 1]
        u = proj_ref[:, w_u + hd * HEAD_A:w_u + (hd + 1) * HEAD_A]
        z = proj_ref[:, w_z + hd * HEAD_A:w_z + (hd + 1) * HEAD_A]
        oin_sc[:, cols] = (u * mix * _silu(z)).astype(BF16)

    for j in range(N_SLABS):
        cols = slice(j * LANES, (j + 1) * LANES)
        st = slice(j * SLAB_STATES, (j + 1) * SLAB_STATES)
        xb = proj_ref[:, w_xb + j * LANES:w_xb + (j + 1) * LANES]
        zb = proj_ref[:, w_zb + j * LANES:w_zb + (j + 1) * LANES]
        bu = _dot(xb.astype(BF16), bmat_ref[j])
        lr = lam_re_ref[j:j + 1, :]
        li = lam_im_ref[j:j + 1, :]
        h0r = h0_re_ref[:, st]
        h0i = h0_im_ref[:, st]
        hr = lr * h0r - li * h0i + bu[:, 0:SLAB_STATES]
        hi = lr * h0i + li * h0r + bu[:, SLAB_STATES:2 * SLAB_STATES]
        hs_re_ref[:, st] = hr
        hs_im_ref[:, st] = hi
        k, half = j % HALF_SLABS, j // HALF_SLABS
        c_j = cboth_ref[k, :, half * LANES:(half + 1) * LANES]
        y = _dot(jnp.concatenate([hr, hi], axis=1).astype(BF16), c_j)
        y = y + dskip_ref[:, cols] * xb
        g = _dot(y.astype(BF16), gmat_ref[j])
        val = g[:, 0:LANES] + bglu_ref[0:1, cols]
        gt = g[:, LANES:2 * LANES] + bglu_ref[1:2, cols]
        oin_sc[:, D_A + j * LANES:D_A + (j + 1) * LANES] = (
            val * jax.nn.sigmoid(gt) * _silu(zb)).astype(BF16)

    o = _dot(oin_sc[...], w_out_ref[...])
    gate = mod_ref[:, 2 * D_MODEL:3 * D_MODEL]
    r = gate * _rms(o, gpost_ref[...])
    for q in range(x_tiles):
        cols = slice(q * LANES, (q + 1) * LANES)
        y2_ref[pl.ds(q, n, stride=x_tiles), :] = x2_ref[pl.ds(q, n, stride=x_tiles), :] + r[:, cols]


def _sample(x2, mod, proj, ln_g, ln_b, w_s, bst, lam_re, lam_im, bmat, cboth, gmat,
            d_skip, b_glu_rows, h0_re, h0_im, g_post, w_out_bf):
    n = proj.shape[0]
    n_state = N_GROUPS_B * P_STATE
    out_shapes = ((n * D_MODEL // LANES, LANES), (n * D_A // LANES, LANES), (n, n_state), (n, n_state))
    return pl.pallas_call(
        _sample_kernel,
        out_shape=tuple(jax.ShapeDtypeStruct(shp, F32) for shp in out_shapes),
        scratch_shapes=[pltpu.VMEM((n, D_A + D_B), BF16)],
        compiler_params=pltpu.CompilerParams(vmem_limit_bytes=VMEM_LIMIT),
        name="sample",
    )(x2, mod, proj, ln_g, ln_b, w_s, bst, lam_re, lam_im, bmat, cboth, gmat,
      d_skip, b_glu_rows, h0_re, h0_im, g_post, w_out_bf)


def kernel(x_prompt, x_sample, c_prompt, c_sample, state_b_re, state_b_im, w_c, b_c, g_pre, w_in,
           ln_v_g, ln_v_b, w_s, b_s, a_re, a_im, log_dt, b_re, b_im, c_re, c_im, d_skip, w_glu, b_glu,
           w_out, g_post):
    n_p = x_prompt.shape[0]
    n_s = x_sample.shape[0]
    assert x_sample.shape[1] == 1 and n_s % SUBLANES == 0

    bmat, cboth, gmat, lam_re, lam_im, b_glu_rows, bst = _s5_prep(
        a_re, a_im, log_dt, b_re, b_im, c_re, c_im, w_glu, b_glu, b_s)

    row = lambda v: v.reshape(1, -1)
    x2_sample = x_sample.reshape(n_s * D_MODEL // LANES, LANES)
    mod_s, mod_p, w_in_bf, h_sample = _adaln(c_sample, c_prompt, w_c, b_c, w_in, x2_sample, row(g_pre))

    s5 = (w_s, bst, lam_re, lam_im, bmat, cboth, gmat, row(d_skip), b_glu_rows)
    oin, st_re, st_im, w_out_bf, proj_s = _mixer(
        x_prompt, mod_p, row(g_pre), w_in_bf, row(ln_v_g), row(ln_v_b), *s5, h_sample, w_out)
    y_prompt = _outproj(oin, x_prompt, mod_p, row(g_post), w_out_bf)

    y_s, v_s, hs_re, hs_im = _sample(
        x2_sample, mod_s, proj_s, row(ln_v_g), row(ln_v_b), *s5,
        state_b_re.reshape(n_s, -1), state_b_im.reshape(n_s, -1), row(g_post), w_out_bf)

    return (y_prompt, y_s.reshape(n_s, 1, D_MODEL), v_s.reshape(n_s, 1, D_A),
            st_re.reshape(n_p, N_GROUPS_B, P_STATE), st_im.reshape(n_p, N_GROUPS_B, P_STATE),
            hs_re.reshape(n_s, N_GROUPS_B, P_STATE), hs_im.reshape(n_s, N_GROUPS_B, P_STATE))
```

```python
import functools

import jax
import jax.numpy as jnp
from jax import lax
from jax.experimental import pallas as pl
from jax.experimental.pallas import tpu as pltpu

F32 = jnp.float32
BF16 = jnp.bfloat16

EPS = 1e-6
D_MODEL = 2048
D_A = 1024
D_B = 1024
D_IN = 3 * D_A + 2 * D_B
CHUNK = 128
HEAD_A = 128
N_HEADS_A = D_A // HEAD_A
GROUP_B = 16
N_GROUPS_B = D_B // GROUP_B
P_STATE = 64

LANES = 128
SUBLANES = 8
SLAB_GROUPS = LANES // GROUP_B
N_SLABS = N_GROUPS_B // SLAB_GROUPS
SLAB_STATES = SLAB_GROUPS * P_STATE
SLAB_TILES = SLAB_STATES // LANES
N_STREAMS = SUBLANES
HALF_SLABS = N_SLABS // 2
STREAM_PITCH = CHUNK + SUBLANES
SCAN_PIECES = 4
PIECE = CHUNK // SCAN_PIECES
V7X_VMEM_BYTES = 64 * 1024 * 1024
VMEM_LIMIT = V7X_VMEM_BYTES - 2 * 1024 * 1024
SAMPLE_PROJ_COLS = 512


def _silu(x):
    return x * jax.nn.sigmoid(x)


def _rms_scale(x):
    return lax.rsqrt(jnp.mean(x * x, axis=-1, keepdims=True) + EPS)


def _rms(x, g):
    return x * _rms_scale(x) * g


def _layernorm(x, g, b):
    mu = jnp.mean(x, axis=-1, keepdims=True)
    xc = x - mu
    return xc * lax.rsqrt(jnp.mean(xc * xc, axis=-1, keepdims=True) + EPS) * g + b


def _dot(a, b):
    return jnp.dot(a, b, preferred_element_type=F32)


def _discretize(a_re, a_im, log_dt):
    dt = jnp.exp(log_dt)
    mag = jnp.exp(dt * a_re)
    abar_re = mag * jnp.cos(dt * a_im)
    abar_im = mag * jnp.sin(dt * a_im)
    return abar_re, abar_im


def _nt_dot(a, b):
    return lax.dot_general(a, b, (((1,), (1,)), ((), ())), preferred_element_type=F32)


def _bf16_terms(x):
    hi = x.astype(BF16)
    r1 = x - hi.astype(F32)
    mid = r1.astype(BF16)
    lo = (r1 - mid.astype(F32)).astype(BF16)
    return hi, mid, lo


def _prep_kernel(a8_ref, bt_re_ref, bt_im_ref, c_re_ref, c_im_ref, wglu_ref, bglu_t_ref, bs_ref,
                 bmat_ref, cboth_ref, gmat_ref, lam_re_ref, lam_im_ref, bglu_row_ref, bst_ref):
    a_re = a8_ref[0:N_SLABS, :]
    a_im = a8_ref[N_SLABS:2 * N_SLABS, :]
    abar_re, abar_im = _discretize(a_re, a_im, a8_ref[2 * N_SLABS:3 * N_SLABS, :])
    lam_re_ref[...] = abar_re
    lam_im_ref[...] = abar_im
    num_re = abar_re - 1.0
    num_im = abar_im
    den = a_re * a_re + a_im * a_im
    coef_re = (num_re * a_re + num_im * a_im) / den
    coef_im = (num_im * a_re - num_re * a_im) / den

    def rep_matrix(k, n, period, offset=0):
        row = lax.broadcasted_iota(jnp.int32, (k, n), 0)
        col = lax.broadcasted_iota(jnp.int32, (k, n), 1)
        return jnp.where((col & (period - 1)) + offset == row, 1.0, 0.0).astype(BF16)

    def rep_rows(m, k, period):
        row = lax.broadcasted_iota(jnp.int32, (m, k), 0)
        col = lax.broadcasted_iota(jnp.int32, (m, k), 1)
        return jnp.where((row & (period - 1)) == col, 1.0, 0.0).astype(BF16)

    def block_mask(m, n, row_shift, col_shift):
        row = lax.broadcasted_iota(jnp.int32, (m, n), 0)
        col = lax.broadcasted_iota(jnp.int32, (m, n), 1)
        return (row >> row_shift) == (col >> col_shift)

    rep_state = rep_matrix(P_STATE, SLAB_STATES, P_STATE)
    rows_state = rep_rows(SLAB_STATES, P_STATE, P_STATE)
    rep_val = rep_matrix(2 * GROUP_B, LANES, GROUP_B)
    rep_gate = rep_matrix(2 * GROUP_B, LANES, GROUP_B, GROUP_B)
    mask_b = block_mask(LANES, SLAB_STATES, 4, 6)
    mask_c = block_mask(SLAB_STATES, LANES, 6, 4)
    mask_g = block_mask(LANES, LANES, 4, 4)

    def spread(x, rep):
        return sum(_dot(term, rep) for term in _bf16_terms(x))

    for j in range(N_SLABS):
        crows = slice(j * LANES, (j + 1) * LANES)
        bt_re = spread(bt_re_ref[crows, :], rep_state)
        bt_im = spread(bt_im_ref[crows, :], rep_state)
        cr = coef_re[j:j + 1, :]
        ci = coef_im[j:j + 1, :]
        bmat_ref[j, :, 0:SLAB_STATES] = jnp.where(mask_b, cr * bt_re - ci * bt_im, 0.0).astype(BF16)
        bmat_ref[j, :, SLAB_STATES:2 * SLAB_STATES] = jnp.where(
            mask_b, cr * bt_im + ci * bt_re, 0.0).astype(BF16)

        k, half = j % HALF_SLABS, j // HALF_SLABS
        ccols = slice(half * LANES, (half + 1) * LANES)
        ct_re = _nt_dot(rows_state, c_re_ref[crows, :].astype(BF16))
        ct_im = _nt_dot(rows_state, c_im_ref[crows, :].astype(BF16))
        cboth_ref[k, 0:SLAB_STATES, ccols] = jnp.where(mask_c, ct_re, 0.0).astype(BF16)
        cboth_ref[k, SLAB_STATES:2 * SLAB_STATES, ccols] = jnp.where(mask_c, -ct_im, 0.0).astype(BF16)

        w = wglu_ref[crows, :].astype(BF16)
        gmat_ref[j, :, 0:LANES] = jnp.where(mask_g, _dot(w, rep_val), 0.0).astype(BF16)
        gmat_ref[j, :, LANES:2 * LANES] = jnp.where(mask_g, _dot(w, rep_gate), 0.0).astype(BF16)

    row = lax.broadcasted_iota(jnp.int32, (2 * GROUP_B, D_B), 0)
    col = lax.broadcasted_iota(jnp.int32, (2 * GROUP_B, D_B), 1)
    by_group = spread(bglu_t_ref[...], jnp.where(block_mask(N_GROUPS_B, D_B, 0, 4), 1.0, 0.0).astype(BF16))
    for i in range(2):
        mine = row == (col & (GROUP_B - 1)) + i * GROUP_B
        bglu_row_ref[i:i + 1, :] = jnp.sum(jnp.where(mine, by_group, 0.0), axis=0, keepdims=True)

    eye = rep_rows(CHUNK, CHUNK, CHUNK)
    bst_ref[...] = sum(_nt_dot(eye, term) for term in _bf16_terms(bs_ref[...]))


def _s5_prep(a_re, a_im, log_dt, b_re, b_im, c_re, c_im, w_glu, b_glu, b_s):
    g, p, c = N_GROUPS_B, P_STATE, GROUP_B
    a8 = jnp.stack([a_re, a_im, jnp.broadcast_to(log_dt[:, None], (g, p))]).reshape(3 * N_SLABS, SLAB_STATES)
    args = (a8, b_re.transpose(0, 2, 1).reshape(g * c, p), b_im.transpose(0, 2, 1).reshape(g * c, p),
            c_re.reshape(g * c, p), c_im.reshape(g * c, p), w_glu.reshape(g * c, 2 * c), b_glu.T, b_s)
    return pl.pallas_call(
        _prep_kernel,
        out_shape=(
            jax.ShapeDtypeStruct((N_SLABS, LANES, 2 * SLAB_STATES), BF16),
            jax.ShapeDtypeStruct((HALF_SLABS, 2 * SLAB_STATES, 2 * LANES), BF16),
            jax.ShapeDtypeStruct((N_SLABS, LANES, 2 * LANES), BF16),
            jax.ShapeDtypeStruct((N_SLABS, SLAB_STATES), F32),
            jax.ShapeDtypeStruct((N_SLABS, SLAB_STATES), F32),
            jax.ShapeDtypeStruct((2, D_B), F32),
            jax.ShapeDtypeStruct((CHUNK, N_HEADS_A), F32),
        ),
        name="s5_prep",
    )(*args)


ADALN_ROWS = 256
ADALN_PROMPT_ROWS = 2 * SUBLANES


def _adaln_kernel(cs_ref, cp_ref, w_ref, b_ref, w_in_ref, x2_ref, gpre_ref,
                  os_ref, op_ref, w_in_bf_ref, hs_ref):
    k = pl.program_id(0)

    @pl.when(k == 0)
    def _():
        os_ref[...] = jnp.broadcast_to(b_ref[...], os_ref.shape)
        op_ref[...] = jnp.broadcast_to(b_ref[...], op_ref.shape)

    w_in_bf_ref[...] = w_in_ref[...].astype(BF16)

    cols = pl.ds(pl.multiple_of(k * ADALN_ROWS, ADALN_ROWS), ADALN_ROWS)
    w = w_ref[...].astype(BF16)
    os_ref[...] += _dot(_silu(cs_ref[:, cols]).astype(BF16), w)
    n_p = cp_ref.shape[0]
    cp = jnp.concatenate([cp_ref[:, cols], jnp.zeros((ADALN_PROMPT_ROWS - n_p, ADALN_ROWS), F32)], axis=0)
    op_ref[...] += _dot(_silu(cp).astype(BF16), w)

    @pl.when(k == pl.num_programs(0) - 1)
    def _():
        x_tiles = D_MODEL // LANES
        n = hs_ref.shape[0]
        x = jnp.concatenate([x2_ref[pl.ds(q, n, stride=x_tiles), :] for q in range(x_tiles)], axis=1)
        shift = os_ref[:, 0:D_MODEL]
        scale = os_ref[:, D_MODEL:2 * D_MODEL]
        hs_ref[...] = (_rms(x, gpre_ref[...]) * (1.0 + scale) + shift).astype(BF16)


def _adaln(c_sample, c_prompt, w_c, b_c, w_in, x2_sample, g_pre):
    n_s, n_p = c_sample.shape[0], c_prompt.shape[0]
    assert n_p <= ADALN_PROMPT_ROWS and D_MODEL % ADALN_ROWS == 0
    return pl.pallas_call(
        _adaln_kernel,
        grid=(D_MODEL // ADALN_ROWS,),
        in_specs=[
            pl.BlockSpec((n_s, D_MODEL), lambda k: (0, 0)),
            pl.BlockSpec((n_p, D_MODEL), lambda k: (0, 0)),
            pl.BlockSpec((ADALN_ROWS, 3 * D_MODEL), lambda k: (k, 0)),
            pl.BlockSpec((1, 3 * D_MODEL), lambda k: (0, 0)),
            pl.BlockSpec((ADALN_ROWS, D_IN), lambda k: (k, 0)),
            pl.BlockSpec(x2_sample.shape, lambda k: (0, 0)),
            pl.BlockSpec(g_pre.shape, lambda k: (0, 0)),
        ],
        out_specs=(pl.BlockSpec((n_s, 3 * D_MODEL), lambda k: (0, 0)),
                   pl.BlockSpec((ADALN_PROMPT_ROWS, 3 * D_MODEL), lambda k: (0, 0)),
                   pl.BlockSpec((ADALN_ROWS, D_IN), lambda k: (k, 0)),
                   pl.BlockSpec((n_s, D_MODEL), lambda k: (0, 0))),
        out_shape=(jax.ShapeDtypeStruct((n_s, 3 * D_MODEL), F32),
                   jax.ShapeDtypeStruct((ADALN_PROMPT_ROWS, 3 * D_MODEL), F32),
                   jax.ShapeDtypeStruct((D_MODEL, D_IN), BF16),
                   jax.ShapeDtypeStruct((n_s, D_MODEL), BF16)),
        compiler_params=pltpu.CompilerParams(
            dimension_semantics=("arbitrary",), vmem_limit_bytes=VMEM_LIMIT),
        name="adaln",
    )(c_sample, c_prompt, w_c, b_c.reshape(1, -1), w_in, x2_sample, g_pre)


def _mixer_kernel(x_ref, mod_ref, gpre_ref, w_in_ref, lng_ref, lnb_ref, ws_ref, bst_ref,
                  lam_re_ref, lam_im_ref, bmat_ref, cboth_ref, gmat_ref, dskip_ref, bglu_ref, hs_ref, wout_ref,
                  oin_ref, st_re_ref, st_im_ref, wout_bf_ref, ps_ref,
                  h_sc, a_sc, b_sc, c_sc, v_sc, xb_sc, bu_sc, y2_sc, st_sc, *, n_seq):
    step = pl.program_id(0)

    @pl.when(step == 0)
    def _():
        st_sc[...] = jnp.zeros_like(st_sc)

    wout_bf_ref[...] = wout_ref[...].astype(BF16)

    @pl.when(step < D_IN // SAMPLE_PROJ_COLS)
    def _():
        cols = pl.ds(pl.multiple_of(step * SAMPLE_PROJ_COLS, SAMPLE_PROJ_COLS), SAMPLE_PROJ_COLS)
        ps_ref[...] = _dot(hs_ref[...], w_in_ref[:, cols])

    w_u, w_v, w_z, w_xb, w_zb = 0, D_A, 2 * D_A, 3 * D_A, 3 * D_A + D_B

    for b in range(n_seq):
        x = x_ref[b]
        shift = mod_ref[b:b + 1, 0:D_MODEL]
        gain = gpre_ref[...] * (1.0 + mod_ref[b:b + 1, D_MODEL:2 * D_MODEL])
        rws = slice(b * CHUNK, (b + 1) * CHUNK)
        h_sc[rws, :] = (x * _rms_scale(x) * gain + shift).astype(BF16)
        b_sc[rws, :] = _dot(h_sc[rws, :], w_in_ref[:, w_xb:w_xb + D_B])
        xb_sc[rws, :] = b_sc[rws, :].astype(BF16)

    h = h_sc[...]

    tril = (lax.broadcasted_iota(jnp.int32, (CHUNK, CHUNK), 0)
            >= lax.broadcasted_iota(jnp.int32, (CHUNK, CHUNK), 1))
    n_fill = HALF_SLABS * SCAN_PIECES // 4
    fill_cols = D_A // n_fill

    def fill_u(q):
        cols = slice(q * fill_cols, (q + 1) * fill_cols)
        a_sc[:, cols] = _dot(h, w_in_ref[:, w_u + q * fill_cols:w_u + (q + 1) * fill_cols])

    def fill_z(q):
        cols = slice(q * fill_cols, (q + 1) * fill_cols)
        z = _dot(h, w_in_ref[:, w_z + q * fill_cols:w_z + (q + 1) * fill_cols])
        a_sc[:, cols] = a_sc[:, cols] * _silu(z)

    def fill_mix(q):
        heads_per_fill = N_HEADS_A // n_fill
        for hd in range(q * heads_per_fill, (q + 1) * heads_per_fill):
            cols = slice(hd * HEAD_A, (hd + 1) * HEAD_A)
            w_t = jnp.where(tril, ws_ref[hd], 0.0).astype(BF16)
            v_h = jnp.concatenate([v_sc[b * CHUNK:(b + 1) * CHUNK, cols] for b in range(n_seq)], axis=1)
            mix = _dot(w_t, v_h) + bst_ref[:, hd:hd + 1]
            for b in range(n_seq):
                rws = slice(b * CHUNK, (b + 1) * CHUNK)
                oin_ref[rws, cols] = (a_sc[rws, cols] * mix[:, b * HEAD_A:(b + 1) * HEAD_A]).astype(BF16)

    def fill_zb(q):
        cols = slice(q * fill_cols, (q + 1) * fill_cols)
        a_sc[:, cols] = _silu(_dot(h, w_in_ref[:, w_zb + q * fill_cols:w_zb + (q + 1) * fill_cols]))

    fillers = [functools.partial(f, q) for f, q in (
        (fill_u, 0), (fill_z, 0), (fill_u, 1), (fill_mix, 0), (fill_z, 1), (fill_u, 2), (fill_z, 2), (fill_mix, 1),
        (fill_u, 3), (fill_z, 3), (fill_zb, 0), (fill_mix, 2), (fill_zb, 1), (fill_zb, 2), (fill_mix, 3), (fill_zb, 3))]
    assert len(fillers) == HALF_SLABS * SCAN_PIECES

    first_half = lax.broadcasted_iota(jnp.int32, (N_STREAMS, LANES), 0) < n_seq

    def bu_piece(k, piece):
        t0 = piece * PIECE
        for half in range(2):
            j = half * HALF_SLABS + k
            lhs = jnp.concatenate(
                [xb_sc[b * CHUNK + t0:b * CHUNK + t0 + PIECE, j * LANES:(j + 1) * LANES]
                 for b in range(n_seq)], axis=0)
            bu = _dot(lhs, bmat_ref[j])
            for b in range(n_seq):
                r0 = (half * n_seq + b) * STREAM_PITCH + t0
                for l in range(2 * SLAB_TILES):
                    bu_sc[l, r0:r0 + PIECE, :] = bu[b * PIECE:(b + 1) * PIECE, l * LANES:(l + 1) * LANES]

    def scan_piece(k, piece, hr, hi, lam_r, lam_i):
        rows = []
        for t in range(piece * PIECE, (piece + 1) * PIECE):
            new_r, new_i = [], []
            for l in range(SLAB_TILES):
                br = bu_sc[l, pl.ds(t, N_STREAMS, stride=STREAM_PITCH), :]
                bi = bu_sc[SLAB_TILES + l, pl.ds(t, N_STREAMS, stride=STREAM_PITCH), :]
                new_r.append(lam_r[l] * hr[l] - lam_i[l] * hi[l] + br)
                new_i.append(lam_r[l] * hi[l] + lam_i[l] * hr[l] + bi)
            hr, hi = new_r, new_i
            rows.append(jnp.concatenate(hr + hi, axis=1))
        return hr, hi, jnp.concatenate(rows, axis=0)

    def c_piece(k, piece, hh):
        t0 = piece * PIECE
        rows = slice(t0 * N_STREAMS, (t0 + PIECE) * N_STREAMS)
        y2 = _dot(hh.astype(BF16), cboth_ref[k])
        y2_sc[0, rows, :] = y2[:, 0:LANES]
        y2_sc[1, rows, :] = y2[:, LANES:2 * LANES]
        for half in range(2):
            j = half * HALF_SLABS + k
            for b in range(n_seq):
                s = half * n_seq + b
                c_sc[b * CHUNK + t0:b * CHUNK + t0 + PIECE, j * LANES:(j + 1) * LANES] = (
                    y2_sc[half, pl.ds(t0 * N_STREAMS + s, PIECE, stride=N_STREAMS), :])

    for piece in range(SCAN_PIECES):
        bu_piece(0, piece)
    c_sc[...] = _dot(h, w_in_ref[:, w_v:w_v + D_A])
    v_sc[...] = _layernorm(c_sc[...], lng_ref[...], lnb_ref[...]).astype(BF16)
    for k in range(HALF_SLABS):
        tiles = [slice(l * LANES, (l + 1) * LANES) for l in range(SLAB_TILES)]
        lam_r = [jnp.where(first_half, lam_re_ref[k:k + 1, t], lam_re_ref[HALF_SLABS + k:HALF_SLABS + k + 1, t])
                 for t in tiles]
        lam_i = [jnp.where(first_half, lam_im_ref[k:k + 1, t], lam_im_ref[HALF_SLABS + k:HALF_SLABS + k + 1, t])
                 for t in tiles]
        hr = [st_sc[k, :, l * LANES:(l + 1) * LANES] for l in range(SLAB_TILES)]
        hi = [st_sc[k, :, (SLAB_TILES + l) * LANES:(SLAB_TILES + l + 1) * LANES] for l in range(SLAB_TILES)]
        for piece in range(SCAN_PIECES):
            hr, hi, hh = scan_piece(k, piece, hr, hi, lam_r, lam_i)
            c_piece(k, piece, hh)
            if k + 1 < HALF_SLABS:
                bu_piece(k + 1, piece)
            fillers[k * SCAN_PIECES + piece]()
        st_sc[k] = jnp.concatenate(hr + hi, axis=1)

    for k in range(HALF_SLABS):
        for half in range(2):
            j = half * HALF_SLABS + k
            for b in range(n_seq):
                s = half * n_seq + b
                st_re_ref[b, j:j + 1, :] = st_sc[k, s:s + 1, 0:SLAB_STATES]
                st_im_ref[b, j:j + 1, :] = st_sc[k, s:s + 1, SLAB_STATES:2 * SLAB_STATES]

    for j in range(N_SLABS):
        cols = slice(j * LANES, (j + 1) * LANES)
        y = c_sc[:, cols] + dskip_ref[:, cols] * b_sc[:, cols]
        g = _dot(y.astype(BF16), gmat_ref[j])
        val = g[:, 0:LANES] + bglu_ref[0:1, cols]
        gate = g[:, LANES:2 * LANES] + bglu_ref[1:2, cols]
        oin_ref[:, D_A + j * LANES:D_A + (j + 1) * LANES] = (
            val * jax.nn.sigmoid(gate) * a_sc[:, cols]).astype(BF16)


def _const_spec(shape):
    zeros = (0,) * len(shape)
    return pl.BlockSpec(shape, lambda i: zeros, pipeline_mode=pl.Buffered(1))


def _mixer(x, mod, g_pre, w_in_bf, ln_g, ln_b, w_s, bst, lam_re, lam_im, bmat, cboth, gmat,
           d_skip, b_glu_rows, h_sample, w_out):
    n_seq, seq, _ = x.shape
    assert 2 * n_seq == N_STREAMS and seq % CHUNK == 0
    rows = n_seq * CHUNK
    n_steps = seq // CHUNK
    wout_rows = w_out.shape[0] // n_steps
    assert wout_rows * n_steps == w_out.shape[0] and wout_rows % (2 * SUBLANES) == 0
    state_shape = (n_seq, N_SLABS, SLAB_STATES)
    n_sample = h_sample.shape[0]
    n_proj = D_IN // SAMPLE_PROJ_COLS
    assert n_proj <= n_steps
    consts = (mod, g_pre, w_in_bf, ln_g, ln_b, w_s, bst, lam_re, lam_im, bmat, cboth, gmat,
              d_skip, b_glu_rows, h_sample)
    return pl.pallas_call(
        functools.partial(_mixer_kernel, n_seq=n_seq),
        grid=(n_steps,),
        in_specs=[pl.BlockSpec((n_seq, CHUNK, D_MODEL), lambda i: (0, i, 0))]
        + [_const_spec(c.shape) for c in consts]
        + [pl.BlockSpec((wout_rows, w_out.shape[1]), lambda i: (i, 0))],
        out_specs=(
            pl.BlockSpec((rows, D_A + D_B), lambda i: (i, 0)),
            pl.BlockSpec(state_shape, lambda i: (0, 0, 0)),
            pl.BlockSpec(state_shape, lambda i: (0, 0, 0)),
            pl.BlockSpec((wout_rows, w_out.shape[1]), lambda i: (i, 0)),
            pl.BlockSpec((n_sample, SAMPLE_PROJ_COLS), lambda i: (0, jnp.minimum(i, n_proj - 1))),
        ),
        out_shape=(
            jax.ShapeDtypeStruct((n_steps * rows, D_A + D_B), BF16),
            jax.ShapeDtypeStruct(state_shape, F32),
            jax.ShapeDtypeStruct(state_shape, F32),
            jax.ShapeDtypeStruct(w_out.shape, BF16),
            jax.ShapeDtypeStruct((n_sample, D_IN), F32),
        ),
        scratch_shapes=[
            pltpu.VMEM((rows, D_MODEL), BF16),
            pltpu.VMEM((rows, D_A), F32),
            pltpu.VMEM((rows, D_A), F32),
            pltpu.VMEM((rows, D_A), F32),
            pltpu.VMEM((rows, D_A), BF16),
            pltpu.VMEM((rows, D_B), BF16),
            pltpu.VMEM((2 * SLAB_TILES, N_STREAMS * STREAM_PITCH, LANES), F32),
            pltpu.VMEM((2, N_STREAMS * CHUNK, LANES), F32),
            pltpu.VMEM((HALF_SLABS, N_STREAMS, 2 * SLAB_STATES), F32),
        ],
        compiler_params=pltpu.CompilerParams(
            dimension_semantics=("arbitrary",), vmem_limit_bytes=VMEM_LIMIT),
        name="mixer",
    )(x, *consts, w_out)


def _outproj_kernel(oin_ref, x_ref, mod_ref, gpost_ref, w_out_ref, y_ref, *, n_seq):
    o = _dot(oin_ref[...], w_out_ref[...])
    r = o * _rms_scale(o)
    for b in range(n_seq):
        gain = gpost_ref[...] * mod_ref[b:b + 1, 2 * D_MODEL:3 * D_MODEL]
        y_ref[b] = x_ref[b] + r[b * CHUNK:(b + 1) * CHUNK, :] * gain


def _outproj(oin, x, mod, g_post, w_out_bf):
    n_seq, seq, _ = x.shape
    rows = n_seq * CHUNK
    return pl.pallas_call(
        functools.partial(_outproj_kernel, n_seq=n_seq),
        grid=(seq // CHUNK,),
        in_specs=[
            pl.BlockSpec((rows, D_A + D_B), lambda i: (i, 0)),
            pl.BlockSpec((n_seq, CHUNK, D_MODEL), lambda i: (0, i, 0)),
            _const_spec(mod.shape), _const_spec(g_post.shape), _const_spec(w_out_bf.shape),
        ],
        out_specs=pl.BlockSpec((n_seq, CHUNK, D_MODEL), lambda i: (0, i, 0)),
        out_shape=jax.ShapeDtypeStruct(x.shape, F32),
        compiler_params=pltpu.CompilerParams(
            dimension_semantics=("arbitrary",), vmem_limit_bytes=VMEM_LIMIT),
        name="outproj",
    )(oin, x, mod, g_post, w_out_bf)


def _sample_kernel(x2_ref, mod_ref, proj_ref, lng_ref, lnb_ref, ws_ref, bst_ref,
                   lam_re_ref, lam_im_ref, bmat_ref, cboth_ref, gmat_ref, dskip_ref, bglu_ref,
                   h0_re_ref, h0_im_ref, gpost_ref, w_out_ref,
                   y2_ref, v2_ref, hs_re_ref, hs_im_ref, oin_sc):
    n = proj_ref.shape[0]
    x_tiles = D_MODEL // LANES
    v_tiles = D_A // LANES
    w_u, w_v, w_z, w_xb, w_zb = 0, D_A, 2 * D_A, 3 * D_A, 3 * D_A + D_B

    v = _layernorm(proj_ref[:, w_v:w_v + D_A], lng_ref[...], lnb_ref[...])
    for q in range(v_tiles):
        v2_ref[pl.ds(q, n, stride=v_tiles), :] = v[:, q * LANES:(q + 1) * LANES]
    for hd in range(N_HEADS_A):
        cols = slice(hd * HEAD_A, (hd + 1) * HEAD_A)
        mix = ws_ref[hd, 0:1, 0:1] * v[:, cols] + bst_ref[0:1, hd:hd + 1]
        u = proj_ref[:, w_u + hd * HEAD_A:w_u + (hd + 1) * HEAD_A]
        z = proj_ref[:, w_z + hd * HEAD_A:w_z + (hd + 1) * HEAD_A]
        oin_sc[:, cols] = (u * mix * _silu(z)).astype(BF16)

    for j in range(N_SLABS):
        cols = slice(j * LANES, (j + 1) * LANES)
        st = slice(j * SLAB_STATES, (j + 1) * SLAB_STATES)
        xb = proj_ref[:, w_xb + j * LANES:w_xb + (j + 1) * LANES]
        zb = proj_ref[:, w_zb + j * LANES:w_zb + (j + 1) * LANES]
        bu = _dot(xb.astype(BF16), bmat_ref[j])
        lr = lam_re_ref[j:j + 1, :]
        li = lam_im_ref[j:j + 1, :]
        h0r = h0_re_ref[:, st]
        h0i = h0_im_ref[:, st]
        hr = lr * h0r - li * h0i + bu[:, 0:SLAB_STATES]
        hi = lr * h0i + li * h0r + bu[:, SLAB_STATES:2 * SLAB_STATES]
        hs_re_ref[:, st] = hr
        hs_im_ref[:, st] = hi
        k, half = j % HALF_SLABS, j // HALF_SLABS
        c_j = cboth_ref[k, :, half * LANES:(half + 1) * LANES]
        y = _dot(jnp.concatenate([hr, hi], axis=1).astype(BF16), c_j)
        y = y + dskip_ref[:, cols] * xb
        g = _dot(y.astype(BF16), gmat_ref[j])
        val = g[:, 0:LANES] + bglu_ref[0:1, cols]
        gt = g[:, LANES:2 * LANES] + bglu_ref[1:2, cols]
        oin_sc[:, D_A + j * LANES:D_A + (j + 1) * LANES] = (
            val * jax.nn.sigmoid(gt) * _silu(zb)).astype(BF16)

    o = _dot(oin_sc[...], w_out_ref[...])
    gate = mod_ref[:, 2 * D_MODEL:3 * D_MODEL]
    r = gate * _rms(o, gpost_ref[...])
    for q in range(x_tiles):
        cols = slice(q * LANES, (q + 1) * LANES)
        y2_ref[pl.ds(q, n, stride=x_tiles), :] = x2_ref[pl.ds(q, n, stride=x_tiles), :] + r[:, cols]


def _sample(x2, mod, proj, ln_g, ln_b, w_s, bst, lam_re, lam_im, bmat, cboth, gmat,
            d_skip, b_glu_rows, h0_re, h0_im, g_post, w_out_bf):
    n = proj.shape[0]
    n_state = N_GROUPS_B * P_STATE
    out_shapes = ((n * D_MODEL // LANES, LANES), (n * D_A // LANES, LANES), (n, n_state), (n, n_state))
    return pl.pallas_call(
        _sample_kernel,
        out_shape=tuple(jax.ShapeDtypeStruct(shp, F32) for shp in out_shapes),
        scratch_shapes=[pltpu.VMEM((n, D_A + D_B), BF16)],
        compiler_params=pltpu.CompilerParams(vmem_limit_bytes=VMEM_LIMIT),
        name="sample",
    )(x2, mod, proj, ln_g, ln_b, w_s, bst, lam_re, lam_im, bmat, cboth, gmat,
      d_skip, b_glu_rows, h0_re, h0_im, g_post, w_out_bf)


def kernel(x_prompt, x_sample, c_prompt, c_sample, state_b_re, state_b_im, w_c, b_c, g_pre, w_in,
           ln_v_g, ln_v_b, w_s, b_s, a_re, a_im, log_dt, b_re, b_im, c_re, c_im, d_skip, w_glu, b_glu,
           w_out, g_post):
    n_p = x_prompt.shape[0]
    n_s = x_sample.shape[0]
    assert x_sample.shape[1] == 1 and n_s % SUBLANES == 0

    bmat, cboth, gmat, lam_re, lam_im, b_glu_rows, bst = _s5_prep(
        a_re, a_im, log_dt, b_re, b_im, c_re, c_im, w_glu, b_glu, b_s)

    row = lambda v: v.reshape(1, -1)
    x2_sample = x_sample.reshape(n_s * D_MODEL // LANES, LANES)
    mod_s, mod_p, w_in_bf, h_sample = _adaln(c_sample, c_prompt, w_c, b_c, w_in, x2_sample, row(g_pre))

    s5 = (w_s, bst, lam_re, lam_im, bmat, cboth, gmat, row(d_skip), b_glu_rows)
    oin, st_re, st_im, w_out_bf, proj_s = _mixer(
        x_prompt, mod_p, row(g_pre), w_in_bf, row(ln_v_g), row(ln_v_b), *s5, h_sample, w_out)
    y_prompt = _outproj(oin, x_prompt, mod_p, row(g_post), w_out_bf)

    y_s, v_s, hs_re, hs_im = _sample(
        x2_sample, mod_s, proj_s, row(ln_v_g), row(ln_v_b), *s5,
        state_b_re.reshape(n_s, -1), state_b_im.reshape(n_s, -1), row(g_post), w_out_bf)

    return (y_prompt, y_s.reshape(n_s, 1, D_MODEL), v_s.reshape(n_s, 1, D_A),
            st_re.reshape(n_p, N_GROUPS_B, P_STATE), st_im.reshape(n_p, N_GROUPS_B, P_STATE),
            hs_re.reshape(n_s, N_GROUPS_B, P_STATE), hs_im.reshape(n_s, N_GROUPS_B, P_STATE))
```

```python
import functools

import jax
import jax.numpy as jnp
from jax import lax
from jax.experimental import pallas as pl
from jax.experimental.pallas import tpu as pltpu

F32 = jnp.float32
BF16 = jnp.bfloat16

EPS = 1e-6
D_MODEL = 2048
D_A = 1024
D_B = 1024
D_IN = 3 * D_A + 2 * D_B
CHUNK = 128
HEAD_A = 128
N_HEADS_A = D_A // HEAD_A
GROUP_B = 16
N_GROUPS_B = D_B // GROUP_B
P_STATE = 64

LANES = 128
SUBLANES = 8
SLAB_GROUPS = LANES // GROUP_B
N_SLABS = N_GROUPS_B // SLAB_GROUPS
SLAB_STATES = SLAB_GROUPS * P_STATE
SLAB_TILES = SLAB_STATES // LANES
N_STREAMS = SUBLANES
HALF_SLABS = N_SLABS // 2
STREAM_PITCH = CHUNK + SUBLANES
SCAN_PIECES = 4
PIECE = CHUNK // SCAN_PIECES
V7X_VMEM_BYTES = 64 * 1024 * 1024
VMEM_LIMIT = V7X_VMEM_BYTES - 2 * 1024 * 1024
SAMPLE_PROJ_COLS = 512


def _silu(x):
    return x * jax.nn.sigmoid(x)


def _rms_scale(x):
    return lax.rsqrt(jnp.mean(x * x, axis=-1, keepdims=True) + EPS)


def _rms(x, g):
    return x * _rms_scale(x) * g


def _layernorm(x, g, b):
    mu = jnp.mean(x, axis=-1, keepdims=True)
    xc = x - mu
    return xc * lax.rsqrt(jnp.mean(xc * xc, axis=-1, keepdims=True) + EPS) * g + b


def _dot(a, b):
    return jnp.dot(a, b, preferred_element_type=F32)


def _discretize(a_re, a_im, log_dt):
    dt = jnp.exp(log_dt)
    mag = jnp.exp(dt * a_re)
    abar_re = mag * jnp.cos(dt * a_im)
    abar_im = mag * jnp.sin(dt * a_im)
    return abar_re, abar_im


def _nt_dot(a, b):
    return lax.dot_general(a, b, (((1,), (1,)), ((), ())), preferred_element_type=F32)


def _bf16_terms(x):
    hi = x.astype(BF16)
    r1 = x - hi.astype(F32)
    mid = r1.astype(BF16)
    lo = (r1 - mid.astype(F32)).astype(BF16)
    return hi, mid, lo


def _prep_kernel(a8_ref, bt_re_ref, bt_im_ref, c_re_ref, c_im_ref, wglu_ref, bglu_t_ref, bs_ref,
                 bmat_ref, cboth_ref, gmat_ref, lam_re_ref, lam_im_ref, bglu_row_ref, bst_ref):
    a_re = a8_ref[0:N_SLABS, :]
    a_im = a8_ref[N_SLABS:2 * N_SLABS, :]
    abar_re, abar_im = _discretize(a_re, a_im, a8_ref[2 * N_SLABS:3 * N_SLABS, :])
    lam_re_ref[...] = abar_re
    lam_im_ref[...] = abar_im
    num_re = abar_re - 1.0
    num_im = abar_im
    den = a_re * a_re + a_im * a_im
    coef_re = (num_re * a_re + num_im * a_im) / den
    coef_im = (num_im * a_re - num_re * a_im) / den

    def rep_matrix(k, n, period, offset=0):
        row = lax.broadcasted_iota(jnp.int32, (k, n), 0)
        col = lax.broadcasted_iota(jnp.int32, (k, n), 1)
        return jnp.where((col & (period - 1)) + offset == row, 1.0, 0.0).astype(BF16)

    def rep_rows(m, k, period):
        row = lax.broadcasted_iota(jnp.int32, (m, k), 0)
        col = lax.broadcasted_iota(jnp.int32, (m, k), 1)
        return jnp.where((row & (period - 1)) == col, 1.0, 0.0).astype(BF16)

    def block_mask(m, n, row_shift, col_shift):
        row = lax.broadcasted_iota(jnp.int32, (m, n), 0)
        col = lax.broadcasted_iota(jnp.int32, (m, n), 1)
        return (row >> row_shift) == (col >> col_shift)

    rep_state = rep_matrix(P_STATE, SLAB_STATES, P_STATE)
    rows_state = rep_rows(SLAB_STATES, P_STATE, P_STATE)
    rep_val = rep_matrix(2 * GROUP_B, LANES, GROUP_B)
    rep_gate = rep_matrix(2 * GROUP_B, LANES, GROUP_B, GROUP_B)
    mask_b = block_mask(LANES, SLAB_STATES, 4, 6)
    mask_c = block_mask(SLAB_STATES, LANES, 6, 4)
    mask_g = block_mask(LANES, LANES, 4, 4)

    def spread(x, rep):
        return sum(_dot(term, rep) for term in _bf16_terms(x))

    for j in range(N_SLABS):
        crows = slice(j * LANES, (j + 1) * LANES)
        bt_re = spread(bt_re_ref[crows, :], rep_state)
        bt_im = spread(bt_im_ref[crows, :], rep_state)
        cr = coef_re[j:j + 1, :]
        ci = coef_im[j:j + 1, :]
        bmat_ref[j, :, 0:SLAB_STATES] = jnp.where(mask_b, cr * bt_re - ci * bt_im, 0.0).astype(BF16)
        bmat_ref[j, :, SLAB_STATES:2 * SLAB_STATES] = jnp.where(
            mask_b, cr * bt_im + ci * bt_re, 0.0).astype(BF16)

        k, half = j % HALF_SLABS, j // HALF_SLABS
        ccols = slice(half * LANES, (half + 1) * LANES)
        ct_re = _nt_dot(rows_state, c_re_ref[crows, :].astype(BF16))
        ct_im = _nt_dot(rows_state, c_im_ref[crows, :].astype(BF16))
        cboth_ref[k, 0:SLAB_STATES, ccols] = jnp.where(mask_c, ct_re, 0.0).astype(BF16)
        cboth_ref[k, SLAB_STATES:2 * SLAB_STATES, ccols] = jnp.where(mask_c, -ct_im, 0.0).astype(BF16)

        w = wglu_ref[crows, :].astype(BF16)
        gmat_ref[j, :, 0:LANES] = jnp.where(mask_g, _dot(w, rep_val), 0.0).astype(BF16)
        gmat_ref[j, :, LANES:2 * LANES] = jnp.where(mask_g, _dot(w, rep_gate), 0.0).astype(BF16)

    row = lax.broadcasted_iota(jnp.int32, (2 * GROUP_B, D_B), 0)
    col = lax.broadcasted_iota(jnp.int32, (2 * GROUP_B, D_B), 1)
    by_group = spread(bglu_t_ref[...], jnp.where(block_mask(N_GROUPS_B, D_B, 0, 4), 1.0, 0.0).astype(BF16))
    for i in range(2):
        mine = row == (col & (GROUP_B - 1)) + i * GROUP_B
        bglu_row_ref[i:i + 1, :] = jnp.sum(jnp.where(mine, by_group, 0.0), axis=0, keepdims=True)

    eye = rep_rows(CHUNK, CHUNK, CHUNK)
    bst_ref[...] = sum(_nt_dot(eye, term) for term in _bf16_terms(bs_ref[...]))


def _s5_prep(a_re, a_im, log_dt, b_re, b_im, c_re, c_im, w_glu, b_glu, b_s):
    g, p, c = N_GROUPS_B, P_STATE, GROUP_B
    a8 = jnp.stack([a_re, a_im, jnp.broadcast_to(log_dt[:, None], (g, p))]).reshape(3 * N_SLABS, SLAB_STATES)
    args = (a8, b_re.transpose(0, 2, 1).reshape(g * c, p), b_im.transpose(0, 2, 1).reshape(g * c, p),
            c_re.reshape(g * c, p), c_im.reshape(g * c, p), w_glu.reshape(g * c, 2 * c), b_glu.T, b_s)
    return pl.pallas_call(
        _prep_kernel,
        out_shape=(
            jax.ShapeDtypeStruct((N_SLABS, LANES, 2 * SLAB_STATES), BF16),
            jax.ShapeDtypeStruct((HALF_SLABS, 2 * SLAB_STATES, 2 * LANES), BF16),
            jax.ShapeDtypeStruct((N_SLABS, LANES, 2 * LANES), BF16),
            jax.ShapeDtypeStruct((N_SLABS, SLAB_STATES), F32),
            jax.ShapeDtypeStruct((N_SLABS, SLAB_STATES), F32),
            jax.ShapeDtypeStruct((2, D_B), F32),
            jax.ShapeDtypeStruct((CHUNK, N_HEADS_A), F32),
        ),
        name="s5_prep",
    )(*args)


ADALN_ROWS = 256
ADALN_PROMPT_ROWS = 2 * SUBLANES


def _adaln_kernel(cs_ref, cp_ref, w_ref, b_ref, w_in_ref, x2_ref, gpre_ref,
                  os_ref, op_ref, w_in_bf_ref, hs_ref):
    k = pl.program_id(0)

    @pl.when(k == 0)
    def _():
        os_ref[...] = jnp.broadcast_to(b_ref[...], os_ref.shape)
        op_ref[...] = jnp.broadcast_to(b_ref[...], op_ref.shape)

    w_in_bf_ref[...] = w_in_ref[...].astype(BF16)

    cols = pl.ds(pl.multiple_of(k * ADALN_ROWS, ADALN_ROWS), ADALN_ROWS)
    w = w_ref[...].astype(BF16)
    os_ref[...] += _dot(_silu(cs_ref[:, cols]).astype(BF16), w)
    n_p = cp_ref.shape[0]
    cp = jnp.concatenate([cp_ref[:, cols], jnp.zeros((ADALN_PROMPT_ROWS - n_p, ADALN_ROWS), F32)], axis=0)
    op_ref[...] += _dot(_silu(cp).astype(BF16), w)

    @pl.when(k == pl.num_programs(0) - 1)
    def _():
        x_tiles = D_MODEL // LANES
        n = hs_ref.shape[0]
        x = jnp.concatenate([x2_ref[pl.ds(q, n, stride=x_tiles), :] for q in range(x_tiles)], axis=1)
        shift = os_ref[:, 0:D_MODEL]
        scale = os_ref[:, D_MODEL:2 * D_MODEL]
        hs_ref[...] = (_rms(x, gpre_ref[...]) * (1.0 + scale) + shift).astype(BF16)


def _adaln(c_sample, c_prompt, w_c, b_c, w_in, x2_sample, g_pre):
    n_s, n_p = c_sample.shape[0], c_prompt.shape[0]
    assert n_p <= ADALN_PROMPT_ROWS and D_MODEL % ADALN_ROWS == 0
    return pl.pallas_call(
        _adaln_kernel,
        grid=(D_MODEL // ADALN_ROWS,),
        in_specs=[
            pl.BlockSpec((n_s, D_MODEL), lambda k: (0, 0)),
            pl.BlockSpec((n_p, D_MODEL), lambda k: (0, 0)),
            pl.BlockSpec((ADALN_ROWS, 3 * D_MODEL), lambda k: (k, 0)),
            pl.BlockSpec((1, 3 * D_MODEL), lambda k: (0, 0)),
            pl.BlockSpec((ADALN_ROWS, D_IN), lambda k: (k, 0)),
            pl.BlockSpec(x2_sample.shape, lambda k: (0, 0)),
            pl.BlockSpec(g_pre.shape, lambda k: (0, 0)),
        ],
        out_specs=(pl.BlockSpec((n_s, 3 * D_MODEL), lambda k: (0, 0)),
                   pl.BlockSpec((ADALN_PROMPT_ROWS, 3 * D_MODEL), lambda k: (0, 0)),
                   pl.BlockSpec((ADALN_ROWS, D_IN), lambda k: (k, 0)),
                   pl.BlockSpec((n_s, D_MODEL), lambda k: (0, 0))),
        out_shape=(jax.ShapeDtypeStruct((n_s, 3 * D_MODEL), F32),
                   jax.ShapeDtypeStruct((ADALN_PROMPT_ROWS, 3 * D_MODEL), F32),
                   jax.ShapeDtypeStruct((D_MODEL, D_IN), BF16),
                   jax.ShapeDtypeStruct((n_s, D_MODEL), BF16)),
        compiler_params=pltpu.CompilerParams(
            dimension_semantics=("arbitrary",), vmem_limit_bytes=VMEM_LIMIT),
        name="adaln",
    )(c_sample, c_prompt, w_c, b_c.reshape(1, -1), w_in, x2_sample, g_pre)


def _mixer_kernel(x_ref, mod_ref, gpre_ref, w_in_ref, lng_ref, lnb_ref, ws_ref, bst_ref,
                  lam_re_ref, lam_im_ref, bmat_ref, cboth_ref, gmat_ref, dskip_ref, bglu_ref, hs_ref, wout_ref,
                  oin_ref, st_re_ref, st_im_ref, wout_bf_ref, ps_ref,
                  h_sc, a_sc, b_sc, c_sc, v_sc, xb_sc, bu_sc, y2_sc, st_sc, *, n_seq):
    step = pl.program_id(0)

    @pl.when(step == 0)
    def _():
        st_sc[...] = jnp.zeros_like(st_sc)

    wout_bf_ref[...] = wout_ref[...].astype(BF16)

    blk = step % (D_IN // SAMPLE_PROJ_COLS)
    cols = pl.ds(pl.multiple_of(blk * SAMPLE_PROJ_COLS, SAMPLE_PROJ_COLS), SAMPLE_PROJ_COLS)
    ps_ref[...] = _dot(hs_ref[...], w_in_ref[:, cols])

    w_u, w_v, w_z, w_xb, w_zb = 0, D_A, 2 * D_A, 3 * D_A, 3 * D_A + D_B

    for b in range(n_seq):
        x = x_ref[b]
        shift = mod_ref[b:b + 1, 0:D_MODEL]
        gain = gpre_ref[...] * (1.0 + mod_ref[b:b + 1, D_MODEL:2 * D_MODEL])
        rws = slice(b * CHUNK, (b + 1) * CHUNK)
        h_sc[rws, :] = (x * _rms_scale(x) * gain + shift).astype(BF16)
        b_sc[rws, :] = _dot(h_sc[rws, :], w_in_ref[:, w_xb:w_xb + D_B])
        xb_sc[rws, :] = b_sc[rws, :].astype(BF16)

    h = h_sc[...]

    tril = (lax.broadcasted_iota(jnp.int32, (CHUNK, CHUNK), 0)
            >= lax.broadcasted_iota(jnp.int32, (CHUNK, CHUNK), 1))
    n_fill = HALF_SLABS * SCAN_PIECES // 4
    fill_cols = D_A // n_fill

    def fill_u(q):
        cols = slice(q * fill_cols, (q + 1) * fill_cols)
        a_sc[:, cols] = _dot(h, w_in_ref[:, w_u + q * fill_cols:w_u + (q + 1) * fill_cols])

    def fill_z(q):
        cols = slice(q * fill_cols, (q + 1) * fill_cols)
        z = _dot(h, w_in_ref[:, w_z + q * fill_cols:w_z + (q + 1) * fill_cols])
        a_sc[:, cols] = a_sc[:, cols] * _silu(z)

    def fill_mix(q):
        heads_per_fill = N_HEADS_A // n_fill
        for hd in range(q * heads_per_fill, (q + 1) * heads_per_fill):
            cols = slice(hd * HEAD_A, (hd + 1) * HEAD_A)
            w_t = jnp.where(tril, ws_ref[hd], 0.0).astype(BF16)
            v_h = jnp.concatenate([v_sc[b * CHUNK:(b + 1) * CHUNK, cols] for b in range(n_seq)], axis=1)
            mix = _dot(w_t, v_h) + bst_ref[:, hd:hd + 1]
            for b in range(n_seq):
                rws = slice(b * CHUNK, (b + 1) * CHUNK)
                oin_ref[rws, cols] = (a_sc[rws, cols] * mix[:, b * HEAD_A:(b + 1) * HEAD_A]).astype(BF16)

    def fill_zb(q):
        cols = slice(q * fill_cols, (q + 1) * fill_cols)
        a_sc[:, cols] = _silu(_dot(h, w_in_ref[:, w_zb + q * fill_cols:w_zb + (q + 1) * fill_cols]))

    fillers = [functools.partial(f, q) for f, q in (
        (fill_u, 0), (fill_z, 0), (fill_u, 1), (fill_mix, 0), (fill_z, 1), (fill_u, 2), (fill_z, 2), (fill_mix, 1),
        (fill_u, 3), (fill_z, 3), (fill_zb, 0), (fill_mix, 2), (fill_zb, 1), (fill_zb, 2), (fill_mix, 3), (fill_zb, 3))]
    assert len(fillers) == HALF_SLABS * SCAN_PIECES

    first_half = lax.broadcasted_iota(jnp.int32, (N_STREAMS, LANES), 0) < n_seq

    def bu_piece(k, piece):
        t0 = piece * PIECE
        for half in range(2):
            j = half * HALF_SLABS + k
            lhs = jnp.concatenate(
                [xb_sc[b * CHUNK + t0:b * CHUNK + t0 + PIECE, j * LANES:(j + 1) * LANES]
                 for b in range(n_seq)], axis=0)
            bu = _dot(lhs, bmat_ref[j])
            for b in range(n_seq):
                r0 = (half * n_seq + b) * STREAM_PITCH + t0
                for l in range(2 * SLAB_TILES):
                    bu_sc[l, r0:r0 + PIECE, :] = bu[b * PIECE:(b + 1) * PIECE, l * LANES:(l + 1) * LANES]

    def scan_piece(k, piece, hr, hi, lam_r, lam_i):
        rows = []
        for t in range(piece * PIECE, (piece + 1) * PIECE):
            new_r, new_i = [], []
            for l in range(SLAB_TILES):
                br = bu_sc[l, pl.ds(t, N_STREAMS, stride=STREAM_PITCH), :]
                bi = bu_sc[SLAB_TILES + l, pl.ds(t, N_STREAMS, stride=STREAM_PITCH), :]
                new_r.append(lam_r[l] * hr[l] - lam_i[l] * hi[l] + br)
                new_i.append(lam_r[l] * hi[l] + lam_i[l] * hr[l] + bi)
            hr, hi = new_r, new_i
            rows.append(jnp.concatenate(hr + hi, axis=1))
        return hr, hi, jnp.concatenate(rows, axis=0)

    def c_piece(k, piece, hh):
        t0 = piece * PIECE
        rows = slice(t0 * N_STREAMS, (t0 + PIECE) * N_STREAMS)
        y2 = _dot(hh.astype(BF16), cboth_ref[k])
        y2_sc[0, rows, :] = y2[:, 0:LANES]
        y2_sc[1, rows, :] = y2[:, LANES:2 * LANES]
        for half in range(2):
            j = half * HALF_SLABS + k
            for b in range(n_seq):
                s = half * n_seq + b
                c_sc[b * CHUNK + t0:b * CHUNK + t0 + PIECE, j * LANES:(j + 1) * LANES] = (
                    y2_sc[half, pl.ds(t0 * N_STREAMS + s, PIECE, stride=N_STREAMS), :])

    for piece in range(SCAN_PIECES):
        bu_piece(0, piece)
    c_sc[...] = _dot(h, w_in_ref[:, w_v:w_v + D_A])
    v_sc[...] = _layernorm(c_sc[...], lng_ref[...], lnb_ref[...]).astype(BF16)
    for k in range(HALF_SLABS):
        tiles = [slice(l * LANES, (l + 1) * LANES) for l in range(SLAB_TILES)]
        lam_r = [jnp.where(first_half, lam_re_ref[k:k + 1, t], lam_re_ref[HALF_SLABS + k:HALF_SLABS + k + 1, t])
                 for t in tiles]
        lam_i = [jnp.where(first_half, lam_im_ref[k:k + 1, t], lam_im_ref[HALF_SLABS + k:HALF_SLABS + k + 1, t])
                 for t in tiles]
        hr = [st_sc[k, :, l * LANES:(l + 1) * LANES] for l in range(SLAB_TILES)]
        hi = [st_sc[k, :, (SLAB_TILES + l) * LANES:(SLAB_TILES + l + 1) * LANES] for l in range(SLAB_TILES)]
        for piece in range(SCAN_PIECES):
            hr, hi, hh = scan_piece(k, piece, hr, hi, lam_r, lam_i)
            c_piece(k, piece, hh)
            if k + 1 < HALF_SLABS:
                bu_piece(k + 1, piece)
            fillers[k * SCAN_PIECES + piece]()
        st_sc[k] = jnp.concatenate(hr + hi, axis=1)

    for k in range(HALF_SLABS):
        for half in range(2):
            j = half * HALF_SLABS + k
            for b in range(n_seq):
                s = half * n_seq + b
                st_re_ref[b, j:j + 1, :] = st_sc[k, s:s + 1, 0:SLAB_STATES]
                st_im_ref[b, j:j + 1, :] = st_sc[k, s:s + 1, SLAB_STATES:2 * SLAB_STATES]

    for j in range(N_SLABS):
        cols = slice(j * LANES, (j + 1) * LANES)
        y = c_sc[:, cols] + dskip_ref[:, cols] * b_sc[:, cols]
        g = _dot(y.astype(BF16), gmat_ref[j])
        val = g[:, 0:LANES] + bglu_ref[0:1, cols]
        gate = g[:, LANES:2 * LANES] + bglu_ref[1:2, cols]
        oin_ref[:, D_A + j * LANES:D_A + (j + 1) * LANES] = (
            val * jax.nn.sigmoid(gate) * a_sc[:, cols]).astype(BF16)


def _const_spec(shape):
    zeros = (0,) * len(shape)
    return pl.BlockSpec(shape, lambda i: zeros, pipeline_mode=pl.Buffered(1))


def _mixer(x, mod, g_pre, w_in_bf, ln_g, ln_b, w_s, bst, lam_re, lam_im, bmat, cboth, gmat,
           d_skip, b_glu_rows, h_sample, w_out):
    n_seq, seq, _ = x.shape
    assert 2 * n_seq == N_STREAMS and seq % CHUNK == 0
    rows = n_seq * CHUNK
    n_steps = seq // CHUNK
    wout_rows = w_out.shape[0] // n_steps
    assert wout_rows * n_steps == w_out.shape[0] and wout_rows % (2 * SUBLANES) == 0
    state_shape = (n_seq, N_SLABS, SLAB_STATES)
    n_sample = h_sample.shape[0]
    n_proj = D_IN // SAMPLE_PROJ_COLS
    assert n_proj <= n_steps
    consts = (mod, g_pre, w_in_bf, ln_g, ln_b, w_s, bst, lam_re, lam_im, bmat, cboth, gmat,
              d_skip, b_glu_rows, h_sample)
    return pl.pallas_call(
        functools.partial(_mixer_kernel, n_seq=n_seq),
        grid=(n_steps,),
        in_specs=[pl.BlockSpec((n_seq, CHUNK, D_MODEL), lambda i: (0, i, 0))]
        + [_const_spec(c.shape) for c in consts]
        + [pl.BlockSpec((wout_rows, w_out.shape[1]), lambda i: (i, 0))],
        out_specs=(
            pl.BlockSpec((rows, D_A + D_B), lambda i: (i, 0)),
            pl.BlockSpec(state_shape, lambda i: (0, 0, 0)),
            pl.BlockSpec(state_shape, lambda i: (0, 0, 0)),
            pl.BlockSpec((wout_rows, w_out.shape[1]), lambda i: (i, 0)),
            pl.BlockSpec((n_sample, SAMPLE_PROJ_COLS), lambda i: (0, i % n_proj)),
        ),
        out_shape=(
            jax.ShapeDtypeStruct((n_steps * rows, D_A + D_B), BF16),
            jax.ShapeDtypeStruct(state_shape, F32),
            jax.ShapeDtypeStruct(state_shape, F32),
            jax.ShapeDtypeStruct(w_out.shape, BF16),
            jax.ShapeDtypeStruct((n_sample, D_IN), F32),
        ),
        scratch_shapes=[
            pltpu.VMEM((rows, D_MODEL), BF16),
            pltpu.VMEM((rows, D_A), F32),
            pltpu.VMEM((rows, D_A), F32),
            pltpu.VMEM((rows, D_A), F32),
            pltpu.VMEM((rows, D_A), BF16),
            pltpu.VMEM((rows, D_B), BF16),
            pltpu.VMEM((2 * SLAB_TILES, N_STREAMS * STREAM_PITCH, LANES), F32),
            pltpu.VMEM((2, N_STREAMS * CHUNK, LANES), F32),
            pltpu.VMEM((HALF_SLABS, N_STREAMS, 2 * SLAB_STATES), F32),
        ],
        compiler_params=pltpu.CompilerParams(
            dimension_semantics=("arbitrary",), vmem_limit_bytes=VMEM_LIMIT),
        name="mixer",
    )(x, *consts, w_out)


def _outproj_kernel(oin_ref, x_ref, mod_ref, gpost_ref, w_out_ref, y_ref, *, n_seq):
    o = _dot(oin_ref[...], w_out_ref[...])
    r = o * _rms_scale(o)
    for b in range(n_seq):
        gain = gpost_ref[...] * mod_ref[b:b + 1, 2 * D_MODEL:3 * D_MODEL]
        y_ref[b] = x_ref[b] + r[b * CHUNK:(b + 1) * CHUNK, :] * gain


def _outproj(oin, x, mod, g_post, w_out_bf):
    n_seq, seq, _ = x.shape
    rows = n_seq * CHUNK
    return pl.pallas_call(
        functools.partial(_outproj_kernel, n_seq=n_seq),
        grid=(seq // CHUNK,),
        in_specs=[
            pl.BlockSpec((rows, D_A + D_B), lambda i: (i, 0)),
            pl.BlockSpec((n_seq, CHUNK, D_MODEL), lambda i: (0, i, 0)),
            _const_spec(mod.shape), _const_spec(g_post.shape), _const_spec(w_out_bf.shape),
        ],
        out_specs=pl.BlockSpec((n_seq, CHUNK, D_MODEL), lambda i: (0, i, 0)),
        out_shape=jax.ShapeDtypeStruct(x.shape, F32),
        compiler_params=pltpu.CompilerParams(
            dimension_semantics=("arbitrary",), vmem_limit_bytes=VMEM_LIMIT),
        name="outproj",
    )(oin, x, mod, g_post, w_out_bf)


def _sample_kernel(x2_ref, mod_ref, proj_ref, lng_ref, lnb_ref, ws_ref, bst_ref,
                   lam_re_ref, lam_im_ref, bmat_ref, cboth_ref, gmat_ref, dskip_ref, bglu_ref,
                   h0_re_ref, h0_im_ref, gpost_ref, w_out_ref,
                   y2_ref, v2_ref, hs_re_ref, hs_im_ref, oin_sc):
    n = proj_ref.shape[0]
    x_tiles = D_MODEL // LANES
    v_tiles = D_A // LANES
    w_u, w_v, w_z, w_xb, w_zb = 0, D_A, 2 * D_A, 3 * D_A, 3 * D_A + D_B

    v = _layernorm(proj_ref[:, w_v:w_v + D_A], lng_ref[...], lnb_ref[...])
    for q in range(v_tiles):
        v2_ref[pl.ds(q, n, stride=v_tiles), :] = v[:, q * LANES:(q + 1) * LANES]
    for hd in range(N_HEADS_A):
        cols = slice(hd * HEAD_A, (hd + 1) * HEAD_A)
        mix = ws_ref[hd, 0:1, 0:1] * v[:, cols] + bst_ref[0:1, hd:hd + 1]
        u = proj_ref[:, w_u + hd * HEAD_A:w_u + (hd + 1) * HEAD_A]
        z = proj_ref[:, w_z + hd * HEAD_A:w_z + (hd + 1) * HEAD_A]
        oin_sc[:, cols] = (u * mix * _silu(z)).astype(BF16)

    for j in range(N_SLABS):
        cols = slice(j * LANES, (j + 1) * LANES)
        st = slice(j * SLAB_STATES, (j + 1) * SLAB_STATES)
        xb = proj_ref[:, w_xb + j * LANES:w_xb + (j + 1) * LANES]
        zb = proj_ref[:, w_zb + j * LANES:w_zb + (j + 1) * LANES]
        bu = _dot(xb.astype(BF16), bmat_ref[j])
        lr = lam_re_ref[j:j + 1, :]
        li = lam_im_ref[j:j + 1, :]
        h0r = h0_re_ref[:, st]
        h0i = h0_im_ref[:, st]
        hr = lr * h0r - li * h0i + bu[:, 0:SLAB_STATES]
        hi = lr * h0i + li * h0r + bu[:, SLAB_STATES:2 * SLAB_STATES]
        hs_re_ref[:, st] = hr
        hs_im_ref[:, st] = hi
        k, half = j % HALF_SLABS, j // HALF_SLABS
        c_j = cboth_ref[k, :, half * LANES:(half + 1) * LANES]
        y = _dot(jnp.concatenate([hr, hi], axis=1).astype(BF16), c_j)
        y = y + dskip_ref[:, cols] * xb
        g = _dot(y.astype(BF16), gmat_ref[j])
        val = g[:, 0:LANES] + bglu_ref[0:1, cols]
        gt = g[:, LANES:2 * LANES] + bglu_ref[1:2, cols]
        oin_sc[:, D_A + j * LANES:D_A + (j + 1) * LANES] = (
            val * jax.nn.sigmoid(gt) * _silu(zb)).astype(BF16)

    o = _dot(oin_sc[...], w_out_ref[...])
    gate = mod_ref[:, 2 * D_MODEL:3 * D_MODEL]
    r = gate * _rms(o, gpost_ref[...])
    for q in range(x_tiles):
        cols = slice(q * LANES, (q + 1) * LANES)
        y2_ref[pl.ds(q, n, stride=x_tiles), :] = x2_ref[pl.ds(q, n, stride=x_tiles), :] + r[:, cols]


def _sample(x2, mod, proj, ln_g, ln_b, w_s, bst, lam_re, lam_im, bmat, cboth, gmat,
            d_skip, b_glu_rows, h0_re, h0_im, g_post, w_out_bf):
    n = proj.shape[0]
    n_state = N_GROUPS_B * P_STATE
    out_shapes = ((n * D_MODEL // LANES, LANES), (n * D_A // LANES, LANES), (n, n_state), (n, n_state))
    return pl.pallas_call(
        _sample_kernel,
        out_shape=tuple(jax.ShapeDtypeStruct(shp, F32) for shp in out_shapes),
        scratch_shapes=[pltpu.VMEM((n, D_A + D_B), BF16)],
        compiler_params=pltpu.CompilerParams(vmem_limit_bytes=VMEM_LIMIT),
        name="sample",
    )(x2, mod, proj, ln_g, ln_b, w_s, bst, lam_re, lam_im, bmat, cboth, gmat,
      d_skip, b_glu_rows, h0_re, h0_im, g_post, w_out_bf)


def kernel(x_prompt, x_sample, c_prompt, c_sample, state_b_re, state_b_im, w_c, b_c, g_pre, w_in,
           ln_v_g, ln_v_b, w_s, b_s, a_re, a_im, log_dt, b_re, b_im, c_re, c_im, d_skip, w_glu, b_glu,
           w_out, g_post):
    n_p = x_prompt.shape[0]
    n_s = x_sample.shape[0]
    assert x_sample.shape[1] == 1 and n_s % SUBLANES == 0

    bmat, cboth, gmat, lam_re, lam_im, b_glu_rows, bst = _s5_prep(
        a_re, a_im, log_dt, b_re, b_im, c_re, c_im, w_glu, b_glu, b_s)

    row = lambda v: v.reshape(1, -1)
    x2_sample = x_sample.reshape(n_s * D_MODEL // LANES, LANES)
    mod_s, mod_p, w_in_bf, h_sample = _adaln(c_sample, c_prompt, w_c, b_c, w_in, x2_sample, row(g_pre))

    s5 = (w_s, bst, lam_re, lam_im, bmat, cboth, gmat, row(d_skip), b_glu_rows)
    oin, st_re, st_im, w_out_bf, proj_s = _mixer(
        x_prompt, mod_p, row(g_pre), w_in_bf, row(ln_v_g), row(ln_v_b), *s5, h_sample, w_out)
    y_prompt = _outproj(oin, x_prompt, mod_p, row(g_post), w_out_bf)

    y_s, v_s, hs_re, hs_im = _sample(
        x2_sample, mod_s, proj_s, row(ln_v_g), row(ln_v_b), *s5,
        state_b_re.reshape(n_s, -1), state_b_im.reshape(n_s, -1), row(g_post), w_out_bf)

    return (y_prompt, y_s.reshape(n_s, 1, D_MODEL), v_s.reshape(n_s, 1, D_A),
            st_re.reshape(n_p, N_GROUPS_B, P_STATE), st_im.reshape(n_p, N_GROUPS_B, P_STATE),
            hs_re.reshape(n_s, N_GROUPS_B, P_STATE), hs_im.reshape(n_s, N_GROUPS_B, P_STATE))
```

```python
import functools

import jax
import jax.numpy as jnp
from jax import lax
from jax.experimental import pallas as pl
from jax.experimental.pallas import tpu as pltpu

F32 = jnp.float32
BF16 = jnp.bfloat16

EPS = 1e-6
D_MODEL = 2048
D_A = 1024
D_B = 1024
D_IN = 3 * D_A + 2 * D_B
CHUNK = 128
HEAD_A = 128
N_HEADS_A = D_A // HEAD_A
GROUP_B = 16
N_GROUPS_B = D_B // GROUP_B
P_STATE = 64

LANES = 128
SUBLANES = 8
SLAB_GROUPS = LANES // GROUP_B
N_SLABS = N_GROUPS_B // SLAB_GROUPS
SLAB_STATES = SLAB_GROUPS * P_STATE
SLAB_TILES = SLAB_STATES // LANES
N_STREAMS = SUBLANES
HALF_SLABS = N_SLABS // 2
STREAM_PITCH = CHUNK + SUBLANES
SCAN_PIECES = 4
PIECE = CHUNK // SCAN_PIECES
V7X_VMEM_BYTES = 64 * 1024 * 1024
VMEM_LIMIT = V7X_VMEM_BYTES - 2 * 1024 * 1024
SAMPLE_PROJ_COLS = 512


def _silu(x):
    return x * jax.nn.sigmoid(x)


def _rms_scale(x):
    return lax.rsqrt(jnp.mean(x * x, axis=-1, keepdims=True) + EPS)


def _rms(x, g):
    return x * _rms_scale(x) * g


def _layernorm(x, g, b):
    mu = jnp.mean(x, axis=-1, keepdims=True)
    xc = x - mu
    return xc * lax.rsqrt(jnp.mean(xc * xc, axis=-1, keepdims=True) + EPS) * g + b


def _dot(a, b):
    return jnp.dot(a, b, preferred_element_type=F32)


def _discretize(a_re, a_im, log_dt):
    dt = jnp.exp(log_dt)
    mag = jnp.exp(dt * a_re)
    abar_re = mag * jnp.cos(dt * a_im)
    abar_im = mag * jnp.sin(dt * a_im)
    return abar_re, abar_im


def _nt_dot(a, b):
    return lax.dot_general(a, b, (((1,), (1,)), ((), ())), preferred_element_type=F32)


def _bf16_terms(x):
    hi = x.astype(BF16)
    r1 = x - hi.astype(F32)
    mid = r1.astype(BF16)
    lo = (r1 - mid.astype(F32)).astype(BF16)
    return hi, mid, lo


def _prep_kernel(a8_ref, bt_re_ref, bt_im_ref, c_re_ref, c_im_ref, wglu_ref, bglu_t_ref, bs_ref,
                 bmat_ref, cboth_ref, gmat_ref, lam_re_ref, lam_im_ref, bglu_row_ref, bst_ref):
    a_re = a8_ref[0:N_SLABS, :]
    a_im = a8_ref[N_SLABS:2 * N_SLABS, :]
    abar_re, abar_im = _discretize(a_re, a_im, a8_ref[2 * N_SLABS:3 * N_SLABS, :])
    lam_re_ref[...] = abar_re
    lam_im_ref[...] = abar_im
    num_re = abar_re - 1.0
    num_im = abar_im
    den = a_re * a_re + a_im * a_im
    coef_re = (num_re * a_re + num_im * a_im) / den
    coef_im = (num_im * a_re - num_re * a_im) / den

    def rep_matrix(k, n, period, offset=0):
        row = lax.broadcasted_iota(jnp.int32, (k, n), 0)
        col = lax.broadcasted_iota(jnp.int32, (k, n), 1)
        return jnp.where((col & (period - 1)) + offset == row, 1.0, 0.0).astype(BF16)

    def rep_rows(m, k, period):
        row = lax.broadcasted_iota(jnp.int32, (m, k), 0)
        col = lax.broadcasted_iota(jnp.int32, (m, k), 1)
        return jnp.where((row & (period - 1)) == col, 1.0, 0.0).astype(BF16)

    def block_mask(m, n, row_shift, col_shift):
        row = lax.broadcasted_iota(jnp.int32, (m, n), 0)
        col = lax.broadcasted_iota(jnp.int32, (m, n), 1)
        return (row >> row_shift) == (col >> col_shift)

    rep_state = rep_matrix(P_STATE, SLAB_STATES, P_STATE)
    rows_state = rep_rows(SLAB_STATES, P_STATE, P_STATE)
    rep_val = rep_matrix(2 * GROUP_B, LANES, GROUP_B)
    rep_gate = rep_matrix(2 * GROUP_B, LANES, GROUP_B, GROUP_B)
    mask_b = block_mask(LANES, SLAB_STATES, 4, 6)
    mask_c = block_mask(SLAB_STATES, LANES, 6, 4)
    mask_g = block_mask(LANES, LANES, 4, 4)

    def spread(x, rep):
        return sum(_dot(term, rep) for term in _bf16_terms(x))

    for j in range(N_SLABS):
        crows = slice(j * LANES, (j + 1) * LANES)
        bt_re = spread(bt_re_ref[crows, :], rep_state)
        bt_im = spread(bt_im_ref[crows, :], rep_state)
        cr = coef_re[j:j + 1, :]
        ci = coef_im[j:j + 1, :]
        bmat_ref[j, :, 0:SLAB_STATES] = jnp.where(mask_b, cr * bt_re - ci * bt_im, 0.0).astype(BF16)
        bmat_ref[j, :, SLAB_STATES:2 * SLAB_STATES] = jnp.where(
            mask_b, cr * bt_im + ci * bt_re, 0.0).astype(BF16)

        k, half = j % HALF_SLABS, j // HALF_SLABS
        ccols = slice(half * LANES, (half + 1) * LANES)
        ct_re = _nt_dot(rows_state, c_re_ref[crows, :].astype(BF16))
        ct_im = _nt_dot(rows_state, c_im_ref[crows, :].astype(BF16))
        cboth_ref[k, 0:SLAB_STATES, ccols] = jnp.where(mask_c, ct_re, 0.0).astype(BF16)
        cboth_ref[k, SLAB_STATES:2 * SLAB_STATES, ccols] = jnp.where(mask_c, -ct_im, 0.0).astype(BF16)

        w = wglu_ref[crows, :].astype(BF16)
        gmat_ref[j, :, 0:LANES] = jnp.where(mask_g, _dot(w, rep_val), 0.0).astype(BF16)
        gmat_ref[j, :, LANES:2 * LANES] = jnp.where(mask_g, _dot(w, rep_gate), 0.0).astype(BF16)

    row = lax.broadcasted_iota(jnp.int32, (2 * GROUP_B, D_B), 0)
    col = lax.broadcasted_iota(jnp.int32, (2 * GROUP_B, D_B), 1)
    by_group = spread(bglu_t_ref[...], jnp.where(block_mask(N_GROUPS_B, D_B, 0, 4), 1.0, 0.0).astype(BF16))
    for i in range(2):
        mine = row == (col & (GROUP_B - 1)) + i * GROUP_B
        bglu_row_ref[i:i + 1, :] = jnp.sum(jnp.where(mine, by_group, 0.0), axis=0, keepdims=True)

    eye = rep_rows(CHUNK, CHUNK, CHUNK)
    bst_ref[...] = sum(_nt_dot(eye, term) for term in _bf16_terms(bs_ref[...]))


def _s5_prep(a_re, a_im, log_dt, b_re, b_im, c_re, c_im, w_glu, b_glu, b_s):
    g, p, c = N_GROUPS_B, P_STATE, GROUP_B
    a8 = jnp.stack([a_re, a_im, jnp.broadcast_to(log_dt[:, None], (g, p))]).reshape(3 * N_SLABS, SLAB_STATES)
    args = (a8, b_re.transpose(0, 2, 1).reshape(g * c, p), b_im.transpose(0, 2, 1).reshape(g * c, p),
            c_re.reshape(g * c, p), c_im.reshape(g * c, p), w_glu.reshape(g * c, 2 * c), b_glu.T, b_s)
    return pl.pallas_call(
        _prep_kernel,
        out_shape=(
            jax.ShapeDtypeStruct((N_SLABS, LANES, 2 * SLAB_STATES), BF16),
            jax.ShapeDtypeStruct((HALF_SLABS, 2 * SLAB_STATES, 2 * LANES), BF16),
            jax.ShapeDtypeStruct((N_SLABS, LANES, 2 * LANES), BF16),
            jax.ShapeDtypeStruct((N_SLABS, SLAB_STATES), F32),
            jax.ShapeDtypeStruct((N_SLABS, SLAB_STATES), F32),
            jax.ShapeDtypeStruct((2, D_B), F32),
            jax.ShapeDtypeStruct((CHUNK, N_HEADS_A), F32),
        ),
        name="s5_prep",
    )(*args)


ADALN_ROWS = 256
ADALN_PROMPT_ROWS = 2 * SUBLANES


def _adaln_kernel(cs_ref, cp_ref, w_ref, b_ref, w_in_ref, x2_ref, gpre_ref,
                  os_ref, op_ref, w_in_bf_ref, hs_ref):
    k = pl.program_id(0)

    @pl.when(k == 0)
    def _():
        os_ref[...] = jnp.broadcast_to(b_ref[...], os_ref.shape)
        op_ref[...] = jnp.broadcast_to(b_ref[...], op_ref.shape)

    w_in_bf_ref[...] = w_in_ref[...].astype(BF16)

    cols = pl.ds(pl.multiple_of(k * ADALN_ROWS, ADALN_ROWS), ADALN_ROWS)
    w = w_ref[...].astype(BF16)
    os_ref[...] += _dot(_silu(cs_ref[:, cols]).astype(BF16), w)
    n_p = cp_ref.shape[0]
    cp = jnp.concatenate([cp_ref[:, cols], jnp.zeros((ADALN_PROMPT_ROWS - n_p, ADALN_ROWS), F32)], axis=0)
    op_ref[...] += _dot(_silu(cp).astype(BF16), w)

    @pl.when(k == pl.num_programs(0) - 1)
    def _():
        x_tiles = D_MODEL // LANES
        n = hs_ref.shape[0]
        x = jnp.concatenate([x2_ref[pl.ds(q, n, stride=x_tiles), :] for q in range(x_tiles)], axis=1)
        shift = os_ref[:, 0:D_MODEL]
        scale = os_ref[:, D_MODEL:2 * D_MODEL]
        hs_ref[...] = (_rms(x, gpre_ref[...]) * (1.0 + scale) + shift).astype(BF16)


def _adaln(c_sample, c_prompt, w_c, b_c, w_in, x2_sample, g_pre):
    n_s, n_p = c_sample.shape[0], c_prompt.shape[0]
    assert n_p <= ADALN_PROMPT_ROWS and D_MODEL % ADALN_ROWS == 0
    return pl.pallas_call(
        _adaln_kernel,
        grid=(D_MODEL // ADALN_ROWS,),
        in_specs=[
            pl.BlockSpec((n_s, D_MODEL), lambda k: (0, 0)),
            pl.BlockSpec((n_p, D_MODEL), lambda k: (0, 0)),
            pl.BlockSpec((ADALN_ROWS, 3 * D_MODEL), lambda k: (k, 0)),
            pl.BlockSpec((1, 3 * D_MODEL), lambda k: (0, 0)),
            pl.BlockSpec((ADALN_ROWS, D_IN), lambda k: (k, 0)),
            pl.BlockSpec(x2_sample.shape, lambda k: (0, 0)),
            pl.BlockSpec(g_pre.shape, lambda k: (0, 0)),
        ],
        out_specs=(pl.BlockSpec((n_s, 3 * D_MODEL), lambda k: (0, 0)),
                   pl.BlockSpec((ADALN_PROMPT_ROWS, 3 * D_MODEL), lambda k: (0, 0)),
                   pl.BlockSpec((ADALN_ROWS, D_IN), lambda k: (k, 0)),
                   pl.BlockSpec((n_s, D_MODEL), lambda k: (0, 0))),
        out_shape=(jax.ShapeDtypeStruct((n_s, 3 * D_MODEL), F32),
                   jax.ShapeDtypeStruct((ADALN_PROMPT_ROWS, 3 * D_MODEL), F32),
                   jax.ShapeDtypeStruct((D_MODEL, D_IN), BF16),
                   jax.ShapeDtypeStruct((n_s, D_MODEL), BF16)),
        compiler_params=pltpu.CompilerParams(
            dimension_semantics=("arbitrary",), vmem_limit_bytes=VMEM_LIMIT),
        name="adaln",
    )(c_sample, c_prompt, w_c, b_c.reshape(1, -1), w_in, x2_sample, g_pre)


def _mixer_kernel(x_ref, mod_ref, gpre_ref, w_in_ref, lng_ref, lnb_ref, ws_ref, bst_ref,
                  lam_re_ref, lam_im_ref, bmat_ref, cboth_ref, gmat_ref, dskip_ref, bglu_ref, hs_ref, wout_ref,
                  oin_ref, st_re_ref, st_im_ref, wout_bf_ref, ps_ref,
                  h_sc, a_sc, b_sc, c_sc, v_sc, xb_sc, bu_sc, y2_sc, st_sc, *, n_seq):
    step = pl.program_id(0)

    @pl.when(step == 0)
    def _():
        st_sc[...] = jnp.zeros_like(st_sc)

    wout_bf_ref[...] = wout_ref[...].astype(BF16)

    @pl.when(step < D_IN // SAMPLE_PROJ_COLS)
    def _():
        cols = pl.ds(pl.multiple_of(step * SAMPLE_PROJ_COLS, SAMPLE_PROJ_COLS), SAMPLE_PROJ_COLS)
        ps_ref[...] = _dot(hs_ref[...], w_in_ref[:, cols])

    w_u, w_v, w_z, w_xb, w_zb = 0, D_A, 2 * D_A, 3 * D_A, 3 * D_A + D_B

    for b in range(n_seq):
        x = x_ref[b]
        shift = mod_ref[b:b + 1, 0:D_MODEL]
        gain = gpre_ref[...] * (1.0 + mod_ref[b:b + 1, D_MODEL:2 * D_MODEL])
        rws = slice(b * CHUNK, (b + 1) * CHUNK)
        h_sc[rws, :] = (x * _rms_scale(x) * gain + shift).astype(BF16)
        b_sc[rws, :] = _dot(h_sc[rws, :], w_in_ref[:, w_xb:w_xb + D_B])
        xb_sc[rws, :] = b_sc[rws, :].astype(BF16)

    h = h_sc[...]

    tril = (lax.broadcasted_iota(jnp.int32, (CHUNK, CHUNK), 0)
            >= lax.broadcasted_iota(jnp.int32, (CHUNK, CHUNK), 1))
    n_fill = HALF_SLABS * SCAN_PIECES // 4
    fill_cols = D_A // n_fill

    def fill_u(q):
        cols = slice(q * fill_cols, (q + 1) * fill_cols)
        a_sc[:, cols] = _dot(h, w_in_ref[:, w_u + q * fill_cols:w_u + (q + 1) * fill_cols])

    def fill_z(q):
        cols = slice(q * fill_cols, (q + 1) * fill_cols)
        z = _dot(h, w_in_ref[:, w_z + q * fill_cols:w_z + (q + 1) * fill_cols])
        a_sc[:, cols] = a_sc[:, cols] * _silu(z)

    def fill_mix(q):
        heads_per_fill = N_HEADS_A // n_fill
        for hd in range(q * heads_per_fill, (q + 1) * heads_per_fill):
            cols = slice(hd * HEAD_A, (hd + 1) * HEAD_A)
            w_t = jnp.where(tril, ws_ref[hd], 0.0).astype(BF16)
            v_h = jnp.concatenate([v_sc[b * CHUNK:(b + 1) * CHUNK, cols] for b in range(n_seq)], axis=1)
            mix = _dot(w_t, v_h) + bst_ref[:, hd:hd + 1]
            for b in range(n_seq):
                rws = slice(b * CHUNK, (b + 1) * CHUNK)
                oin_ref[rws, cols] = (a_sc[rws, cols] * mix[:, b * HEAD_A:(b + 1) * HEAD_A]).astype(BF16)

    def fill_zb(q):
        cols = slice(q * fill_cols, (q + 1) * fill_cols)
        a_sc[:, cols] = _silu(_dot(h, w_in_ref[:, w_zb + q * fill_cols:w_zb + (q + 1) * fill_cols]))

    fillers = [functools.partial(f, q) for f, q in (
        (fill_u, 0), (fill_z, 0), (fill_u, 1), (fill_mix, 0), (fill_z, 1), (fill_u, 2), (fill_z, 2), (fill_mix, 1),
        (fill_u, 3), (fill_z, 3), (fill_zb, 0), (fill_mix, 2), (fill_zb, 1), (fill_zb, 2), (fill_mix, 3), (fill_zb, 3))]
    assert len(fillers) == HALF_SLABS * SCAN_PIECES

    first_half = lax.broadcasted_iota(jnp.int32, (N_STREAMS, LANES), 0) < n_seq

    def bu_piece(k, piece):
        t0 = piece * PIECE
        for half in range(2):
            j = half * HALF_SLABS + k
            lhs = jnp.concatenate(
                [xb_sc[b * CHUNK + t0:b * CHUNK + t0 + PIECE, j * LANES:(j + 1) * LANES]
                 for b in range(n_seq)], axis=0)
            bu = _dot(lhs, bmat_ref[j])
            for b in range(n_seq):
                r0 = (half * n_seq + b) * STREAM_PITCH + t0
                for l in range(2 * SLAB_TILES):
                    bu_sc[l, r0:r0 + PIECE, :] = bu[b * PIECE:(b + 1) * PIECE, l * LANES:(l + 1) * LANES]

    def scan_piece(k, piece, hr, hi, lam_r, lam_i):
        rows = []
        for t in range(piece * PIECE, (piece + 1) * PIECE):
            new_r, new_i = [], []
            for l in range(SLAB_TILES):
                br = bu_sc[l, pl.ds(t, N_STREAMS, stride=STREAM_PITCH), :]
                bi = bu_sc[SLAB_TILES + l, pl.ds(t, N_STREAMS, stride=STREAM_PITCH), :]
                new_r.append(lam_r[l] * hr[l] - lam_i[l] * hi[l] + br)
                new_i.append(lam_r[l] * hi[l] + lam_i[l] * hr[l] + bi)
            hr, hi = new_r, new_i
            rows.append(jnp.concatenate(hr + hi, axis=1))
        return hr, hi, jnp.concatenate(rows, axis=0)

    def c_piece(k, piece, hh):
        t0 = piece * PIECE
        rows = slice(t0 * N_STREAMS, (t0 + PIECE) * N_STREAMS)
        y2 = _dot(hh.astype(BF16), cboth_ref[k])
        y2_sc[0, rows, :] = y2[:, 0:LANES]
        y2_sc[1, rows, :] = y2[:, LANES:2 * LANES]
        for half in range(2):
            j = half * HALF_SLABS + k
            for b in range(n_seq):
                s = half * n_seq + b
                c_sc[b * CHUNK + t0:b * CHUNK + t0 + PIECE, j * LANES:(j + 1) * LANES] = (
                    y2_sc[half, pl.ds(t0 * N_STREAMS + s, PIECE, stride=N_STREAMS), :])

    for piece in range(SCAN_PIECES):
        bu_piece(0, piece)
    c_sc[...] = _dot(h, w_in_ref[:, w_v:w_v + D_A])
    v_sc[...] = _layernorm(c_sc[...], lng_ref[...], lnb_ref[...]).astype(BF16)
    for k in range(HALF_SLABS):
        tiles = [slice(l * LANES, (l + 1) * LANES) for l in range(SLAB_TILES)]
        lam_r = [jnp.where(first_half, lam_re_ref[k:k + 1, t], lam_re_ref[HALF_SLABS + k:HALF_SLABS + k + 1, t])
                 for t in tiles]
        lam_i = [jnp.where(first_half, lam_im_ref[k:k + 1, t], lam_im_ref[HALF_SLABS + k:HALF_SLABS + k + 1, t])
                 for t in tiles]
        hr = [st_sc[k, :, l * LANES:(l + 1) * LANES] for l in range(SLAB_TILES)]
        hi = [st_sc[k, :, (SLAB_TILES + l) * LANES:(SLAB_TILES + l + 1) * LANES] for l in range(SLAB_TILES)]
        for piece in range(SCAN_PIECES):
            hr, hi, hh = scan_piece(k, piece, hr, hi, lam_r, lam_i)
            c_piece(k, piece, hh)
            if k + 1 < HALF_SLABS:
                bu_piece(k + 1, piece)
            fillers[k * SCAN_PIECES + piece]()
        st_sc[k] = jnp.concatenate(hr + hi, axis=1)

    for k in range(HALF_SLABS):
        for half in range(2):
            j = half * HALF_SLABS + k
            for b in range(n_seq):
                s = half * n_seq + b
                st_re_ref[b, j:j + 1, :] = st_sc[k, s:s + 1, 0:SLAB_STATES]
                st_im_ref[b, j:j + 1, :] = st_sc[k, s:s + 1, SLAB_STATES:2 * SLAB_STATES]

    for j in range(N_SLABS):
        cols = slice(j * LANES, (j + 1) * LANES)
        y = c_sc[:, cols] + dskip_ref[:, cols] * b_sc[:, cols]
        g = _dot(y.astype(BF16), gmat_ref[j])
        val = g[:, 0:LANES] + bglu_ref[0:1, cols]
        gate = g[:, LANES:2 * LANES] + bglu_ref[1:2, cols]
        oin_ref[:, D_A + j * LANES:D_A + (j + 1) * LANES] = (
            val * jax.nn.sigmoid(gate) * a_sc[:, cols]).astype(BF16)


def _const_spec(shape):
    zeros = (0,) * len(shape)
    return pl.BlockSpec(shape, lambda i: zeros, pipeline_mode=pl.Buffered(1))


def _mixer(x, mod, g_pre, w_in_bf, ln_g, ln_b, w_s, bst, lam_re, lam_im, bmat, cboth, gmat,
           d_skip, b_glu_rows, h_sample, w_out):
    n_seq, seq, _ = x.shape
    assert 2 * n_seq == N_STREAMS and seq % CHUNK == 0
    rows = n_seq * CHUNK
    n_steps = seq // CHUNK
    wout_rows = w_out.shape[0] // n_steps
    assert wout_rows * n_steps == w_out.shape[0] and wout_rows % (2 * SUBLANES) == 0
    state_shape = (n_seq, N_SLABS, SLAB_STATES)
    n_sample = h_sample.shape[0]
    n_proj = D_IN // SAMPLE_PROJ_COLS
    assert n_proj <= n_steps
    consts = (mod, g_pre, w_in_bf, ln_g, ln_b, w_s, bst, lam_re, lam_im, bmat, cboth, gmat,
              d_skip, b_glu_rows, h_sample)
    return pl.pallas_call(
        functools.partial(_mixer_kernel, n_seq=n_seq),
        grid=(n_steps,),
        in_specs=[pl.BlockSpec((n_seq, CHUNK, D_MODEL), lambda i: (0, i, 0))]
        + [_const_spec(c.shape) for c in consts]
        + [pl.BlockSpec((wout_rows, w_out.shape[1]), lambda i: (i, 0))],
        out_specs=(
            pl.BlockSpec((rows, D_A + D_B), lambda i: (i, 0)),
            pl.BlockSpec(state_shape, lambda i: (0, 0, 0)),
            pl.BlockSpec(state_shape, lambda i: (0, 0, 0)),
            pl.BlockSpec((wout_rows, w_out.shape[1]), lambda i: (i, 0)),
            pl.BlockSpec((n_sample, SAMPLE_PROJ_COLS), lambda i: (0, jnp.minimum(i, n_proj - 1))),
        ),
        out_shape=(
            jax.ShapeDtypeStruct((n_steps * rows, D_A + D_B), BF16),
            jax.ShapeDtypeStruct(state_shape, F32),
            jax.ShapeDtypeStruct(state_shape, F32),
            jax.ShapeDtypeStruct(w_out.shape, BF16),
            jax.ShapeDtypeStruct((n_sample, D_IN), F32),
        ),
        scratch_shapes=[
            pltpu.VMEM((rows, D_MODEL), BF16),
            pltpu.VMEM((rows, D_A), F32),
            pltpu.VMEM((rows, D_A), F32),
            pltpu.VMEM((rows, D_A), F32),
            pltpu.VMEM((rows, D_A), BF16),
            pltpu.VMEM((rows, D_B), BF16),
            pltpu.VMEM((2 * SLAB_TILES, N_STREAMS * STREAM_PITCH, LANES), F32),
            pltpu.VMEM((2, N_STREAMS * CHUNK, LANES), F32),
            pltpu.VMEM((HALF_SLABS, N_STREAMS, 2 * SLAB_STATES), F32),
        ],
        compiler_params=pltpu.CompilerParams(
            dimension_semantics=("arbitrary",), vmem_limit_bytes=VMEM_LIMIT),
        name="mixer",
    )(x, *consts, w_out)


def _outproj_kernel(oin_ref, x_ref, mod_ref, gpost_ref, w_out_ref, y_ref, *, n_seq):
    o = _dot(oin_ref[...], w_out_ref[...])
    r = o * _rms_scale(o)
    for b in range(n_seq):
        gain = gpost_ref[...] * mod_ref[b:b + 1, 2 * D_MODEL:3 * D_MODEL]
        y_ref[b] = x_ref[b] + r[b * CHUNK:(b + 1) * CHUNK, :] * gain


def _outproj(oin, x, mod, g_post, w_out_bf):
    n_seq, seq, _ = x.shape
    rows = n_seq * CHUNK
    return pl.pallas_call(
        functools.partial(_outproj_kernel, n_seq=n_seq),
        grid=(seq // CHUNK,),
        in_specs=[
            pl.BlockSpec((rows, D_A + D_B), lambda i: (i, 0)),
            pl.BlockSpec((n_seq, CHUNK, D_MODEL), lambda i: (0, i, 0)),
            _const_spec(mod.shape), _const_spec(g_post.shape), _const_spec(w_out_bf.shape),
        ],
        out_specs=pl.BlockSpec((n_seq, CHUNK, D_MODEL), lambda i: (0, i, 0)),
        out_shape=jax.ShapeDtypeStruct(x.shape, F32),
        compiler_params=pltpu.CompilerParams(
            dimension_semantics=("arbitrary",), vmem_limit_bytes=VMEM_LIMIT),
        name="outproj",
    )(oin, x, mod, g_post, w_out_bf)


def _sample_kernel(x2_ref, mod_ref, proj_ref, lng_ref, lnb_ref, ws_ref, bst_ref,
                   lam_re_ref, lam_im_ref, bmat_ref, cboth_ref, gmat_ref, dskip_ref, bglu_ref,
                   h0t_re_ref, h0t_im_ref, gpost_ref, w_out_ref,
                   y2_ref, v2_ref, hst_re_ref, hst_im_ref, oin_sc):
    n = proj_ref.shape[0]
    x_tiles = D_MODEL // LANES
    v_tiles = D_A // LANES
    w_u, w_v, w_z, w_xb, w_zb = 0, D_A, 2 * D_A, 3 * D_A, 3 * D_A + D_B

    v = _layernorm(proj_ref[:, w_v:w_v + D_A], lng_ref[...], lnb_ref[...])
    for q in range(v_tiles):
        v2_ref[pl.ds(q, n, stride=v_tiles), :] = v[:, q * LANES:(q + 1) * LANES]
    for hd in range(N_HEADS_A):
        cols = slice(hd * HEAD_A, (hd + 1) * HEAD_A)
        mix = ws_ref[hd, 0:1, 0:1] * v[:, cols] + bst_ref[0:1, hd:hd + 1]
        u = proj_ref[:, w_u + hd * HEAD_A:w_u + (hd + 1) * HEAD_A]
        z = proj_ref[:, w_z + hd * HEAD_A:w_z + (hd + 1) * HEAD_A]
        oin_sc[:, cols] = (u * mix * _silu(z)).astype(BF16)

    for j in range(N_SLABS):
        cols = slice(j * LANES, (j + 1) * LANES)
        st = slice(j * SLAB_STATES, (j + 1) * SLAB_STATES)
        xb = proj_ref[:, w_xb + j * LANES:w_xb + (j + 1) * LANES]
        zb = proj_ref[:, w_zb + j * LANES:w_zb + (j + 1) * LANES]
        bu = _dot(xb.astype(BF16), bmat_ref[j])
        lr = lam_re_ref[j:j + 1, :]
        li = lam_im_ref[j:j + 1, :]
        h0r = h0t_re_ref[st, :].T
        h0i = h0t_im_ref[st, :].T
        hr = lr * h0r - li * h0i + bu[:, 0:SLAB_STATES]
        hi = lr * h0i + li * h0r + bu[:, SLAB_STATES:2 * SLAB_STATES]
        hst_re_ref[st, :] = hr.T
        hst_im_ref[st, :] = hi.T
        k, half = j % HALF_SLABS, j // HALF_SLABS
        c_j = cboth_ref[k, :, half * LANES:(half + 1) * LANES]
        y = _dot(jnp.concatenate([hr, hi], axis=1).astype(BF16), c_j)
        y = y + dskip_ref[:, cols] * xb
        g = _dot(y.astype(BF16), gmat_ref[j])
        val = g[:, 0:LANES] + bglu_ref[0:1, cols]
        gt = g[:, LANES:2 * LANES] + bglu_ref[1:2, cols]
        oin_sc[:, D_A + j * LANES:D_A + (j + 1) * LANES] = (
            val * jax.nn.sigmoid(gt) * _silu(zb)).astype(BF16)

    o = _dot(oin_sc[...], w_out_ref[...])
    gate = mod_ref[:, 2 * D_MODEL:3 * D_MODEL]
    r = gate * _rms(o, gpost_ref[...])
    for q in range(x_tiles):
        cols = slice(q * LANES, (q + 1) * LANES)
        y2_ref[pl.ds(q, n, stride=x_tiles), :] = x2_ref[pl.ds(q, n, stride=x_tiles), :] + r[:, cols]


def _sample(x2, mod, proj, ln_g, ln_b, w_s, bst, lam_re, lam_im, bmat, cboth, gmat,
            d_skip, b_glu_rows, h0t_re, h0t_im, g_post, w_out_bf):
    n = proj.shape[0]
    n_state = N_GROUPS_B * P_STATE
    out_shapes = ((n * D_MODEL // LANES, LANES), (n * D_A // LANES, LANES), (n_state, n), (n_state, n))
    return pl.pallas_call(
        _sample_kernel,
        out_shape=tuple(jax.ShapeDtypeStruct(shp, F32) for shp in out_shapes),
        scratch_shapes=[pltpu.VMEM((n, D_A + D_B), BF16)],
        compiler_params=pltpu.CompilerParams(vmem_limit_bytes=VMEM_LIMIT),
        name="sample",
    )(x2, mod, proj, ln_g, ln_b, w_s, bst, lam_re, lam_im, bmat, cboth, gmat,
      d_skip, b_glu_rows, h0t_re, h0t_im, g_post, w_out_bf)


def kernel(x_prompt, x_sample, c_prompt, c_sample, state_b_re, state_b_im, w_c, b_c, g_pre, w_in,
           ln_v_g, ln_v_b, w_s, b_s, a_re, a_im, log_dt, b_re, b_im, c_re, c_im, d_skip, w_glu, b_glu,
           w_out, g_post):
    n_p = x_prompt.shape[0]
    n_s = x_sample.shape[0]
    assert x_sample.shape[1] == 1 and n_s % SUBLANES == 0

    bmat, cboth, gmat, lam_re, lam_im, b_glu_rows, bst = _s5_prep(
        a_re, a_im, log_dt, b_re, b_im, c_re, c_im, w_glu, b_glu, b_s)

    row = lambda v: v.reshape(1, -1)
    x2_sample = x_sample.reshape(n_s * D_MODEL // LANES, LANES)
    mod_s, mod_p, w_in_bf, h_sample = _adaln(c_sample, c_prompt, w_c, b_c, w_in, x2_sample, row(g_pre))

    s5 = (w_s, bst, lam_re, lam_im, bmat, cboth, gmat, row(d_skip), b_glu_rows)
    oin, st_re, st_im, w_out_bf, proj_s = _mixer(
        x_prompt, mod_p, row(g_pre), w_in_bf, row(ln_v_g), row(ln_v_b), *s5, h_sample, w_out)
    y_prompt = _outproj(oin, x_prompt, mod_p, row(g_post), w_out_bf)

    to_rows = lambda st: st.transpose(1, 2, 0).reshape(N_GROUPS_B * P_STATE, n_s)
    from_rows = lambda st: st.reshape(N_GROUPS_B, P_STATE, n_s).transpose(2, 0, 1)
    y_s, v_s, hs_re, hs_im = _sample(
        x2_sample, mod_s, proj_s, row(ln_v_g), row(ln_v_b), *s5,
        to_rows(state_b_re), to_rows(state_b_im), row(g_post), w_out_bf)

    return (y_prompt, y_s.reshape(n_s, 1, D_MODEL), v_s.reshape(n_s, 1, D_A),
            st_re.reshape(n_p, N_GROUPS_B, P_STATE), st_im.reshape(n_p, N_GROUPS_B, P_STATE),
            from_rows(hs_re), from_rows(hs_im))
```

```python
import functools

import jax
import jax.numpy as jnp
from jax import lax
from jax.experimental import pallas as pl
from jax.experimental.pallas import tpu as pltpu

F32 = jnp.float32
BF16 = jnp.bfloat16

EPS = 1e-6
D_MODEL = 2048
D_A = 1024
D_B = 1024
D_IN = 3 * D_A + 2 * D_B
CHUNK = 128
HEAD_A = 128
N_HEADS_A = D_A // HEAD_A
GROUP_B = 16
N_GROUPS_B = D_B // GROUP_B
P_STATE = 64

LANES = 128
SUBLANES = 8
SLAB_GROUPS = LANES // GROUP_B
N_SLABS = N_GROUPS_B // SLAB_GROUPS
SLAB_STATES = SLAB_GROUPS * P_STATE
SLAB_TILES = SLAB_STATES // LANES
N_STREAMS = SUBLANES
HALF_SLABS = N_SLABS // 2
STREAM_PITCH = CHUNK + SUBLANES
SCAN_PIECES = 4
PIECE = CHUNK // SCAN_PIECES
V7X_VMEM_BYTES = 64 * 1024 * 1024
VMEM_LIMIT = V7X_VMEM_BYTES - 1024 * 1024
SAMPLE_PROJ_COLS = 512


def _silu(x):
    return x * jax.nn.sigmoid(x)


def _rms_scale(x):
    return lax.rsqrt(jnp.mean(x * x, axis=-1, keepdims=True) + EPS)


def _rms(x, g):
    return x * _rms_scale(x) * g


def _layernorm(x, g, b):
    mu = jnp.mean(x, axis=-1, keepdims=True)
    xc = x - mu
    return xc * lax.rsqrt(jnp.mean(xc * xc, axis=-1, keepdims=True) + EPS) * g + b


def _dot(a, b):
    return jnp.dot(a, b, preferred_element_type=F32)


def _discretize(a_re, a_im, log_dt):
    dt = jnp.exp(log_dt)
    mag = jnp.exp(dt * a_re)
    abar_re = mag * jnp.cos(dt * a_im)
    abar_im = mag * jnp.sin(dt * a_im)
    return abar_re, abar_im


def _nt_dot(a, b):
    return lax.dot_general(a, b, (((1,), (1,)), ((), ())), preferred_element_type=F32)


def _bf16_terms(x):
    hi = x.astype(BF16)
    r1 = x - hi.astype(F32)
    mid = r1.astype(BF16)
    lo = (r1 - mid.astype(F32)).astype(BF16)
    return hi, mid, lo


def _prep_kernel(a8_ref, bt_re_ref, bt_im_ref, c_re_ref, c_im_ref, wglu_ref, bglu_t_ref, bs_ref,
                 bmat_ref, cboth_ref, gmat_ref, lam_re_ref, lam_im_ref, bglu_row_ref, bst_ref):
    a_re = a8_ref[0:N_SLABS, :]
    a_im = a8_ref[N_SLABS:2 * N_SLABS, :]
    abar_re, abar_im = _discretize(a_re, a_im, a8_ref[2 * N_SLABS:3 * N_SLABS, :])
    lam_re_ref[...] = abar_re
    lam_im_ref[...] = abar_im
    num_re = abar_re - 1.0
    num_im = abar_im
    den = a_re * a_re + a_im * a_im
    coef_re = (num_re * a_re + num_im * a_im) / den
    coef_im = (num_im * a_re - num_re * a_im) / den

    def rep_matrix(k, n, period, offset=0):
        row = lax.broadcasted_iota(jnp.int32, (k, n), 0)
        col = lax.broadcasted_iota(jnp.int32, (k, n), 1)
        return jnp.where((col & (period - 1)) + offset == row, 1.0, 0.0).astype(BF16)

    def rep_rows(m, k, period):
        row = lax.broadcasted_iota(jnp.int32, (m, k), 0)
        col = lax.broadcasted_iota(jnp.int32, (m, k), 1)
        return jnp.where((row & (period - 1)) == col, 1.0, 0.0).astype(BF16)

    def block_mask(m, n, row_shift, col_shift):
        row = lax.broadcasted_iota(jnp.int32, (m, n), 0)
        col = lax.broadcasted_iota(jnp.int32, (m, n), 1)
        return (row >> row_shift) == (col >> col_shift)

    rep_state = rep_matrix(P_STATE, SLAB_STATES, P_STATE)
    rows_state = rep_rows(SLAB_STATES, P_STATE, P_STATE)
    rep_val = rep_matrix(2 * GROUP_B, LANES, GROUP_B)
    rep_gate = rep_matrix(2 * GROUP_B, LANES, GROUP_B, GROUP_B)
    mask_b = block_mask(LANES, SLAB_STATES, 4, 6)
    mask_c = block_mask(SLAB_STATES, LANES, 6, 4)
    mask_g = block_mask(LANES, LANES, 4, 4)

    def spread(x, rep):
        return sum(_dot(term, rep) for term in _bf16_terms(x))

    for j in range(N_SLABS):
        crows = slice(j * LANES, (j + 1) * LANES)
        bt_re = spread(bt_re_ref[crows, :], rep_state)
        bt_im = spread(bt_im_ref[crows, :], rep_state)
        cr = coef_re[j:j + 1, :]
        ci = coef_im[j:j + 1, :]
        bmat_ref[j, :, 0:SLAB_STATES] = jnp.where(mask_b, cr * bt_re - ci * bt_im, 0.0).astype(BF16)
        bmat_ref[j, :, SLAB_STATES:2 * SLAB_STATES] = jnp.where(
            mask_b, cr * bt_im + ci * bt_re, 0.0).astype(BF16)

        k, half = j % HALF_SLABS, j // HALF_SLABS
        ccols = slice(half * LANES, (half + 1) * LANES)
        ct_re = _nt_dot(rows_state, c_re_ref[crows, :].astype(BF16))
        ct_im = _nt_dot(rows_state, c_im_ref[crows, :].astype(BF16))
        cboth_ref[k, 0:SLAB_STATES, ccols] = jnp.where(mask_c, ct_re, 0.0).astype(BF16)
        cboth_ref[k, SLAB_STATES:2 * SLAB_STATES, ccols] = jnp.where(mask_c, -ct_im, 0.0).astype(BF16)

        w = wglu_ref[crows, :].astype(BF16)
        gmat_ref[j, :, 0:LANES] = jnp.where(mask_g, _dot(w, rep_val), 0.0).astype(BF16)
        gmat_ref[j, :, LANES:2 * LANES] = jnp.where(mask_g, _dot(w, rep_gate), 0.0).astype(BF16)

    row = lax.broadcasted_iota(jnp.int32, (2 * GROUP_B, D_B), 0)
    col = lax.broadcasted_iota(jnp.int32, (2 * GROUP_B, D_B), 1)
    by_group = spread(bglu_t_ref[...], jnp.where(block_mask(N_GROUPS_B, D_B, 0, 4), 1.0, 0.0).astype(BF16))
    for i in range(2):
        mine = row == (col & (GROUP_B - 1)) + i * GROUP_B
        bglu_row_ref[i:i + 1, :] = jnp.sum(jnp.where(mine, by_group, 0.0), axis=0, keepdims=True)

    eye = rep_rows(CHUNK, CHUNK, CHUNK)
    bst_ref[...] = sum(_nt_dot(eye, term) for term in _bf16_terms(bs_ref[...]))


def _s5_prep(a_re, a_im, log_dt, b_re, b_im, c_re, c_im, w_glu, b_glu, b_s):
    g, p, c = N_GROUPS_B, P_STATE, GROUP_B
    a8 = jnp.stack([a_re, a_im, jnp.broadcast_to(log_dt[:, None], (g, p))]).reshape(3 * N_SLABS, SLAB_STATES)
    args = (a8, b_re.transpose(0, 2, 1).reshape(g * c, p), b_im.transpose(0, 2, 1).reshape(g * c, p),
            c_re.reshape(g * c, p), c_im.reshape(g * c, p), w_glu.reshape(g * c, 2 * c), b_glu.T, b_s)
    return pl.pallas_call(
        _prep_kernel,
        out_shape=(
            jax.ShapeDtypeStruct((N_SLABS, LANES, 2 * SLAB_STATES), BF16),
            jax.ShapeDtypeStruct((HALF_SLABS, 2 * SLAB_STATES, 2 * LANES), BF16),
            jax.ShapeDtypeStruct((N_SLABS, LANES, 2 * LANES), BF16),
            jax.ShapeDtypeStruct((N_SLABS, SLAB_STATES), F32),
            jax.ShapeDtypeStruct((N_SLABS, SLAB_STATES), F32),
            jax.ShapeDtypeStruct((2, D_B), F32),
            jax.ShapeDtypeStruct((CHUNK, N_HEADS_A), F32),
        ),
        name="s5_prep",
    )(*args)


ADALN_ROWS = 256
ADALN_PROMPT_ROWS = 2 * SUBLANES


def _adaln_kernel(cs_ref, cp_ref, w_ref, b_ref, x2_ref, gpre_ref, os_ref, op_ref, hs_ref):
    k = pl.program_id(0)

    @pl.when(k == 0)
    def _():
        os_ref[...] = jnp.broadcast_to(b_ref[...], os_ref.shape)
        op_ref[...] = jnp.broadcast_to(b_ref[...], op_ref.shape)

    cols = pl.ds(pl.multiple_of(k * ADALN_ROWS, ADALN_ROWS), ADALN_ROWS)
    w = w_ref[...].astype(BF16)
    os_ref[...] += _dot(_silu(cs_ref[:, cols]).astype(BF16), w)
    n_p = cp_ref.shape[0]
    cp = jnp.concatenate([cp_ref[:, cols], jnp.zeros((ADALN_PROMPT_ROWS - n_p, ADALN_ROWS), F32)], axis=0)
    op_ref[...] += _dot(_silu(cp).astype(BF16), w)

    @pl.when(k == pl.num_programs(0) - 1)
    def _():
        x_tiles = D_MODEL // LANES
        n = hs_ref.shape[0]
        x = jnp.concatenate([x2_ref[pl.ds(q, n, stride=x_tiles), :] for q in range(x_tiles)], axis=1)
        shift = os_ref[:, 0:D_MODEL]
        scale = os_ref[:, D_MODEL:2 * D_MODEL]
        hs_ref[...] = (_rms(x, gpre_ref[...]) * (1.0 + scale) + shift).astype(BF16)


def _adaln(c_sample, c_prompt, w_c, b_c, x2_sample, g_pre):
    n_s, n_p = c_sample.shape[0], c_prompt.shape[0]
    assert n_p <= ADALN_PROMPT_ROWS and D_MODEL % ADALN_ROWS == 0
    return pl.pallas_call(
        _adaln_kernel,
        grid=(D_MODEL // ADALN_ROWS,),
        in_specs=[
            pl.BlockSpec((n_s, D_MODEL), lambda k: (0, 0)),
            pl.BlockSpec((n_p, D_MODEL), lambda k: (0, 0)),
            pl.BlockSpec((ADALN_ROWS, 3 * D_MODEL), lambda k: (k, 0)),
            pl.BlockSpec((1, 3 * D_MODEL), lambda k: (0, 0)),
            pl.BlockSpec(x2_sample.shape, lambda k: (0, 0)),
            pl.BlockSpec(g_pre.shape, lambda k: (0, 0)),
        ],
        out_specs=(pl.BlockSpec((n_s, 3 * D_MODEL), lambda k: (0, 0)),
                   pl.BlockSpec((ADALN_PROMPT_ROWS, 3 * D_MODEL), lambda k: (0, 0)),
                   pl.BlockSpec((n_s, D_MODEL), lambda k: (0, 0))),
        out_shape=(jax.ShapeDtypeStruct((n_s, 3 * D_MODEL), F32),
                   jax.ShapeDtypeStruct((ADALN_PROMPT_ROWS, 3 * D_MODEL), F32),
                   jax.ShapeDtypeStruct((n_s, D_MODEL), BF16)),
        compiler_params=pltpu.CompilerParams(
            dimension_semantics=("arbitrary",), vmem_limit_bytes=VMEM_LIMIT),
        name="adaln",
    )(c_sample, c_prompt, w_c, b_c.reshape(1, -1), x2_sample, g_pre)


def _mixer_kernel(x_ref, mod_ref, gpre_ref, w_in_hbm, lng_ref, lnb_ref, ws_ref, bst_ref,
                  lam_re_ref, lam_im_ref, bmat_ref, cboth_ref, gmat_ref, dskip_ref, bglu_ref, hs_ref, wout_ref,
                  oin_ref, st_re_ref, st_im_ref, wout_bf_ref, ps_ref,
                  h_sc, a_sc, b_sc, c_sc, v_sc, xb_sc, bu_sc, y2_sc, st_sc, w_in_ref, w_sem, *, n_seq):
    step = pl.program_id(0)

    @pl.when(step == 0)
    def _():
        st_sc[...] = jnp.zeros_like(st_sc)

        stage = (a_sc, b_sc, c_sc)
        t_rows, t_cols = a_sc.shape
        tiles = [(r, c) for c in range(D_IN // t_cols) for r in range(D_MODEL // t_rows)]

        def tile_copy(n):
            r, c = tiles[n]
            src = w_in_hbm.at[pl.ds(r * t_rows, t_rows), pl.ds(c * t_cols, t_cols)]
            return pltpu.make_async_copy(src, stage[n % len(stage)], w_sem.at[n % len(stage)])

        for n in range(len(stage)):
            tile_copy(n).start()
        for n, (r, c) in enumerate(tiles):
            tile_copy(n).wait()
            w_in_ref[r * t_rows:(r + 1) * t_rows, c * t_cols:(c + 1) * t_cols] = (
                stage[n % len(stage)][...].astype(BF16))
            if n + len(stage) < len(tiles):
                tile_copy(n + len(stage)).start()

    wout_bf_ref[...] = wout_ref[...].astype(BF16)

    @pl.when(step < D_IN // SAMPLE_PROJ_COLS)
    def _():
        cols = pl.ds(pl.multiple_of(step * SAMPLE_PROJ_COLS, SAMPLE_PROJ_COLS), SAMPLE_PROJ_COLS)
        ps_ref[...] = _dot(hs_ref[...], w_in_ref[:, cols])

    w_u, w_v, w_z, w_xb, w_zb = 0, D_A, 2 * D_A, 3 * D_A, 3 * D_A + D_B

    for b in range(n_seq):
        x = x_ref[b]
        shift = mod_ref[b:b + 1, 0:D_MODEL]
        gain = gpre_ref[...] * (1.0 + mod_ref[b:b + 1, D_MODEL:2 * D_MODEL])
        rws = slice(b * CHUNK, (b + 1) * CHUNK)
        h_sc[rws, :] = (x * _rms_scale(x) * gain + shift).astype(BF16)
        b_sc[rws, :] = _dot(h_sc[rws, :], w_in_ref[:, w_xb:w_xb + D_B])
        xb_sc[rws, :] = b_sc[rws, :].astype(BF16)

    h = h_sc[...]

    tril = (lax.broadcasted_iota(jnp.int32, (CHUNK, CHUNK), 0)
            >= lax.broadcasted_iota(jnp.int32, (CHUNK, CHUNK), 1))
    n_fill = HALF_SLABS * SCAN_PIECES // 4
    fill_cols = D_A // n_fill

    def fill_u(q):
        cols = slice(q * fill_cols, (q + 1) * fill_cols)
        a_sc[:, cols] = _dot(h, w_in_ref[:, w_u + q * fill_cols:w_u + (q + 1) * fill_cols])

    def fill_z(q):
        cols = slice(q * fill_cols, (q + 1) * fill_cols)
        z = _dot(h, w_in_ref[:, w_z + q * fill_cols:w_z + (q + 1) * fill_cols])
        a_sc[:, cols] = a_sc[:, cols] * _silu(z)

    def fill_mix(q):
        heads_per_fill = N_HEADS_A // n_fill
        for hd in range(q * heads_per_fill, (q + 1) * heads_per_fill):
            cols = slice(hd * HEAD_A, (hd + 1) * HEAD_A)
            w_t = jnp.where(tril, ws_ref[hd], 0.0).astype(BF16)
            v_h = jnp.concatenate([v_sc[b * CHUNK:(b + 1) * CHUNK, cols] for b in range(n_seq)], axis=1)
            mix = _dot(w_t, v_h) + bst_ref[:, hd:hd + 1]
            for b in range(n_seq):
                rws = slice(b * CHUNK, (b + 1) * CHUNK)
                oin_ref[rws, cols] = (a_sc[rws, cols] * mix[:, b * HEAD_A:(b + 1) * HEAD_A]).astype(BF16)

    def fill_zb(q):
        cols = slice(q * fill_cols, (q + 1) * fill_cols)
        a_sc[:, cols] = _silu(_dot(h, w_in_ref[:, w_zb + q * fill_cols:w_zb + (q + 1) * fill_cols]))

    fillers = [functools.partial(f, q) for f, q in (
        (fill_u, 0), (fill_z, 0), (fill_u, 1), (fill_mix, 0), (fill_z, 1), (fill_u, 2), (fill_z, 2), (fill_mix, 1),
        (fill_u, 3), (fill_z, 3), (fill_zb, 0), (fill_mix, 2), (fill_zb, 1), (fill_zb, 2), (fill_mix, 3), (fill_zb, 3))]
    assert len(fillers) == HALF_SLABS * SCAN_PIECES

    first_half = lax.broadcasted_iota(jnp.int32, (N_STREAMS, LANES), 0) < n_seq

    def bu_piece(k, piece):
        t0 = piece * PIECE
        for half in range(2):
            j = half * HALF_SLABS + k
            lhs = jnp.concatenate(
                [xb_sc[b * CHUNK + t0:b * CHUNK + t0 + PIECE, j * LANES:(j + 1) * LANES]
                 for b in range(n_seq)], axis=0)
            bu = _dot(lhs, bmat_ref[j])
            for b in range(n_seq):
                r0 = (half * n_seq + b) * STREAM_PITCH + t0
                for l in range(2 * SLAB_TILES):
                    bu_sc[l, r0:r0 + PIECE, :] = bu[b * PIECE:(b + 1) * PIECE, l * LANES:(l + 1) * LANES]

    def scan_piece(k, piece, hr, hi, lam_r, lam_i):
        rows = []
        for t in range(piece * PIECE, (piece + 1) * PIECE):
            new_r, new_i = [], []
            for l in range(SLAB_TILES):
                br = bu_sc[l, pl.ds(t, N_STREAMS, stride=STREAM_PITCH), :]
                bi = bu_sc[SLAB_TILES + l, pl.ds(t, N_STREAMS, stride=STREAM_PITCH), :]
                new_r.append(lam_r[l] * hr[l] - lam_i[l] * hi[l] + br)
                new_i.append(lam_r[l] * hi[l] + lam_i[l] * hr[l] + bi)
            hr, hi = new_r, new_i
            rows.append(jnp.concatenate(hr + hi, axis=1))
        return hr, hi, jnp.concatenate(rows, axis=0)

    def c_piece(k, piece, hh):
        t0 = piece * PIECE
        rows = slice(t0 * N_STREAMS, (t0 + PIECE) * N_STREAMS)
        y2 = _dot(hh.astype(BF16), cboth_ref[k])
        y2_sc[0, rows, :] = y2[:, 0:LANES]
        y2_sc[1, rows, :] = y2[:, LANES:2 * LANES]
        for half in range(2):
            j = half * HALF_SLABS + k
            for b in range(n_seq):
                s = half * n_seq + b
                c_sc[b * CHUNK + t0:b * CHUNK + t0 + PIECE, j * LANES:(j + 1) * LANES] = (
                    y2_sc[half, pl.ds(t0 * N_STREAMS + s, PIECE, stride=N_STREAMS), :])

    for piece in range(SCAN_PIECES):
        bu_piece(0, piece)
    c_sc[...] = _dot(h, w_in_ref[:, w_v:w_v + D_A])
    v_sc[...] = _layernorm(c_sc[...], lng_ref[...], lnb_ref[...]).astype(BF16)
    for k in range(HALF_SLABS):
        tiles = [slice(l * LANES, (l + 1) * LANES) for l in range(SLAB_TILES)]
        lam_r = [jnp.where(first_half, lam_re_ref[k:k + 1, t], lam_re_ref[HALF_SLABS + k:HALF_SLABS + k + 1, t])
                 for t in tiles]
        lam_i = [jnp.where(first_half, lam_im_ref[k:k + 1, t], lam_im_ref[HALF_SLABS + k:HALF_SLABS + k + 1, t])
                 for t in tiles]
        hr = [st_sc[k, :, l * LANES:(l + 1) * LANES] for l in range(SLAB_TILES)]
        hi = [st_sc[k, :, (SLAB_TILES + l) * LANES:(SLAB_TILES + l + 1) * LANES] for l in range(SLAB_TILES)]
        for piece in range(SCAN_PIECES):
            hr, hi, hh = scan_piece(k, piece, hr, hi, lam_r, lam_i)
            c_piece(k, piece, hh)
            if k + 1 < HALF_SLABS:
                bu_piece(k + 1, piece)
            fillers[k * SCAN_PIECES + piece]()
        st_sc[k] = jnp.concatenate(hr + hi, axis=1)

    for k in range(HALF_SLABS):
        for half in range(2):
            j = half * HALF_SLABS + k
            for b in range(n_seq):
                s = half * n_seq + b
                st_re_ref[b, j:j + 1, :] = st_sc[k, s:s + 1, 0:SLAB_STATES]
                st_im_ref[b, j:j + 1, :] = st_sc[k, s:s + 1, SLAB_STATES:2 * SLAB_STATES]

    for j in range(N_SLABS):
        cols = slice(j * LANES, (j + 1) * LANES)
        y = c_sc[:, cols] + dskip_ref[:, cols] * b_sc[:, cols]
        g = _dot(y.astype(BF16), gmat_ref[j])
        val = g[:, 0:LANES] + bglu_ref[0:1, cols]
        gate = g[:, LANES:2 * LANES] + bglu_ref[1:2, cols]
        oin_ref[:, D_A + j * LANES:D_A + (j + 1) * LANES] = (
            val * jax.nn.sigmoid(gate) * a_sc[:, cols]).astype(BF16)


def _const_spec(shape):
    zeros = (0,) * len(shape)
    return pl.BlockSpec(shape, lambda i: zeros, pipeline_mode=pl.Buffered(1))


def _mixer(x, mod, g_pre, w_in, ln_g, ln_b, w_s, bst, lam_re, lam_im, bmat, cboth, gmat,
           d_skip, b_glu_rows, h_sample, w_out):
    n_seq, seq, _ = x.shape
    assert 2 * n_seq == N_STREAMS and seq % CHUNK == 0
    rows = n_seq * CHUNK
    n_steps = seq // CHUNK
    wout_rows = w_out.shape[0] // n_steps
    assert wout_rows * n_steps == w_out.shape[0] and wout_rows % (2 * SUBLANES) == 0
    state_shape = (n_seq, N_SLABS, SLAB_STATES)
    n_sample = h_sample.shape[0]
    n_proj = D_IN // SAMPLE_PROJ_COLS
    assert n_proj <= n_steps
    consts = (mod, g_pre, w_in, ln_g, ln_b, w_s, bst, lam_re, lam_im, bmat, cboth, gmat,
              d_skip, b_glu_rows, h_sample)
    assert w_in.shape == (D_MODEL, D_IN) and D_MODEL % rows == 0 and D_IN % D_A == 0
    return pl.pallas_call(
        functools.partial(_mixer_kernel, n_seq=n_seq),
        grid=(n_steps,),
        in_specs=[pl.BlockSpec((n_seq, CHUNK, D_MODEL), lambda i: (0, i, 0))]
        + [pl.BlockSpec(memory_space=pl.ANY) if c is w_in else _const_spec(c.shape) for c in consts]
        + [pl.BlockSpec((wout_rows, w_out.shape[1]), lambda i: (i, 0))],
        out_specs=(
            pl.BlockSpec((rows, D_A + D_B), lambda i: (i, 0)),
            pl.BlockSpec(state_shape, lambda i: (0, 0, 0)),
            pl.BlockSpec(state_shape, lambda i: (0, 0, 0)),
            pl.BlockSpec((wout_rows, w_out.shape[1]), lambda i: (i, 0)),
            pl.BlockSpec((n_sample, SAMPLE_PROJ_COLS), lambda i: (0, jnp.minimum(i, n_proj - 1))),
        ),
        out_shape=(
            jax.ShapeDtypeStruct((n_steps * rows, D_A + D_B), BF16),
            jax.ShapeDtypeStruct(state_shape, F32),
            jax.ShapeDtypeStruct(state_shape, F32),
            jax.ShapeDtypeStruct(w_out.shape, BF16),
            jax.ShapeDtypeStruct((n_sample, D_IN), F32),
        ),
        scratch_shapes=[
            pltpu.VMEM((rows, D_MODEL), BF16),
            pltpu.VMEM((rows, D_A), F32),
            pltpu.VMEM((rows, D_A), F32),
            pltpu.VMEM((rows, D_A), F32),
            pltpu.VMEM((rows, D_A), BF16),
            pltpu.VMEM((rows, D_B), BF16),
            pltpu.VMEM((2 * SLAB_TILES, N_STREAMS * STREAM_PITCH, LANES), F32),
            pltpu.VMEM((2, N_STREAMS * CHUNK, LANES), F32),
            pltpu.VMEM((HALF_SLABS, N_STREAMS, 2 * SLAB_STATES), F32),
            pltpu.VMEM((D_MODEL, D_IN), BF16),
            pltpu.SemaphoreType.DMA((3,)),
        ],
        compiler_params=pltpu.CompilerParams(
            dimension_semantics=("arbitrary",), vmem_limit_bytes=VMEM_LIMIT),
        name="mixer",
    )(x, *consts, w_out)


def _outproj_kernel(oin_ref, x_ref, mod_ref, gpost_ref, w_out_ref, y_ref, *, n_seq):
    o = _dot(oin_ref[...], w_out_ref[...])
    r = o * _rms_scale(o)
    for b in range(n_seq):
        gain = gpost_ref[...] * mod_ref[b:b + 1, 2 * D_MODEL:3 * D_MODEL]
        y_ref[b] = x_ref[b] + r[b * CHUNK:(b + 1) * CHUNK, :] * gain


def _outproj(oin, x, mod, g_post, w_out_bf):
    n_seq, seq, _ = x.shape
    rows = n_seq * CHUNK
    return pl.pallas_call(
        functools.partial(_outproj_kernel, n_seq=n_seq),
        grid=(seq // CHUNK,),
        in_specs=[
            pl.BlockSpec((rows, D_A + D_B), lambda i: (i, 0)),
            pl.BlockSpec((n_seq, CHUNK, D_MODEL), lambda i: (0, i, 0)),
            _const_spec(mod.shape), _const_spec(g_post.shape), _const_spec(w_out_bf.shape),
        ],
        out_specs=pl.BlockSpec((n_seq, CHUNK, D_MODEL), lambda i: (0, i, 0)),
        out_shape=jax.ShapeDtypeStruct(x.shape, F32),
        compiler_params=pltpu.CompilerParams(
            dimension_semantics=("arbitrary",), vmem_limit_bytes=VMEM_LIMIT),
        name="outproj",
    )(oin, x, mod, g_post, w_out_bf)


def _sample_kernel(x2_ref, mod_ref, proj_ref, lng_ref, lnb_ref, ws_ref, bst_ref,
                   lam_re_ref, lam_im_ref, bmat_ref, cboth_ref, gmat_ref, dskip_ref, bglu_ref,
                   h0t_re_ref, h0t_im_ref, gpost_ref, w_out_ref,
                   y2_ref, v2_ref, hst_re_ref, hst_im_ref, oin_sc):
    n = proj_ref.shape[0]
    x_tiles = D_MODEL // LANES
    v_tiles = D_A // LANES
    w_u, w_v, w_z, w_xb, w_zb = 0, D_A, 2 * D_A, 3 * D_A, 3 * D_A + D_B

    v = _layernorm(proj_ref[:, w_v:w_v + D_A], lng_ref[...], lnb_ref[...])
    for q in range(v_tiles):
        v2_ref[pl.ds(q, n, stride=v_tiles), :] = v[:, q * LANES:(q + 1) * LANES]
    for hd in range(N_HEADS_A):
        cols = slice(hd * HEAD_A, (hd + 1) * HEAD_A)
        mix = ws_ref[hd, 0:1, 0:1] * v[:, cols] + bst_ref[0:1, hd:hd + 1]
        u = proj_ref[:, w_u + hd * HEAD_A:w_u + (hd + 1) * HEAD_A]
        z = proj_ref[:, w_z + hd * HEAD_A:w_z + (hd + 1) * HEAD_A]
        oin_sc[:, cols] = (u * mix * _silu(z)).astype(BF16)

    for j in range(N_SLABS):
        cols = slice(j * LANES, (j + 1) * LANES)
        st = slice(j * SLAB_STATES, (j + 1) * SLAB_STATES)
        xb = proj_ref[:, w_xb + j * LANES:w_xb + (j + 1) * LANES]
        zb = proj_ref[:, w_zb + j * LANES:w_zb + (j + 1) * LANES]
        bu = _dot(xb.astype(BF16), bmat_ref[j])
        lr = lam_re_ref[j:j + 1, :]
        li = lam_im_ref[j:j + 1, :]
        h0r = h0t_re_ref[st, :].T
        h0i = h0t_im_ref[st, :].T
        hr = lr * h0r - li * h0i + bu[:, 0:SLAB_STATES]
        hi = lr * h0i + li * h0r + bu[:, SLAB_STATES:2 * SLAB_STATES]
        hst_re_ref[st, :] = hr.T
        hst_im_ref[st, :] = hi.T
        k, half = j % HALF_SLABS, j // HALF_SLABS
        c_j = cboth_ref[k, :, half * LANES:(half + 1) * LANES]
        y = _dot(jnp.concatenate([hr, hi], axis=1).astype(BF16), c_j)
        y = y + dskip_ref[:, cols] * xb
        g = _dot(y.astype(BF16), gmat_ref[j])
        val = g[:, 0:LANES] + bglu_ref[0:1, cols]
        gt = g[:, LANES:2 * LANES] + bglu_ref[1:2, cols]
        oin_sc[:, D_A + j * LANES:D_A + (j + 1) * LANES] = (
            val * jax.nn.sigmoid(gt) * _silu(zb)).astype(BF16)

    o = _dot(oin_sc[...], w_out_ref[...])
    gate = mod_ref[:, 2 * D_MODEL:3 * D_MODEL]
    r = gate * _rms(o, gpost_ref[...])
    for q in range(x_tiles):
        cols = slice(q * LANES, (q + 1) * LANES)
        y2_ref[pl.ds(q, n, stride=x_tiles), :] = x2_ref[pl.ds(q, n, stride=x_tiles), :] + r[:, cols]


def _sample(x2, mod, proj, ln_g, ln_b, w_s, bst, lam_re, lam_im, bmat, cboth, gmat,
            d_skip, b_glu_rows, h0t_re, h0t_im, g_post, w_out_bf):
    n = proj.shape[0]
    n_state = N_GROUPS_B * P_STATE
    out_shapes = ((n * D_MODEL // LANES, LANES), (n * D_A // LANES, LANES), (n_state, n), (n_state, n))
    return pl.pallas_call(
        _sample_kernel,
        out_shape=tuple(jax.ShapeDtypeStruct(shp, F32) for shp in out_shapes),
        scratch_shapes=[pltpu.VMEM((n, D_A + D_B), BF16)],
        compiler_params=pltpu.CompilerParams(vmem_limit_bytes=VMEM_LIMIT),
        name="sample",
    )(x2, mod, proj, ln_g, ln_b, w_s, bst, lam_re, lam_im, bmat, cboth, gmat,
      d_skip, b_glu_rows, h0t_re, h0t_im, g_post, w_out_bf)


def kernel(x_prompt, x_sample, c_prompt, c_sample, state_b_re, state_b_im, w_c, b_c, g_pre, w_in,
           ln_v_g, ln_v_b, w_s, b_s, a_re, a_im, log_dt, b_re, b_im, c_re, c_im, d_skip, w_glu, b_glu,
           w_out, g_post):
    n_p = x_prompt.shape[0]
    n_s = x_sample.shape[0]
    assert x_sample.shape[1] == 1 and n_s % SUBLANES == 0

    bmat, cboth, gmat, lam_re, lam_im, b_glu_rows, bst = _s5_prep(
        a_re, a_im, log_dt, b_re, b_im, c_re, c_im, w_glu, b_glu, b_s)

    row = lambda v: v.reshape(1, -1)
    x2_sample = x_sample.reshape(n_s * D_MODEL // LANES, LANES)
    mod_s, mod_p, h_sample = _adaln(c_sample, c_prompt, w_c, b_c, x2_sample, row(g_pre))

    s5 = (w_s, bst, lam_re, lam_im, bmat, cboth, gmat, row(d_skip), b_glu_rows)
    oin, st_re, st_im, w_out_bf, proj_s = _mixer(
        x_prompt, mod_p, row(g_pre), w_in, row(ln_v_g), row(ln_v_b), *s5, h_sample, w_out)
    y_prompt = _outproj(oin, x_prompt, mod_p, row(g_post), w_out_bf)

    to_rows = lambda st: st.transpose(1, 2, 0).reshape(N_GROUPS_B * P_STATE, n_s)
    from_rows = lambda st: st.reshape(N_GROUPS_B, P_STATE, n_s).transpose(2, 0, 1)
    y_s, v_s, hs_re, hs_im = _sample(
        x2_sample, mod_s, proj_s, row(ln_v_g), row(ln_v_b), *s5,
        to_rows(state_b_re), to_rows(state_b_im), row(g_post), w_out_bf)

    return (y_prompt, y_s.reshape(n_s, 1, D_MODEL), v_s.reshape(n_s, 1, D_A),
            st_re.reshape(n_p, N_GROUPS_B, P_STATE), st_im.reshape(n_p, N_GROUPS_B, P_STATE),
            from_rows(hs_re), from_rows(hs_im))
```

```python
import functools

import jax
import jax.numpy as jnp
from jax import lax
from jax.experimental import pallas as pl
from jax.experimental.pallas import tpu as pltpu

F32 = jnp.float32
BF16 = jnp.bfloat16

EPS = 1e-6
D_MODEL = 2048
D_A = 1024
D_B = 1024
D_IN = 3 * D_A + 2 * D_B
CHUNK = 128
HEAD_A = 128
N_HEADS_A = D_A // HEAD_A
GROUP_B = 16
N_GROUPS_B = D_B // GROUP_B
P_STATE = 64

LANES = 128
SUBLANES = 8
SLAB_GROUPS = LANES // GROUP_B
N_SLABS = N_GROUPS_B // SLAB_GROUPS
SLAB_STATES = SLAB_GROUPS * P_STATE
SLAB_TILES = SLAB_STATES // LANES
N_STREAMS = SUBLANES
HALF_SLABS = N_SLABS // 2
STREAM_PITCH = CHUNK + SUBLANES
SCAN_PIECES = 4
PIECE = CHUNK // SCAN_PIECES
V7X_VMEM_BYTES = 64 * 1024 * 1024
VMEM_LIMIT = V7X_VMEM_BYTES - 1024 * 1024
SAMPLE_PROJ_COLS = 512
W_IN_RING = 6


def _silu(x):
    return x * jax.nn.sigmoid(x)


def _rms_scale(x):
    return lax.rsqrt(jnp.mean(x * x, axis=-1, keepdims=True) + EPS)


def _rms(x, g):
    return x * _rms_scale(x) * g


def _layernorm(x, g, b):
    mu = jnp.mean(x, axis=-1, keepdims=True)
    xc = x - mu
    return xc * lax.rsqrt(jnp.mean(xc * xc, axis=-1, keepdims=True) + EPS) * g + b


def _dot(a, b):
    return jnp.dot(a, b, preferred_element_type=F32)


def _discretize(a_re, a_im, log_dt):
    dt = jnp.exp(log_dt)
    mag = jnp.exp(dt * a_re)
    abar_re = mag * jnp.cos(dt * a_im)
    abar_im = mag * jnp.sin(dt * a_im)
    return abar_re, abar_im


def _nt_dot(a, b):
    return lax.dot_general(a, b, (((1,), (1,)), ((), ())), preferred_element_type=F32)


def _bf16_terms(x):
    hi = x.astype(BF16)
    r1 = x - hi.astype(F32)
    mid = r1.astype(BF16)
    lo = (r1 - mid.astype(F32)).astype(BF16)
    return hi, mid, lo


def _prep_kernel(a8_ref, bt_re_ref, bt_im_ref, c_re_ref, c_im_ref, wglu_ref, bglu_t_ref, bs_ref,
                 bmat_ref, cboth_ref, gmat_ref, lam_re_ref, lam_im_ref, bglu_row_ref, bst_ref):
    a_re = a8_ref[0:N_SLABS, :]
    a_im = a8_ref[N_SLABS:2 * N_SLABS, :]
    abar_re, abar_im = _discretize(a_re, a_im, a8_ref[2 * N_SLABS:3 * N_SLABS, :])
    lam_re_ref[...] = abar_re
    lam_im_ref[...] = abar_im
    num_re = abar_re - 1.0
    num_im = abar_im
    den = a_re * a_re + a_im * a_im
    coef_re = (num_re * a_re + num_im * a_im) / den
    coef_im = (num_im * a_re - num_re * a_im) / den

    def rep_matrix(k, n, period, offset=0):
        row = lax.broadcasted_iota(jnp.int32, (k, n), 0)
        col = lax.broadcasted_iota(jnp.int32, (k, n), 1)
        return jnp.where((col & (period - 1)) + offset == row, 1.0, 0.0).astype(BF16)

    def rep_rows(m, k, period):
        row = lax.broadcasted_iota(jnp.int32, (m, k), 0)
        col = lax.broadcasted_iota(jnp.int32, (m, k), 1)
        return jnp.where((row & (period - 1)) == col, 1.0, 0.0).astype(BF16)

    def block_mask(m, n, row_shift, col_shift):
        row = lax.broadcasted_iota(jnp.int32, (m, n), 0)
        col = lax.broadcasted_iota(jnp.int32, (m, n), 1)
        return (row >> row_shift) == (col >> col_shift)

    rep_state = rep_matrix(P_STATE, SLAB_STATES, P_STATE)
    rows_state = rep_rows(SLAB_STATES, P_STATE, P_STATE)
    rep_val = rep_matrix(2 * GROUP_B, LANES, GROUP_B)
    rep_gate = rep_matrix(2 * GROUP_B, LANES, GROUP_B, GROUP_B)
    mask_b = block_mask(LANES, SLAB_STATES, 4, 6)
    mask_c = block_mask(SLAB_STATES, LANES, 6, 4)
    mask_g = block_mask(LANES, LANES, 4, 4)

    def spread(x, rep):
        return sum(_dot(term, rep) for term in _bf16_terms(x))

    for j in range(N_SLABS):
        crows = slice(j * LANES, (j + 1) * LANES)
        bt_re = spread(bt_re_ref[crows, :], rep_state)
        bt_im = spread(bt_im_ref[crows, :], rep_state)
        cr = coef_re[j:j + 1, :]
        ci = coef_im[j:j + 1, :]
        bmat_ref[j, :, 0:SLAB_STATES] = jnp.where(mask_b, cr * bt_re - ci * bt_im, 0.0).astype(BF16)
        bmat_ref[j, :, SLAB_STATES:2 * SLAB_STATES] = jnp.where(
            mask_b, cr * bt_im + ci * bt_re, 0.0).astype(BF16)

        k, half = j % HALF_SLABS, j // HALF_SLABS
        ccols = slice(half * LANES, (half + 1) * LANES)
        ct_re = _nt_dot(rows_state, c_re_ref[crows, :].astype(BF16))
        ct_im = _nt_dot(rows_state, c_im_ref[crows, :].astype(BF16))
        cboth_ref[k, 0:SLAB_STATES, ccols] = jnp.where(mask_c, ct_re, 0.0).astype(BF16)
        cboth_ref[k, SLAB_STATES:2 * SLAB_STATES, ccols] = jnp.where(mask_c, -ct_im, 0.0).astype(BF16)

        w = wglu_ref[crows, :].astype(BF16)
        gmat_ref[j, :, 0:LANES] = jnp.where(mask_g, _dot(w, rep_val), 0.0).astype(BF16)
        gmat_ref[j, :, LANES:2 * LANES] = jnp.where(mask_g, _dot(w, rep_gate), 0.0).astype(BF16)

    row = lax.broadcasted_iota(jnp.int32, (2 * GROUP_B, D_B), 0)
    col = lax.broadcasted_iota(jnp.int32, (2 * GROUP_B, D_B), 1)
    by_group = spread(bglu_t_ref[...], jnp.where(block_mask(N_GROUPS_B, D_B, 0, 4), 1.0, 0.0).astype(BF16))
    for i in range(2):
        mine = row == (col & (GROUP_B - 1)) + i * GROUP_B
        bglu_row_ref[i:i + 1, :] = jnp.sum(jnp.where(mine, by_group, 0.0), axis=0, keepdims=True)

    eye = rep_rows(CHUNK, CHUNK, CHUNK)
    bst_ref[...] = sum(_nt_dot(eye, term) for term in _bf16_terms(bs_ref[...]))


def _s5_prep(a_re, a_im, log_dt, b_re, b_im, c_re, c_im, w_glu, b_glu, b_s):
    g, p, c = N_GROUPS_B, P_STATE, GROUP_B
    a8 = jnp.stack([a_re, a_im, jnp.broadcast_to(log_dt[:, None], (g, p))]).reshape(3 * N_SLABS, SLAB_STATES)
    args = (a8, b_re.transpose(0, 2, 1).reshape(g * c, p), b_im.transpose(0, 2, 1).reshape(g * c, p),
            c_re.reshape(g * c, p), c_im.reshape(g * c, p), w_glu.reshape(g * c, 2 * c), b_glu.T, b_s)
    return pl.pallas_call(
        _prep_kernel,
        out_shape=(
            jax.ShapeDtypeStruct((N_SLABS, LANES, 2 * SLAB_STATES), BF16),
            jax.ShapeDtypeStruct((HALF_SLABS, 2 * SLAB_STATES, 2 * LANES), BF16),
            jax.ShapeDtypeStruct((N_SLABS, LANES, 2 * LANES), BF16),
            jax.ShapeDtypeStruct((N_SLABS, SLAB_STATES), F32),
            jax.ShapeDtypeStruct((N_SLABS, SLAB_STATES), F32),
            jax.ShapeDtypeStruct((2, D_B), F32),
            jax.ShapeDtypeStruct((CHUNK, N_HEADS_A), F32),
        ),
        name="s5_prep",
    )(*args)


ADALN_ROWS = 512
ADALN_PROMPT_ROWS = 2 * SUBLANES


def _adaln_kernel(cs_ref, cp_ref, w_ref, b_ref, x2_ref, gpre_ref, os_ref, op_ref, hs_ref):
    k = pl.program_id(0)

    @pl.when(k == 0)
    def _():
        os_ref[...] = jnp.broadcast_to(b_ref[...], os_ref.shape)
        op_ref[...] = jnp.broadcast_to(b_ref[...], op_ref.shape)

    cols = pl.ds(pl.multiple_of(k * ADALN_ROWS, ADALN_ROWS), ADALN_ROWS)
    w = w_ref[...].astype(BF16)
    os_ref[...] += _dot(_silu(cs_ref[:, cols]).astype(BF16), w)
    n_p = cp_ref.shape[0]
    cp = jnp.concatenate([cp_ref[:, cols], jnp.zeros((ADALN_PROMPT_ROWS - n_p, ADALN_ROWS), F32)], axis=0)
    op_ref[...] += _dot(_silu(cp).astype(BF16), w)

    @pl.when(k == pl.num_programs(0) - 1)
    def _():
        x_tiles = D_MODEL // LANES
        n = hs_ref.shape[0]
        x = jnp.concatenate([x2_ref[pl.ds(q, n, stride=x_tiles), :] for q in range(x_tiles)], axis=1)
        shift = os_ref[:, 0:D_MODEL]
        scale = os_ref[:, D_MODEL:2 * D_MODEL]
        hs_ref[...] = (_rms(x, gpre_ref[...]) * (1.0 + scale) + shift).astype(BF16)


def _adaln(c_sample, c_prompt, w_c, b_c, x2_sample, g_pre):
    n_s, n_p = c_sample.shape[0], c_prompt.shape[0]
    assert n_p <= ADALN_PROMPT_ROWS and D_MODEL % ADALN_ROWS == 0
    return pl.pallas_call(
        _adaln_kernel,
        grid=(D_MODEL // ADALN_ROWS,),
        in_specs=[
            pl.BlockSpec((n_s, D_MODEL), lambda k: (0, 0)),
            pl.BlockSpec((n_p, D_MODEL), lambda k: (0, 0)),
            pl.BlockSpec((ADALN_ROWS, 3 * D_MODEL), lambda k: (k, 0)),
            pl.BlockSpec((1, 3 * D_MODEL), lambda k: (0, 0)),
            pl.BlockSpec(x2_sample.shape, lambda k: (0, 0)),
            pl.BlockSpec(g_pre.shape, lambda k: (0, 0)),
        ],
        out_specs=(pl.BlockSpec((n_s, 3 * D_MODEL), lambda k: (0, 0)),
                   pl.BlockSpec((ADALN_PROMPT_ROWS, 3 * D_MODEL), lambda k: (0, 0)),
                   pl.BlockSpec((n_s, D_MODEL), lambda k: (0, 0))),
        out_shape=(jax.ShapeDtypeStruct((n_s, 3 * D_MODEL), F32),
                   jax.ShapeDtypeStruct((ADALN_PROMPT_ROWS, 3 * D_MODEL), F32),
                   jax.ShapeDtypeStruct((n_s, D_MODEL), BF16)),
        compiler_params=pltpu.CompilerParams(
            dimension_semantics=("arbitrary",), vmem_limit_bytes=VMEM_LIMIT),
        name="adaln",
    )(c_sample, c_prompt, w_c, b_c.reshape(1, -1), x2_sample, g_pre)


def _mixer_kernel(x_ref, mod_ref, gpre_ref, w_in_hbm, lng_ref, lnb_ref, ws_ref, bst_ref,
                  lam_re_ref, lam_im_ref, bmat_ref, cboth_ref, gmat_ref, dskip_ref, bglu_ref, hs_ref, wout_ref,
                  oin_ref, st_re_ref, st_im_ref, wout_bf_ref, ps_ref,
                  h_sc, a_sc, b_sc, c_sc, v_sc, xb_sc, bu_sc, y2_sc, st_sc, w_in_ref, w_sem, *, n_seq):
    step = pl.program_id(0)

    @pl.when(step == 0)
    def _():
        st_sc[...] = jnp.zeros_like(st_sc)

        t_rows, t_cols = a_sc.shape[0] // 2, a_sc.shape[1]
        stage = [buf.at[pl.ds(i * t_rows, t_rows), :] for buf in (a_sc, b_sc, c_sc) for i in range(2)]
        assert len(stage) == W_IN_RING
        tiles = [(r, c) for c in range(D_IN // t_cols) for r in range(D_MODEL // t_rows)]

        def tile_copy(n):
            r, c = tiles[n]
            src = w_in_hbm.at[pl.ds(r * t_rows, t_rows), pl.ds(c * t_cols, t_cols)]
            return pltpu.make_async_copy(src, stage[n % len(stage)], w_sem.at[n % len(stage)])

        for n in range(len(stage)):
            tile_copy(n).start()
        for n, (r, c) in enumerate(tiles):
            tile_copy(n).wait()
            w_in_ref[r * t_rows:(r + 1) * t_rows, c * t_cols:(c + 1) * t_cols] = (
                stage[n % len(stage)][...].astype(BF16))
            if n + len(stage) < len(tiles):
                tile_copy(n + len(stage)).start()

    wout_bf_ref[...] = wout_ref[...].astype(BF16)

    @pl.when(step < D_IN // SAMPLE_PROJ_COLS)
    def _():
        cols = pl.ds(pl.multiple_of(step * SAMPLE_PROJ_COLS, SAMPLE_PROJ_COLS), SAMPLE_PROJ_COLS)
        ps_ref[...] = _dot(hs_ref[...], w_in_ref[:, cols])

    w_u, w_v, w_z, w_xb, w_zb = 0, D_A, 2 * D_A, 3 * D_A, 3 * D_A + D_B

    for b in range(n_seq):
        x = x_ref[b]
        shift = mod_ref[b:b + 1, 0:D_MODEL]
        gain = gpre_ref[...] * (1.0 + mod_ref[b:b + 1, D_MODEL:2 * D_MODEL])
        rws = slice(b * CHUNK, (b + 1) * CHUNK)
        h_sc[rws, :] = (x * _rms_scale(x) * gain + shift).astype(BF16)
        b_sc[rws, :] = _dot(h_sc[rws, :], w_in_ref[:, w_xb:w_xb + D_B])
        xb_sc[rws, :] = b_sc[rws, :].astype(BF16)

    h = h_sc[...]

    tril = (lax.broadcasted_iota(jnp.int32, (CHUNK, CHUNK), 0)
            >= lax.broadcasted_iota(jnp.int32, (CHUNK, CHUNK), 1))
    n_fill = HALF_SLABS * SCAN_PIECES // 4
    fill_cols = D_A // n_fill

    def fill_u(q):
        cols = slice(q * fill_cols, (q + 1) * fill_cols)
        a_sc[:, cols] = _dot(h, w_in_ref[:, w_u + q * fill_cols:w_u + (q + 1) * fill_cols])

    def fill_z(q):
        cols = slice(q * fill_cols, (q + 1) * fill_cols)
        z = _dot(h, w_in_ref[:, w_z + q * fill_cols:w_z + (q + 1) * fill_cols])
        a_sc[:, cols] = a_sc[:, cols] * _silu(z)

    def fill_mix(q):
        heads_per_fill = N_HEADS_A // n_fill
        for hd in range(q * heads_per_fill, (q + 1) * heads_per_fill):
            cols = slice(hd * HEAD_A, (hd + 1) * HEAD_A)
            w_t = jnp.where(tril, ws_ref[hd], 0.0).astype(BF16)
            v_h = jnp.concatenate([v_sc[b * CHUNK:(b + 1) * CHUNK, cols] for b in range(n_seq)], axis=1)
            mix = _dot(w_t, v_h) + bst_ref[:, hd:hd + 1]
            for b in range(n_seq):
                rws = slice(b * CHUNK, (b + 1) * CHUNK)
                oin_ref[rws, cols] = (a_sc[rws, cols] * mix[:, b * HEAD_A:(b + 1) * HEAD_A]).astype(BF16)

    def fill_zb(q):
        cols = slice(q * fill_cols, (q + 1) * fill_cols)
        a_sc[:, cols] = _silu(_dot(h, w_in_ref[:, w_zb + q * fill_cols:w_zb + (q + 1) * fill_cols]))

    fillers = [functools.partial(f, q) for f, q in (
        (fill_u, 0), (fill_z, 0), (fill_u, 1), (fill_mix, 0), (fill_z, 1), (fill_u, 2), (fill_z, 2), (fill_mix, 1),
        (fill_u, 3), (fill_z, 3), (fill_zb, 0), (fill_mix, 2), (fill_zb, 1), (fill_zb, 2), (fill_mix, 3), (fill_zb, 3))]
    assert len(fillers) == HALF_SLABS * SCAN_PIECES

    first_half = lax.broadcasted_iota(jnp.int32, (N_STREAMS, LANES), 0) < n_seq

    def bu_piece(k, piece):
        t0 = piece * PIECE
        for half in range(2):
            j = half * HALF_SLABS + k
            lhs = jnp.concatenate(
                [xb_sc[b * CHUNK + t0:b * CHUNK + t0 + PIECE, j * LANES:(j + 1) * LANES]
                 for b in range(n_seq)], axis=0)
            bu = _dot(lhs, bmat_ref[j])
            for b in range(n_seq):
                r0 = (half * n_seq + b) * STREAM_PITCH + t0
                for l in range(2 * SLAB_TILES):
                    bu_sc[l, r0:r0 + PIECE, :] = bu[b * PIECE:(b + 1) * PIECE, l * LANES:(l + 1) * LANES]

    def scan_piece(k, piece, hr, hi, lam_r, lam_i):
        rows = []
        for t in range(piece * PIECE, (piece + 1) * PIECE):
            new_r, new_i = [], []
            for l in range(SLAB_TILES):
                br = bu_sc[l, pl.ds(t, N_STREAMS, stride=STREAM_PITCH), :]
                bi = bu_sc[SLAB_TILES + l, pl.ds(t, N_STREAMS, stride=STREAM_PITCH), :]
                new_r.append(lam_r[l] * hr[l] - lam_i[l] * hi[l] + br)
                new_i.append(lam_r[l] * hi[l] + lam_i[l] * hr[l] + bi)
            hr, hi = new_r, new_i
            rows.append(jnp.concatenate(hr + hi, axis=1))
        return hr, hi, jnp.concatenate(rows, axis=0)

    def c_piece(k, piece, hh):
        t0 = piece * PIECE
        rows = slice(t0 * N_STREAMS, (t0 + PIECE) * N_STREAMS)
        y2 = _dot(hh.astype(BF16), cboth_ref[k])
        y2_sc[0, rows, :] = y2[:, 0:LANES]
        y2_sc[1, rows, :] = y2[:, LANES:2 * LANES]
        for half in range(2):
            j = half * HALF_SLABS + k
            for b in range(n_seq):
                s = half * n_seq + b
                c_sc[b * CHUNK + t0:b * CHUNK + t0 + PIECE, j * LANES:(j + 1) * LANES] = (
                    y2_sc[half, pl.ds(t0 * N_STREAMS + s, PIECE, stride=N_STREAMS), :])

    for piece in range(SCAN_PIECES):
        bu_piece(0, piece)
    c_sc[...] = _dot(h, w_in_ref[:, w_v:w_v + D_A])
    v_sc[...] = _layernorm(c_sc[...], lng_ref[...], lnb_ref[...]).astype(BF16)
    for k in range(HALF_SLABS):
        tiles = [slice(l * LANES, (l + 1) * LANES) for l in range(SLAB_TILES)]
        lam_r = [jnp.where(first_half, lam_re_ref[k:k + 1, t], lam_re_ref[HALF_SLABS + k:HALF_SLABS + k + 1, t])
                 for t in tiles]
        lam_i = [jnp.where(first_half, lam_im_ref[k:k + 1, t], lam_im_ref[HALF_SLABS + k:HALF_SLABS + k + 1, t])
                 for t in tiles]
        hr = [st_sc[k, :, l * LANES:(l + 1) * LANES] for l in range(SLAB_TILES)]
        hi = [st_sc[k, :, (SLAB_TILES + l) * LANES:(SLAB_TILES + l + 1) * LANES] for l in range(SLAB_TILES)]
        for piece in range(SCAN_PIECES):
            hr, hi, hh = scan_piece(k, piece, hr, hi, lam_r, lam_i)
            c_piece(k, piece, hh)
            if k + 1 < HALF_SLABS:
                bu_piece(k + 1, piece)
            fillers[k * SCAN_PIECES + piece]()
        st_sc[k] = jnp.concatenate(hr + hi, axis=1)

    for k in range(HALF_SLABS):
        for half in range(2):
            j = half * HALF_SLABS + k
            for b in range(n_seq):
                s = half * n_seq + b
                st_re_ref[b, j:j + 1, :] = st_sc[k, s:s + 1, 0:SLAB_STATES]
                st_im_ref[b, j:j + 1, :] = st_sc[k, s:s + 1, SLAB_STATES:2 * SLAB_STATES]

    for j in range(N_SLABS):
        cols = slice(j * LANES, (j + 1) * LANES)
        y = c_sc[:, cols] + dskip_ref[:, cols] * b_sc[:, cols]
        g = _dot(y.astype(BF16), gmat_ref[j])
        val = g[:, 0:LANES] + bglu_ref[0:1, cols]
        gate = g[:, LANES:2 * LANES] + bglu_ref[1:2, cols]
        oin_ref[:, D_A + j * LANES:D_A + (j + 1) * LANES] = (
            val * jax.nn.sigmoid(gate) * a_sc[:, cols]).astype(BF16)


def _const_spec(shape):
    zeros = (0,) * len(shape)
    return pl.BlockSpec(shape, lambda i: zeros, pipeline_mode=pl.Buffered(1))


def _mixer(x, mod, g_pre, w_in, ln_g, ln_b, w_s, bst, lam_re, lam_im, bmat, cboth, gmat,
           d_skip, b_glu_rows, h_sample, w_out):
    n_seq, seq, _ = x.shape
    assert 2 * n_seq == N_STREAMS and seq % CHUNK == 0
    rows = n_seq * CHUNK
    n_steps = seq // CHUNK
    wout_rows = w_out.shape[0] // n_steps
    assert wout_rows * n_steps == w_out.shape[0] and wout_rows % (2 * SUBLANES) == 0
    state_shape = (n_seq, N_SLABS, SLAB_STATES)
    n_sample = h_sample.shape[0]
    n_proj = D_IN // SAMPLE_PROJ_COLS
    assert n_proj <= n_steps
    consts = (mod, g_pre, w_in, ln_g, ln_b, w_s, bst, lam_re, lam_im, bmat, cboth, gmat,
              d_skip, b_glu_rows, h_sample)
    assert w_in.shape == (D_MODEL, D_IN) and D_MODEL % rows == 0 and D_IN % D_A == 0
    return pl.pallas_call(
        functools.partial(_mixer_kernel, n_seq=n_seq),
        grid=(n_steps,),
        in_specs=[pl.BlockSpec((n_seq, CHUNK, D_MODEL), lambda i: (0, i, 0))]
        + [pl.BlockSpec(memory_space=pl.ANY) if c is w_in else _const_spec(c.shape) for c in consts]
        + [pl.BlockSpec((wout_rows, w_out.shape[1]), lambda i: (i, 0))],
        out_specs=(
            pl.BlockSpec((rows, D_A + D_B), lambda i: (i, 0)),
            pl.BlockSpec(state_shape, lambda i: (0, 0, 0)),
            pl.BlockSpec(state_shape, lambda i: (0, 0, 0)),
            pl.BlockSpec((wout_rows, w_out.shape[1]), lambda i: (i, 0)),
            pl.BlockSpec((n_sample, SAMPLE_PROJ_COLS), lambda i: (0, jnp.minimum(i, n_proj - 1))),
        ),
        out_shape=(
            jax.ShapeDtypeStruct((n_steps * rows, D_A + D_B), BF16),
            jax.ShapeDtypeStruct(state_shape, F32),
            jax.ShapeDtypeStruct(state_shape, F32),
            jax.ShapeDtypeStruct(w_out.shape, BF16),
            jax.ShapeDtypeStruct((n_sample, D_IN), F32),
        ),
        scratch_shapes=[
            pltpu.VMEM((rows, D_MODEL), BF16),
            pltpu.VMEM((rows, D_A), F32),
            pltpu.VMEM((rows, D_A), F32),
            pltpu.VMEM((rows, D_A), F32),
            pltpu.VMEM((rows, D_A), BF16),
            pltpu.VMEM((rows, D_B), BF16),
            pltpu.VMEM((2 * SLAB_TILES, N_STREAMS * STREAM_PITCH, LANES), F32),
            pltpu.VMEM((2, N_STREAMS * CHUNK, LANES), F32),
            pltpu.VMEM((HALF_SLABS, N_STREAMS, 2 * SLAB_STATES), F32),
            pltpu.VMEM((D_MODEL, D_IN), BF16),
            pltpu.SemaphoreType.DMA((W_IN_RING,)),
        ],
        compiler_params=pltpu.CompilerParams(
            dimension_semantics=("arbitrary",), vmem_limit_bytes=VMEM_LIMIT),
        name="mixer",
    )(x, *consts, w_out)


def _outproj_kernel(oin_ref, x_ref, mod_ref, gpost_ref, w_out_ref, y_ref, *, n_seq):
    o = _dot(oin_ref[...], w_out_ref[...])
    r = o * _rms_scale(o)
    for b in range(n_seq):
        gain = gpost_ref[...] * mod_ref[b:b + 1, 2 * D_MODEL:3 * D_MODEL]
        y_ref[b] = x_ref[b] + r[b * CHUNK:(b + 1) * CHUNK, :] * gain


def _outproj(oin, x, mod, g_post, w_out_bf):
    n_seq, seq, _ = x.shape
    rows = n_seq * CHUNK
    return pl.pallas_call(
        functools.partial(_outproj_kernel, n_seq=n_seq),
        grid=(seq // CHUNK,),
        in_specs=[
            pl.BlockSpec((rows, D_A + D_B), lambda i: (i, 0)),
            pl.BlockSpec((n_seq, CHUNK, D_MODEL), lambda i: (0, i, 0)),
            _const_spec(mod.shape), _const_spec(g_post.shape), _const_spec(w_out_bf.shape),
        ],
        out_specs=pl.BlockSpec((n_seq, CHUNK, D_MODEL), lambda i: (0, i, 0)),
        out_shape=jax.ShapeDtypeStruct(x.shape, F32),
        compiler_params=pltpu.CompilerParams(
            dimension_semantics=("arbitrary",), vmem_limit_bytes=VMEM_LIMIT),
        name="outproj",
    )(oin, x, mod, g_post, w_out_bf)


def _sample_kernel(x2_ref, mod_ref, proj_ref, lng_ref, lnb_ref, ws_ref, bst_ref,
                   lam_re_ref, lam_im_ref, bmat_ref, cboth_ref, gmat_ref, dskip_ref, bglu_ref,
                   h0t_re_ref, h0t_im_ref, gpost_ref, w_out_ref,
                   y2_ref, v2_ref, hst_re_ref, hst_im_ref, oin_sc):
    n = proj_ref.shape[0]
    x_tiles = D_MODEL // LANES
    v_tiles = D_A // LANES
    w_u, w_v, w_z, w_xb, w_zb = 0, D_A, 2 * D_A, 3 * D_A, 3 * D_A + D_B

    v = _layernorm(proj_ref[:, w_v:w_v + D_A], lng_ref[...], lnb_ref[...])
    for q in range(v_tiles):
        v2_ref[pl.ds(q, n, stride=v_tiles), :] = v[:, q * LANES:(q + 1) * LANES]
    for hd in range(N_HEADS_A):
        cols = slice(hd * HEAD_A, (hd + 1) * HEAD_A)
        mix = ws_ref[hd, 0:1, 0:1] * v[:, cols] + bst_ref[0:1, hd:hd + 1]
        u = proj_ref[:, w_u + hd * HEAD_A:w_u + (hd + 1) * HEAD_A]
        z = proj_ref[:, w_z + hd * HEAD_A:w_z + (hd + 1) * HEAD_A]
        oin_sc[:, cols] = (u * mix * _silu(z)).astype(BF16)

    for j in range(N_SLABS):
        cols = slice(j * LANES, (j + 1) * LANES)
        st = slice(j * SLAB_STATES, (j + 1) * SLAB_STATES)
        xb = proj_ref[:, w_xb + j * LANES:w_xb + (j + 1) * LANES]
        zb = proj_ref[:, w_zb + j * LANES:w_zb + (j + 1) * LANES]
        bu = _dot(xb.astype(BF16), bmat_ref[j])
        lr = lam_re_ref[j:j + 1, :]
        li = lam_im_ref[j:j + 1, :]
        h0r = h0t_re_ref[st, :].T
        h0i = h0t_im_ref[st, :].T
        hr = lr * h0r - li * h0i + bu[:, 0:SLAB_STATES]
        hi = lr * h0i + li * h0r + bu[:, SLAB_STATES:2 * SLAB_STATES]
        hst_re_ref[st, :] = hr.T
        hst_im_ref[st, :] = hi.T
        k, half = j % HALF_SLABS, j // HALF_SLABS
        c_j = cboth_ref[k, :, half * LANES:(half + 1) * LANES]
        y = _dot(jnp.concatenate([hr, hi], axis=1).astype(BF16), c_j)
        y = y + dskip_ref[:, cols] * xb
        g = _dot(y.astype(BF16), gmat_ref[j])
        val = g[:, 0:LANES] + bglu_ref[0:1, cols]
        gt = g[:, LANES:2 * LANES] + bglu_ref[1:2, cols]
        oin_sc[:, D_A + j * LANES:D_A + (j + 1) * LANES] = (
            val * jax.nn.sigmoid(gt) * _silu(zb)).astype(BF16)

    o = _dot(oin_sc[...], w_out_ref[...])
    gate = mod_ref[:, 2 * D_MODEL:3 * D_MODEL]
    r = gate * _rms(o, gpost_ref[...])
    for q in range(x_tiles):
        cols = slice(q * LANES, (q + 1) * LANES)
        y2_ref[pl.ds(q, n, stride=x_tiles), :] = x2_ref[pl.ds(q, n, stride=x_tiles), :] + r[:, cols]


def _sample(x2, mod, proj, ln_g, ln_b, w_s, bst, lam_re, lam_im, bmat, cboth, gmat,
            d_skip, b_glu_rows, h0t_re, h0t_im, g_post, w_out_bf):
    n = proj.shape[0]
    n_state = N_GROUPS_B * P_STATE
    out_shapes = ((n * D_MODEL // LANES, LANES), (n * D_A // LANES, LANES), (n_state, n), (n_state, n))
    return pl.pallas_call(
        _sample_kernel,
        out_shape=tuple(jax.ShapeDtypeStruct(shp, F32) for shp in out_shapes),
        scratch_shapes=[pltpu.VMEM((n, D_A + D_B), BF16)],
        compiler_params=pltpu.CompilerParams(vmem_limit_bytes=VMEM_LIMIT),
        name="sample",
    )(x2, mod, proj, ln_g, ln_b, w_s, bst, lam_re, lam_im, bmat, cboth, gmat,
      d_skip, b_glu_rows, h0t_re, h0t_im, g_post, w_out_bf)


def kernel(x_prompt, x_sample, c_prompt, c_sample, state_b_re, state_b_im, w_c, b_c, g_pre, w_in,
           ln_v_g, ln_v_b, w_s, b_s, a_re, a_im, log_dt, b_re, b_im, c_re, c_im, d_skip, w_glu, b_glu,
           w_out, g_post):
    n_p = x_prompt.shape[0]
    n_s = x_sample.shape[0]
    assert x_sample.shape[1] == 1 and n_s % SUBLANES == 0

    bmat, cboth, gmat, lam_re, lam_im, b_glu_rows, bst = _s5_prep(
        a_re, a_im, log_dt, b_re, b_im, c_re, c_im, w_glu, b_glu, b_s)

    row = lambda v: v.reshape(1, -1)
    x2_sample = x_sample.reshape(n_s * D_MODEL // LANES, LANES)
    mod_s, mod_p, h_sample = _adaln(c_sample, c_prompt, w_c, b_c, x2_sample, row(g_pre))

    s5 = (w_s, bst, lam_re, lam_im, bmat, cboth, gmat, row(d_skip), b_glu_rows)
    oin, st_re, st_im, w_out_bf, proj_s = _mixer(
        x_prompt, mod_p, row(g_pre), w_in, row(ln_v_g), row(ln_v_b), *s5, h_sample, w_out)
    y_prompt = _outproj(oin, x_prompt, mod_p, row(g_post), w_out_bf)

    to_rows = lambda st: st.transpose(1, 2, 0).reshape(N_GROUPS_B * P_STATE, n_s)
    from_rows = lambda st: st.reshape(N_GROUPS_B, P_STATE, n_s).transpose(2, 0, 1)
    y_s, v_s, hs_re, hs_im = _sample(
        x2_sample, mod_s, proj_s, row(ln_v_g), row(ln_v_b), *s5,
        to_rows(state_b_re), to_rows(state_b_im), row(g_post), w_out_bf)

    return (y_prompt, y_s.reshape(n_s, 1, D_MODEL), v_s.reshape(n_s, 1, D_A),
            st_re.reshape(n_p, N_GROUPS_B, P_STATE), st_im.reshape(n_p, N_GROUPS_B, P_STATE),
            from_rows(hs_re), from_rows(hs_im))
```

```python
import functools

import jax
import jax.numpy as jnp
from jax import lax
from jax.experimental import pallas as pl
from jax.experimental.pallas import tpu as pltpu

F32 = jnp.float32
BF16 = jnp.bfloat16

EPS = 1e-6
D_MODEL = 2048
D_A = 1024
D_B = 1024
D_IN = 3 * D_A + 2 * D_B
CHUNK = 128
HEAD_A = 128
N_HEADS_A = D_A // HEAD_A
GROUP_B = 16
N_GROUPS_B = D_B // GROUP_B
P_STATE = 64

LANES = 128
SUBLANES = 8
SLAB_GROUPS = LANES // GROUP_B
N_SLABS = N_GROUPS_B // SLAB_GROUPS
SLAB_STATES = SLAB_GROUPS * P_STATE
SLAB_TILES = SLAB_STATES // LANES
N_STREAMS = SUBLANES
HALF_SLABS = N_SLABS // 2
STREAM_PITCH = CHUNK + SUBLANES
SCAN_PIECES = 4
PIECE = CHUNK // SCAN_PIECES
V7X_VMEM_BYTES = 64 * 1024 * 1024
VMEM_LIMIT = V7X_VMEM_BYTES - 1024 * 1024
SAMPLE_PROJ_COLS = 512
W_IN_RING = 6


def _silu(x):
    return x * jax.nn.sigmoid(x)


def _rms_scale(x):
    return lax.rsqrt(jnp.mean(x * x, axis=-1, keepdims=True) + EPS)


def _rms(x, g):
    return x * _rms_scale(x) * g


def _layernorm(x, g, b):
    mu = jnp.mean(x, axis=-1, keepdims=True)
    xc = x - mu
    return xc * lax.rsqrt(jnp.mean(xc * xc, axis=-1, keepdims=True) + EPS) * g + b


def _dot(a, b):
    return jnp.dot(a, b, preferred_element_type=F32)


def _discretize(a_re, a_im, log_dt):
    dt = jnp.exp(log_dt)
    mag = jnp.exp(dt * a_re)
    abar_re = mag * jnp.cos(dt * a_im)
    abar_im = mag * jnp.sin(dt * a_im)
    return abar_re, abar_im


def _nt_dot(a, b):
    return lax.dot_general(a, b, (((1,), (1,)), ((), ())), preferred_element_type=F32)


def _bf16_terms(x):
    hi = x.astype(BF16)
    r1 = x - hi.astype(F32)
    mid = r1.astype(BF16)
    lo = (r1 - mid.astype(F32)).astype(BF16)
    return hi, mid, lo


def _prep_kernel(a8_ref, bt_re_ref, bt_im_ref, c_re_ref, c_im_ref, wglu_ref, bglu_t_ref, bs_ref,
                 bmat_ref, cboth_ref, gmat_ref, lam_re_ref, lam_im_ref, bglu_row_ref, bst_ref):
    a_re = a8_ref[0:N_SLABS, :]
    a_im = a8_ref[N_SLABS:2 * N_SLABS, :]
    abar_re, abar_im = _discretize(a_re, a_im, a8_ref[2 * N_SLABS:3 * N_SLABS, :])
    lam_re_ref[...] = abar_re
    lam_im_ref[...] = abar_im
    num_re = abar_re - 1.0
    num_im = abar_im
    den = a_re * a_re + a_im * a_im
    coef_re = (num_re * a_re + num_im * a_im) / den
    coef_im = (num_im * a_re - num_re * a_im) / den

    def rep_matrix(k, n, period, offset=0):
        row = lax.broadcasted_iota(jnp.int32, (k, n), 0)
        col = lax.broadcasted_iota(jnp.int32, (k, n), 1)
        return jnp.where((col & (period - 1)) + offset == row, 1.0, 0.0).astype(BF16)

    def rep_rows(m, k, period):
        row = lax.broadcasted_iota(jnp.int32, (m, k), 0)
        col = lax.broadcasted_iota(jnp.int32, (m, k), 1)
        return jnp.where((row & (period - 1)) == col, 1.0, 0.0).astype(BF16)

    def block_mask(m, n, row_shift, col_shift):
        row = lax.broadcasted_iota(jnp.int32, (m, n), 0)
        col = lax.broadcasted_iota(jnp.int32, (m, n), 1)
        return (row >> row_shift) == (col >> col_shift)

    rep_state = rep_matrix(P_STATE, SLAB_STATES, P_STATE)
    rows_state = rep_rows(SLAB_STATES, P_STATE, P_STATE)
    rep_val = rep_matrix(2 * GROUP_B, LANES, GROUP_B)
    rep_gate = rep_matrix(2 * GROUP_B, LANES, GROUP_B, GROUP_B)
    mask_b = block_mask(LANES, SLAB_STATES, 4, 6)
    mask_c = block_mask(SLAB_STATES, LANES, 6, 4)
    mask_g = block_mask(LANES, LANES, 4, 4)

    def spread(x, rep):
        return sum(_dot(term, rep) for term in _bf16_terms(x))

    for j in range(N_SLABS):
        crows = slice(j * LANES, (j + 1) * LANES)
        bt_re = spread(bt_re_ref[crows, :], rep_state)
        bt_im = spread(bt_im_ref[crows, :], rep_state)
        cr = coef_re[j:j + 1, :]
        ci = coef_im[j:j + 1, :]
        bmat_ref[j, :, 0:SLAB_STATES] = jnp.where(mask_b, cr * bt_re - ci * bt_im, 0.0).astype(BF16)
        bmat_ref[j, :, SLAB_STATES:2 * SLAB_STATES] = jnp.where(
            mask_b, cr * bt_im + ci * bt_re, 0.0).astype(BF16)

        k, half = j % HALF_SLABS, j // HALF_SLABS
        ccols = slice(half * LANES, (half + 1) * LANES)
        ct_re = _nt_dot(rows_state, c_re_ref[crows, :].astype(BF16))
        ct_im = _nt_dot(rows_state, c_im_ref[crows, :].astype(BF16))
        cboth_ref[k, 0:SLAB_STATES, ccols] = jnp.where(mask_c, ct_re, 0.0).astype(BF16)
        cboth_ref[k, SLAB_STATES:2 * SLAB_STATES, ccols] = jnp.where(mask_c, -ct_im, 0.0).astype(BF16)

        w = wglu_ref[crows, :].astype(BF16)
        gmat_ref[j, :, 0:LANES] = jnp.where(mask_g, _dot(w, rep_val), 0.0).astype(BF16)
        gmat_ref[j, :, LANES:2 * LANES] = jnp.where(mask_g, _dot(w, rep_gate), 0.0).astype(BF16)

    row = lax.broadcasted_iota(jnp.int32, (2 * GROUP_B, D_B), 0)
    col = lax.broadcasted_iota(jnp.int32, (2 * GROUP_B, D_B), 1)
    by_group = spread(bglu_t_ref[...], jnp.where(block_mask(N_GROUPS_B, D_B, 0, 4), 1.0, 0.0).astype(BF16))
    for i in range(2):
        mine = row == (col & (GROUP_B - 1)) + i * GROUP_B
        bglu_row_ref[i:i + 1, :] = jnp.sum(jnp.where(mine, by_group, 0.0), axis=0, keepdims=True)

    eye = rep_rows(CHUNK, CHUNK, CHUNK)
    bst_ref[...] = sum(_nt_dot(eye, term) for term in _bf16_terms(bs_ref[...]))


def _s5_prep(a_re, a_im, log_dt, b_re, b_im, c_re, c_im, w_glu, b_glu, b_s):
    g, p, c = N_GROUPS_B, P_STATE, GROUP_B
    a8 = jnp.stack([a_re, a_im, jnp.broadcast_to(log_dt[:, None], (g, p))]).reshape(3 * N_SLABS, SLAB_STATES)
    args = (a8, b_re.transpose(0, 2, 1).reshape(g * c, p), b_im.transpose(0, 2, 1).reshape(g * c, p),
            c_re.reshape(g * c, p), c_im.reshape(g * c, p), w_glu.reshape(g * c, 2 * c), b_glu.T, b_s)
    return pl.pallas_call(
        _prep_kernel,
        out_shape=(
            jax.ShapeDtypeStruct((N_SLABS, LANES, 2 * SLAB_STATES), BF16),
            jax.ShapeDtypeStruct((HALF_SLABS, 2 * SLAB_STATES, 2 * LANES), BF16),
            jax.ShapeDtypeStruct((N_SLABS, LANES, 2 * LANES), BF16),
            jax.ShapeDtypeStruct((N_SLABS, SLAB_STATES), F32),
            jax.ShapeDtypeStruct((N_SLABS, SLAB_STATES), F32),
            jax.ShapeDtypeStruct((2, D_B), F32),
            jax.ShapeDtypeStruct((CHUNK, N_HEADS_A), F32),
        ),
        name="s5_prep",
    )(*args)


ADALN_ROWS = 512
ADALN_PROMPT_ROWS = 2 * SUBLANES


def _adaln_kernel(cs_ref, cp_ref, w_ref, b_ref, x2_ref, gpre_ref, os_ref, op_ref, hs_ref):
    k = pl.program_id(0)

    @pl.when(k == 0)
    def _():
        os_ref[...] = jnp.broadcast_to(b_ref[...], os_ref.shape)
        op_ref[...] = jnp.broadcast_to(b_ref[...], op_ref.shape)

    cols = pl.ds(pl.multiple_of(k * ADALN_ROWS, ADALN_ROWS), ADALN_ROWS)
    w = w_ref[...].astype(BF16)
    os_ref[...] += _dot(_silu(cs_ref[:, cols]).astype(BF16), w)
    n_p = cp_ref.shape[0]
    cp = jnp.concatenate([cp_ref[:, cols], jnp.zeros((ADALN_PROMPT_ROWS - n_p, ADALN_ROWS), F32)], axis=0)
    op_ref[...] += _dot(_silu(cp).astype(BF16), w)

    @pl.when(k == pl.num_programs(0) - 1)
    def _():
        x_tiles = D_MODEL // LANES
        n = hs_ref.shape[0]
        x = jnp.concatenate([x2_ref[pl.ds(q, n, stride=x_tiles), :] for q in range(x_tiles)], axis=1)
        shift = os_ref[:, 0:D_MODEL]
        scale = os_ref[:, D_MODEL:2 * D_MODEL]
        hs_ref[...] = (_rms(x, gpre_ref[...]) * (1.0 + scale) + shift).astype(BF16)


def _adaln(c_sample, c_prompt, w_c, b_c, x2_sample, g_pre):
    n_s, n_p = c_sample.shape[0], c_prompt.shape[0]
    assert n_p <= ADALN_PROMPT_ROWS and D_MODEL % ADALN_ROWS == 0
    return pl.pallas_call(
        _adaln_kernel,
        grid=(D_MODEL // ADALN_ROWS,),
        in_specs=[
            pl.BlockSpec((n_s, D_MODEL), lambda k: (0, 0)),
            pl.BlockSpec((n_p, D_MODEL), lambda k: (0, 0)),
            pl.BlockSpec((ADALN_ROWS, 3 * D_MODEL), lambda k: (k, 0)),
            pl.BlockSpec((1, 3 * D_MODEL), lambda k: (0, 0)),
            pl.BlockSpec(x2_sample.shape, lambda k: (0, 0)),
            pl.BlockSpec(g_pre.shape, lambda k: (0, 0)),
        ],
        out_specs=(pl.BlockSpec((n_s, 3 * D_MODEL), lambda k: (0, 0)),
                   pl.BlockSpec((ADALN_PROMPT_ROWS, 3 * D_MODEL), lambda k: (0, 0)),
                   pl.BlockSpec((n_s, D_MODEL), lambda k: (0, 0))),
        out_shape=(jax.ShapeDtypeStruct((n_s, 3 * D_MODEL), F32),
                   jax.ShapeDtypeStruct((ADALN_PROMPT_ROWS, 3 * D_MODEL), F32),
                   jax.ShapeDtypeStruct((n_s, D_MODEL), BF16)),
        compiler_params=pltpu.CompilerParams(
            dimension_semantics=("arbitrary",), vmem_limit_bytes=VMEM_LIMIT),
        name="adaln",
    )(c_sample, c_prompt, w_c, b_c.reshape(1, -1), x2_sample, g_pre)


def _mixer_kernel(x_ref, mod_ref, gpre_ref, w_in_hbm, lng_ref, lnb_ref, ws_ref, bst_ref,
                  lam_re_ref, lam_im_ref, bmat_ref, cboth_ref, gmat_ref, dskip_ref, bglu_ref, hs_ref, wout_ref,
                  oin_ref, st_re_ref, st_im_ref, wout_bf_ref, ps_ref,
                  h_sc, a_sc, b_sc, c_sc, v_sc, xb_sc, bu_sc, y2_sc, st_sc, w_in_ref, w_sem, *, n_seq):
    step = pl.program_id(0)

    @pl.when(step == 0)
    def _():
        st_sc[...] = jnp.zeros_like(st_sc)

        t_rows, t_cols = a_sc.shape[0] // 2, a_sc.shape[1]
        stage = [buf.at[pl.ds(i * t_rows, t_rows), :] for buf in (a_sc, b_sc, c_sc) for i in range(2)]
        assert len(stage) == W_IN_RING
        tiles = [(r, c) for c in range(D_IN // t_cols) for r in range(D_MODEL // t_rows)]

        def tile_copy(n):
            r, c = tiles[n]
            src = w_in_hbm.at[pl.ds(r * t_rows, t_rows), pl.ds(c * t_cols, t_cols)]
            return pltpu.make_async_copy(src, stage[n % len(stage)], w_sem.at[n % len(stage)])

        for n in range(len(stage)):
            tile_copy(n).start()
        for n, (r, c) in enumerate(tiles):
            tile_copy(n).wait()
            w_in_ref[r * t_rows:(r + 1) * t_rows, c * t_cols:(c + 1) * t_cols] = (
                stage[n % len(stage)][...].astype(BF16))
            if n + len(stage) < len(tiles):
                tile_copy(n + len(stage)).start()

    wout_bf_ref[...] = wout_ref[...].astype(BF16)

    @pl.when(step < D_IN // SAMPLE_PROJ_COLS)
    def _():
        cols = pl.ds(pl.multiple_of(step * SAMPLE_PROJ_COLS, SAMPLE_PROJ_COLS), SAMPLE_PROJ_COLS)
        ps_ref[...] = _dot(hs_ref[...], w_in_ref[:, cols])

    w_u, w_v, w_z, w_xb, w_zb = 0, D_A, 2 * D_A, 3 * D_A, 3 * D_A + D_B

    for b in range(n_seq):
        x = x_ref[b]
        shift = mod_ref[b:b + 1, 0:D_MODEL]
        gain = gpre_ref[...] * (1.0 + mod_ref[b:b + 1, D_MODEL:2 * D_MODEL])
        rws = slice(b * CHUNK, (b + 1) * CHUNK)
        h_sc[rws, :] = (x * _rms_scale(x) * gain + shift).astype(BF16)
        b_sc[rws, :] = _dot(h_sc[rws, :], w_in_ref[:, w_xb:w_xb + D_B])
        xb_sc[rws, :] = b_sc[rws, :].astype(BF16)

    h = h_sc[...]

    tril = (lax.broadcasted_iota(jnp.int32, (CHUNK, CHUNK), 0)
            >= lax.broadcasted_iota(jnp.int32, (CHUNK, CHUNK), 1))
    n_fill = HALF_SLABS * SCAN_PIECES // 4
    fill_cols = D_A // n_fill

    def fill_u(q):
        cols = slice(q * fill_cols, (q + 1) * fill_cols)
        a_sc[:, cols] = _dot(h, w_in_ref[:, w_u + q * fill_cols:w_u + (q + 1) * fill_cols])

    def fill_z(q):
        cols = slice(q * fill_cols, (q + 1) * fill_cols)
        z = _dot(h, w_in_ref[:, w_z + q * fill_cols:w_z + (q + 1) * fill_cols])
        a_sc[:, cols] = a_sc[:, cols] * _silu(z)

    def fill_mix(q):
        heads_per_fill = N_HEADS_A // n_fill
        for hd in range(q * heads_per_fill, (q + 1) * heads_per_fill):
            cols = slice(hd * HEAD_A, (hd + 1) * HEAD_A)
            w_t = jnp.where(tril, ws_ref[hd], 0.0).astype(BF16)
            v_h = jnp.concatenate([v_sc[b * CHUNK:(b + 1) * CHUNK, cols] for b in range(n_seq)], axis=1)
            mix = _dot(w_t, v_h) + bst_ref[:, hd:hd + 1]
            for b in range(n_seq):
                rws = slice(b * CHUNK, (b + 1) * CHUNK)
                oin_ref[rws, cols] = (a_sc[rws, cols] * mix[:, b * HEAD_A:(b + 1) * HEAD_A]).astype(BF16)

    def fill_zb(q):
        cols = slice(q * fill_cols, (q + 1) * fill_cols)
        a_sc[:, cols] = _silu(_dot(h, w_in_ref[:, w_zb + q * fill_cols:w_zb + (q + 1) * fill_cols]))

    fillers = [functools.partial(f, q) for f, q in (
        (fill_u, 0), (fill_z, 0), (fill_u, 1), (fill_mix, 0), (fill_z, 1), (fill_u, 2), (fill_z, 2), (fill_mix, 1),
        (fill_u, 3), (fill_z, 3), (fill_zb, 0), (fill_mix, 2), (fill_zb, 1), (fill_zb, 2), (fill_mix, 3), (fill_zb, 3))]
    assert len(fillers) == HALF_SLABS * SCAN_PIECES

    first_half = lax.broadcasted_iota(jnp.int32, (N_STREAMS, LANES), 0) < n_seq

    def bu_piece(k, piece):
        t0 = piece * PIECE
        for half in range(2):
            j = half * HALF_SLABS + k
            lhs = jnp.concatenate(
                [xb_sc[b * CHUNK + t0:b * CHUNK + t0 + PIECE, j * LANES:(j + 1) * LANES]
                 for b in range(n_seq)], axis=0)
            bu = _dot(lhs, bmat_ref[j])
            for b in range(n_seq):
                r0 = (half * n_seq + b) * STREAM_PITCH + t0
                for l in range(2 * SLAB_TILES):
                    bu_sc[l, r0:r0 + PIECE, :] = bu[b * PIECE:(b + 1) * PIECE, l * LANES:(l + 1) * LANES]

    def scan_piece(k, piece, hr, hi, lam_r, lam_i):
        rows = []
        for t in range(piece * PIECE, (piece + 1) * PIECE):
            new_r, new_i = [], []
            for l in range(SLAB_TILES):
                br = bu_sc[l, pl.ds(t, N_STREAMS, stride=STREAM_PITCH), :]
                bi = bu_sc[SLAB_TILES + l, pl.ds(t, N_STREAMS, stride=STREAM_PITCH), :]
                new_r.append(lam_r[l] * hr[l] - lam_i[l] * hi[l] + br)
                new_i.append(lam_r[l] * hi[l] + lam_i[l] * hr[l] + bi)
            hr, hi = new_r, new_i
            rows.append(jnp.concatenate(hr + hi, axis=1))
        return hr, hi, jnp.concatenate(rows, axis=0)

    def c_piece(k, piece, hh):
        t0 = piece * PIECE
        rows = slice(t0 * N_STREAMS, (t0 + PIECE) * N_STREAMS)
        y2 = _dot(hh.astype(BF16), cboth_ref[k])
        y2_sc[0, rows, :] = y2[:, 0:LANES]
        y2_sc[1, rows, :] = y2[:, LANES:2 * LANES]
        for half in range(2):
            j = half * HALF_SLABS + k
            for b in range(n_seq):
                s = half * n_seq + b
                c_sc[b * CHUNK + t0:b * CHUNK + t0 + PIECE, j * LANES:(j + 1) * LANES] = (
                    y2_sc[half, pl.ds(t0 * N_STREAMS + s, PIECE, stride=N_STREAMS), :])

    for piece in range(SCAN_PIECES):
        bu_piece(0, piece)
    c_sc[...] = _dot(h, w_in_ref[:, w_v:w_v + D_A])
    v_sc[...] = _layernorm(c_sc[...], lng_ref[...], lnb_ref[...]).astype(BF16)
    for k in range(HALF_SLABS):
        tiles = [slice(l * LANES, (l + 1) * LANES) for l in range(SLAB_TILES)]
        lam_r = [jnp.where(first_half, lam_re_ref[k:k + 1, t], lam_re_ref[HALF_SLABS + k:HALF_SLABS + k + 1, t])
                 for t in tiles]
        lam_i = [jnp.where(first_half, lam_im_ref[k:k + 1, t], lam_im_ref[HALF_SLABS + k:HALF_SLABS + k + 1, t])
                 for t in tiles]
        hr = [st_sc[k, :, l * LANES:(l + 1) * LANES] for l in range(SLAB_TILES)]
        hi = [st_sc[k, :, (SLAB_TILES + l) * LANES:(SLAB_TILES + l + 1) * LANES] for l in range(SLAB_TILES)]
        for piece in range(SCAN_PIECES):
            hr, hi, hh = scan_piece(k, piece, hr, hi, lam_r, lam_i)
            c_piece(k, piece, hh)
            if k + 1 < HALF_SLABS:
                bu_piece(k + 1, piece)
            fillers[k * SCAN_PIECES + piece]()
        st_sc[k] = jnp.concatenate(hr + hi, axis=1)

    for k in range(HALF_SLABS):
        for half in range(2):
            j = half * HALF_SLABS + k
            for b in range(n_seq):
                s = half * n_seq + b
                st_re_ref[b, j:j + 1, :] = st_sc[k, s:s + 1, 0:SLAB_STATES]
                st_im_ref[b, j:j + 1, :] = st_sc[k, s:s + 1, SLAB_STATES:2 * SLAB_STATES]

    for j in range(N_SLABS):
        cols = slice(j * LANES, (j + 1) * LANES)
        y = c_sc[:, cols] + dskip_ref[:, cols] * b_sc[:, cols]
        g = _dot(y.astype(BF16), gmat_ref[j])
        val = g[:, 0:LANES] + bglu_ref[0:1, cols]
        gate = g[:, LANES:2 * LANES] + bglu_ref[1:2, cols]
        oin_ref[:, D_A + j * LANES:D_A + (j + 1) * LANES] = (
            val * jax.nn.sigmoid(gate) * a_sc[:, cols]).astype(BF16)


def _const_spec(shape):
    zeros = (0,) * len(shape)
    return pl.BlockSpec(shape, lambda i: zeros, pipeline_mode=pl.Buffered(1))


def _mixer(x, mod, g_pre, w_in, ln_g, ln_b, w_s, bst, lam_re, lam_im, bmat, cboth, gmat,
           d_skip, b_glu_rows, h_sample, w_out):
    n_seq, seq, _ = x.shape
    assert 2 * n_seq == N_STREAMS and seq % CHUNK == 0
    rows = n_seq * CHUNK
    n_steps = seq // CHUNK
    wout_rows = w_out.shape[0] // n_steps
    assert wout_rows * n_steps == w_out.shape[0] and wout_rows % (2 * SUBLANES) == 0
    state_shape = (n_seq, N_SLABS, SLAB_STATES)
    n_sample = h_sample.shape[0]
    n_proj = D_IN // SAMPLE_PROJ_COLS
    assert n_proj <= n_steps
    consts = (mod, g_pre, w_in, ln_g, ln_b, w_s, bst, lam_re, lam_im, bmat, cboth, gmat,
              d_skip, b_glu_rows, h_sample)
    assert w_in.shape == (D_MODEL, D_IN) and D_MODEL % rows == 0 and D_IN % D_A == 0
    return pl.pallas_call(
        functools.partial(_mixer_kernel, n_seq=n_seq),
        grid=(n_steps,),
        in_specs=[pl.BlockSpec((n_seq, CHUNK, D_MODEL), lambda i: (0, i, 0))]
        + [pl.BlockSpec(memory_space=pl.ANY) if c is w_in else _const_spec(c.shape) for c in consts]
        + [pl.BlockSpec((wout_rows, w_out.shape[1]), lambda i: (i, 0))],
        out_specs=(
            pl.BlockSpec((rows, D_A + D_B), lambda i: (i, 0)),
            pl.BlockSpec(state_shape, lambda i: (0, 0, 0)),
            pl.BlockSpec(state_shape, lambda i: (0, 0, 0)),
            pl.BlockSpec((wout_rows, w_out.shape[1]), lambda i: (i, 0)),
            pl.BlockSpec((n_sample, SAMPLE_PROJ_COLS), lambda i: (0, jnp.minimum(i, n_proj - 1))),
        ),
        out_shape=(
            jax.ShapeDtypeStruct((n_steps * rows, D_A + D_B), BF16),
            jax.ShapeDtypeStruct(state_shape, F32),
            jax.ShapeDtypeStruct(state_shape, F32),
            jax.ShapeDtypeStruct(w_out.shape, BF16),
            jax.ShapeDtypeStruct((n_sample, D_IN), F32),
        ),
        scratch_shapes=[
            pltpu.VMEM((rows, D_MODEL), BF16),
            pltpu.VMEM((rows, D_A), F32),
            pltpu.VMEM((rows, D_A), F32),
            pltpu.VMEM((rows, D_A), F32),
            pltpu.VMEM((rows, D_A), BF16),
            pltpu.VMEM((rows, D_B), BF16),
            pltpu.VMEM((2 * SLAB_TILES, N_STREAMS * STREAM_PITCH, LANES), F32),
            pltpu.VMEM((2, N_STREAMS * CHUNK, LANES), F32),
            pltpu.VMEM((HALF_SLABS, N_STREAMS, 2 * SLAB_STATES), F32),
            pltpu.VMEM((D_MODEL, D_IN), BF16),
            pltpu.SemaphoreType.DMA((W_IN_RING,)),
        ],
        compiler_params=pltpu.CompilerParams(
            dimension_semantics=("arbitrary",), vmem_limit_bytes=VMEM_LIMIT),
        name="mixer",
    )(x, *consts, w_out)


OUTPROJ_TILES = 2


def _outproj_kernel(oin_ref, x_ref, mod_ref, gpost_ref, w_out_ref, y_ref, *, n_seq):
    o = _dot(oin_ref[...], w_out_ref[...])
    r = o * _rms_scale(o)
    for b in range(n_seq):
        gain = gpost_ref[...] * mod_ref[b:b + 1, 2 * D_MODEL:3 * D_MODEL]
        for t in range(OUTPROJ_TILES):
            rows = slice((t * n_seq + b) * CHUNK, (t * n_seq + b + 1) * CHUNK)
            y_ref[b, t * CHUNK:(t + 1) * CHUNK, :] = x_ref[b, t * CHUNK:(t + 1) * CHUNK, :] + r[rows, :] * gain


def _outproj(oin, x, mod, g_post, w_out_bf):
    n_seq, seq, _ = x.shape
    rows = n_seq * CHUNK * OUTPROJ_TILES
    t_block = CHUNK * OUTPROJ_TILES
    assert seq % t_block == 0
    return pl.pallas_call(
        functools.partial(_outproj_kernel, n_seq=n_seq),
        grid=(seq // t_block,),
        in_specs=[
            pl.BlockSpec((rows, D_A + D_B), lambda i: (i, 0)),
            pl.BlockSpec((n_seq, t_block, D_MODEL), lambda i: (0, i, 0)),
            _const_spec(mod.shape), _const_spec(g_post.shape), _const_spec(w_out_bf.shape),
        ],
        out_specs=pl.BlockSpec((n_seq, t_block, D_MODEL), lambda i: (0, i, 0)),
        out_shape=jax.ShapeDtypeStruct(x.shape, F32),
        compiler_params=pltpu.CompilerParams(
            dimension_semantics=("arbitrary",), vmem_limit_bytes=VMEM_LIMIT),
        name="outproj",
    )(oin, x, mod, g_post, w_out_bf)


def _sample_kernel(x2_ref, mod_ref, proj_ref, lng_ref, lnb_ref, ws_ref, bst_ref,
                   lam_re_ref, lam_im_ref, bmat_ref, cboth_ref, gmat_ref, dskip_ref, bglu_ref,
                   h0t_re_ref, h0t_im_ref, gpost_ref, w_out_ref,
                   y2_ref, v2_ref, hst_re_ref, hst_im_ref, oin_sc):
    n = proj_ref.shape[0]
    x_tiles = D_MODEL // LANES
    v_tiles = D_A // LANES
    w_u, w_v, w_z, w_xb, w_zb = 0, D_A, 2 * D_A, 3 * D_A, 3 * D_A + D_B

    v = _layernorm(proj_ref[:, w_v:w_v + D_A], lng_ref[...], lnb_ref[...])
    for q in range(v_tiles):
        v2_ref[pl.ds(q, n, stride=v_tiles), :] = v[:, q * LANES:(q + 1) * LANES]
    for hd in range(N_HEADS_A):
        cols = slice(hd * HEAD_A, (hd + 1) * HEAD_A)
        mix = ws_ref[hd, 0:1, 0:1] * v[:, cols] + bst_ref[0:1, hd:hd + 1]
        u = proj_ref[:, w_u + hd * HEAD_A:w_u + (hd + 1) * HEAD_A]
        z = proj_ref[:, w_z + hd * HEAD_A:w_z + (hd + 1) * HEAD_A]
        oin_sc[:, cols] = (u * mix * _silu(z)).astype(BF16)

    for j in range(N_SLABS):
        cols = slice(j * LANES, (j + 1) * LANES)
        st = slice(j * SLAB_STATES, (j + 1) * SLAB_STATES)
        xb = proj_ref[:, w_xb + j * LANES:w_xb + (j + 1) * LANES]
        zb = proj_ref[:, w_zb + j * LANES:w_zb + (j + 1) * LANES]
        bu = _dot(xb.astype(BF16), bmat_ref[j])
        lr = lam_re_ref[j:j + 1, :]
        li = lam_im_ref[j:j + 1, :]
        h0r = h0t_re_ref[st, :].T
        h0i = h0t_im_ref[st, :].T
        hr = lr * h0r - li * h0i + bu[:, 0:SLAB_STATES]
        hi = lr * h0i + li * h0r + bu[:, SLAB_STATES:2 * SLAB_STATES]
        hst_re_ref[st, :] = hr.T
        hst_im_ref[st, :] = hi.T
        k, half = j % HALF_SLABS, j // HALF_SLABS
        c_j = cboth_ref[k, :, half * LANES:(half + 1) * LANES]
        y = _dot(jnp.concatenate([hr, hi], axis=1).astype(BF16), c_j)
        y = y + dskip_ref[:, cols] * xb
        g = _dot(y.astype(BF16), gmat_ref[j])
        val = g[:, 0:LANES] + bglu_ref[0:1, cols]
        gt = g[:, LANES:2 * LANES] + bglu_ref[1:2, cols]
        oin_sc[:, D_A + j * LANES:D_A + (j + 1) * LANES] = (
            val * jax.nn.sigmoid(gt) * _silu(zb)).astype(BF16)

    o = _dot(oin_sc[...], w_out_ref[...])
    gate = mod_ref[:, 2 * D_MODEL:3 * D_MODEL]
    r = gate * _rms(o, gpost_ref[...])
    for q in range(x_tiles):
        cols = slice(q * LANES, (q + 1) * LANES)
        y2_ref[pl.ds(q, n, stride=x_tiles), :] = x2_ref[pl.ds(q, n, stride=x_tiles), :] + r[:, cols]


def _sample(x2, mod, proj, ln_g, ln_b, w_s, bst, lam_re, lam_im, bmat, cboth, gmat,
            d_skip, b_glu_rows, h0t_re, h0t_im, g_post, w_out_bf):
    n = proj.shape[0]
    n_state = N_GROUPS_B * P_STATE
    out_shapes = ((n * D_MODEL // LANES, LANES), (n * D_A // LANES, LANES), (n_state, n), (n_state, n))
    return pl.pallas_call(
        _sample_kernel,
        out_shape=tuple(jax.ShapeDtypeStruct(shp, F32) for shp in out_shapes),
        scratch_shapes=[pltpu.VMEM((n, D_A + D_B), BF16)],
        compiler_params=pltpu.CompilerParams(vmem_limit_bytes=VMEM_LIMIT),
        name="sample",
    )(x2, mod, proj, ln_g, ln_b, w_s, bst, lam_re, lam_im, bmat, cboth, gmat,
      d_skip, b_glu_rows, h0t_re, h0t_im, g_post, w_out_bf)


def kernel(x_prompt, x_sample, c_prompt, c_sample, state_b_re, state_b_im, w_c, b_c, g_pre, w_in,
           ln_v_g, ln_v_b, w_s, b_s, a_re, a_im, log_dt, b_re, b_im, c_re, c_im, d_skip, w_glu, b_glu,
           w_out, g_post):
    n_p = x_prompt.shape[0]
    n_s = x_sample.shape[0]
    assert x_sample.shape[1] == 1 and n_s % SUBLANES == 0

    bmat, cboth, gmat, lam_re, lam_im, b_glu_rows, bst = _s5_prep(
        a_re, a_im, log_dt, b_re, b_im, c_re, c_im, w_glu, b_glu, b_s)

    row = lambda v: v.reshape(1, -1)
    x2_sample = x_sample.reshape(n_s * D_MODEL // LANES, LANES)
    mod_s, mod_p, h_sample = _adaln(c_sample, c_prompt, w_c, b_c, x2_sample, row(g_pre))

    s5 = (w_s, bst, lam_re, lam_im, bmat, cboth, gmat, row(d_skip), b_glu_rows)
    oin, st_re, st_im, w_out_bf, proj_s = _mixer(
        x_prompt, mod_p, row(g_pre), w_in, row(ln_v_g), row(ln_v_b), *s5, h_sample, w_out)
    y_prompt = _outproj(oin, x_prompt, mod_p, row(g_post), w_out_bf)

    to_rows = lambda st: st.transpose(1, 2, 0).reshape(N_GROUPS_B * P_STATE, n_s)
    from_rows = lambda st: st.reshape(N_GROUPS_B, P_STATE, n_s).transpose(2, 0, 1)
    y_s, v_s, hs_re, hs_im = _sample(
        x2_sample, mod_s, proj_s, row(ln_v_g), row(ln_v_b), *s5,
        to_rows(state_b_re), to_rows(state_b_im), row(g_post), w_out_bf)

    return (y_prompt, y_s.reshape(n_s, 1, D_MODEL), v_s.reshape(n_s, 1, D_A),
            st_re.reshape(n_p, N_GROUPS_B, P_STATE), st_im.reshape(n_p, N_GROUPS_B, P_STATE),
            from_rows(hs_re), from_rows(hs_im))
```

```python
import functools

import jax
import jax.numpy as jnp
from jax import lax
from jax.experimental import pallas as pl
from jax.experimental.pallas import tpu as pltpu

F32 = jnp.float32
BF16 = jnp.bfloat16

EPS = 1e-6
D_MODEL = 2048
D_A = 1024
D_B = 1024
D_IN = 3 * D_A + 2 * D_B
CHUNK = 128
HEAD_A = 128
N_HEADS_A = D_A // HEAD_A
GROUP_B = 16
N_GROUPS_B = D_B // GROUP_B
P_STATE = 64

LANES = 128
SUBLANES = 8
SLAB_GROUPS = LANES // GROUP_B
N_SLABS = N_GROUPS_B // SLAB_GROUPS
SLAB_STATES = SLAB_GROUPS * P_STATE
SLAB_TILES = SLAB_STATES // LANES
N_STREAMS = SUBLANES
HALF_SLABS = N_SLABS // 2
STREAM_PITCH = CHUNK + SUBLANES
SCAN_PIECES = 4
PIECE = CHUNK // SCAN_PIECES
V7X_VMEM_BYTES = 64 * 1024 * 1024
VMEM_LIMIT = V7X_VMEM_BYTES - 1024 * 1024
SAMPLE_PROJ_COLS = 512
W_IN_RING = 6


def _silu(x):
    return x * jax.nn.sigmoid(x)


def _rms_scale(x):
    return lax.rsqrt(jnp.mean(x * x, axis=-1, keepdims=True) + EPS)


def _rms(x, g):
    return x * _rms_scale(x) * g


def _layernorm(x, g, b):
    mu = jnp.mean(x, axis=-1, keepdims=True)
    xc = x - mu
    return xc * lax.rsqrt(jnp.mean(xc * xc, axis=-1, keepdims=True) + EPS) * g + b


def _dot(a, b):
    return jnp.dot(a, b, preferred_element_type=F32)


def _discretize(a_re, a_im, log_dt):
    dt = jnp.exp(log_dt)
    mag = jnp.exp(dt * a_re)
    abar_re = mag * jnp.cos(dt * a_im)
    abar_im = mag * jnp.sin(dt * a_im)
    return abar_re, abar_im


def _nt_dot(a, b):
    return lax.dot_general(a, b, (((1,), (1,)), ((), ())), preferred_element_type=F32)


def _bf16_terms(x):
    hi = x.astype(BF16)
    r1 = x - hi.astype(F32)
    mid = r1.astype(BF16)
    lo = (r1 - mid.astype(F32)).astype(BF16)
    return hi, mid, lo


def _prep_kernel(a8_ref, bt_re_ref, bt_im_ref, c_re_ref, c_im_ref, wglu_ref, bglu_t_ref, bs_ref,
                 bmat_ref, cboth_ref, gmat_ref, lam_re_ref, lam_im_ref, bglu_row_ref, bst_ref):
    a_re = a8_ref[0:N_SLABS, :]
    a_im = a8_ref[N_SLABS:2 * N_SLABS, :]
    abar_re, abar_im = _discretize(a_re, a_im, a8_ref[2 * N_SLABS:3 * N_SLABS, :])
    lam_re_ref[...] = abar_re
    lam_im_ref[...] = abar_im
    num_re = abar_re - 1.0
    num_im = abar_im
    den = a_re * a_re + a_im * a_im
    coef_re = (num_re * a_re + num_im * a_im) / den
    coef_im = (num_im * a_re - num_re * a_im) / den

    def rep_matrix(k, n, period, offset=0):
        row = lax.broadcasted_iota(jnp.int32, (k, n), 0)
        col = lax.broadcasted_iota(jnp.int32, (k, n), 1)
        return jnp.where((col & (period - 1)) + offset == row, 1.0, 0.0).astype(BF16)

    def rep_rows(m, k, period):
        row = lax.broadcasted_iota(jnp.int32, (m, k), 0)
        col = lax.broadcasted_iota(jnp.int32, (m, k), 1)
        return jnp.where((row & (period - 1)) == col, 1.0, 0.0).astype(BF16)

    def block_mask(m, n, row_shift, col_shift):
        row = lax.broadcasted_iota(jnp.int32, (m, n), 0)
        col = lax.broadcasted_iota(jnp.int32, (m, n), 1)
        return (row >> row_shift) == (col >> col_shift)

    rep_state = rep_matrix(P_STATE, SLAB_STATES, P_STATE)
    rows_state = rep_rows(SLAB_STATES, P_STATE, P_STATE)
    rep_val = rep_matrix(2 * GROUP_B, LANES, GROUP_B)
    rep_gate = rep_matrix(2 * GROUP_B, LANES, GROUP_B, GROUP_B)
    mask_b = block_mask(LANES, SLAB_STATES, 4, 6)
    mask_c = block_mask(SLAB_STATES, LANES, 6, 4)
    mask_g = block_mask(LANES, LANES, 4, 4)

    def spread(x, rep):
        return sum(_dot(term, rep) for term in _bf16_terms(x))

    for j in range(N_SLABS):
        crows = slice(j * LANES, (j + 1) * LANES)
        bt_re = spread(bt_re_ref[crows, :], rep_state)
        bt_im = spread(bt_im_ref[crows, :], rep_state)
        cr = coef_re[j:j + 1, :]
        ci = coef_im[j:j + 1, :]
        bmat_ref[j, :, 0:SLAB_STATES] = jnp.where(mask_b, cr * bt_re - ci * bt_im, 0.0).astype(BF16)
        bmat_ref[j, :, SLAB_STATES:2 * SLAB_STATES] = jnp.where(
            mask_b, cr * bt_im + ci * bt_re, 0.0).astype(BF16)

        k, half = j % HALF_SLABS, j // HALF_SLABS
        ccols = slice(half * LANES, (half + 1) * LANES)
        ct_re = _nt_dot(rows_state, c_re_ref[crows, :].astype(BF16))
        ct_im = _nt_dot(rows_state, c_im_ref[crows, :].astype(BF16))
        cboth_ref[k, 0:SLAB_STATES, ccols] = jnp.where(mask_c, ct_re, 0.0).astype(BF16)
        cboth_ref[k, SLAB_STATES:2 * SLAB_STATES, ccols] = jnp.where(mask_c, -ct_im, 0.0).astype(BF16)

        w = wglu_ref[crows, :].astype(BF16)
        gmat_ref[j, :, 0:LANES] = jnp.where(mask_g, _dot(w, rep_val), 0.0).astype(BF16)
        gmat_ref[j, :, LANES:2 * LANES] = jnp.where(mask_g, _dot(w, rep_gate), 0.0).astype(BF16)

    row = lax.broadcasted_iota(jnp.int32, (2 * GROUP_B, D_B), 0)
    col = lax.broadcasted_iota(jnp.int32, (2 * GROUP_B, D_B), 1)
    by_group = spread(bglu_t_ref[...], jnp.where(block_mask(N_GROUPS_B, D_B, 0, 4), 1.0, 0.0).astype(BF16))
    for i in range(2):
        mine = row == (col & (GROUP_B - 1)) + i * GROUP_B
        bglu_row_ref[i:i + 1, :] = jnp.sum(jnp.where(mine, by_group, 0.0), axis=0, keepdims=True)

    eye = rep_rows(CHUNK, CHUNK, CHUNK)
    bst_ref[...] = sum(_nt_dot(eye, term) for term in _bf16_terms(bs_ref[...]))


def _s5_prep_operands(a_re, a_im, log_dt, b_re, b_im, c_re, c_im, w_glu, b_glu, b_s):
    g, p, c = N_GROUPS_B, P_STATE, GROUP_B
    a8 = jnp.stack([a_re, a_im, jnp.broadcast_to(log_dt[:, None], (g, p))]).reshape(3 * N_SLABS, SLAB_STATES)
    args = (a8, b_re.transpose(0, 2, 1).reshape(g * c, p), b_im.transpose(0, 2, 1).reshape(g * c, p),
            c_re.reshape(g * c, p), c_im.reshape(g * c, p), w_glu.reshape(g * c, 2 * c), b_glu.T, b_s)
    out_shape = (
        jax.ShapeDtypeStruct((N_SLABS, LANES, 2 * SLAB_STATES), BF16),
        jax.ShapeDtypeStruct((HALF_SLABS, 2 * SLAB_STATES, 2 * LANES), BF16),
        jax.ShapeDtypeStruct((N_SLABS, LANES, 2 * LANES), BF16),
        jax.ShapeDtypeStruct((N_SLABS, SLAB_STATES), F32),
        jax.ShapeDtypeStruct((N_SLABS, SLAB_STATES), F32),
        jax.ShapeDtypeStruct((2, D_B), F32),
        jax.ShapeDtypeStruct((CHUNK, N_HEADS_A), F32),
    )
    return args, out_shape


ADALN_ROWS = 512
ADALN_PROMPT_ROWS = 2 * SUBLANES


N_PREP_IN, N_PREP_OUT = 8, 7


def _adaln_kernel(cs_ref, cp_ref, w_ref, b_ref, x2_ref, gpre_ref, *refs):
    prep_in = refs[:N_PREP_IN]
    os_ref, op_ref, hs_ref = refs[N_PREP_IN:N_PREP_IN + 3]
    prep_out = refs[N_PREP_IN + 3:]
    assert len(prep_out) == N_PREP_OUT
    k = pl.program_id(0)

    @pl.when(k == 0)
    def _():
        os_ref[...] = jnp.broadcast_to(b_ref[...], os_ref.shape)
        op_ref[...] = jnp.broadcast_to(b_ref[...], op_ref.shape)
        _prep_kernel(*prep_in, *prep_out)

    cols = pl.ds(pl.multiple_of(k * ADALN_ROWS, ADALN_ROWS), ADALN_ROWS)
    w = w_ref[...].astype(BF16)
    os_ref[...] += _dot(_silu(cs_ref[:, cols]).astype(BF16), w)
    n_p = cp_ref.shape[0]
    cp = jnp.concatenate([cp_ref[:, cols], jnp.zeros((ADALN_PROMPT_ROWS - n_p, ADALN_ROWS), F32)], axis=0)
    op_ref[...] += _dot(_silu(cp).astype(BF16), w)

    @pl.when(k == pl.num_programs(0) - 1)
    def _():
        x_tiles = D_MODEL // LANES
        n = hs_ref.shape[0]
        x = jnp.concatenate([x2_ref[pl.ds(q, n, stride=x_tiles), :] for q in range(x_tiles)], axis=1)
        shift = os_ref[:, 0:D_MODEL]
        scale = os_ref[:, D_MODEL:2 * D_MODEL]
        hs_ref[...] = (_rms(x, gpre_ref[...]) * (1.0 + scale) + shift).astype(BF16)


def _adaln(c_sample, c_prompt, w_c, b_c, x2_sample, g_pre, prep_args, prep_out_shape):
    n_s, n_p = c_sample.shape[0], c_prompt.shape[0]
    assert n_p <= ADALN_PROMPT_ROWS and D_MODEL % ADALN_ROWS == 0
    assert len(prep_args) == N_PREP_IN and len(prep_out_shape) == N_PREP_OUT
    whole = lambda a: pl.BlockSpec(a.shape, lambda k, nd=len(a.shape): (0,) * nd)
    return pl.pallas_call(
        _adaln_kernel,
        grid=(D_MODEL // ADALN_ROWS,),
        in_specs=[
            pl.BlockSpec((n_s, D_MODEL), lambda k: (0, 0)),
            pl.BlockSpec((n_p, D_MODEL), lambda k: (0, 0)),
            pl.BlockSpec((ADALN_ROWS, 3 * D_MODEL), lambda k: (k, 0)),
            pl.BlockSpec((1, 3 * D_MODEL), lambda k: (0, 0)),
            pl.BlockSpec(x2_sample.shape, lambda k: (0, 0)),
            pl.BlockSpec(g_pre.shape, lambda k: (0, 0)),
        ] + [whole(a) for a in prep_args],
        out_specs=(pl.BlockSpec((n_s, 3 * D_MODEL), lambda k: (0, 0)),
                   pl.BlockSpec((ADALN_PROMPT_ROWS, 3 * D_MODEL), lambda k: (0, 0)),
                   pl.BlockSpec((n_s, D_MODEL), lambda k: (0, 0))) + tuple(whole(a) for a in prep_out_shape),
        out_shape=(jax.ShapeDtypeStruct((n_s, 3 * D_MODEL), F32),
                   jax.ShapeDtypeStruct((ADALN_PROMPT_ROWS, 3 * D_MODEL), F32),
                   jax.ShapeDtypeStruct((n_s, D_MODEL), BF16)) + tuple(prep_out_shape),
        compiler_params=pltpu.CompilerParams(
            dimension_semantics=("arbitrary",), vmem_limit_bytes=VMEM_LIMIT),
        name="adaln",
    )(c_sample, c_prompt, w_c, b_c.reshape(1, -1), x2_sample, g_pre, *prep_args)


def _mixer_kernel(x_ref, mod_ref, gpre_ref, w_in_hbm, lng_ref, lnb_ref, ws_ref, bst_ref,
                  lam_re_ref, lam_im_ref, bmat_ref, cboth_ref, gmat_ref, dskip_ref, bglu_ref, hs_ref, wout_ref,
                  oin_ref, st_re_ref, st_im_ref, wout_bf_ref, ps_ref,
                  h_sc, a_sc, b_sc, c_sc, v_sc, xb_sc, bu_sc, y2_sc, st_sc, w_in_ref, w_sem, *, n_seq):
    step = pl.program_id(0)

    @pl.when(step == 0)
    def _():
        st_sc[...] = jnp.zeros_like(st_sc)

        t_rows, t_cols = a_sc.shape[0] // 2, a_sc.shape[1]
        stage = [buf.at[pl.ds(i * t_rows, t_rows), :] for buf in (a_sc, b_sc, c_sc) for i in range(2)]
        assert len(stage) == W_IN_RING
        tiles = [(r, c) for c in range(D_IN // t_cols) for r in range(D_MODEL // t_rows)]

        def tile_copy(n):
            r, c = tiles[n]
            src = w_in_hbm.at[pl.ds(r * t_rows, t_rows), pl.ds(c * t_cols, t_cols)]
            return pltpu.make_async_copy(src, stage[n % len(stage)], w_sem.at[n % len(stage)])

        for n in range(len(stage)):
            tile_copy(n).start()
        for n, (r, c) in enumerate(tiles):
            tile_copy(n).wait()
            w_in_ref[r * t_rows:(r + 1) * t_rows, c * t_cols:(c + 1) * t_cols] = (
                stage[n % len(stage)][...].astype(BF16))
            if n + len(stage) < len(tiles):
                tile_copy(n + len(stage)).start()

    wout_bf_ref[...] = wout_ref[...].astype(BF16)

    @pl.when(step < D_IN // SAMPLE_PROJ_COLS)
    def _():
        cols = pl.ds(pl.multiple_of(step * SAMPLE_PROJ_COLS, SAMPLE_PROJ_COLS), SAMPLE_PROJ_COLS)
        ps_ref[...] = _dot(hs_ref[...], w_in_ref[:, cols])

    w_u, w_v, w_z, w_xb, w_zb = 0, D_A, 2 * D_A, 3 * D_A, 3 * D_A + D_B

    for b in range(n_seq):
        x = x_ref[b]
        shift = mod_ref[b:b + 1, 0:D_MODEL]
        gain = gpre_ref[...] * (1.0 + mod_ref[b:b + 1, D_MODEL:2 * D_MODEL])
        rws = slice(b * CHUNK, (b + 1) * CHUNK)
        h_sc[rws, :] = (x * _rms_scale(x) * gain + shift).astype(BF16)
        b_sc[rws, :] = _dot(h_sc[rws, :], w_in_ref[:, w_xb:w_xb + D_B])
        xb_sc[rws, :] = b_sc[rws, :].astype(BF16)

    h = h_sc[...]

    tril = (lax.broadcasted_iota(jnp.int32, (CHUNK, CHUNK), 0)
            >= lax.broadcasted_iota(jnp.int32, (CHUNK, CHUNK), 1))
    n_fill = HALF_SLABS * SCAN_PIECES // 4
    fill_cols = D_A // n_fill

    def fill_u(q):
        cols = slice(q * fill_cols, (q + 1) * fill_cols)
        a_sc[:, cols] = _dot(h, w_in_ref[:, w_u + q * fill_cols:w_u + (q + 1) * fill_cols])

    def fill_z(q):
        cols = slice(q * fill_cols, (q + 1) * fill_cols)
        z = _dot(h, w_in_ref[:, w_z + q * fill_cols:w_z + (q + 1) * fill_cols])
        a_sc[:, cols] = a_sc[:, cols] * _silu(z)

    def fill_mix(q):
        heads_per_fill = N_HEADS_A // n_fill
        for hd in range(q * heads_per_fill, (q + 1) * heads_per_fill):
            cols = slice(hd * HEAD_A, (hd + 1) * HEAD_A)
            w_t = jnp.where(tril, ws_ref[hd], 0.0).astype(BF16)
            v_h = jnp.concatenate([v_sc[b * CHUNK:(b + 1) * CHUNK, cols] for b in range(n_seq)], axis=1)
            mix = _dot(w_t, v_h) + bst_ref[:, hd:hd + 1]
            for b in range(n_seq):
                rws = slice(b * CHUNK, (b + 1) * CHUNK)
                oin_ref[rws, cols] = (a_sc[rws, cols] * mix[:, b * HEAD_A:(b + 1) * HEAD_A]).astype(BF16)

    def fill_zb(q):
        cols = slice(q * fill_cols, (q + 1) * fill_cols)
        a_sc[:, cols] = _silu(_dot(h, w_in_ref[:, w_zb + q * fill_cols:w_zb + (q + 1) * fill_cols]))

    fillers = [functools.partial(f, q) for f, q in (
        (fill_u, 0), (fill_z, 0), (fill_u, 1), (fill_mix, 0), (fill_z, 1), (fill_u, 2), (fill_z, 2), (fill_mix, 1),
        (fill_u, 3), (fill_z, 3), (fill_zb, 0), (fill_mix, 2), (fill_zb, 1), (fill_zb, 2), (fill_mix, 3), (fill_zb, 3))]
    assert len(fillers) == HALF_SLABS * SCAN_PIECES

    first_half = lax.broadcasted_iota(jnp.int32, (N_STREAMS, LANES), 0) < n_seq

    def bu_piece(k, piece):
        t0 = piece * PIECE
        for half in range(2):
            j = half * HALF_SLABS + k
            lhs = jnp.concatenate(
                [xb_sc[b * CHUNK + t0:b * CHUNK + t0 + PIECE, j * LANES:(j + 1) * LANES]
                 for b in range(n_seq)], axis=0)
            bu = _dot(lhs, bmat_ref[j])
            for b in range(n_seq):
                r0 = (half * n_seq + b) * STREAM_PITCH + t0
                for l in range(2 * SLAB_TILES):
                    bu_sc[l, r0:r0 + PIECE, :] = bu[b * PIECE:(b + 1) * PIECE, l * LANES:(l + 1) * LANES]

    def scan_piece(k, piece, hr, hi, lam_r, lam_i):
        rows = []
        for t in range(piece * PIECE, (piece + 1) * PIECE):
            new_r, new_i = [], []
            for l in range(SLAB_TILES):
                br = bu_sc[l, pl.ds(t, N_STREAMS, stride=STREAM_PITCH), :]
                bi = bu_sc[SLAB_TILES + l, pl.ds(t, N_STREAMS, stride=STREAM_PITCH), :]
                new_r.append(lam_r[l] * hr[l] - lam_i[l] * hi[l] + br)
                new_i.append(lam_r[l] * hi[l] + lam_i[l] * hr[l] + bi)
            hr, hi = new_r, new_i
            rows.append(jnp.concatenate(hr + hi, axis=1))
        return hr, hi, jnp.concatenate(rows, axis=0)

    def c_piece(k, piece, hh):
        t0 = piece * PIECE
        rows = slice(t0 * N_STREAMS, (t0 + PIECE) * N_STREAMS)
        y2 = _dot(hh.astype(BF16), cboth_ref[k])
        y2_sc[0, rows, :] = y2[:, 0:LANES]
        y2_sc[1, rows, :] = y2[:, LANES:2 * LANES]
        for half in range(2):
            j = half * HALF_SLABS + k
            for b in range(n_seq):
                s = half * n_seq + b
                c_sc[b * CHUNK + t0:b * CHUNK + t0 + PIECE, j * LANES:(j + 1) * LANES] = (
                    y2_sc[half, pl.ds(t0 * N_STREAMS + s, PIECE, stride=N_STREAMS), :])

    for piece in range(SCAN_PIECES):
        bu_piece(0, piece)
    c_sc[...] = _dot(h, w_in_ref[:, w_v:w_v + D_A])
    v_sc[...] = _layernorm(c_sc[...], lng_ref[...], lnb_ref[...]).astype(BF16)
    for k in range(HALF_SLABS):
        tiles = [slice(l * LANES, (l + 1) * LANES) for l in range(SLAB_TILES)]
        lam_r = [jnp.where(first_half, lam_re_ref[k:k + 1, t], lam_re_ref[HALF_SLABS + k:HALF_SLABS + k + 1, t])
                 for t in tiles]
        lam_i = [jnp.where(first_half, lam_im_ref[k:k + 1, t], lam_im_ref[HALF_SLABS + k:HALF_SLABS + k + 1, t])
                 for t in tiles]
        hr = [st_sc[k, :, l * LANES:(l + 1) * LANES] for l in range(SLAB_TILES)]
        hi = [st_sc[k, :, (SLAB_TILES + l) * LANES:(SLAB_TILES + l + 1) * LANES] for l in range(SLAB_TILES)]
        for piece in range(SCAN_PIECES):
            hr, hi, hh = scan_piece(k, piece, hr, hi, lam_r, lam_i)
            c_piece(k, piece, hh)
            if k + 1 < HALF_SLABS:
                bu_piece(k + 1, piece)
            fillers[k * SCAN_PIECES + piece]()
        st_sc[k] = jnp.concatenate(hr + hi, axis=1)

    for k in range(HALF_SLABS):
        for half in range(2):
            j = half * HALF_SLABS + k
            for b in range(n_seq):
                s = half * n_seq + b
                st_re_ref[b, j:j + 1, :] = st_sc[k, s:s + 1, 0:SLAB_STATES]
                st_im_ref[b, j:j + 1, :] = st_sc[k, s:s + 1, SLAB_STATES:2 * SLAB_STATES]

    for j in range(N_SLABS):
        cols = slice(j * LANES, (j + 1) * LANES)
        y = c_sc[:, cols] + dskip_ref[:, cols] * b_sc[:, cols]
        g = _dot(y.astype(BF16), gmat_ref[j])
        val = g[:, 0:LANES] + bglu_ref[0:1, cols]
        gate = g[:, LANES:2 * LANES] + bglu_ref[1:2, cols]
        oin_ref[:, D_A + j * LANES:D_A + (j + 1) * LANES] = (
            val * jax.nn.sigmoid(gate) * a_sc[:, cols]).astype(BF16)


def _const_spec(shape):
    zeros = (0,) * len(shape)
    return pl.BlockSpec(shape, lambda i: zeros, pipeline_mode=pl.Buffered(1))


def _mixer(x, mod, g_pre, w_in, ln_g, ln_b, w_s, bst, lam_re, lam_im, bmat, cboth, gmat,
           d_skip, b_glu_rows, h_sample, w_out):
    n_seq, seq, _ = x.shape
    assert 2 * n_seq == N_STREAMS and seq % CHUNK == 0
    rows = n_seq * CHUNK
    n_steps = seq // CHUNK
    wout_rows = w_out.shape[0] // n_steps
    assert wout_rows * n_steps == w_out.shape[0] and wout_rows % (2 * SUBLANES) == 0
    state_shape = (n_seq, N_SLABS, SLAB_STATES)
    n_sample = h_sample.shape[0]
    n_proj = D_IN // SAMPLE_PROJ_COLS
    assert n_proj <= n_steps
    consts = (mod, g_pre, w_in, ln_g, ln_b, w_s, bst, lam_re, lam_im, bmat, cboth, gmat,
              d_skip, b_glu_rows, h_sample)
    assert w_in.shape == (D_MODEL, D_IN) and D_MODEL % rows == 0 and D_IN % D_A == 0
    return pl.pallas_call(
        functools.partial(_mixer_kernel, n_seq=n_seq),
        grid=(n_steps,),
        in_specs=[pl.BlockSpec((n_seq, CHUNK, D_MODEL), lambda i: (0, i, 0))]
        + [pl.BlockSpec(memory_space=pl.ANY) if c is w_in else _const_spec(c.shape) for c in consts]
        + [pl.BlockSpec((wout_rows, w_out.shape[1]), lambda i: (i, 0))],
        out_specs=(
            pl.BlockSpec((rows, D_A + D_B), lambda i: (i, 0)),
            pl.BlockSpec(state_shape, lambda i: (0, 0, 0)),
            pl.BlockSpec(state_shape, lambda i: (0, 0, 0)),
            pl.BlockSpec((wout_rows, w_out.shape[1]), lambda i: (i, 0)),
            pl.BlockSpec((n_sample, SAMPLE_PROJ_COLS), lambda i: (0, jnp.minimum(i, n_proj - 1))),
        ),
        out_shape=(
            jax.ShapeDtypeStruct((n_steps * rows, D_A + D_B), BF16),
            jax.ShapeDtypeStruct(state_shape, F32),
            jax.ShapeDtypeStruct(state_shape, F32),
            jax.ShapeDtypeStruct(w_out.shape, BF16),
            jax.ShapeDtypeStruct((n_sample, D_IN), F32),
        ),
        scratch_shapes=[
            pltpu.VMEM((rows, D_MODEL), BF16),
            pltpu.VMEM((rows, D_A), F32),
            pltpu.VMEM((rows, D_A), F32),
            pltpu.VMEM((rows, D_A), F32),
            pltpu.VMEM((rows, D_A), BF16),
            pltpu.VMEM((rows, D_B), BF16),
            pltpu.VMEM((2 * SLAB_TILES, N_STREAMS * STREAM_PITCH, LANES), F32),
            pltpu.VMEM((2, N_STREAMS * CHUNK, LANES), F32),
            pltpu.VMEM((HALF_SLABS, N_STREAMS, 2 * SLAB_STATES), F32),
            pltpu.VMEM((D_MODEL, D_IN), BF16),
            pltpu.SemaphoreType.DMA((W_IN_RING,)),
        ],
        compiler_params=pltpu.CompilerParams(
            dimension_semantics=("arbitrary",), vmem_limit_bytes=VMEM_LIMIT),
        name="mixer",
    )(x, *consts, w_out)


def _outproj_kernel(oin_ref, x_ref, mod_ref, gpost_ref, w_out_ref, y_ref, *, n_seq):
    o = _dot(oin_ref[...], w_out_ref[...])
    r = o * _rms_scale(o)
    for b in range(n_seq):
        gain = gpost_ref[...] * mod_ref[b:b + 1, 2 * D_MODEL:3 * D_MODEL]
        y_ref[b] = x_ref[b] + r[b * CHUNK:(b + 1) * CHUNK, :] * gain


def _outproj(oin, x, mod, g_post, w_out_bf):
    n_seq, seq, _ = x.shape
    rows = n_seq * CHUNK
    return pl.pallas_call(
        functools.partial(_outproj_kernel, n_seq=n_seq),
        grid=(seq // CHUNK,),
        in_specs=[
            pl.BlockSpec((rows, D_A + D_B), lambda i: (i, 0)),
            pl.BlockSpec((n_seq, CHUNK, D_MODEL), lambda i: (0, i, 0)),
            _const_spec(mod.shape), _const_spec(g_post.shape), _const_spec(w_out_bf.shape),
        ],
        out_specs=pl.BlockSpec((n_seq, CHUNK, D_MODEL), lambda i: (0, i, 0)),
        out_shape=jax.ShapeDtypeStruct(x.shape, F32),
        compiler_params=pltpu.CompilerParams(
            dimension_semantics=("arbitrary",), vmem_limit_bytes=VMEM_LIMIT),
        name="outproj",
    )(oin, x, mod, g_post, w_out_bf)


def _sample_kernel(x2_ref, mod_ref, proj_ref, lng_ref, lnb_ref, ws_ref, bst_ref,
                   lam_re_ref, lam_im_ref, bmat_ref, cboth_ref, gmat_ref, dskip_ref, bglu_ref,
                   h0t_re_ref, h0t_im_ref, gpost_ref, w_out_ref,
                   y2_ref, v2_ref, hst_re_ref, hst_im_ref, oin_sc):
    n = proj_ref.shape[0]
    x_tiles = D_MODEL // LANES
    v_tiles = D_A // LANES
    w_u, w_v, w_z, w_xb, w_zb = 0, D_A, 2 * D_A, 3 * D_A, 3 * D_A + D_B

    v = _layernorm(proj_ref[:, w_v:w_v + D_A], lng_ref[...], lnb_ref[...])
    for q in range(v_tiles):
        v2_ref[pl.ds(q, n, stride=v_tiles), :] = v[:, q * LANES:(q + 1) * LANES]
    for hd in range(N_HEADS_A):
        cols = slice(hd * HEAD_A, (hd + 1) * HEAD_A)
        mix = ws_ref[hd, 0:1, 0:1] * v[:, cols] + bst_ref[0:1, hd:hd + 1]
        u = proj_ref[:, w_u + hd * HEAD_A:w_u + (hd + 1) * HEAD_A]
        z = proj_ref[:, w_z + hd * HEAD_A:w_z + (hd + 1) * HEAD_A]
        oin_sc[:, cols] = (u * mix * _silu(z)).astype(BF16)

    for j in range(N_SLABS):
        cols = slice(j * LANES, (j + 1) * LANES)
        st = slice(j * SLAB_STATES, (j + 1) * SLAB_STATES)
        xb = proj_ref[:, w_xb + j * LANES:w_xb + (j + 1) * LANES]
        zb = proj_ref[:, w_zb + j * LANES:w_zb + (j + 1) * LANES]
        bu = _dot(xb.astype(BF16), bmat_ref[j])
        lr = lam_re_ref[j:j + 1, :]
        li = lam_im_ref[j:j + 1, :]
        h0r = h0t_re_ref[st, :].T
        h0i = h0t_im_ref[st, :].T
        hr = lr * h0r - li * h0i + bu[:, 0:SLAB_STATES]
        hi = lr * h0i + li * h0r + bu[:, SLAB_STATES:2 * SLAB_STATES]
        hst_re_ref[st, :] = hr.T
        hst_im_ref[st, :] = hi.T
        k, half = j % HALF_SLABS, j // HALF_SLABS
        c_j = cboth_ref[k, :, half * LANES:(half + 1) * LANES]
        y = _dot(jnp.concatenate([hr, hi], axis=1).astype(BF16), c_j)
        y = y + dskip_ref[:, cols] * xb
        g = _dot(y.astype(BF16), gmat_ref[j])
        val = g[:, 0:LANES] + bglu_ref[0:1, cols]
        gt = g[:, LANES:2 * LANES] + bglu_ref[1:2, cols]
        oin_sc[:, D_A + j * LANES:D_A + (j + 1) * LANES] = (
            val * jax.nn.sigmoid(gt) * _silu(zb)).astype(BF16)

    o = _dot(oin_sc[...], w_out_ref[...])
    gate = mod_ref[:, 2 * D_MODEL:3 * D_MODEL]
    r = gate * _rms(o, gpost_ref[...])
    for q in range(x_tiles):
        cols = slice(q * LANES, (q + 1) * LANES)
        y2_ref[pl.ds(q, n, stride=x_tiles), :] = x2_ref[pl.ds(q, n, stride=x_tiles), :] + r[:, cols]


def _sample(x2, mod, proj, ln_g, ln_b, w_s, bst, lam_re, lam_im, bmat, cboth, gmat,
            d_skip, b_glu_rows, h0t_re, h0t_im, g_post, w_out_bf):
    n = proj.shape[0]
    n_state = N_GROUPS_B * P_STATE
    out_shapes = ((n * D_MODEL // LANES, LANES), (n * D_A // LANES, LANES), (n_state, n), (n_state, n))
    return pl.pallas_call(
        _sample_kernel,
        out_shape=tuple(jax.ShapeDtypeStruct(shp, F32) for shp in out_shapes),
        scratch_shapes=[pltpu.VMEM((n, D_A + D_B), BF16)],
        compiler_params=pltpu.CompilerParams(vmem_limit_bytes=VMEM_LIMIT),
        name="sample",
    )(x2, mod, proj, ln_g, ln_b, w_s, bst, lam_re, lam_im, bmat, cboth, gmat,
      d_skip, b_glu_rows, h0t_re, h0t_im, g_post, w_out_bf)


def kernel(x_prompt, x_sample, c_prompt, c_sample, state_b_re, state_b_im, w_c, b_c, g_pre, w_in,
           ln_v_g, ln_v_b, w_s, b_s, a_re, a_im, log_dt, b_re, b_im, c_re, c_im, d_skip, w_glu, b_glu,
           w_out, g_post):
    n_p = x_prompt.shape[0]
    n_s = x_sample.shape[0]
    assert x_sample.shape[1] == 1 and n_s % SUBLANES == 0

    row = lambda v: v.reshape(1, -1)
    x2_sample = x_sample.reshape(n_s * D_MODEL // LANES, LANES)
    prep_args, prep_out_shape = _s5_prep_operands(a_re, a_im, log_dt, b_re, b_im, c_re, c_im, w_glu, b_glu, b_s)
    (mod_s, mod_p, h_sample, bmat, cboth, gmat, lam_re, lam_im, b_glu_rows, bst) = _adaln(
        c_sample, c_prompt, w_c, b_c, x2_sample, row(g_pre), prep_args, prep_out_shape)

    s5 = (w_s, bst, lam_re, lam_im, bmat, cboth, gmat, row(d_skip), b_glu_rows)
    oin, st_re, st_im, w_out_bf, proj_s = _mixer(
        x_prompt, mod_p, row(g_pre), w_in, row(ln_v_g), row(ln_v_b), *s5, h_sample, w_out)
    y_prompt = _outproj(oin, x_prompt, mod_p, row(g_post), w_out_bf)

    to_rows = lambda st: st.transpose(1, 2, 0).reshape(N_GROUPS_B * P_STATE, n_s)
    from_rows = lambda st: st.reshape(N_GROUPS_B, P_STATE, n_s).transpose(2, 0, 1)
    y_s, v_s, hs_re, hs_im = _sample(
        x2_sample, mod_s, proj_s, row(ln_v_g), row(ln_v_b), *s5,
        to_rows(state_b_re), to_rows(state_b_im), row(g_post), w_out_bf)

    return (y_prompt, y_s.reshape(n_s, 1, D_MODEL), v_s.reshape(n_s, 1, D_A),
            st_re.reshape(n_p, N_GROUPS_B, P_STATE), st_im.reshape(n_p, N_GROUPS_B, P_STATE),
            from_rows(hs_re), from_rows(hs_im))
```

```python
import functools

import jax
import jax.numpy as jnp
from jax import lax
from jax.experimental import pallas as pl
from jax.experimental.pallas import tpu as pltpu

F32 = jnp.float32
BF16 = jnp.bfloat16

EPS = 1e-6
D_MODEL = 2048
D_A = 1024
D_B = 1024
D_IN = 3 * D_A + 2 * D_B
CHUNK = 128
HEAD_A = 128
N_HEADS_A = D_A // HEAD_A
GROUP_B = 16
N_GROUPS_B = D_B // GROUP_B
P_STATE = 64

LANES = 128
SUBLANES = 8
SLAB_GROUPS = LANES // GROUP_B
N_SLABS = N_GROUPS_B // SLAB_GROUPS
SLAB_STATES = SLAB_GROUPS * P_STATE
SLAB_TILES = SLAB_STATES // LANES
N_STREAMS = SUBLANES
HALF_SLABS = N_SLABS // 2
STREAM_PITCH = CHUNK + SUBLANES
SCAN_PIECES = 4
PIECE = CHUNK // SCAN_PIECES
V7X_VMEM_BYTES = 64 * 1024 * 1024
VMEM_LIMIT = V7X_VMEM_BYTES - 1024 * 1024
SAMPLE_PROJ_COLS = 512
W_IN_RING = 6


def _silu(x):
    return x * jax.nn.sigmoid(x)


def _rms_scale(x):
    return lax.rsqrt(jnp.mean(x * x, axis=-1, keepdims=True) + EPS)


def _rms(x, g):
    return x * _rms_scale(x) * g


def _layernorm(x, g, b):
    mu = jnp.mean(x, axis=-1, keepdims=True)
    xc = x - mu
    return xc * lax.rsqrt(jnp.mean(xc * xc, axis=-1, keepdims=True) + EPS) * g + b


def _dot(a, b):
    return jnp.dot(a, b, preferred_element_type=F32)


def _discretize(a_re, a_im, log_dt):
    dt = jnp.exp(log_dt)
    mag = jnp.exp(dt * a_re)
    abar_re = mag * jnp.cos(dt * a_im)
    abar_im = mag * jnp.sin(dt * a_im)
    return abar_re, abar_im


def _nt_dot(a, b):
    return lax.dot_general(a, b, (((1,), (1,)), ((), ())), preferred_element_type=F32)


def _bf16_terms(x):
    hi = x.astype(BF16)
    r1 = x - hi.astype(F32)
    mid = r1.astype(BF16)
    lo = (r1 - mid.astype(F32)).astype(BF16)
    return hi, mid, lo


def _prep_kernel(a8_ref, bt_re_ref, bt_im_ref, c_re_ref, c_im_ref, wglu_ref, bglu_t_ref, bs_ref,
                 bmat_ref, cboth_ref, gmat_ref, lam_re_ref, lam_im_ref, bglu_row_ref, bst_ref):
    a_re = a8_ref[0:N_SLABS, :]
    a_im = a8_ref[N_SLABS:2 * N_SLABS, :]
    abar_re, abar_im = _discretize(a_re, a_im, a8_ref[2 * N_SLABS:3 * N_SLABS, :])
    lam_re_ref[...] = abar_re
    lam_im_ref[...] = abar_im
    num_re = abar_re - 1.0
    num_im = abar_im
    den = a_re * a_re + a_im * a_im
    coef_re = (num_re * a_re + num_im * a_im) / den
    coef_im = (num_im * a_re - num_re * a_im) / den

    def rep_matrix(k, n, period, offset=0):
        row = lax.broadcasted_iota(jnp.int32, (k, n), 0)
        col = lax.broadcasted_iota(jnp.int32, (k, n), 1)
        return jnp.where((col & (period - 1)) + offset == row, 1.0, 0.0).astype(BF16)

    def rep_rows(m, k, period):
        row = lax.broadcasted_iota(jnp.int32, (m, k), 0)
        col = lax.broadcasted_iota(jnp.int32, (m, k), 1)
        return jnp.where((row & (period - 1)) == col, 1.0, 0.0).astype(BF16)

    def block_mask(m, n, row_shift, col_shift):
        row = lax.broadcasted_iota(jnp.int32, (m, n), 0)
        col = lax.broadcasted_iota(jnp.int32, (m, n), 1)
        return (row >> row_shift) == (col >> col_shift)

    rep_state = rep_matrix(P_STATE, SLAB_STATES, P_STATE)
    rows_state = rep_rows(SLAB_STATES, P_STATE, P_STATE)
    rep_val = rep_matrix(2 * GROUP_B, LANES, GROUP_B)
    rep_gate = rep_matrix(2 * GROUP_B, LANES, GROUP_B, GROUP_B)
    mask_b = block_mask(LANES, SLAB_STATES, 4, 6)
    mask_c = block_mask(SLAB_STATES, LANES, 6, 4)
    mask_g = block_mask(LANES, LANES, 4, 4)

    def spread(x, rep):
        return sum(_dot(term, rep) for term in _bf16_terms(x))

    for j in range(N_SLABS):
        crows = slice(j * LANES, (j + 1) * LANES)
        bt_re = spread(bt_re_ref[crows, :], rep_state)
        bt_im = spread(bt_im_ref[crows, :], rep_state)
        cr = coef_re[j:j + 1, :]
        ci = coef_im[j:j + 1, :]
        bmat_ref[j, :, 0:SLAB_STATES] = jnp.where(mask_b, cr * bt_re - ci * bt_im, 0.0).astype(BF16)
        bmat_ref[j, :, SLAB_STATES:2 * SLAB_STATES] = jnp.where(
            mask_b, cr * bt_im + ci * bt_re, 0.0).astype(BF16)

        k, half = j % HALF_SLABS, j // HALF_SLABS
        ccols = slice(half * LANES, (half + 1) * LANES)
        ct_re = _nt_dot(rows_state, c_re_ref[crows, :].astype(BF16))
        ct_im = _nt_dot(rows_state, c_im_ref[crows, :].astype(BF16))
        cboth_ref[k, 0:SLAB_STATES, ccols] = jnp.where(mask_c, ct_re, 0.0).astype(BF16)
        cboth_ref[k, SLAB_STATES:2 * SLAB_STATES, ccols] = jnp.where(mask_c, -ct_im, 0.0).astype(BF16)

        w = wglu_ref[crows, :].astype(BF16)
        gmat_ref[j, :, 0:LANES] = jnp.where(mask_g, _dot(w, rep_val), 0.0).astype(BF16)
        gmat_ref[j, :, LANES:2 * LANES] = jnp.where(mask_g, _dot(w, rep_gate), 0.0).astype(BF16)

    row = lax.broadcasted_iota(jnp.int32, (2 * GROUP_B, D_B), 0)
    col = lax.broadcasted_iota(jnp.int32, (2 * GROUP_B, D_B), 1)
    by_group = spread(bglu_t_ref[...], jnp.where(block_mask(N_GROUPS_B, D_B, 0, 4), 1.0, 0.0).astype(BF16))
    for i in range(2):
        mine = row == (col & (GROUP_B - 1)) + i * GROUP_B
        bglu_row_ref[i:i + 1, :] = jnp.sum(jnp.where(mine, by_group, 0.0), axis=0, keepdims=True)

    eye = rep_rows(CHUNK, CHUNK, CHUNK)
    bst_ref[...] = sum(_nt_dot(eye, term) for term in _bf16_terms(bs_ref[...]))


def _s5_prep_operands(a_re, a_im, log_dt, b_re, b_im, c_re, c_im, w_glu, b_glu, b_s):
    g, p, c = N_GROUPS_B, P_STATE, GROUP_B
    a8 = jnp.stack([a_re, a_im, jnp.broadcast_to(log_dt[:, None], (g, p))]).reshape(3 * N_SLABS, SLAB_STATES)
    args = (a8, b_re.transpose(0, 2, 1).reshape(g * c, p), b_im.transpose(0, 2, 1).reshape(g * c, p),
            c_re.reshape(g * c, p), c_im.reshape(g * c, p), w_glu.reshape(g * c, 2 * c), b_glu.T, b_s)
    out_shape = (
        jax.ShapeDtypeStruct((N_SLABS, LANES, 2 * SLAB_STATES), BF16),
        jax.ShapeDtypeStruct((HALF_SLABS, 2 * SLAB_STATES, 2 * LANES), BF16),
        jax.ShapeDtypeStruct((N_SLABS, LANES, 2 * LANES), BF16),
        jax.ShapeDtypeStruct((N_SLABS, SLAB_STATES), F32),
        jax.ShapeDtypeStruct((N_SLABS, SLAB_STATES), F32),
        jax.ShapeDtypeStruct((2, D_B), F32),
        jax.ShapeDtypeStruct((CHUNK, N_HEADS_A), F32),
    )
    return args, out_shape


ADALN_ROWS = 512
ADALN_PROMPT_ROWS = 2 * SUBLANES


N_PREP_IN, N_PREP_OUT = 8, 7


def _adaln_kernel(cs_ref, cp_ref, w_ref, b_ref, x2_ref, gpre_ref, *refs):
    prep_in = refs[:N_PREP_IN]
    os_ref, op_ref, hs_ref = refs[N_PREP_IN:N_PREP_IN + 3]
    prep_out = refs[N_PREP_IN + 3:]
    assert len(prep_out) == N_PREP_OUT
    k = pl.program_id(0)

    @pl.when(k == 0)
    def _():
        os_ref[...] = jnp.broadcast_to(b_ref[...], os_ref.shape)
        op_ref[...] = jnp.broadcast_to(b_ref[...], op_ref.shape)
        _prep_kernel(*prep_in, *prep_out)

    cols = pl.ds(pl.multiple_of(k * ADALN_ROWS, ADALN_ROWS), ADALN_ROWS)
    w = w_ref[...].astype(BF16)
    os_ref[...] += _dot(_silu(cs_ref[:, cols]).astype(BF16), w)
    n_p = cp_ref.shape[0]
    cp = jnp.concatenate([cp_ref[:, cols], jnp.zeros((ADALN_PROMPT_ROWS - n_p, ADALN_ROWS), F32)], axis=0)
    op_ref[...] += _dot(_silu(cp).astype(BF16), w)

    @pl.when(k == pl.num_programs(0) - 1)
    def _():
        x_tiles = D_MODEL // LANES
        n = hs_ref.shape[0]
        x = jnp.concatenate([x2_ref[pl.ds(q, n, stride=x_tiles), :] for q in range(x_tiles)], axis=1)
        shift = os_ref[:, 0:D_MODEL]
        scale = os_ref[:, D_MODEL:2 * D_MODEL]
        hs_ref[...] = (_rms(x, gpre_ref[...]) * (1.0 + scale) + shift).astype(BF16)


def _adaln(c_sample, c_prompt, w_c, b_c, x2_sample, g_pre, prep_args, prep_out_shape):
    n_s, n_p = c_sample.shape[0], c_prompt.shape[0]
    assert n_p <= ADALN_PROMPT_ROWS and D_MODEL % ADALN_ROWS == 0
    assert len(prep_args) == N_PREP_IN and len(prep_out_shape) == N_PREP_OUT
    whole = lambda a: pl.BlockSpec(a.shape, lambda k, nd=len(a.shape): (0,) * nd)
    return pl.pallas_call(
        _adaln_kernel,
        grid=(D_MODEL // ADALN_ROWS,),
        in_specs=[
            pl.BlockSpec((n_s, D_MODEL), lambda k: (0, 0)),
            pl.BlockSpec((n_p, D_MODEL), lambda k: (0, 0)),
            pl.BlockSpec((ADALN_ROWS, 3 * D_MODEL), lambda k: (k, 0)),
            pl.BlockSpec((1, 3 * D_MODEL), lambda k: (0, 0)),
            pl.BlockSpec(x2_sample.shape, lambda k: (0, 0)),
            pl.BlockSpec(g_pre.shape, lambda k: (0, 0)),
        ] + [whole(a) for a in prep_args],
        out_specs=(pl.BlockSpec((n_s, 3 * D_MODEL), lambda k: (0, 0)),
                   pl.BlockSpec((ADALN_PROMPT_ROWS, 3 * D_MODEL), lambda k: (0, 0)),
                   pl.BlockSpec((n_s, D_MODEL), lambda k: (0, 0))) + tuple(whole(a) for a in prep_out_shape),
        out_shape=(jax.ShapeDtypeStruct((n_s, 3 * D_MODEL), F32),
                   jax.ShapeDtypeStruct((ADALN_PROMPT_ROWS, 3 * D_MODEL), F32),
                   jax.ShapeDtypeStruct((n_s, D_MODEL), BF16)) + tuple(prep_out_shape),
        compiler_params=pltpu.CompilerParams(
            dimension_semantics=("arbitrary",), vmem_limit_bytes=VMEM_LIMIT),
        name="adaln",
    )(c_sample, c_prompt, w_c, b_c.reshape(1, -1), x2_sample, g_pre, *prep_args)


def _mixer_kernel(x_ref, mod_ref, gpre_ref, w_in_hbm, lng_ref, lnb_ref, ws_ref, bst_ref,
                  lam_re_ref, lam_im_ref, bmat_ref, cboth_ref, gmat_ref, dskip_ref, bglu_ref, hs_ref, wout_ref,
                  oin_ref, st_re_ref, st_im_ref, wout_bf_ref, ps_ref,
                  h_sc, a_sc, b_sc, c_sc, v_sc, xb_sc, bu_sc, y2_sc, st_sc, w_in_ref, w_sem, *, n_seq):
    step = pl.program_id(0)

    @pl.when(step == 0)
    def _():
        st_sc[...] = jnp.zeros_like(st_sc)

        t_rows, t_cols = a_sc.shape[0] // 2, a_sc.shape[1]
        stage = [buf.at[pl.ds(i * t_rows, t_rows), :] for buf in (a_sc, b_sc, c_sc) for i in range(2)]
        assert len(stage) == W_IN_RING
        tiles = [(r, c) for c in range(D_IN // t_cols) for r in range(D_MODEL // t_rows)]

        def tile_copy(n):
            r, c = tiles[n]
            src = w_in_hbm.at[pl.ds(r * t_rows, t_rows), pl.ds(c * t_cols, t_cols)]
            return pltpu.make_async_copy(src, stage[n % len(stage)], w_sem.at[n % len(stage)])

        for n in range(len(stage)):
            tile_copy(n).start()
        for n, (r, c) in enumerate(tiles):
            tile_copy(n).wait()
            w_in_ref[r * t_rows:(r + 1) * t_rows, c * t_cols:(c + 1) * t_cols] = (
                stage[n % len(stage)][...].astype(BF16))
            if n + len(stage) < len(tiles):
                tile_copy(n + len(stage)).start()

    wout_bf_ref[...] = wout_ref[...].astype(BF16)

    @pl.when(step < D_IN // SAMPLE_PROJ_COLS)
    def _():
        cols = pl.ds(pl.multiple_of(step * SAMPLE_PROJ_COLS, SAMPLE_PROJ_COLS), SAMPLE_PROJ_COLS)
        ps_ref[...] = _dot(hs_ref[...], w_in_ref[:, cols])

    w_u, w_v, w_z, w_xb, w_zb = 0, D_A, 2 * D_A, 3 * D_A, 3 * D_A + D_B

    for b in range(n_seq):
        x = x_ref[b]
        shift = mod_ref[b:b + 1, 0:D_MODEL]
        gain = gpre_ref[...] * (1.0 + mod_ref[b:b + 1, D_MODEL:2 * D_MODEL])
        rws = slice(b * CHUNK, (b + 1) * CHUNK)
        h_sc[rws, :] = (x * _rms_scale(x) * gain + shift).astype(BF16)
        b_sc[rws, :] = _dot(h_sc[rws, :], w_in_ref[:, w_xb:w_xb + D_B])
        xb_sc[rws, :] = b_sc[rws, :].astype(BF16)

    h = h_sc[...]

    tril = (lax.broadcasted_iota(jnp.int32, (CHUNK, CHUNK), 0)
            >= lax.broadcasted_iota(jnp.int32, (CHUNK, CHUNK), 1))
    n_fill = HALF_SLABS * SCAN_PIECES // 4
    fill_cols = D_A // n_fill

    def fill_u(q):
        cols = slice(q * fill_cols, (q + 1) * fill_cols)
        a_sc[:, cols] = _dot(h, w_in_ref[:, w_u + q * fill_cols:w_u + (q + 1) * fill_cols])

    def fill_z(q):
        cols = slice(q * fill_cols, (q + 1) * fill_cols)
        z = _dot(h, w_in_ref[:, w_z + q * fill_cols:w_z + (q + 1) * fill_cols])
        a_sc[:, cols] = a_sc[:, cols] * _silu(z)

    def fill_mix(q):
        heads_per_fill = N_HEADS_A // n_fill
        for hd in range(q * heads_per_fill, (q + 1) * heads_per_fill):
            cols = slice(hd * HEAD_A, (hd + 1) * HEAD_A)
            w_t = jnp.where(tril, ws_ref[hd], 0.0).astype(BF16)
            v_h = jnp.concatenate([v_sc[b * CHUNK:(b + 1) * CHUNK, cols] for b in range(n_seq)], axis=1)
            mix = _dot(w_t, v_h) + bst_ref[:, hd:hd + 1]
            for b in range(n_seq):
                rws = slice(b * CHUNK, (b + 1) * CHUNK)
                oin_ref[rws, cols] = (a_sc[rws, cols] * mix[:, b * HEAD_A:(b + 1) * HEAD_A]).astype(BF16)

    def fill_zb(q):
        cols = slice(q * fill_cols, (q + 1) * fill_cols)
        a_sc[:, cols] = _silu(_dot(h, w_in_ref[:, w_zb + q * fill_cols:w_zb + (q + 1) * fill_cols]))

    fillers = [functools.partial(f, q) for f, q in (
        (fill_u, 0), (fill_z, 0), (fill_u, 1), (fill_mix, 0), (fill_z, 1), (fill_u, 2), (fill_z, 2), (fill_mix, 1),
        (fill_u, 3), (fill_z, 3), (fill_zb, 0), (fill_mix, 2), (fill_zb, 1), (fill_zb, 2), (fill_mix, 3), (fill_zb, 3))]
    assert len(fillers) == HALF_SLABS * SCAN_PIECES

    first_half = lax.broadcasted_iota(jnp.int32, (N_STREAMS, LANES), 0) < n_seq

    def bu_piece(k, piece):
        t0 = piece * PIECE
        for half in range(2):
            j = half * HALF_SLABS + k
            lhs = jnp.concatenate(
                [xb_sc[b * CHUNK + t0:b * CHUNK + t0 + PIECE, j * LANES:(j + 1) * LANES]
                 for b in range(n_seq)], axis=0)
            bu = _dot(lhs, bmat_ref[j])
            for b in range(n_seq):
                r0 = (half * n_seq + b) * STREAM_PITCH + t0
                for l in range(2 * SLAB_TILES):
                    bu_sc[l, r0:r0 + PIECE, :] = bu[b * PIECE:(b + 1) * PIECE, l * LANES:(l + 1) * LANES]

    def scan_piece(k, piece, hr, hi, lam_r, lam_i):
        rows = []
        for t in range(piece * PIECE, (piece + 1) * PIECE):
            new_r, new_i = [], []
            for l in range(SLAB_TILES):
                br = bu_sc[l, pl.ds(t, N_STREAMS, stride=STREAM_PITCH), :]
                bi = bu_sc[SLAB_TILES + l, pl.ds(t, N_STREAMS, stride=STREAM_PITCH), :]
                new_r.append(lam_r[l] * hr[l] - lam_i[l] * hi[l] + br)
                new_i.append(lam_r[l] * hi[l] + lam_i[l] * hr[l] + bi)
            hr, hi = new_r, new_i
            rows.append(jnp.concatenate(hr + hi, axis=1))
        return hr, hi, jnp.concatenate(rows, axis=0)

    def c_piece(k, piece, hh):
        t0 = piece * PIECE
        rows = slice(t0 * N_STREAMS, (t0 + PIECE) * N_STREAMS)
        y2 = _dot(hh.astype(BF16), cboth_ref[k])
        y2_sc[0, rows, :] = y2[:, 0:LANES]
        y2_sc[1, rows, :] = y2[:, LANES:2 * LANES]
        for half in range(2):
            j = half * HALF_SLABS + k
            for b in range(n_seq):
                s = half * n_seq + b
                c_sc[b * CHUNK + t0:b * CHUNK + t0 + PIECE, j * LANES:(j + 1) * LANES] = (
                    y2_sc[half, pl.ds(t0 * N_STREAMS + s, PIECE, stride=N_STREAMS), :])

    for piece in range(SCAN_PIECES):
        bu_piece(0, piece)
    c_sc[...] = _dot(h, w_in_ref[:, w_v:w_v + D_A])
    v_sc[...] = _layernorm(c_sc[...], lng_ref[...], lnb_ref[...]).astype(BF16)
    for k in range(HALF_SLABS):
        tiles = [slice(l * LANES, (l + 1) * LANES) for l in range(SLAB_TILES)]
        lam_r = [jnp.where(first_half, lam_re_ref[k:k + 1, t], lam_re_ref[HALF_SLABS + k:HALF_SLABS + k + 1, t])
                 for t in tiles]
        lam_i = [jnp.where(first_half, lam_im_ref[k:k + 1, t], lam_im_ref[HALF_SLABS + k:HALF_SLABS + k + 1, t])
                 for t in tiles]
        hr = [st_sc[k, :, l * LANES:(l + 1) * LANES] for l in range(SLAB_TILES)]
        hi = [st_sc[k, :, (SLAB_TILES + l) * LANES:(SLAB_TILES + l + 1) * LANES] for l in range(SLAB_TILES)]
        for piece in range(SCAN_PIECES):
            hr, hi, hh = scan_piece(k, piece, hr, hi, lam_r, lam_i)
            c_piece(k, piece, hh)
            if k + 1 < HALF_SLABS:
                bu_piece(k + 1, piece)
            fillers[k * SCAN_PIECES + piece]()
        st_sc[k] = jnp.concatenate(hr + hi, axis=1)

    for k in range(HALF_SLABS):
        for half in range(2):
            j = half * HALF_SLABS + k
            for b in range(n_seq):
                s = half * n_seq + b
                st_re_ref[b, j:j + 1, :] = st_sc[k, s:s + 1, 0:SLAB_STATES]
                st_im_ref[b, j:j + 1, :] = st_sc[k, s:s + 1, SLAB_STATES:2 * SLAB_STATES]

    for j in range(N_SLABS):
        cols = slice(j * LANES, (j + 1) * LANES)
        y = c_sc[:, cols] + dskip_ref[:, cols] * b_sc[:, cols]
        g = _dot(y.astype(BF16), gmat_ref[j])
        val = g[:, 0:LANES] + bglu_ref[0:1, cols]
        gate = g[:, LANES:2 * LANES] + bglu_ref[1:2, cols]
        oin_ref[:, D_A + j * LANES:D_A + (j + 1) * LANES] = (
            val * jax.nn.sigmoid(gate) * a_sc[:, cols]).astype(BF16)


def _const_spec(shape):
    zeros = (0,) * len(shape)
    return pl.BlockSpec(shape, lambda i: zeros, pipeline_mode=pl.Buffered(1))


def _mixer(x, mod, g_pre, w_in, ln_g, ln_b, w_s, bst, lam_re, lam_im, bmat, cboth, gmat,
           d_skip, b_glu_rows, h_sample, w_out):
    n_seq, seq, _ = x.shape
    assert 2 * n_seq == N_STREAMS and seq % CHUNK == 0
    rows = n_seq * CHUNK
    n_steps = seq // CHUNK
    wout_rows = w_out.shape[0] // n_steps
    assert wout_rows * n_steps == w_out.shape[0] and wout_rows % (2 * SUBLANES) == 0
    state_shape = (n_seq, N_SLABS, SLAB_STATES)
    n_sample = h_sample.shape[0]
    n_proj = D_IN // SAMPLE_PROJ_COLS
    assert n_proj <= n_steps
    consts = (mod, g_pre, w_in, ln_g, ln_b, w_s, bst, lam_re, lam_im, bmat, cboth, gmat,
              d_skip, b_glu_rows, h_sample)
    assert w_in.shape == (D_MODEL, D_IN) and D_MODEL % rows == 0 and D_IN % D_A == 0
    return pl.pallas_call(
        functools.partial(_mixer_kernel, n_seq=n_seq),
        grid=(n_steps,),
        in_specs=[pl.BlockSpec((n_seq, CHUNK, D_MODEL), lambda i: (0, i, 0))]
        + [pl.BlockSpec(memory_space=pl.ANY) if c is w_in else _const_spec(c.shape) for c in consts]
        + [pl.BlockSpec((wout_rows, w_out.shape[1]), lambda i: (i, 0))],
        out_specs=(
            pl.BlockSpec((rows, D_A + D_B), lambda i: (i, 0)),
            pl.BlockSpec(state_shape, lambda i: (0, 0, 0)),
            pl.BlockSpec(state_shape, lambda i: (0, 0, 0)),
            pl.BlockSpec((wout_rows, w_out.shape[1]), lambda i: (i, 0)),
            pl.BlockSpec((n_sample, SAMPLE_PROJ_COLS), lambda i: (0, jnp.minimum(i, n_proj - 1))),
        ),
        out_shape=(
            jax.ShapeDtypeStruct((n_steps * rows, D_A + D_B), BF16),
            jax.ShapeDtypeStruct(state_shape, F32),
            jax.ShapeDtypeStruct(state_shape, F32),
            jax.ShapeDtypeStruct(w_out.shape, BF16),
            jax.ShapeDtypeStruct((n_sample, D_IN), F32),
        ),
        scratch_shapes=[
            pltpu.VMEM((rows, D_MODEL), BF16),
            pltpu.VMEM((rows, D_A), F32),
            pltpu.VMEM((rows, D_A), F32),
            pltpu.VMEM((rows, D_A), F32),
            pltpu.VMEM((rows, D_A), BF16),
            pltpu.VMEM((rows, D_B), BF16),
            pltpu.VMEM((2 * SLAB_TILES, N_STREAMS * STREAM_PITCH, LANES), F32),
            pltpu.VMEM((2, N_STREAMS * CHUNK, LANES), F32),
            pltpu.VMEM((HALF_SLABS, N_STREAMS, 2 * SLAB_STATES), F32),
            pltpu.VMEM((D_MODEL, D_IN), BF16),
            pltpu.SemaphoreType.DMA((W_IN_RING,)),
        ],
        compiler_params=pltpu.CompilerParams(
            dimension_semantics=("arbitrary",), vmem_limit_bytes=VMEM_LIMIT),
        name="mixer",
    )(x, *consts, w_out)


def _outproj_kernel(oin_ref, x_ref, mod_ref, gpost_ref, w_out_ref, y_ref, *, n_seq):
    o = _dot(oin_ref[...], w_out_ref[...])
    r = o * _rms_scale(o)
    for b in range(n_seq):
        gain = gpost_ref[...] * mod_ref[b:b + 1, 2 * D_MODEL:3 * D_MODEL]
        y_ref[b] = x_ref[b] + r[b * CHUNK:(b + 1) * CHUNK, :] * gain


def _outproj(oin, x, mod, g_post, w_out_bf):
    n_seq, seq, _ = x.shape
    rows = n_seq * CHUNK
    return pl.pallas_call(
        functools.partial(_outproj_kernel, n_seq=n_seq),
        grid=(seq // CHUNK,),
        in_specs=[
            pl.BlockSpec((rows, D_A + D_B), lambda i: (i, 0)),
            pl.BlockSpec((n_seq, CHUNK, D_MODEL), lambda i: (0, i, 0)),
            _const_spec(mod.shape), _const_spec(g_post.shape), _const_spec(w_out_bf.shape),
        ],
        out_specs=pl.BlockSpec((n_seq, CHUNK, D_MODEL), lambda i: (0, i, 0)),
        out_shape=jax.ShapeDtypeStruct(x.shape, F32),
        compiler_params=pltpu.CompilerParams(
            dimension_semantics=("arbitrary",), vmem_limit_bytes=VMEM_LIMIT),
        name="outproj",
    )(oin, x, mod, g_post, w_out_bf)


def _sample_kernel(x2_ref, mod_ref, proj_ref, lng_ref, lnb_ref, ws_ref, bst_ref,
                   lam_re_ref, lam_im_ref, bmat_ref, cboth_ref, gmat_ref, dskip_ref, bglu_ref,
                   h0t_re_ref, h0t_im_ref, gpost_ref, wa_ref, wb_ref,
                   y2_ref, v2_ref, hst_re_ref, hst_im_ref, oin_a_sc, o_sc):
    j = pl.program_id(0)
    n = proj_ref.shape[0]
    x_tiles = D_MODEL // LANES
    v_tiles = D_A // LANES
    w_u, w_v, w_z, w_xb, w_zb = 0, D_A, 2 * D_A, 3 * D_A, 3 * D_A + D_B

    @pl.when(j == 0)
    def _():
        v = _layernorm(proj_ref[:, w_v:w_v + D_A], lng_ref[...], lnb_ref[...])
        for q in range(v_tiles):
            v2_ref[pl.ds(q, n, stride=v_tiles), :] = v[:, q * LANES:(q + 1) * LANES]
        for hd in range(N_HEADS_A):
            cols = slice(hd * HEAD_A, (hd + 1) * HEAD_A)
            mix = ws_ref[hd, 0:1, 0:1] * v[:, cols] + bst_ref[0:1, hd:hd + 1]
            u = proj_ref[:, w_u + hd * HEAD_A:w_u + (hd + 1) * HEAD_A]
            z = proj_ref[:, w_z + hd * HEAD_A:w_z + (hd + 1) * HEAD_A]
            oin_a_sc[:, cols] = (u * mix * _silu(z)).astype(BF16)
        o_sc[...] = jnp.zeros_like(o_sc)

    @pl.when(j < N_SLABS)
    def _():
        cols = pl.ds(pl.multiple_of(j * LANES, LANES), LANES)
        xb = proj_ref[:, pl.ds(pl.multiple_of(w_xb + j * LANES, LANES), LANES)]
        zb = proj_ref[:, pl.ds(pl.multiple_of(w_zb + j * LANES, LANES), LANES)]
        bu = _dot(xb.astype(BF16), bmat_ref[0])
        lr = lam_re_ref[pl.ds(j, 1), :]
        li = lam_im_ref[pl.ds(j, 1), :]
        h0r = h0t_re_ref[...].T
        h0i = h0t_im_ref[...].T
        hr = lr * h0r - li * h0i + bu[:, 0:SLAB_STATES]
        hi = lr * h0i + li * h0r + bu[:, SLAB_STATES:2 * SLAB_STATES]
        hst_re_ref[...] = hr.T
        hst_im_ref[...] = hi.T
        half = j // HALF_SLABS
        c_j = cboth_ref[0, :, pl.ds(pl.multiple_of(half * LANES, LANES), LANES)]
        y = _dot(jnp.concatenate([hr, hi], axis=1).astype(BF16), c_j)
        y = y + dskip_ref[:, cols] * xb
        g = _dot(y.astype(BF16), gmat_ref[0])
        val = g[:, 0:LANES] + bglu_ref[0:1, cols]
        gt = g[:, LANES:2 * LANES] + bglu_ref[1:2, cols]
        oin_b = (val * jax.nn.sigmoid(gt) * _silu(zb)).astype(BF16)
        o_sc[...] += _dot(oin_a_sc[:, cols], wa_ref[...]) + _dot(oin_b, wb_ref[...])

    @pl.when(j == N_SLABS)
    def _():
        gate = mod_ref[:, 2 * D_MODEL:3 * D_MODEL]
        r = gate * _rms(o_sc[...], gpost_ref[...])
        for q in range(x_tiles):
            cols = slice(q * LANES, (q + 1) * LANES)
            y2_ref[pl.ds(q, n, stride=x_tiles), :] = x2_ref[pl.ds(q, n, stride=x_tiles), :] + r[:, cols]


def _sample(x2, mod, proj, ln_g, ln_b, w_s, bst, lam_re, lam_im, bmat, cboth, gmat,
            d_skip, b_glu_rows, h0t_re, h0t_im, g_post, w_out_bf):
    n = proj.shape[0]
    n_state = N_GROUPS_B * P_STATE
    assert N_HEADS_A == N_SLABS and HEAD_A == LANES
    slab = lambda j: jnp.minimum(j, N_SLABS - 1)
    per_slab = {
        id(bmat): pl.BlockSpec((1,) + bmat.shape[1:], lambda j: (slab(j), 0, 0)),
        id(cboth): pl.BlockSpec((1,) + cboth.shape[1:], lambda j: (slab(j) % HALF_SLABS, 0, 0)),
        id(gmat): pl.BlockSpec((1,) + gmat.shape[1:], lambda j: (slab(j), 0, 0)),
        id(h0t_re): pl.BlockSpec((SLAB_STATES, n), lambda j: (slab(j), 0)),
        id(h0t_im): pl.BlockSpec((SLAB_STATES, n), lambda j: (slab(j), 0)),
    }
    operands = (x2, mod, proj, ln_g, ln_b, w_s, bst, lam_re, lam_im, bmat, cboth, gmat,
                d_skip, b_glu_rows, h0t_re, h0t_im, g_post)
    in_specs = [per_slab.get(id(a), _const_spec(a.shape)) for a in operands] + [
        pl.BlockSpec((LANES, D_MODEL), lambda j: (slab(j), 0)),
        pl.BlockSpec((LANES, D_MODEL), lambda j: (N_HEADS_A + slab(j), 0)),
    ]
    whole = lambda shp: pl.BlockSpec(shp, lambda j: (0, 0))
    state_spec = pl.BlockSpec((SLAB_STATES, n), lambda j: (slab(j), 0))
    out_shapes = ((n * D_MODEL // LANES, LANES), (n * D_A // LANES, LANES), (n_state, n), (n_state, n))
    return pl.pallas_call(
        _sample_kernel,
        grid=(N_SLABS + 1,),
        in_specs=in_specs,
        out_specs=(whole(out_shapes[0]), whole(out_shapes[1]), state_spec, state_spec),
        out_shape=tuple(jax.ShapeDtypeStruct(shp, F32) for shp in out_shapes),
        scratch_shapes=[pltpu.VMEM((n, D_A), BF16),
                        pltpu.VMEM((n, D_MODEL), F32)],
        compiler_params=pltpu.CompilerParams(
            dimension_semantics=("arbitrary",), vmem_limit_bytes=VMEM_LIMIT),
        name="sample",
    )(*operands, w_out_bf, w_out_bf)


def kernel(x_prompt, x_sample, c_prompt, c_sample, state_b_re, state_b_im, w_c, b_c, g_pre, w_in,
           ln_v_g, ln_v_b, w_s, b_s, a_re, a_im, log_dt, b_re, b_im, c_re, c_im, d_skip, w_glu, b_glu,
           w_out, g_post):
    n_p = x_prompt.shape[0]
    n_s = x_sample.shape[0]
    assert x_sample.shape[1] == 1 and n_s % SUBLANES == 0

    row = lambda v: v.reshape(1, -1)
    x2_sample = x_sample.reshape(n_s * D_MODEL // LANES, LANES)
    prep_args, prep_out_shape = _s5_prep_operands(a_re, a_im, log_dt, b_re, b_im, c_re, c_im, w_glu, b_glu, b_s)
    (mod_s, mod_p, h_sample, bmat, cboth, gmat, lam_re, lam_im, b_glu_rows, bst) = _adaln(
        c_sample, c_prompt, w_c, b_c, x2_sample, row(g_pre), prep_args, prep_out_shape)

    s5 = (w_s, bst, lam_re, lam_im, bmat, cboth, gmat, row(d_skip), b_glu_rows)
    oin, st_re, st_im, w_out_bf, proj_s = _mixer(
        x_prompt, mod_p, row(g_pre), w_in, row(ln_v_g), row(ln_v_b), *s5, h_sample, w_out)
    y_prompt = _outproj(oin, x_prompt, mod_p, row(g_post), w_out_bf)

    to_rows = lambda st: st.transpose(1, 2, 0).reshape(N_GROUPS_B * P_STATE, n_s)
    from_rows = lambda st: st.reshape(N_GROUPS_B, P_STATE, n_s).transpose(2, 0, 1)
    y_s, v_s, hs_re, hs_im = _sample(
        x2_sample, mod_s, proj_s, row(ln_v_g), row(ln_v_b), *s5,
        to_rows(state_b_re), to_rows(state_b_im), row(g_post), w_out_bf)

    return (y_prompt, y_s.reshape(n_s, 1, D_MODEL), v_s.reshape(n_s, 1, D_A),
            st_re.reshape(n_p, N_GROUPS_B, P_STATE), st_im.reshape(n_p, N_GROUPS_B, P_STATE),
            from_rows(hs_re), from_rows(hs_im))
```

```python
import functools

import jax
import jax.numpy as jnp
from jax import lax
from jax.experimental import pallas as pl
from jax.experimental.pallas import tpu as pltpu

F32 = jnp.float32
BF16 = jnp.bfloat16

EPS = 1e-6
D_MODEL = 2048
D_A = 1024
D_B = 1024
D_IN = 3 * D_A + 2 * D_B
CHUNK = 128
HEAD_A = 128
N_HEADS_A = D_A // HEAD_A
GROUP_B = 16
N_GROUPS_B = D_B // GROUP_B
P_STATE = 64

LANES = 128
SUBLANES = 8
SLAB_GROUPS = LANES // GROUP_B
N_SLABS = N_GROUPS_B // SLAB_GROUPS
SLAB_STATES = SLAB_GROUPS * P_STATE
SLAB_TILES = SLAB_STATES // LANES
N_STREAMS = SUBLANES
HALF_SLABS = N_SLABS // 2
STREAM_PITCH = CHUNK + SUBLANES
SCAN_PIECES = 4
PIECE = CHUNK // SCAN_PIECES
V7X_VMEM_BYTES = 64 * 1024 * 1024
VMEM_LIMIT = V7X_VMEM_BYTES - 1024 * 1024
SAMPLE_PROJ_COLS = 512
W_IN_RING = 6


def _silu(x):
    return x * jax.nn.sigmoid(x)


def _rms_scale(x):
    return lax.rsqrt(jnp.mean(x * x, axis=-1, keepdims=True) + EPS)


def _rms(x, g):
    return x * _rms_scale(x) * g


def _layernorm(x, g, b):
    mu = jnp.mean(x, axis=-1, keepdims=True)
    xc = x - mu
    return xc * lax.rsqrt(jnp.mean(xc * xc, axis=-1, keepdims=True) + EPS) * g + b


def _dot(a, b):
    return jnp.dot(a, b, preferred_element_type=F32)


def _discretize(a_re, a_im, log_dt):
    dt = jnp.exp(log_dt)
    mag = jnp.exp(dt * a_re)
    abar_re = mag * jnp.cos(dt * a_im)
    abar_im = mag * jnp.sin(dt * a_im)
    return abar_re, abar_im


def _nt_dot(a, b):
    return lax.dot_general(a, b, (((1,), (1,)), ((), ())), preferred_element_type=F32)


def _bf16_terms(x):
    hi = x.astype(BF16)
    r1 = x - hi.astype(F32)
    mid = r1.astype(BF16)
    lo = (r1 - mid.astype(F32)).astype(BF16)
    return hi, mid, lo


def _prep_kernel(a8_ref, bt_re_ref, bt_im_ref, c_re_ref, c_im_ref, wglu_ref, bglu_t_ref, bs_ref,
                 bmat_ref, cboth_ref, gmat_ref, lam_re_ref, lam_im_ref, bglu_row_ref, bst_ref):
    a_re = a8_ref[0:N_SLABS, :]
    a_im = a8_ref[N_SLABS:2 * N_SLABS, :]
    abar_re, abar_im = _discretize(a_re, a_im, a8_ref[2 * N_SLABS:3 * N_SLABS, :])
    lam_re_ref[...] = abar_re
    lam_im_ref[...] = abar_im
    num_re = abar_re - 1.0
    num_im = abar_im
    den = a_re * a_re + a_im * a_im
    coef_re = (num_re * a_re + num_im * a_im) / den
    coef_im = (num_im * a_re - num_re * a_im) / den

    def rep_matrix(k, n, period, offset=0):
        row = lax.broadcasted_iota(jnp.int32, (k, n), 0)
        col = lax.broadcasted_iota(jnp.int32, (k, n), 1)
        return jnp.where((col & (period - 1)) + offset == row, 1.0, 0.0).astype(BF16)

    def rep_rows(m, k, period):
        row = lax.broadcasted_iota(jnp.int32, (m, k), 0)
        col = lax.broadcasted_iota(jnp.int32, (m, k), 1)
        return jnp.where((row & (period - 1)) == col, 1.0, 0.0).astype(BF16)

    def block_mask(m, n, row_shift, col_shift):
        row = lax.broadcasted_iota(jnp.int32, (m, n), 0)
        col = lax.broadcasted_iota(jnp.int32, (m, n), 1)
        return (row >> row_shift) == (col >> col_shift)

    rep_state = rep_matrix(P_STATE, SLAB_STATES, P_STATE)
    rows_state = rep_rows(SLAB_STATES, P_STATE, P_STATE)
    rep_val = rep_matrix(2 * GROUP_B, LANES, GROUP_B)
    rep_gate = rep_matrix(2 * GROUP_B, LANES, GROUP_B, GROUP_B)
    mask_b = block_mask(LANES, SLAB_STATES, 4, 6)
    mask_c = block_mask(SLAB_STATES, LANES, 6, 4)
    mask_g = block_mask(LANES, LANES, 4, 4)

    def spread(x, rep):
        return sum(_dot(term, rep) for term in _bf16_terms(x))

    for j in range(N_SLABS):
        crows = slice(j * LANES, (j + 1) * LANES)
        bt_re = spread(bt_re_ref[crows, :], rep_state)
        bt_im = spread(bt_im_ref[crows, :], rep_state)
        cr = coef_re[j:j + 1, :]
        ci = coef_im[j:j + 1, :]
        bmat_ref[j, :, 0:SLAB_STATES] = jnp.where(mask_b, cr * bt_re - ci * bt_im, 0.0).astype(BF16)
        bmat_ref[j, :, SLAB_STATES:2 * SLAB_STATES] = jnp.where(
            mask_b, cr * bt_im + ci * bt_re, 0.0).astype(BF16)

        k, half = j % HALF_SLABS, j // HALF_SLABS
        ccols = slice(half * LANES, (half + 1) * LANES)
        ct_re = _nt_dot(rows_state, c_re_ref[crows, :].astype(BF16))
        ct_im = _nt_dot(rows_state, c_im_ref[crows, :].astype(BF16))
        cboth_ref[k, 0:SLAB_STATES, ccols] = jnp.where(mask_c, ct_re, 0.0).astype(BF16)
        cboth_ref[k, SLAB_STATES:2 * SLAB_STATES, ccols] = jnp.where(mask_c, -ct_im, 0.0).astype(BF16)

        w = wglu_ref[crows, :].astype(BF16)
        gmat_ref[j, :, 0:LANES] = jnp.where(mask_g, _dot(w, rep_val), 0.0).astype(BF16)
        gmat_ref[j, :, LANES:2 * LANES] = jnp.where(mask_g, _dot(w, rep_gate), 0.0).astype(BF16)

    row = lax.broadcasted_iota(jnp.int32, (2 * GROUP_B, D_B), 0)
    col = lax.broadcasted_iota(jnp.int32, (2 * GROUP_B, D_B), 1)
    by_group = spread(bglu_t_ref[...], jnp.where(block_mask(N_GROUPS_B, D_B, 0, 4), 1.0, 0.0).astype(BF16))
    for i in range(2):
        mine = row == (col & (GROUP_B - 1)) + i * GROUP_B
        bglu_row_ref[i:i + 1, :] = jnp.sum(jnp.where(mine, by_group, 0.0), axis=0, keepdims=True)

    eye = rep_rows(CHUNK, CHUNK, CHUNK)
    bst_ref[...] = sum(_nt_dot(eye, term) for term in _bf16_terms(bs_ref[...]))


def _s5_prep_operands(a_re, a_im, log_dt, b_re, b_im, c_re, c_im, w_glu, b_glu, b_s):
    g, p, c = N_GROUPS_B, P_STATE, GROUP_B
    a8 = jnp.stack([a_re, a_im, jnp.broadcast_to(log_dt[:, None], (g, p))]).reshape(3 * N_SLABS, SLAB_STATES)
    args = (a8, b_re.transpose(0, 2, 1).reshape(g * c, p), b_im.transpose(0, 2, 1).reshape(g * c, p),
            c_re.reshape(g * c, p), c_im.reshape(g * c, p), w_glu.reshape(g * c, 2 * c), b_glu.T, b_s)
    out_shape = (
        jax.ShapeDtypeStruct((N_SLABS, LANES, 2 * SLAB_STATES), BF16),
        jax.ShapeDtypeStruct((HALF_SLABS, 2 * SLAB_STATES, 2 * LANES), BF16),
        jax.ShapeDtypeStruct((N_SLABS, LANES, 2 * LANES), BF16),
        jax.ShapeDtypeStruct((N_SLABS, SLAB_STATES), F32),
        jax.ShapeDtypeStruct((N_SLABS, SLAB_STATES), F32),
        jax.ShapeDtypeStruct((2, D_B), F32),
        jax.ShapeDtypeStruct((CHUNK, N_HEADS_A), F32),
    )
    return args, out_shape


ADALN_ROWS = 512
ADALN_PROMPT_ROWS = 2 * SUBLANES


N_PREP_IN, N_PREP_OUT = 8, 7


def _adaln_kernel(cs_ref, cp_ref, w_ref, b_ref, x2_ref, gpre_ref, *refs):
    prep_in = refs[:N_PREP_IN]
    os_ref, op_ref, hs_ref = refs[N_PREP_IN:N_PREP_IN + 3]
    prep_out = refs[N_PREP_IN + 3:]
    assert len(prep_out) == N_PREP_OUT
    k = pl.program_id(0)

    @pl.when(k == 0)
    def _():
        os_ref[...] = jnp.broadcast_to(b_ref[...], os_ref.shape)
        op_ref[...] = jnp.broadcast_to(b_ref[...], op_ref.shape)
        _prep_kernel(*prep_in, *prep_out)

    cols = pl.ds(pl.multiple_of(k * ADALN_ROWS, ADALN_ROWS), ADALN_ROWS)
    w = w_ref[...].astype(BF16)
    os_ref[...] += _dot(_silu(cs_ref[:, cols]).astype(BF16), w)
    n_p = cp_ref.shape[0]
    cp = jnp.concatenate([cp_ref[:, cols], jnp.zeros((ADALN_PROMPT_ROWS - n_p, ADALN_ROWS), F32)], axis=0)
    op_ref[...] += _dot(_silu(cp).astype(BF16), w)

    @pl.when(k == pl.num_programs(0) - 1)
    def _():
        x_tiles = D_MODEL // LANES
        n = hs_ref.shape[0]
        x = jnp.concatenate([x2_ref[pl.ds(q, n, stride=x_tiles), :] for q in range(x_tiles)], axis=1)
        shift = os_ref[:, 0:D_MODEL]
        scale = os_ref[:, D_MODEL:2 * D_MODEL]
        hs_ref[...] = (_rms(x, gpre_ref[...]) * (1.0 + scale) + shift).astype(BF16)


def _adaln(c_sample, c_prompt, w_c, b_c, x2_sample, g_pre, prep_args, prep_out_shape):
    n_s, n_p = c_sample.shape[0], c_prompt.shape[0]
    assert n_p <= ADALN_PROMPT_ROWS and D_MODEL % ADALN_ROWS == 0
    assert len(prep_args) == N_PREP_IN and len(prep_out_shape) == N_PREP_OUT
    whole = lambda a: pl.BlockSpec(a.shape, lambda k, nd=len(a.shape): (0,) * nd)
    return pl.pallas_call(
        _adaln_kernel,
        grid=(D_MODEL // ADALN_ROWS,),
        in_specs=[
            pl.BlockSpec((n_s, D_MODEL), lambda k: (0, 0)),
            pl.BlockSpec((n_p, D_MODEL), lambda k: (0, 0)),
            pl.BlockSpec((ADALN_ROWS, 3 * D_MODEL), lambda k: (k, 0)),
            pl.BlockSpec((1, 3 * D_MODEL), lambda k: (0, 0)),
            pl.BlockSpec(x2_sample.shape, lambda k: (0, 0)),
            pl.BlockSpec(g_pre.shape, lambda k: (0, 0)),
        ] + [whole(a) for a in prep_args],
        out_specs=(pl.BlockSpec((n_s, 3 * D_MODEL), lambda k: (0, 0)),
                   pl.BlockSpec((ADALN_PROMPT_ROWS, 3 * D_MODEL), lambda k: (0, 0)),
                   pl.BlockSpec((n_s, D_MODEL), lambda k: (0, 0))) + tuple(whole(a) for a in prep_out_shape),
        out_shape=(jax.ShapeDtypeStruct((n_s, 3 * D_MODEL), F32),
                   jax.ShapeDtypeStruct((ADALN_PROMPT_ROWS, 3 * D_MODEL), F32),
                   jax.ShapeDtypeStruct((n_s, D_MODEL), BF16)) + tuple(prep_out_shape),
        compiler_params=pltpu.CompilerParams(
            dimension_semantics=("arbitrary",), vmem_limit_bytes=VMEM_LIMIT),
        name="adaln",
    )(c_sample, c_prompt, w_c, b_c.reshape(1, -1), x2_sample, g_pre, *prep_args)


def _mixer_kernel(x_ref, mod_ref, gpre_ref, w_in_hbm, lng_ref, lnb_ref, ws_ref, bst_ref,
                  lam_re_ref, lam_im_ref, bmat_ref, cboth_ref, gmat_ref, dskip_ref, bglu_ref, hs_ref, wout_ref,
                  oin_ref, st_re_ref, st_im_ref, wout_bf_ref, ps_ref,
                  h_sc, a_sc, b_sc, c_sc, v_sc, xb_sc, bu_sc, y2_sc, st_sc, w_in_ref, w_sem, *, n_seq):
    step = pl.program_id(0)

    @pl.when(step == 0)
    def _():
        st_sc[...] = jnp.zeros_like(st_sc)

        t_rows, t_cols = a_sc.shape[0] // 2, a_sc.shape[1]
        stage = [buf.at[pl.ds(i * t_rows, t_rows), :] for buf in (a_sc, b_sc, c_sc) for i in range(2)]
        assert len(stage) == W_IN_RING
        tiles = [(r, c) for c in range(D_IN // t_cols) for r in range(D_MODEL // t_rows)]

        def tile_copy(n):
            r, c = tiles[n]
            src = w_in_hbm.at[pl.ds(r * t_rows, t_rows), pl.ds(c * t_cols, t_cols)]
            return pltpu.make_async_copy(src, stage[n % len(stage)], w_sem.at[n % len(stage)])

        for n in range(len(stage)):
            tile_copy(n).start()
        for n, (r, c) in enumerate(tiles):
            tile_copy(n).wait()
            w_in_ref[r * t_rows:(r + 1) * t_rows, c * t_cols:(c + 1) * t_cols] = (
                stage[n % len(stage)][...].astype(BF16))
            if n + len(stage) < len(tiles):
                tile_copy(n + len(stage)).start()

    wout_bf_ref[...] = wout_ref[...].astype(BF16)

    @pl.when(step < D_IN // SAMPLE_PROJ_COLS)
    def _():
        cols = pl.ds(pl.multiple_of(step * SAMPLE_PROJ_COLS, SAMPLE_PROJ_COLS), SAMPLE_PROJ_COLS)
        ps_ref[...] = _dot(hs_ref[...], w_in_ref[:, cols])

    w_u, w_v, w_z, w_xb, w_zb = 0, D_A, 2 * D_A, 3 * D_A, 3 * D_A + D_B

    for b in range(n_seq):
        x = x_ref[b]
        shift = mod_ref[b:b + 1, 0:D_MODEL]
        gain = gpre_ref[...] * (1.0 + mod_ref[b:b + 1, D_MODEL:2 * D_MODEL])
        rws = slice(b * CHUNK, (b + 1) * CHUNK)
        h_sc[rws, :] = (x * _rms_scale(x) * gain + shift).astype(BF16)
        b_sc[rws, :] = _dot(h_sc[rws, :], w_in_ref[:, w_xb:w_xb + D_B])
        xb_sc[rws, :] = b_sc[rws, :].astype(BF16)

    h = h_sc[...]

    tril = (lax.broadcasted_iota(jnp.int32, (CHUNK, CHUNK), 0)
            >= lax.broadcasted_iota(jnp.int32, (CHUNK, CHUNK), 1))
    n_fill = HALF_SLABS * SCAN_PIECES // 4
    fill_cols = D_A // n_fill

    def fill_u(q):
        cols = slice(q * fill_cols, (q + 1) * fill_cols)
        a_sc[:, cols] = _dot(h, w_in_ref[:, w_u + q * fill_cols:w_u + (q + 1) * fill_cols])

    def fill_z(q):
        cols = slice(q * fill_cols, (q + 1) * fill_cols)
        z = _dot(h, w_in_ref[:, w_z + q * fill_cols:w_z + (q + 1) * fill_cols])
        a_sc[:, cols] = a_sc[:, cols] * _silu(z)

    def fill_mix(q):
        heads_per_fill = N_HEADS_A // n_fill
        for hd in range(q * heads_per_fill, (q + 1) * heads_per_fill):
            cols = slice(hd * HEAD_A, (hd + 1) * HEAD_A)
            w_t = jnp.where(tril, ws_ref[hd], 0.0).astype(BF16)
            v_h = jnp.concatenate([v_sc[b * CHUNK:(b + 1) * CHUNK, cols] for b in range(n_seq)], axis=1)
            mix = _dot(w_t, v_h) + bst_ref[:, hd:hd + 1]
            for b in range(n_seq):
                rws = slice(b * CHUNK, (b + 1) * CHUNK)
                oin_ref[rws, cols] = (a_sc[rws, cols] * mix[:, b * HEAD_A:(b + 1) * HEAD_A]).astype(BF16)

    def fill_zb(q):
        cols = slice(q * fill_cols, (q + 1) * fill_cols)
        a_sc[:, cols] = _silu(_dot(h, w_in_ref[:, w_zb + q * fill_cols:w_zb + (q + 1) * fill_cols]))

    fillers = [functools.partial(f, q) for f, q in (
        (fill_u, 0), (fill_z, 0), (fill_u, 1), (fill_mix, 0), (fill_z, 1), (fill_u, 2), (fill_z, 2), (fill_mix, 1),
        (fill_u, 3), (fill_z, 3), (fill_zb, 0), (fill_mix, 2), (fill_zb, 1), (fill_zb, 2), (fill_mix, 3), (fill_zb, 3))]
    assert len(fillers) == HALF_SLABS * SCAN_PIECES

    first_half = lax.broadcasted_iota(jnp.int32, (N_STREAMS, LANES), 0) < n_seq

    def bu_piece(k, piece):
        t0 = piece * PIECE
        for half in range(2):
            j = half * HALF_SLABS + k
            lhs = jnp.concatenate(
                [xb_sc[b * CHUNK + t0:b * CHUNK + t0 + PIECE, j * LANES:(j + 1) * LANES]
                 for b in range(n_seq)], axis=0)
            bu = _dot(lhs, bmat_ref[j])
            for b in range(n_seq):
                r0 = (half * n_seq + b) * STREAM_PITCH + t0
                for l in range(2 * SLAB_TILES):
                    bu_sc[l, r0:r0 + PIECE, :] = bu[b * PIECE:(b + 1) * PIECE, l * LANES:(l + 1) * LANES]

    def scan_piece(k, piece, hr, hi, lam_r, lam_i):
        rows = []
        for t in range(piece * PIECE, (piece + 1) * PIECE):
            new_r, new_i = [], []
            for l in range(SLAB_TILES):
                br = bu_sc[l, pl.ds(t, N_STREAMS, stride=STREAM_PITCH), :]
                bi = bu_sc[SLAB_TILES + l, pl.ds(t, N_STREAMS, stride=STREAM_PITCH), :]
                new_r.append(lam_r[l] * hr[l] - lam_i[l] * hi[l] + br)
                new_i.append(lam_r[l] * hi[l] + lam_i[l] * hr[l] + bi)
            hr, hi = new_r, new_i
            rows.append(jnp.concatenate(hr + hi, axis=1))
        return hr, hi, jnp.concatenate(rows, axis=0)

    def c_piece(k, piece, hh):
        t0 = piece * PIECE
        rows = slice(t0 * N_STREAMS, (t0 + PIECE) * N_STREAMS)
        y2 = _dot(hh.astype(BF16), cboth_ref[k])
        y2_sc[0, rows, :] = y2[:, 0:LANES]
        y2_sc[1, rows, :] = y2[:, LANES:2 * LANES]
        for half in range(2):
            j = half * HALF_SLABS + k
            for b in range(n_seq):
                s = half * n_seq + b
                c_sc[b * CHUNK + t0:b * CHUNK + t0 + PIECE, j * LANES:(j + 1) * LANES] = (
                    y2_sc[half, pl.ds(t0 * N_STREAMS + s, PIECE, stride=N_STREAMS), :])

    for piece in range(SCAN_PIECES):
        bu_piece(0, piece)
    c_sc[...] = _dot(h, w_in_ref[:, w_v:w_v + D_A])
    v_sc[...] = _layernorm(c_sc[...], lng_ref[...], lnb_ref[...]).astype(BF16)
    for k in range(HALF_SLABS):
        tiles = [slice(l * LANES, (l + 1) * LANES) for l in range(SLAB_TILES)]
        lam_r = [jnp.where(first_half, lam_re_ref[k:k + 1, t], lam_re_ref[HALF_SLABS + k:HALF_SLABS + k + 1, t])
                 for t in tiles]
        lam_i = [jnp.where(first_half, lam_im_ref[k:k + 1, t], lam_im_ref[HALF_SLABS + k:HALF_SLABS + k + 1, t])
                 for t in tiles]
        hr = [st_sc[k, :, l * LANES:(l + 1) * LANES] for l in range(SLAB_TILES)]
        hi = [st_sc[k, :, (SLAB_TILES + l) * LANES:(SLAB_TILES + l + 1) * LANES] for l in range(SLAB_TILES)]
        for piece in range(SCAN_PIECES):
            hr, hi, hh = scan_piece(k, piece, hr, hi, lam_r, lam_i)
            c_piece(k, piece, hh)
            if k + 1 < HALF_SLABS:
                bu_piece(k + 1, piece)
            fillers[k * SCAN_PIECES + piece]()
        st_sc[k] = jnp.concatenate(hr + hi, axis=1)

    for k in range(HALF_SLABS):
        for half in range(2):
            j = half * HALF_SLABS + k
            for b in range(n_seq):
                s = half * n_seq + b
                st_re_ref[b, j:j + 1, :] = st_sc[k, s:s + 1, 0:SLAB_STATES]
                st_im_ref[b, j:j + 1, :] = st_sc[k, s:s + 1, SLAB_STATES:2 * SLAB_STATES]

    for j in range(N_SLABS):
        cols = slice(j * LANES, (j + 1) * LANES)
        y = c_sc[:, cols] + dskip_ref[:, cols] * b_sc[:, cols]
        g = _dot(y.astype(BF16), gmat_ref[j])
        val = g[:, 0:LANES] + bglu_ref[0:1, cols]
        gate = g[:, LANES:2 * LANES] + bglu_ref[1:2, cols]
        oin_ref[:, D_A + j * LANES:D_A + (j + 1) * LANES] = (
            val * jax.nn.sigmoid(gate) * a_sc[:, cols]).astype(BF16)


def _const_spec(shape):
    zeros = (0,) * len(shape)
    return pl.BlockSpec(shape, lambda i: zeros, pipeline_mode=pl.Buffered(1))


def _mixer(x, mod, g_pre, w_in, ln_g, ln_b, w_s, bst, lam_re, lam_im, bmat, cboth, gmat,
           d_skip, b_glu_rows, h_sample, w_out):
    n_seq, seq, _ = x.shape
    assert 2 * n_seq == N_STREAMS and seq % CHUNK == 0
    rows = n_seq * CHUNK
    n_steps = seq // CHUNK
    wout_rows = w_out.shape[0] // n_steps
    assert wout_rows * n_steps == w_out.shape[0] and wout_rows % (2 * SUBLANES) == 0
    state_shape = (n_seq, N_SLABS, SLAB_STATES)
    n_sample = h_sample.shape[0]
    n_proj = D_IN // SAMPLE_PROJ_COLS
    assert n_proj <= n_steps
    consts = (mod, g_pre, w_in, ln_g, ln_b, w_s, bst, lam_re, lam_im, bmat, cboth, gmat,
              d_skip, b_glu_rows, h_sample)
    assert w_in.shape == (D_MODEL, D_IN) and D_MODEL % rows == 0 and D_IN % D_A == 0
    return pl.pallas_call(
        functools.partial(_mixer_kernel, n_seq=n_seq),
        grid=(n_steps,),
        in_specs=[pl.BlockSpec((n_seq, CHUNK, D_MODEL), lambda i: (0, i, 0))]
        + [pl.BlockSpec(memory_space=pl.ANY) if c is w_in else _const_spec(c.shape) for c in consts]
        + [pl.BlockSpec((wout_rows, w_out.shape[1]), lambda i: (i, 0))],
        out_specs=(
            pl.BlockSpec((rows, D_A + D_B), lambda i: (i, 0)),
            pl.BlockSpec(state_shape, lambda i: (0, 0, 0)),
            pl.BlockSpec(state_shape, lambda i: (0, 0, 0)),
            pl.BlockSpec((wout_rows, w_out.shape[1]), lambda i: (i, 0)),
            pl.BlockSpec((n_sample, SAMPLE_PROJ_COLS), lambda i: (0, jnp.minimum(i, n_proj - 1))),
        ),
        out_shape=(
            jax.ShapeDtypeStruct((n_steps * rows, D_A + D_B), BF16),
            jax.ShapeDtypeStruct(state_shape, F32),
            jax.ShapeDtypeStruct(state_shape, F32),
            jax.ShapeDtypeStruct(w_out.shape, BF16),
            jax.ShapeDtypeStruct((n_sample, D_IN), F32),
        ),
        scratch_shapes=[
            pltpu.VMEM((rows, D_MODEL), BF16),
            pltpu.VMEM((rows, D_A), F32),
            pltpu.VMEM((rows, D_A), F32),
            pltpu.VMEM((rows, D_A), F32),
            pltpu.VMEM((rows, D_A), BF16),
            pltpu.VMEM((rows, D_B), BF16),
            pltpu.VMEM((2 * SLAB_TILES, N_STREAMS * STREAM_PITCH, LANES), F32),
            pltpu.VMEM((2, N_STREAMS * CHUNK, LANES), F32),
            pltpu.VMEM((HALF_SLABS, N_STREAMS, 2 * SLAB_STATES), F32),
            pltpu.VMEM((D_MODEL, D_IN), BF16),
            pltpu.SemaphoreType.DMA((W_IN_RING,)),
        ],
        compiler_params=pltpu.CompilerParams(
            dimension_semantics=("arbitrary",), vmem_limit_bytes=VMEM_LIMIT),
        name="mixer",
    )(x, *consts, w_out)


OUTPROJ_COLS = 2 * LANES


def _outproj_kernel(oin_ref, x_ref, mod_ref, gpost_ref, w_out_ref, y_ref, o_sc, *, n_seq):
    oin = oin_ref[...]
    sq_sum = jnp.zeros((oin.shape[0], LANES), F32)
    for c in range(D_MODEL // OUTPROJ_COLS):
        cols = slice(c * OUTPROJ_COLS, (c + 1) * OUTPROJ_COLS)
        o = _dot(oin, w_out_ref[:, cols])
        o_sc[:, cols] = o
        sq = o * o
        sq_sum = sq_sum + sq[:, 0:LANES] + sq[:, LANES:2 * LANES]
    rs = lax.rsqrt(jnp.sum(sq_sum, axis=-1, keepdims=True) * (1.0 / D_MODEL) + EPS)
    for b in range(n_seq):
        rws = slice(b * CHUNK, (b + 1) * CHUNK)
        gain = gpost_ref[...] * mod_ref[b:b + 1, 2 * D_MODEL:3 * D_MODEL]
        y_ref[b] = x_ref[b] + o_sc[rws, :] * rs[rws, :] * gain


def _outproj(oin, x, mod, g_post, w_out_bf):
    n_seq, seq, _ = x.shape
    rows = n_seq * CHUNK
    return pl.pallas_call(
        functools.partial(_outproj_kernel, n_seq=n_seq),
        grid=(seq // CHUNK,),
        in_specs=[
            pl.BlockSpec((rows, D_A + D_B), lambda i: (i, 0)),
            pl.BlockSpec((n_seq, CHUNK, D_MODEL), lambda i: (0, i, 0)),
            _const_spec(mod.shape), _const_spec(g_post.shape), _const_spec(w_out_bf.shape),
        ],
        out_specs=pl.BlockSpec((n_seq, CHUNK, D_MODEL), lambda i: (0, i, 0)),
        out_shape=jax.ShapeDtypeStruct(x.shape, F32),
        scratch_shapes=[pltpu.VMEM((rows, D_MODEL), F32)],
        compiler_params=pltpu.CompilerParams(
            dimension_semantics=("arbitrary",), vmem_limit_bytes=VMEM_LIMIT),
        name="outproj",
    )(oin, x, mod, g_post, w_out_bf)


def _sample_kernel(x2_ref, mod_ref, proj_ref, lng_ref, lnb_ref, ws_ref, bst_ref,
                   lam_re_ref, lam_im_ref, bmat_ref, cboth_ref, gmat_ref, dskip_ref, bglu_ref,
                   h0t_re_ref, h0t_im_ref, gpost_ref, w_out_ref,
                   y2_ref, v2_ref, hst_re_ref, hst_im_ref, oin_sc):
    n = proj_ref.shape[0]
    x_tiles = D_MODEL // LANES
    v_tiles = D_A // LANES
    w_u, w_v, w_z, w_xb, w_zb = 0, D_A, 2 * D_A, 3 * D_A, 3 * D_A + D_B

    v = _layernorm(proj_ref[:, w_v:w_v + D_A], lng_ref[...], lnb_ref[...])
    for q in range(v_tiles):
        v2_ref[pl.ds(q, n, stride=v_tiles), :] = v[:, q * LANES:(q + 1) * LANES]
    for hd in range(N_HEADS_A):
        cols = slice(hd * HEAD_A, (hd + 1) * HEAD_A)
        mix = ws_ref[hd, 0:1, 0:1] * v[:, cols] + bst_ref[0:1, hd:hd + 1]
        u = proj_ref[:, w_u + hd * HEAD_A:w_u + (hd + 1) * HEAD_A]
        z = proj_ref[:, w_z + hd * HEAD_A:w_z + (hd + 1) * HEAD_A]
        oin_sc[:, cols] = (u * mix * _silu(z)).astype(BF16)

    for j in range(N_SLABS):
        cols = slice(j * LANES, (j + 1) * LANES)
        st = slice(j * SLAB_STATES, (j + 1) * SLAB_STATES)
        xb = proj_ref[:, w_xb + j * LANES:w_xb + (j + 1) * LANES]
        zb = proj_ref[:, w_zb + j * LANES:w_zb + (j + 1) * LANES]
        bu = _dot(xb.astype(BF16), bmat_ref[j])
        lr = lam_re_ref[j:j + 1, :]
        li = lam_im_ref[j:j + 1, :]
        h0r = h0t_re_ref[st, :].T
        h0i = h0t_im_ref[st, :].T
        hr = lr * h0r - li * h0i + bu[:, 0:SLAB_STATES]
        hi = lr * h0i + li * h0r + bu[:, SLAB_STATES:2 * SLAB_STATES]
        hst_re_ref[st, :] = hr.T
        hst_im_ref[st, :] = hi.T
        k, half = j % HALF_SLABS, j // HALF_SLABS
        c_j = cboth_ref[k, :, half * LANES:(half + 1) * LANES]
        y = _dot(jnp.concatenate([hr, hi], axis=1).astype(BF16), c_j)
        y = y + dskip_ref[:, cols] * xb
        g = _dot(y.astype(BF16), gmat_ref[j])
        val = g[:, 0:LANES] + bglu_ref[0:1, cols]
        gt = g[:, LANES:2 * LANES] + bglu_ref[1:2, cols]
        oin_sc[:, D_A + j * LANES:D_A + (j + 1) * LANES] = (
            val * jax.nn.sigmoid(gt) * _silu(zb)).astype(BF16)

    o = _dot(oin_sc[...], w_out_ref[...])
    gate = mod_ref[:, 2 * D_MODEL:3 * D_MODEL]
    r = gate * _rms(o, gpost_ref[...])
    for q in range(x_tiles):
        cols = slice(q * LANES, (q + 1) * LANES)
        y2_ref[pl.ds(q, n, stride=x_tiles), :] = x2_ref[pl.ds(q, n, stride=x_tiles), :] + r[:, cols]


def _sample(x2, mod, proj, ln_g, ln_b, w_s, bst, lam_re, lam_im, bmat, cboth, gmat,
            d_skip, b_glu_rows, h0t_re, h0t_im, g_post, w_out_bf):
    n = proj.shape[0]
    n_state = N_GROUPS_B * P_STATE
    out_shapes = ((n * D_MODEL // LANES, LANES), (n * D_A // LANES, LANES), (n_state, n), (n_state, n))
    return pl.pallas_call(
        _sample_kernel,
        out_shape=tuple(jax.ShapeDtypeStruct(shp, F32) for shp in out_shapes),
        scratch_shapes=[pltpu.VMEM((n, D_A + D_B), BF16)],
        compiler_params=pltpu.CompilerParams(vmem_limit_bytes=VMEM_LIMIT),
        name="sample",
    )(x2, mod, proj, ln_g, ln_b, w_s, bst, lam_re, lam_im, bmat, cboth, gmat,
      d_skip, b_glu_rows, h0t_re, h0t_im, g_post, w_out_bf)


def kernel(x_prompt, x_sample, c_prompt, c_sample, state_b_re, state_b_im, w_c, b_c, g_pre, w_in,
           ln_v_g, ln_v_b, w_s, b_s, a_re, a_im, log_dt, b_re, b_im, c_re, c_im, d_skip, w_glu, b_glu,
           w_out, g_post):
    n_p = x_prompt.shape[0]
    n_s = x_sample.shape[0]
    assert x_sample.shape[1] == 1 and n_s % SUBLANES == 0

    row = lambda v: v.reshape(1, -1)
    x2_sample = x_sample.reshape(n_s * D_MODEL // LANES, LANES)
    prep_args, prep_out_shape = _s5_prep_operands(a_re, a_im, log_dt, b_re, b_im, c_re, c_im, w_glu, b_glu, b_s)
    (mod_s, mod_p, h_sample, bmat, cboth, gmat, lam_re, lam_im, b_glu_rows, bst) = _adaln(
        c_sample, c_prompt, w_c, b_c, x2_sample, row(g_pre), prep_args, prep_out_shape)

    s5 = (w_s, bst, lam_re, lam_im, bmat, cboth, gmat, row(d_skip), b_glu_rows)
    oin, st_re, st_im, w_out_bf, proj_s = _mixer(
        x_prompt, mod_p, row(g_pre), w_in, row(ln_v_g), row(ln_v_b), *s5, h_sample, w_out)
    y_prompt = _outproj(oin, x_prompt, mod_p, row(g_post), w_out_bf)

    to_rows = lambda st: st.transpose(1, 2, 0).reshape(N_GROUPS_B * P_STATE, n_s)
    from_rows = lambda st: st.reshape(N_GROUPS_B, P_STATE, n_s).transpose(2, 0, 1)
    y_s, v_s, hs_re, hs_im = _sample(
        x2_sample, mod_s, proj_s, row(ln_v_g), row(ln_v_b), *s5,
        to_rows(state_b_re), to_rows(state_b_im), row(g_post), w_out_bf)

    return (y_prompt, y_s.reshape(n_s, 1, D_MODEL), v_s.reshape(n_s, 1, D_A),
            st_re.reshape(n_p, N_GROUPS_B, P_STATE), st_im.reshape(n_p, N_GROUPS_B, P_STATE),
            from_rows(hs_re), from_rows(hs_im))
```

```python
import functools

import jax
import jax.numpy as jnp
from jax import lax
from jax.experimental import pallas as pl
from jax.experimental.pallas import tpu as pltpu

F32 = jnp.float32
BF16 = jnp.bfloat16

EPS = 1e-6
D_MODEL = 2048
D_A = 1024
D_B = 1024
D_IN = 3 * D_A + 2 * D_B
CHUNK = 128
HEAD_A = 128
N_HEADS_A = D_A // HEAD_A
GROUP_B = 16
N_GROUPS_B = D_B // GROUP_B
P_STATE = 64

LANES = 128
SUBLANES = 8
SLAB_GROUPS = LANES // GROUP_B
N_SLABS = N_GROUPS_B // SLAB_GROUPS
SLAB_STATES = SLAB_GROUPS * P_STATE
SLAB_TILES = SLAB_STATES // LANES
N_STREAMS = SUBLANES
HALF_SLABS = N_SLABS // 2
STREAM_PITCH = CHUNK + SUBLANES
SCAN_PIECES = 4
PIECE = CHUNK // SCAN_PIECES
V7X_VMEM_BYTES = 64 * 1024 * 1024
VMEM_LIMIT = V7X_VMEM_BYTES - 1024 * 1024
SAMPLE_PROJ_COLS = 1024
W_IN_RING = 6


def _silu(x):
    return x * jax.nn.sigmoid(x)


def _rms_scale(x):
    return lax.rsqrt(jnp.mean(x * x, axis=-1, keepdims=True) + EPS)


def _rms(x, g):
    return x * _rms_scale(x) * g


def _layernorm(x, g, b):
    mu = jnp.mean(x, axis=-1, keepdims=True)
    xc = x - mu
    return xc * lax.rsqrt(jnp.mean(xc * xc, axis=-1, keepdims=True) + EPS) * g + b


def _dot(a, b):
    return jnp.dot(a, b, preferred_element_type=F32)


def _discretize(a_re, a_im, log_dt):
    dt = jnp.exp(log_dt)
    mag = jnp.exp(dt * a_re)
    abar_re = mag * jnp.cos(dt * a_im)
    abar_im = mag * jnp.sin(dt * a_im)
    return abar_re, abar_im


def _nt_dot(a, b):
    return lax.dot_general(a, b, (((1,), (1,)), ((), ())), preferred_element_type=F32)


def _bf16_terms(x):
    hi = x.astype(BF16)
    r1 = x - hi.astype(F32)
    mid = r1.astype(BF16)
    lo = (r1 - mid.astype(F32)).astype(BF16)
    return hi, mid, lo


def _prep_kernel(a8_ref, bt_re_ref, bt_im_ref, c_re_ref, c_im_ref, wglu_ref, bglu_t_ref, bs_ref,
                 bmat_ref, cboth_ref, gmat_ref, lam_re_ref, lam_im_ref, bglu_row_ref, bst_ref):
    a_re = a8_ref[0:N_SLABS, :]
    a_im = a8_ref[N_SLABS:2 * N_SLABS, :]
    abar_re, abar_im = _discretize(a_re, a_im, a8_ref[2 * N_SLABS:3 * N_SLABS, :])
    lam_re_ref[...] = abar_re
    lam_im_ref[...] = abar_im
    num_re = abar_re - 1.0
    num_im = abar_im
    den = a_re * a_re + a_im * a_im
    coef_re = (num_re * a_re + num_im * a_im) / den
    coef_im = (num_im * a_re - num_re * a_im) / den

    def rep_matrix(k, n, period, offset=0):
        row = lax.broadcasted_iota(jnp.int32, (k, n), 0)
        col = lax.broadcasted_iota(jnp.int32, (k, n), 1)
        return jnp.where((col & (period - 1)) + offset == row, 1.0, 0.0).astype(BF16)

    def rep_rows(m, k, period):
        row = lax.broadcasted_iota(jnp.int32, (m, k), 0)
        col = lax.broadcasted_iota(jnp.int32, (m, k), 1)
        return jnp.where((row & (period - 1)) == col, 1.0, 0.0).astype(BF16)

    def block_mask(m, n, row_shift, col_shift):
        row = lax.broadcasted_iota(jnp.int32, (m, n), 0)
        col = lax.broadcasted_iota(jnp.int32, (m, n), 1)
        return (row >> row_shift) == (col >> col_shift)

    rep_state = rep_matrix(P_STATE, SLAB_STATES, P_STATE)
    rows_state = rep_rows(SLAB_STATES, P_STATE, P_STATE)
    rep_val = rep_matrix(2 * GROUP_B, LANES, GROUP_B)
    rep_gate = rep_matrix(2 * GROUP_B, LANES, GROUP_B, GROUP_B)
    mask_b = block_mask(LANES, SLAB_STATES, 4, 6)
    mask_c = block_mask(SLAB_STATES, LANES, 6, 4)
    mask_g = block_mask(LANES, LANES, 4, 4)

    def spread(x, rep):
        return sum(_dot(term, rep) for term in _bf16_terms(x))

    for j in range(N_SLABS):
        crows = slice(j * LANES, (j + 1) * LANES)
        bt_re = spread(bt_re_ref[crows, :], rep_state)
        bt_im = spread(bt_im_ref[crows, :], rep_state)
        cr = coef_re[j:j + 1, :]
        ci = coef_im[j:j + 1, :]
        bmat_ref[j, :, 0:SLAB_STATES] = jnp.where(mask_b, cr * bt_re - ci * bt_im, 0.0).astype(BF16)
        bmat_ref[j, :, SLAB_STATES:2 * SLAB_STATES] = jnp.where(
            mask_b, cr * bt_im + ci * bt_re, 0.0).astype(BF16)

        k, half = j % HALF_SLABS, j // HALF_SLABS
        ccols = slice(half * LANES, (half + 1) * LANES)
        ct_re = _nt_dot(rows_state, c_re_ref[crows, :].astype(BF16))
        ct_im = _nt_dot(rows_state, c_im_ref[crows, :].astype(BF16))
        cboth_ref[k, 0:SLAB_STATES, ccols] = jnp.where(mask_c, ct_re, 0.0).astype(BF16)
        cboth_ref[k, SLAB_STATES:2 * SLAB_STATES, ccols] = jnp.where(mask_c, -ct_im, 0.0).astype(BF16)

        w = wglu_ref[crows, :].astype(BF16)
        gmat_ref[j, :, 0:LANES] = jnp.where(mask_g, _dot(w, rep_val), 0.0).astype(BF16)
        gmat_ref[j, :, LANES:2 * LANES] = jnp.where(mask_g, _dot(w, rep_gate), 0.0).astype(BF16)

    row = lax.broadcasted_iota(jnp.int32, (2 * GROUP_B, D_B), 0)
    col = lax.broadcasted_iota(jnp.int32, (2 * GROUP_B, D_B), 1)
    by_group = spread(bglu_t_ref[...], jnp.where(block_mask(N_GROUPS_B, D_B, 0, 4), 1.0, 0.0).astype(BF16))
    for i in range(2):
        mine = row == (col & (GROUP_B - 1)) + i * GROUP_B
        bglu_row_ref[i:i + 1, :] = jnp.sum(jnp.where(mine, by_group, 0.0), axis=0, keepdims=True)

    eye = rep_rows(CHUNK, CHUNK, CHUNK)
    bst_ref[...] = sum(_nt_dot(eye, term) for term in _bf16_terms(bs_ref[...]))


def _s5_prep_operands(a_re, a_im, log_dt, b_re, b_im, c_re, c_im, w_glu, b_glu, b_s):
    g, p, c = N_GROUPS_B, P_STATE, GROUP_B
    a8 = jnp.stack([a_re, a_im, jnp.broadcast_to(log_dt[:, None], (g, p))]).reshape(3 * N_SLABS, SLAB_STATES)
    args = (a8, b_re.transpose(0, 2, 1).reshape(g * c, p), b_im.transpose(0, 2, 1).reshape(g * c, p),
            c_re.reshape(g * c, p), c_im.reshape(g * c, p), w_glu.reshape(g * c, 2 * c), b_glu.T, b_s)
    out_shape = (
        jax.ShapeDtypeStruct((N_SLABS, LANES, 2 * SLAB_STATES), BF16),
        jax.ShapeDtypeStruct((HALF_SLABS, 2 * SLAB_STATES, 2 * LANES), BF16),
        jax.ShapeDtypeStruct((N_SLABS, LANES, 2 * LANES), BF16),
        jax.ShapeDtypeStruct((N_SLABS, SLAB_STATES), F32),
        jax.ShapeDtypeStruct((N_SLABS, SLAB_STATES), F32),
        jax.ShapeDtypeStruct((2, D_B), F32),
        jax.ShapeDtypeStruct((CHUNK, N_HEADS_A), F32),
    )
    return args, out_shape


ADALN_ROWS = 512
ADALN_PROMPT_ROWS = 2 * SUBLANES


N_PREP_IN, N_PREP_OUT = 8, 7


def _adaln_kernel(cs_ref, cp_ref, w_ref, b_ref, x2_ref, gpre_ref, *refs):
    prep_in = refs[:N_PREP_IN]
    os_ref, op_ref, hs_ref = refs[N_PREP_IN:N_PREP_IN + 3]
    prep_out = refs[N_PREP_IN + 3:]
    assert len(prep_out) == N_PREP_OUT
    k = pl.program_id(0)

    @pl.when(k == 0)
    def _():
        os_ref[...] = jnp.broadcast_to(b_ref[...], os_ref.shape)
        op_ref[...] = jnp.broadcast_to(b_ref[...], op_ref.shape)
        _prep_kernel(*prep_in, *prep_out)

    cols = pl.ds(pl.multiple_of(k * ADALN_ROWS, ADALN_ROWS), ADALN_ROWS)
    w = w_ref[...].astype(BF16)
    os_ref[...] += _dot(_silu(cs_ref[:, cols]).astype(BF16), w)
    n_p = cp_ref.shape[0]
    cp = jnp.concatenate([cp_ref[:, cols], jnp.zeros((ADALN_PROMPT_ROWS - n_p, ADALN_ROWS), F32)], axis=0)
    op_ref[...] += _dot(_silu(cp).astype(BF16), w)

    @pl.when(k == pl.num_programs(0) - 1)
    def _():
        x_tiles = D_MODEL // LANES
        n = hs_ref.shape[0]
        x = jnp.concatenate([x2_ref[pl.ds(q, n, stride=x_tiles), :] for q in range(x_tiles)], axis=1)
        shift = os_ref[:, 0:D_MODEL]
        scale = os_ref[:, D_MODEL:2 * D_MODEL]
        hs_ref[...] = (_rms(x, gpre_ref[...]) * (1.0 + scale) + shift).astype(BF16)


def _adaln(c_sample, c_prompt, w_c, b_c, x2_sample, g_pre, prep_args, prep_out_shape):
    n_s, n_p = c_sample.shape[0], c_prompt.shape[0]
    assert n_p <= ADALN_PROMPT_ROWS and D_MODEL % ADALN_ROWS == 0
    assert len(prep_args) == N_PREP_IN and len(prep_out_shape) == N_PREP_OUT
    whole = lambda a: pl.BlockSpec(a.shape, lambda k, nd=len(a.shape): (0,) * nd)
    return pl.pallas_call(
        _adaln_kernel,
        grid=(D_MODEL // ADALN_ROWS,),
        in_specs=[
            pl.BlockSpec((n_s, D_MODEL), lambda k: (0, 0)),
            pl.BlockSpec((n_p, D_MODEL), lambda k: (0, 0)),
            pl.BlockSpec((ADALN_ROWS, 3 * D_MODEL), lambda k: (k, 0)),
            pl.BlockSpec((1, 3 * D_MODEL), lambda k: (0, 0)),
            pl.BlockSpec(x2_sample.shape, lambda k: (0, 0)),
            pl.BlockSpec(g_pre.shape, lambda k: (0, 0)),
        ] + [whole(a) for a in prep_args],
        out_specs=(pl.BlockSpec((n_s, 3 * D_MODEL), lambda k: (0, 0)),
                   pl.BlockSpec((ADALN_PROMPT_ROWS, 3 * D_MODEL), lambda k: (0, 0)),
                   pl.BlockSpec((n_s, D_MODEL), lambda k: (0, 0))) + tuple(whole(a) for a in prep_out_shape),
        out_shape=(jax.ShapeDtypeStruct((n_s, 3 * D_MODEL), F32),
                   jax.ShapeDtypeStruct((ADALN_PROMPT_ROWS, 3 * D_MODEL), F32),
                   jax.ShapeDtypeStruct((n_s, D_MODEL), BF16)) + tuple(prep_out_shape),
        compiler_params=pltpu.CompilerParams(
            dimension_semantics=("arbitrary",), vmem_limit_bytes=VMEM_LIMIT),
        name="adaln",
    )(c_sample, c_prompt, w_c, b_c.reshape(1, -1), x2_sample, g_pre, *prep_args)


def _mixer_kernel(x_ref, mod_ref, gpre_ref, w_in_hbm, lng_ref, lnb_ref, ws_ref, bst_ref,
                  lam_re_ref, lam_im_ref, bmat_ref, cboth_ref, gmat_ref, dskip_ref, bglu_ref, hs_ref, wout_ref,
                  oin_ref, st_re_ref, st_im_ref, wout_bf_ref, ps_ref,
                  h_sc, a_sc, b_sc, c_sc, v_sc, xb_sc, bu_sc, y2_sc, st_sc, w_in_ref, w_sem, *, n_seq):
    step = pl.program_id(0)

    @pl.when(step == 0)
    def _():
        st_sc[...] = jnp.zeros_like(st_sc)

        t_rows, t_cols = a_sc.shape[0] // 2, a_sc.shape[1]
        stage = [buf.at[pl.ds(i * t_rows, t_rows), :] for buf in (a_sc, b_sc, c_sc) for i in range(2)]
        assert len(stage) == W_IN_RING
        tiles = [(r, c) for c in range(D_IN // t_cols) for r in range(D_MODEL // t_rows)]

        def tile_copy(n):
            r, c = tiles[n]
            src = w_in_hbm.at[pl.ds(r * t_rows, t_rows), pl.ds(c * t_cols, t_cols)]
            return pltpu.make_async_copy(src, stage[n % len(stage)], w_sem.at[n % len(stage)])

        for n in range(len(stage)):
            tile_copy(n).start()
        for n, (r, c) in enumerate(tiles):
            tile_copy(n).wait()
            w_in_ref[r * t_rows:(r + 1) * t_rows, c * t_cols:(c + 1) * t_cols] = (
                stage[n % len(stage)][...].astype(BF16))
            if n + len(stage) < len(tiles):
                tile_copy(n + len(stage)).start()

    wout_bf_ref[...] = wout_ref[...].astype(BF16)

    @pl.when(step < D_IN // SAMPLE_PROJ_COLS)
    def _():
        cols = pl.ds(pl.multiple_of(step * SAMPLE_PROJ_COLS, SAMPLE_PROJ_COLS), SAMPLE_PROJ_COLS)
        ps_ref[...] = _dot(hs_ref[...], w_in_ref[:, cols])

    w_u, w_v, w_z, w_xb, w_zb = 0, D_A, 2 * D_A, 3 * D_A, 3 * D_A + D_B

    for b in range(n_seq):
        x = x_ref[b]
        shift = mod_ref[b:b + 1, 0:D_MODEL]
        gain = gpre_ref[...] * (1.0 + mod_ref[b:b + 1, D_MODEL:2 * D_MODEL])
        rws = slice(b * CHUNK, (b + 1) * CHUNK)
        h_sc[rws, :] = (x * _rms_scale(x) * gain + shift).astype(BF16)
        b_sc[rws, :] = _dot(h_sc[rws, :], w_in_ref[:, w_xb:w_xb + D_B])
        xb_sc[rws, :] = b_sc[rws, :].astype(BF16)

    h = h_sc[...]

    tril = (lax.broadcasted_iota(jnp.int32, (CHUNK, CHUNK), 0)
            >= lax.broadcasted_iota(jnp.int32, (CHUNK, CHUNK), 1))
    n_fill = HALF_SLABS * SCAN_PIECES // 4
    fill_cols = D_A // n_fill

    def fill_u(q):
        cols = slice(q * fill_cols, (q + 1) * fill_cols)
        a_sc[:, cols] = _dot(h, w_in_ref[:, w_u + q * fill_cols:w_u + (q + 1) * fill_cols])

    def fill_z(q):
        cols = slice(q * fill_cols, (q + 1) * fill_cols)
        z = _dot(h, w_in_ref[:, w_z + q * fill_cols:w_z + (q + 1) * fill_cols])
        a_sc[:, cols] = a_sc[:, cols] * _silu(z)

    def fill_mix(q):
        heads_per_fill = N_HEADS_A // n_fill
        for hd in range(q * heads_per_fill, (q + 1) * heads_per_fill):
            cols = slice(hd * HEAD_A, (hd + 1) * HEAD_A)
            w_t = jnp.where(tril, ws_ref[hd], 0.0).astype(BF16)
            v_h = jnp.concatenate([v_sc[b * CHUNK:(b + 1) * CHUNK, cols] for b in range(n_seq)], axis=1)
            mix = _dot(w_t, v_h) + bst_ref[:, hd:hd + 1]
            for b in range(n_seq):
                rws = slice(b * CHUNK, (b + 1) * CHUNK)
                oin_ref[rws, cols] = (a_sc[rws, cols] * mix[:, b * HEAD_A:(b + 1) * HEAD_A]).astype(BF16)

    def fill_zb(q):
        cols = slice(q * fill_cols, (q + 1) * fill_cols)
        a_sc[:, cols] = _silu(_dot(h, w_in_ref[:, w_zb + q * fill_cols:w_zb + (q + 1) * fill_cols]))

    fillers = [functools.partial(f, q) for f, q in (
        (fill_u, 0), (fill_z, 0), (fill_u, 1), (fill_mix, 0), (fill_z, 1), (fill_u, 2), (fill_z, 2), (fill_mix, 1),
        (fill_u, 3), (fill_z, 3), (fill_zb, 0), (fill_mix, 2), (fill_zb, 1), (fill_zb, 2), (fill_mix, 3), (fill_zb, 3))]
    assert len(fillers) == HALF_SLABS * SCAN_PIECES

    first_half = lax.broadcasted_iota(jnp.int32, (N_STREAMS, LANES), 0) < n_seq

    def bu_piece(k, piece):
        t0 = piece * PIECE
        for half in range(2):
            j = half * HALF_SLABS + k
            lhs = jnp.concatenate(
                [xb_sc[b * CHUNK + t0:b * CHUNK + t0 + PIECE, j * LANES:(j + 1) * LANES]
                 for b in range(n_seq)], axis=0)
            bu = _dot(lhs, bmat_ref[j])
            for b in range(n_seq):
                r0 = (half * n_seq + b) * STREAM_PITCH + t0
                for l in range(2 * SLAB_TILES):
                    bu_sc[l, r0:r0 + PIECE, :] = bu[b * PIECE:(b + 1) * PIECE, l * LANES:(l + 1) * LANES]

    def scan_piece(k, piece, hr, hi, lam_r, lam_i):
        rows = []
        for t in range(piece * PIECE, (piece + 1) * PIECE):
            new_r, new_i = [], []
            for l in range(SLAB_TILES):
                br = bu_sc[l, pl.ds(t, N_STREAMS, stride=STREAM_PITCH), :]
                bi = bu_sc[SLAB_TILES + l, pl.ds(t, N_STREAMS, stride=STREAM_PITCH), :]
                new_r.append(lam_r[l] * hr[l] - lam_i[l] * hi[l] + br)
                new_i.append(lam_r[l] * hi[l] + lam_i[l] * hr[l] + bi)
            hr, hi = new_r, new_i
            rows.append(jnp.concatenate(hr + hi, axis=1))
        return hr, hi, jnp.concatenate(rows, axis=0)

    def c_piece(k, piece, hh):
        t0 = piece * PIECE
        rows = slice(t0 * N_STREAMS, (t0 + PIECE) * N_STREAMS)
        y2 = _dot(hh.astype(BF16), cboth_ref[k])
        y2_sc[0, rows, :] = y2[:, 0:LANES]
        y2_sc[1, rows, :] = y2[:, LANES:2 * LANES]
        for half in range(2):
            j = half * HALF_SLABS + k
            for b in range(n_seq):
                s = half * n_seq + b
                c_sc[b * CHUNK + t0:b * CHUNK + t0 + PIECE, j * LANES:(j + 1) * LANES] = (
                    y2_sc[half, pl.ds(t0 * N_STREAMS + s, PIECE, stride=N_STREAMS), :])

    for piece in range(SCAN_PIECES):
        bu_piece(0, piece)
    c_sc[...] = _dot(h, w_in_ref[:, w_v:w_v + D_A])
    v_sc[...] = _layernorm(c_sc[...], lng_ref[...], lnb_ref[...]).astype(BF16)
    for k in range(HALF_SLABS):
        tiles = [slice(l * LANES, (l + 1) * LANES) for l in range(SLAB_TILES)]
        lam_r = [jnp.where(first_half, lam_re_ref[k:k + 1, t], lam_re_ref[HALF_SLABS + k:HALF_SLABS + k + 1, t])
                 for t in tiles]
        lam_i = [jnp.where(first_half, lam_im_ref[k:k + 1, t], lam_im_ref[HALF_SLABS + k:HALF_SLABS + k + 1, t])
                 for t in tiles]
        hr = [st_sc[k, :, l * LANES:(l + 1) * LANES] for l in range(SLAB_TILES)]
        hi = [st_sc[k, :, (SLAB_TILES + l) * LANES:(SLAB_TILES + l + 1) * LANES] for l in range(SLAB_TILES)]
        for piece in range(SCAN_PIECES):
            hr, hi, hh = scan_piece(k, piece, hr, hi, lam_r, lam_i)
            c_piece(k, piece, hh)
            if k + 1 < HALF_SLABS:
                bu_piece(k + 1, piece)
            fillers[k * SCAN_PIECES + piece]()
        st_sc[k] = jnp.concatenate(hr + hi, axis=1)

    for k in range(HALF_SLABS):
        for half in range(2):
            j = half * HALF_SLABS + k
            for b in range(n_seq):
                s = half * n_seq + b
                st_re_ref[b, j:j + 1, :] = st_sc[k, s:s + 1, 0:SLAB_STATES]
                st_im_ref[b, j:j + 1, :] = st_sc[k, s:s + 1, SLAB_STATES:2 * SLAB_STATES]

    for j in range(N_SLABS):
        cols = slice(j * LANES, (j + 1) * LANES)
        y = c_sc[:, cols] + dskip_ref[:, cols] * b_sc[:, cols]
        g = _dot(y.astype(BF16), gmat_ref[j])
        val = g[:, 0:LANES] + bglu_ref[0:1, cols]
        gate = g[:, LANES:2 * LANES] + bglu_ref[1:2, cols]
        oin_ref[:, D_A + j * LANES:D_A + (j + 1) * LANES] = (
            val * jax.nn.sigmoid(gate) * a_sc[:, cols]).astype(BF16)


def _const_spec(shape):
    zeros = (0,) * len(shape)
    return pl.BlockSpec(shape, lambda i: zeros, pipeline_mode=pl.Buffered(1))


def _mixer(x, mod, g_pre, w_in, ln_g, ln_b, w_s, bst, lam_re, lam_im, bmat, cboth, gmat,
           d_skip, b_glu_rows, h_sample, w_out):
    n_seq, seq, _ = x.shape
    assert 2 * n_seq == N_STREAMS and seq % CHUNK == 0
    rows = n_seq * CHUNK
    n_steps = seq // CHUNK
    wout_rows = w_out.shape[0] // n_steps
    assert wout_rows * n_steps == w_out.shape[0] and wout_rows % (2 * SUBLANES) == 0
    state_shape = (n_seq, N_SLABS, SLAB_STATES)
    n_sample = h_sample.shape[0]
    n_proj = D_IN // SAMPLE_PROJ_COLS
    assert n_proj <= n_steps
    consts = (mod, g_pre, w_in, ln_g, ln_b, w_s, bst, lam_re, lam_im, bmat, cboth, gmat,
              d_skip, b_glu_rows, h_sample)
    assert w_in.shape == (D_MODEL, D_IN) and D_MODEL % rows == 0 and D_IN % D_A == 0
    return pl.pallas_call(
        functools.partial(_mixer_kernel, n_seq=n_seq),
        grid=(n_steps,),
        in_specs=[pl.BlockSpec((n_seq, CHUNK, D_MODEL), lambda i: (0, i, 0))]
        + [pl.BlockSpec(memory_space=pl.ANY) if c is w_in else _const_spec(c.shape) for c in consts]
        + [pl.BlockSpec((wout_rows, w_out.shape[1]), lambda i: (i, 0))],
        out_specs=(
            pl.BlockSpec((rows, D_A + D_B), lambda i: (i, 0)),
            pl.BlockSpec(state_shape, lambda i: (0, 0, 0)),
            pl.BlockSpec(state_shape, lambda i: (0, 0, 0)),
            pl.BlockSpec((wout_rows, w_out.shape[1]), lambda i: (i, 0)),
            pl.BlockSpec((n_sample, SAMPLE_PROJ_COLS), lambda i: (0, jnp.minimum(i, n_proj - 1))),
        ),
        out_shape=(
            jax.ShapeDtypeStruct((n_steps * rows, D_A + D_B), BF16),
            jax.ShapeDtypeStruct(state_shape, F32),
            jax.ShapeDtypeStruct(state_shape, F32),
            jax.ShapeDtypeStruct(w_out.shape, BF16),
            jax.ShapeDtypeStruct((n_sample, D_IN), F32),
        ),
        scratch_shapes=[
            pltpu.VMEM((rows, D_MODEL), BF16),
            pltpu.VMEM((rows, D_A), F32),
            pltpu.VMEM((rows, D_A), F32),
            pltpu.VMEM((rows, D_A), F32),
            pltpu.VMEM((rows, D_A), BF16),
            pltpu.VMEM((rows, D_B), BF16),
            pltpu.VMEM((2 * SLAB_TILES, N_STREAMS * STREAM_PITCH, LANES), F32),
            pltpu.VMEM((2, N_STREAMS * CHUNK, LANES), F32),
            pltpu.VMEM((HALF_SLABS, N_STREAMS, 2 * SLAB_STATES), F32),
            pltpu.VMEM((D_MODEL, D_IN), BF16),
            pltpu.SemaphoreType.DMA((W_IN_RING,)),
        ],
        compiler_params=pltpu.CompilerParams(
            dimension_semantics=("arbitrary",), vmem_limit_bytes=VMEM_LIMIT),
        name="mixer",
    )(x, *consts, w_out)


def _outproj_kernel(oin_ref, x_ref, mod_ref, gpost_ref, w_out_ref, y_ref, *, n_seq):
    o = _dot(oin_ref[...], w_out_ref[...])
    r = o * _rms_scale(o)
    for b in range(n_seq):
        gain = gpost_ref[...] * mod_ref[b:b + 1, 2 * D_MODEL:3 * D_MODEL]
        y_ref[b] = x_ref[b] + r[b * CHUNK:(b + 1) * CHUNK, :] * gain


def _outproj(oin, x, mod, g_post, w_out_bf):
    n_seq, seq, _ = x.shape
    rows = n_seq * CHUNK
    return pl.pallas_call(
        functools.partial(_outproj_kernel, n_seq=n_seq),
        grid=(seq // CHUNK,),
        in_specs=[
            pl.BlockSpec((rows, D_A + D_B), lambda i: (i, 0)),
            pl.BlockSpec((n_seq, CHUNK, D_MODEL), lambda i: (0, i, 0)),
            _const_spec(mod.shape), _const_spec(g_post.shape), _const_spec(w_out_bf.shape),
        ],
        out_specs=pl.BlockSpec((n_seq, CHUNK, D_MODEL), lambda i: (0, i, 0)),
        out_shape=jax.ShapeDtypeStruct(x.shape, F32),
        compiler_params=pltpu.CompilerParams(
            dimension_semantics=("arbitrary",), vmem_limit_bytes=VMEM_LIMIT),
        name="outproj",
    )(oin, x, mod, g_post, w_out_bf)


def _sample_kernel(x2_ref, mod_ref, proj_ref, lng_ref, lnb_ref, ws_ref, bst_ref,
                   lam_re_ref, lam_im_ref, bmat_ref, cboth_ref, gmat_ref, dskip_ref, bglu_ref,
                   h0t_re_ref, h0t_im_ref, gpost_ref, w_out_ref,
                   y2_ref, v2_ref, hst_re_ref, hst_im_ref, oin_sc):
    n = proj_ref.shape[0]
    x_tiles = D_MODEL // LANES
    v_tiles = D_A // LANES
    w_u, w_v, w_z, w_xb, w_zb = 0, D_A, 2 * D_A, 3 * D_A, 3 * D_A + D_B

    v = _layernorm(proj_ref[:, w_v:w_v + D_A], lng_ref[...], lnb_ref[...])
    for q in range(v_tiles):
        v2_ref[pl.ds(q, n, stride=v_tiles), :] = v[:, q * LANES:(q + 1) * LANES]
    for hd in range(N_HEADS_A):
        cols = slice(hd * HEAD_A, (hd + 1) * HEAD_A)
        mix = ws_ref[hd, 0:1, 0:1] * v[:, cols] + bst_ref[0:1, hd:hd + 1]
        u = proj_ref[:, w_u + hd * HEAD_A:w_u + (hd + 1) * HEAD_A]
        z = proj_ref[:, w_z + hd * HEAD_A:w_z + (hd + 1) * HEAD_A]
        oin_sc[:, cols] = (u * mix * _silu(z)).astype(BF16)

    for j in range(N_SLABS):
        cols = slice(j * LANES, (j + 1) * LANES)
        st = slice(j * SLAB_STATES, (j + 1) * SLAB_STATES)
        xb = proj_ref[:, w_xb + j * LANES:w_xb + (j + 1) * LANES]
        zb = proj_ref[:, w_zb + j * LANES:w_zb + (j + 1) * LANES]
        bu = _dot(xb.astype(BF16), bmat_ref[j])
        lr = lam_re_ref[j:j + 1, :]
        li = lam_im_ref[j:j + 1, :]
        h0r = h0t_re_ref[st, :].T
        h0i = h0t_im_ref[st, :].T
        hr = lr * h0r - li * h0i + bu[:, 0:SLAB_STATES]
        hi = lr * h0i + li * h0r + bu[:, SLAB_STATES:2 * SLAB_STATES]
        hst_re_ref[st, :] = hr.T
        hst_im_ref[st, :] = hi.T
        k, half = j % HALF_SLABS, j // HALF_SLABS
        c_j = cboth_ref[k, :, half * LANES:(half + 1) * LANES]
        y = _dot(jnp.concatenate([hr, hi], axis=1).astype(BF16), c_j)
        y = y + dskip_ref[:, cols] * xb
        g = _dot(y.astype(BF16), gmat_ref[j])
        val = g[:, 0:LANES] + bglu_ref[0:1, cols]
        gt = g[:, LANES:2 * LANES] + bglu_ref[1:2, cols]
        oin_sc[:, D_A + j * LANES:D_A + (j + 1) * LANES] = (
            val * jax.nn.sigmoid(gt) * _silu(zb)).astype(BF16)

    o = _dot(oin_sc[...], w_out_ref[...])
    gate = mod_ref[:, 2 * D_MODEL:3 * D_MODEL]
    r = gate * _rms(o, gpost_ref[...])
    for q in range(x_tiles):
        cols = slice(q * LANES, (q + 1) * LANES)
        y2_ref[pl.ds(q, n, stride=x_tiles), :] = x2_ref[pl.ds(q, n, stride=x_tiles), :] + r[:, cols]


def _sample(x2, mod, proj, ln_g, ln_b, w_s, bst, lam_re, lam_im, bmat, cboth, gmat,
            d_skip, b_glu_rows, h0t_re, h0t_im, g_post, w_out_bf):
    n = proj.shape[0]
    n_state = N_GROUPS_B * P_STATE
    out_shapes = ((n * D_MODEL // LANES, LANES), (n * D_A // LANES, LANES), (n_state, n), (n_state, n))
    return pl.pallas_call(
        _sample_kernel,
        out_shape=tuple(jax.ShapeDtypeStruct(shp, F32) for shp in out_shapes),
        scratch_shapes=[pltpu.VMEM((n, D_A + D_B), BF16)],
        compiler_params=pltpu.CompilerParams(vmem_limit_bytes=VMEM_LIMIT),
        name="sample",
    )(x2, mod, proj, ln_g, ln_b, w_s, bst, lam_re, lam_im, bmat, cboth, gmat,
      d_skip, b_glu_rows, h0t_re, h0t_im, g_post, w_out_bf)


def kernel(x_prompt, x_sample, c_prompt, c_sample, state_b_re, state_b_im, w_c, b_c, g_pre, w_in,
           ln_v_g, ln_v_b, w_s, b_s, a_re, a_im, log_dt, b_re, b_im, c_re, c_im, d_skip, w_glu, b_glu,
           w_out, g_post):
    n_p = x_prompt.shape[0]
    n_s = x_sample.shape[0]
    assert x_sample.shape[1] == 1 and n_s % SUBLANES == 0

    row = lambda v: v.reshape(1, -1)
    x2_sample = x_sample.reshape(n_s * D_MODEL // LANES, LANES)
    prep_args, prep_out_shape = _s5_prep_operands(a_re, a_im, log_dt, b_re, b_im, c_re, c_im, w_glu, b_glu, b_s)
    (mod_s, mod_p, h_sample, bmat, cboth, gmat, lam_re, lam_im, b_glu_rows, bst) = _adaln(
        c_sample, c_prompt, w_c, b_c, x2_sample, row(g_pre), prep_args, prep_out_shape)

    s5 = (w_s, bst, lam_re, lam_im, bmat, cboth, gmat, row(d_skip), b_glu_rows)
    oin, st_re, st_im, w_out_bf, proj_s = _mixer(
        x_prompt, mod_p, row(g_pre), w_in, row(ln_v_g), row(ln_v_b), *s5, h_sample, w_out)
    y_prompt = _outproj(oin, x_prompt, mod_p, row(g_post), w_out_bf)

    to_rows = lambda st: st.transpose(1, 2, 0).reshape(N_GROUPS_B * P_STATE, n_s)
    from_rows = lambda st: st.reshape(N_GROUPS_B, P_STATE, n_s).transpose(2, 0, 1)
    y_s, v_s, hs_re, hs_im = _sample(
        x2_sample, mod_s, proj_s, row(ln_v_g), row(ln_v_b), *s5,
        to_rows(state_b_re), to_rows(state_b_im), row(g_post), w_out_bf)

    return (y_prompt, y_s.reshape(n_s, 1, D_MODEL), v_s.reshape(n_s, 1, D_A),
            st_re.reshape(n_p, N_GROUPS_B, P_STATE), st_im.reshape(n_p, N_GROUPS_B, P_STATE),
            from_rows(hs_re), from_rows(hs_im))
```

```python
import functools

import jax
import jax.numpy as jnp
from jax import lax
from jax.experimental import pallas as pl
from jax.experimental.pallas import tpu as pltpu

F32 = jnp.float32
BF16 = jnp.bfloat16

EPS = 1e-6
D_MODEL = 2048
D_A = 1024
D_B = 1024
D_IN = 3 * D_A + 2 * D_B
CHUNK = 128
HEAD_A = 128
N_HEADS_A = D_A // HEAD_A
GROUP_B = 16
N_GROUPS_B = D_B // GROUP_B
P_STATE = 64

LANES = 128
SUBLANES = 8
SLAB_GROUPS = LANES // GROUP_B
N_SLABS = N_GROUPS_B // SLAB_GROUPS
SLAB_STATES = SLAB_GROUPS * P_STATE
SLAB_TILES = SLAB_STATES // LANES
N_STREAMS = SUBLANES
HALF_SLABS = N_SLABS // 2
STREAM_PITCH = CHUNK + SUBLANES
SCAN_PIECES = 4
PIECE = CHUNK // SCAN_PIECES
V7X_VMEM_BYTES = 64 * 1024 * 1024
VMEM_LIMIT = V7X_VMEM_BYTES - 1024 * 1024
SAMPLE_PROJ_COLS = 1024
W_IN_RING = 12


def _silu(x):
    return x * jax.nn.sigmoid(x)


def _rms_scale(x):
    return lax.rsqrt(jnp.mean(x * x, axis=-1, keepdims=True) + EPS)


def _rms(x, g):
    return x * _rms_scale(x) * g


def _layernorm(x, g, b):
    mu = jnp.mean(x, axis=-1, keepdims=True)
    xc = x - mu
    return xc * lax.rsqrt(jnp.mean(xc * xc, axis=-1, keepdims=True) + EPS) * g + b


def _dot(a, b):
    return jnp.dot(a, b, preferred_element_type=F32)


def _discretize(a_re, a_im, log_dt):
    dt = jnp.exp(log_dt)
    mag = jnp.exp(dt * a_re)
    abar_re = mag * jnp.cos(dt * a_im)
    abar_im = mag * jnp.sin(dt * a_im)
    return abar_re, abar_im


def _nt_dot(a, b):
    return lax.dot_general(a, b, (((1,), (1,)), ((), ())), preferred_element_type=F32)


def _bf16_terms(x):
    hi = x.astype(BF16)
    r1 = x - hi.astype(F32)
    mid = r1.astype(BF16)
    lo = (r1 - mid.astype(F32)).astype(BF16)
    return hi, mid, lo


def _prep_kernel(a8_ref, bt_re_ref, bt_im_ref, c_re_ref, c_im_ref, wglu_ref, bglu_t_ref, bs_ref,
                 bmat_ref, cboth_ref, gmat_ref, lam_re_ref, lam_im_ref, bglu_row_ref, bst_ref):
    a_re = a8_ref[0:N_SLABS, :]
    a_im = a8_ref[N_SLABS:2 * N_SLABS, :]
    abar_re, abar_im = _discretize(a_re, a_im, a8_ref[2 * N_SLABS:3 * N_SLABS, :])
    lam_re_ref[...] = abar_re
    lam_im_ref[...] = abar_im
    num_re = abar_re - 1.0
    num_im = abar_im
    den = a_re * a_re + a_im * a_im
    coef_re = (num_re * a_re + num_im * a_im) / den
    coef_im = (num_im * a_re - num_re * a_im) / den

    def rep_matrix(k, n, period, offset=0):
        row = lax.broadcasted_iota(jnp.int32, (k, n), 0)
        col = lax.broadcasted_iota(jnp.int32, (k, n), 1)
        return jnp.where((col & (period - 1)) + offset == row, 1.0, 0.0).astype(BF16)

    def rep_rows(m, k, period):
        row = lax.broadcasted_iota(jnp.int32, (m, k), 0)
        col = lax.broadcasted_iota(jnp.int32, (m, k), 1)
        return jnp.where((row & (period - 1)) == col, 1.0, 0.0).astype(BF16)

    def block_mask(m, n, row_shift, col_shift):
        row = lax.broadcasted_iota(jnp.int32, (m, n), 0)
        col = lax.broadcasted_iota(jnp.int32, (m, n), 1)
        return (row >> row_shift) == (col >> col_shift)

    rep_state = rep_matrix(P_STATE, SLAB_STATES, P_STATE)
    rows_state = rep_rows(SLAB_STATES, P_STATE, P_STATE)
    rep_val = rep_matrix(2 * GROUP_B, LANES, GROUP_B)
    rep_gate = rep_matrix(2 * GROUP_B, LANES, GROUP_B, GROUP_B)
    mask_b = block_mask(LANES, SLAB_STATES, 4, 6)
    mask_c = block_mask(SLAB_STATES, LANES, 6, 4)
    mask_g = block_mask(LANES, LANES, 4, 4)

    def spread(x, rep):
        return sum(_dot(term, rep) for term in _bf16_terms(x))

    for j in range(N_SLABS):
        crows = slice(j * LANES, (j + 1) * LANES)
        bt_re = spread(bt_re_ref[crows, :], rep_state)
        bt_im = spread(bt_im_ref[crows, :], rep_state)
        cr = coef_re[j:j + 1, :]
        ci = coef_im[j:j + 1, :]
        bmat_ref[j, :, 0:SLAB_STATES] = jnp.where(mask_b, cr * bt_re - ci * bt_im, 0.0).astype(BF16)
        bmat_ref[j, :, SLAB_STATES:2 * SLAB_STATES] = jnp.where(
            mask_b, cr * bt_im + ci * bt_re, 0.0).astype(BF16)

        k, half = j % HALF_SLABS, j // HALF_SLABS
        ccols = slice(half * LANES, (half + 1) * LANES)
        ct_re = _nt_dot(rows_state, c_re_ref[crows, :].astype(BF16))
        ct_im = _nt_dot(rows_state, c_im_ref[crows, :].astype(BF16))
        cboth_ref[k, 0:SLAB_STATES, ccols] = jnp.where(mask_c, ct_re, 0.0).astype(BF16)
        cboth_ref[k, SLAB_STATES:2 * SLAB_STATES, ccols] = jnp.where(mask_c, -ct_im, 0.0).astype(BF16)

        w = wglu_ref[crows, :].astype(BF16)
        gmat_ref[j, :, 0:LANES] = jnp.where(mask_g, _dot(w, rep_val), 0.0).astype(BF16)
        gmat_ref[j, :, LANES:2 * LANES] = jnp.where(mask_g, _dot(w, rep_gate), 0.0).astype(BF16)

    row = lax.broadcasted_iota(jnp.int32, (2 * GROUP_B, D_B), 0)
    col = lax.broadcasted_iota(jnp.int32, (2 * GROUP_B, D_B), 1)
    by_group = spread(bglu_t_ref[...], jnp.where(block_mask(N_GROUPS_B, D_B, 0, 4), 1.0, 0.0).astype(BF16))
    for i in range(2):
        mine = row == (col & (GROUP_B - 1)) + i * GROUP_B
        bglu_row_ref[i:i + 1, :] = jnp.sum(jnp.where(mine, by_group, 0.0), axis=0, keepdims=True)

    eye = rep_rows(CHUNK, CHUNK, CHUNK)
    bst_ref[...] = sum(_nt_dot(eye, term) for term in _bf16_terms(bs_ref[...]))


def _s5_prep_operands(a_re, a_im, log_dt, b_re, b_im, c_re, c_im, w_glu, b_glu, b_s):
    g, p, c = N_GROUPS_B, P_STATE, GROUP_B
    a8 = jnp.stack([a_re, a_im, jnp.broadcast_to(log_dt[:, None], (g, p))]).reshape(3 * N_SLABS, SLAB_STATES)
    args = (a8, b_re.transpose(0, 2, 1).reshape(g * c, p), b_im.transpose(0, 2, 1).reshape(g * c, p),
            c_re.reshape(g * c, p), c_im.reshape(g * c, p), w_glu.reshape(g * c, 2 * c), b_glu.T, b_s)
    out_shape = (
        jax.ShapeDtypeStruct((N_SLABS, LANES, 2 * SLAB_STATES), BF16),
        jax.ShapeDtypeStruct((HALF_SLABS, 2 * SLAB_STATES, 2 * LANES), BF16),
        jax.ShapeDtypeStruct((N_SLABS, LANES, 2 * LANES), BF16),
        jax.ShapeDtypeStruct((N_SLABS, SLAB_STATES), F32),
        jax.ShapeDtypeStruct((N_SLABS, SLAB_STATES), F32),
        jax.ShapeDtypeStruct((2, D_B), F32),
        jax.ShapeDtypeStruct((CHUNK, N_HEADS_A), F32),
    )
    return args, out_shape


ADALN_ROWS = 512
ADALN_PROMPT_ROWS = 2 * SUBLANES


N_PREP_IN, N_PREP_OUT = 8, 7


def _adaln_kernel(cs_ref, cp_ref, w_ref, b_ref, x2_ref, gpre_ref, *refs):
    prep_in = refs[:N_PREP_IN]
    os_ref, op_ref, hs_ref = refs[N_PREP_IN:N_PREP_IN + 3]
    prep_out = refs[N_PREP_IN + 3:]
    assert len(prep_out) == N_PREP_OUT
    k = pl.program_id(0)

    @pl.when(k == 0)
    def _():
        os_ref[...] = jnp.broadcast_to(b_ref[...], os_ref.shape)
        op_ref[...] = jnp.broadcast_to(b_ref[...], op_ref.shape)
        _prep_kernel(*prep_in, *prep_out)

    cols = pl.ds(pl.multiple_of(k * ADALN_ROWS, ADALN_ROWS), ADALN_ROWS)
    w = w_ref[...].astype(BF16)
    os_ref[...] += _dot(_silu(cs_ref[:, cols]).astype(BF16), w)
    n_p = cp_ref.shape[0]
    cp = jnp.concatenate([cp_ref[:, cols], jnp.zeros((ADALN_PROMPT_ROWS - n_p, ADALN_ROWS), F32)], axis=0)
    op_ref[...] += _dot(_silu(cp).astype(BF16), w)

    @pl.when(k == pl.num_programs(0) - 1)
    def _():
        x_tiles = D_MODEL // LANES
        n = hs_ref.shape[0]
        x = jnp.concatenate([x2_ref[pl.ds(q, n, stride=x_tiles), :] for q in range(x_tiles)], axis=1)
        shift = os_ref[:, 0:D_MODEL]
        scale = os_ref[:, D_MODEL:2 * D_MODEL]
        hs_ref[...] = (_rms(x, gpre_ref[...]) * (1.0 + scale) + shift).astype(BF16)


def _adaln(c_sample, c_prompt, w_c, b_c, x2_sample, g_pre, prep_args, prep_out_shape):
    n_s, n_p = c_sample.shape[0], c_prompt.shape[0]
    assert n_p <= ADALN_PROMPT_ROWS and D_MODEL % ADALN_ROWS == 0
    assert len(prep_args) == N_PREP_IN and len(prep_out_shape) == N_PREP_OUT
    whole = lambda a: pl.BlockSpec(a.shape, lambda k, nd=len(a.shape): (0,) * nd)
    return pl.pallas_call(
        _adaln_kernel,
        grid=(D_MODEL // ADALN_ROWS,),
        in_specs=[
            pl.BlockSpec((n_s, D_MODEL), lambda k: (0, 0)),
            pl.BlockSpec((n_p, D_MODEL), lambda k: (0, 0)),
            pl.BlockSpec((ADALN_ROWS, 3 * D_MODEL), lambda k: (k, 0)),
            pl.BlockSpec((1, 3 * D_MODEL), lambda k: (0, 0)),
            pl.BlockSpec(x2_sample.shape, lambda k: (0, 0)),
            pl.BlockSpec(g_pre.shape, lambda k: (0, 0)),
        ] + [whole(a) for a in prep_args],
        out_specs=(pl.BlockSpec((n_s, 3 * D_MODEL), lambda k: (0, 0)),
                   pl.BlockSpec((ADALN_PROMPT_ROWS, 3 * D_MODEL), lambda k: (0, 0)),
                   pl.BlockSpec((n_s, D_MODEL), lambda k: (0, 0))) + tuple(whole(a) for a in prep_out_shape),
        out_shape=(jax.ShapeDtypeStruct((n_s, 3 * D_MODEL), F32),
                   jax.ShapeDtypeStruct((ADALN_PROMPT_ROWS, 3 * D_MODEL), F32),
                   jax.ShapeDtypeStruct((n_s, D_MODEL), BF16)) + tuple(prep_out_shape),
        compiler_params=pltpu.CompilerParams(
            dimension_semantics=("arbitrary",), vmem_limit_bytes=VMEM_LIMIT),
        name="adaln",
    )(c_sample, c_prompt, w_c, b_c.reshape(1, -1), x2_sample, g_pre, *prep_args)


def _mixer_kernel(x_ref, mod_ref, gpre_ref, w_in_hbm, lng_ref, lnb_ref, ws_ref, bst_ref,
                  lam_re_ref, lam_im_ref, bmat_ref, cboth_ref, gmat_ref, dskip_ref, bglu_ref, hs_ref, wout_ref,
                  oin_ref, st_re_ref, st_im_ref, wout_bf_ref, ps_ref,
                  h_sc, a_sc, b_sc, c_sc, v_sc, xb_sc, bu_sc, y2_sc, st_sc, w_in_ref, w_sem, *, n_seq):
    step = pl.program_id(0)

    @pl.when(step == 0)
    def _():
        st_sc[...] = jnp.zeros_like(st_sc)

        per_buf = W_IN_RING // 3
        t_rows, t_cols = a_sc.shape[0] // per_buf, a_sc.shape[1]
        stage = [buf.at[pl.ds(i * t_rows, t_rows), :] for buf in (a_sc, b_sc, c_sc) for i in range(per_buf)]
        assert len(stage) == W_IN_RING
        tiles = [(r, c) for c in range(D_IN // t_cols) for r in range(D_MODEL // t_rows)]

        def tile_copy(n):
            r, c = tiles[n]
            src = w_in_hbm.at[pl.ds(r * t_rows, t_rows), pl.ds(c * t_cols, t_cols)]
            return pltpu.make_async_copy(src, stage[n % len(stage)], w_sem.at[n % len(stage)])

        for n in range(len(stage)):
            tile_copy(n).start()
        for n, (r, c) in enumerate(tiles):
            tile_copy(n).wait()
            w_in_ref[r * t_rows:(r + 1) * t_rows, c * t_cols:(c + 1) * t_cols] = (
                stage[n % len(stage)][...].astype(BF16))
            if n + len(stage) < len(tiles):
                tile_copy(n + len(stage)).start()

    wout_bf_ref[...] = wout_ref[...].astype(BF16)

    @pl.when(step < D_IN // SAMPLE_PROJ_COLS)
    def _():
        cols = pl.ds(pl.multiple_of(step * SAMPLE_PROJ_COLS, SAMPLE_PROJ_COLS), SAMPLE_PROJ_COLS)
        ps_ref[...] = _dot(hs_ref[...], w_in_ref[:, cols])

    w_u, w_v, w_z, w_xb, w_zb = 0, D_A, 2 * D_A, 3 * D_A, 3 * D_A + D_B

    for b in range(n_seq):
        x = x_ref[b]
        shift = mod_ref[b:b + 1, 0:D_MODEL]
        gain = gpre_ref[...] * (1.0 + mod_ref[b:b + 1, D_MODEL:2 * D_MODEL])
        rws = slice(b * CHUNK, (b + 1) * CHUNK)
        h_sc[rws, :] = (x * _rms_scale(x) * gain + shift).astype(BF16)
        b_sc[rws, :] = _dot(h_sc[rws, :], w_in_ref[:, w_xb:w_xb + D_B])
        xb_sc[rws, :] = b_sc[rws, :].astype(BF16)

    h = h_sc[...]

    tril = (lax.broadcasted_iota(jnp.int32, (CHUNK, CHUNK), 0)
            >= lax.broadcasted_iota(jnp.int32, (CHUNK, CHUNK), 1))
    n_fill = HALF_SLABS * SCAN_PIECES // 4
    fill_cols = D_A // n_fill

    def fill_u(q):
        cols = slice(q * fill_cols, (q + 1) * fill_cols)
        a_sc[:, cols] = _dot(h, w_in_ref[:, w_u + q * fill_cols:w_u + (q + 1) * fill_cols])

    def fill_z(q):
        cols = slice(q * fill_cols, (q + 1) * fill_cols)
        z = _dot(h, w_in_ref[:, w_z + q * fill_cols:w_z + (q + 1) * fill_cols])
        a_sc[:, cols] = a_sc[:, cols] * _silu(z)

    def fill_mix(q):
        heads_per_fill = N_HEADS_A // n_fill
        for hd in range(q * heads_per_fill, (q + 1) * heads_per_fill):
            cols = slice(hd * HEAD_A, (hd + 1) * HEAD_A)
            w_t = jnp.where(tril, ws_ref[hd], 0.0).astype(BF16)
            v_h = jnp.concatenate([v_sc[b * CHUNK:(b + 1) * CHUNK, cols] for b in range(n_seq)], axis=1)
            mix = _dot(w_t, v_h) + bst_ref[:, hd:hd + 1]
            for b in range(n_seq):
                rws = slice(b * CHUNK, (b + 1) * CHUNK)
                oin_ref[rws, cols] = (a_sc[rws, cols] * mix[:, b * HEAD_A:(b + 1) * HEAD_A]).astype(BF16)

    def fill_zb(q):
        cols = slice(q * fill_cols, (q + 1) * fill_cols)
        a_sc[:, cols] = _silu(_dot(h, w_in_ref[:, w_zb + q * fill_cols:w_zb + (q + 1) * fill_cols]))

    fillers = [functools.partial(f, q) for f, q in (
        (fill_u, 0), (fill_z, 0), (fill_u, 1), (fill_mix, 0), (fill_z, 1), (fill_u, 2), (fill_z, 2), (fill_mix, 1),
        (fill_u, 3), (fill_z, 3), (fill_zb, 0), (fill_mix, 2), (fill_zb, 1), (fill_zb, 2), (fill_mix, 3), (fill_zb, 3))]
    assert len(fillers) == HALF_SLABS * SCAN_PIECES

    first_half = lax.broadcasted_iota(jnp.int32, (N_STREAMS, LANES), 0) < n_seq

    def bu_piece(k, piece):
        t0 = piece * PIECE
        for half in range(2):
            j = half * HALF_SLABS + k
            lhs = jnp.concatenate(
                [xb_sc[b * CHUNK + t0:b * CHUNK + t0 + PIECE, j * LANES:(j + 1) * LANES]
                 for b in range(n_seq)], axis=0)
            bu = _dot(lhs, bmat_ref[j])
            for b in range(n_seq):
                r0 = (half * n_seq + b) * STREAM_PITCH + t0
                for l in range(2 * SLAB_TILES):
                    bu_sc[l, r0:r0 + PIECE, :] = bu[b * PIECE:(b + 1) * PIECE, l * LANES:(l + 1) * LANES]

    def scan_piece(k, piece, hr, hi, lam_r, lam_i):
        rows = []
        for t in range(piece * PIECE, (piece + 1) * PIECE):
            new_r, new_i = [], []
            for l in range(SLAB_TILES):
                br = bu_sc[l, pl.ds(t, N_STREAMS, stride=STREAM_PITCH), :]
                bi = bu_sc[SLAB_TILES + l, pl.ds(t, N_STREAMS, stride=STREAM_PITCH), :]
                new_r.append(lam_r[l] * hr[l] - lam_i[l] * hi[l] + br)
                new_i.append(lam_r[l] * hi[l] + lam_i[l] * hr[l] + bi)
            hr, hi = new_r, new_i
            rows.append(jnp.concatenate(hr + hi, axis=1))
        return hr, hi, jnp.concatenate(rows, axis=0)

    def c_piece(k, piece, hh):
        t0 = piece * PIECE
        rows = slice(t0 * N_STREAMS, (t0 + PIECE) * N_STREAMS)
        y2 = _dot(hh.astype(BF16), cboth_ref[k])
        y2_sc[0, rows, :] = y2[:, 0:LANES]
        y2_sc[1, rows, :] = y2[:, LANES:2 * LANES]
        for half in range(2):
            j = half * HALF_SLABS + k
            for b in range(n_seq):
                s = half * n_seq + b
                c_sc[b * CHUNK + t0:b * CHUNK + t0 + PIECE, j * LANES:(j + 1) * LANES] = (
                    y2_sc[half, pl.ds(t0 * N_STREAMS + s, PIECE, stride=N_STREAMS), :])

    for piece in range(SCAN_PIECES):
        bu_piece(0, piece)
    c_sc[...] = _dot(h, w_in_ref[:, w_v:w_v + D_A])
    v_sc[...] = _layernorm(c_sc[...], lng_ref[...], lnb_ref[...]).astype(BF16)
    for k in range(HALF_SLABS):
        tiles = [slice(l * LANES, (l + 1) * LANES) for l in range(SLAB_TILES)]
        lam_r = [jnp.where(first_half, lam_re_ref[k:k + 1, t], lam_re_ref[HALF_SLABS + k:HALF_SLABS + k + 1, t])
                 for t in tiles]
        lam_i = [jnp.where(first_half, lam_im_ref[k:k + 1, t], lam_im_ref[HALF_SLABS + k:HALF_SLABS + k + 1, t])
                 for t in tiles]
        hr = [st_sc[k, :, l * LANES:(l + 1) * LANES] for l in range(SLAB_TILES)]
        hi = [st_sc[k, :, (SLAB_TILES + l) * LANES:(SLAB_TILES + l + 1) * LANES] for l in range(SLAB_TILES)]
        for piece in range(SCAN_PIECES):
            hr, hi, hh = scan_piece(k, piece, hr, hi, lam_r, lam_i)
            c_piece(k, piece, hh)
            if k + 1 < HALF_SLABS:
                bu_piece(k + 1, piece)
            fillers[k * SCAN_PIECES + piece]()
        st_sc[k] = jnp.concatenate(hr + hi, axis=1)

    for k in range(HALF_SLABS):
        for half in range(2):
            j = half * HALF_SLABS + k
            for b in range(n_seq):
                s = half * n_seq + b
                st_re_ref[b, j:j + 1, :] = st_sc[k, s:s + 1, 0:SLAB_STATES]
                st_im_ref[b, j:j + 1, :] = st_sc[k, s:s + 1, SLAB_STATES:2 * SLAB_STATES]

    for j in range(N_SLABS):
        cols = slice(j * LANES, (j + 1) * LANES)
        y = c_sc[:, cols] + dskip_ref[:, cols] * b_sc[:, cols]
        g = _dot(y.astype(BF16), gmat_ref[j])
        val = g[:, 0:LANES] + bglu_ref[0:1, cols]
        gate = g[:, LANES:2 * LANES] + bglu_ref[1:2, cols]
        oin_ref[:, D_A + j * LANES:D_A + (j + 1) * LANES] = (
            val * jax.nn.sigmoid(gate) * a_sc[:, cols]).astype(BF16)


def _const_spec(shape):
    zeros = (0,) * len(shape)
    return pl.BlockSpec(shape, lambda i: zeros, pipeline_mode=pl.Buffered(1))


def _mixer(x, mod, g_pre, w_in, ln_g, ln_b, w_s, bst, lam_re, lam_im, bmat, cboth, gmat,
           d_skip, b_glu_rows, h_sample, w_out):
    n_seq, seq, _ = x.shape
    assert 2 * n_seq == N_STREAMS and seq % CHUNK == 0
    rows = n_seq * CHUNK
    n_steps = seq // CHUNK
    wout_rows = w_out.shape[0] // n_steps
    assert wout_rows * n_steps == w_out.shape[0] and wout_rows % (2 * SUBLANES) == 0
    state_shape = (n_seq, N_SLABS, SLAB_STATES)
    n_sample = h_sample.shape[0]
    n_proj = D_IN // SAMPLE_PROJ_COLS
    assert n_proj <= n_steps
    consts = (mod, g_pre, w_in, ln_g, ln_b, w_s, bst, lam_re, lam_im, bmat, cboth, gmat,
              d_skip, b_glu_rows, h_sample)
    assert w_in.shape == (D_MODEL, D_IN) and D_MODEL % rows == 0 and D_IN % D_A == 0
    return pl.pallas_call(
        functools.partial(_mixer_kernel, n_seq=n_seq),
        grid=(n_steps,),
        in_specs=[pl.BlockSpec((n_seq, CHUNK, D_MODEL), lambda i: (0, i, 0))]
        + [pl.BlockSpec(memory_space=pl.ANY) if c is w_in else _const_spec(c.shape) for c in consts]
        + [pl.BlockSpec((wout_rows, w_out.shape[1]), lambda i: (i, 0))],
        out_specs=(
            pl.BlockSpec((rows, D_A + D_B), lambda i: (i, 0)),
            pl.BlockSpec(state_shape, lambda i: (0, 0, 0)),
            pl.BlockSpec(state_shape, lambda i: (0, 0, 0)),
            pl.BlockSpec((wout_rows, w_out.shape[1]), lambda i: (i, 0)),
            pl.BlockSpec((n_sample, SAMPLE_PROJ_COLS), lambda i: (0, jnp.minimum(i, n_proj - 1))),
        ),
        out_shape=(
            jax.ShapeDtypeStruct((n_steps * rows, D_A + D_B), BF16),
            jax.ShapeDtypeStruct(state_shape, F32),
            jax.ShapeDtypeStruct(state_shape, F32),
            jax.ShapeDtypeStruct(w_out.shape, BF16),
            jax.ShapeDtypeStruct((n_sample, D_IN), F32),
        ),
        scratch_shapes=[
            pltpu.VMEM((rows, D_MODEL), BF16),
            pltpu.VMEM((rows, D_A), F32),
            pltpu.VMEM((rows, D_A), F32),
            pltpu.VMEM((rows, D_A), F32),
            pltpu.VMEM((rows, D_A), BF16),
            pltpu.VMEM((rows, D_B), BF16),
            pltpu.VMEM((2 * SLAB_TILES, N_STREAMS * STREAM_PITCH, LANES), F32),
            pltpu.VMEM((2, N_STREAMS * CHUNK, LANES), F32),
            pltpu.VMEM((HALF_SLABS, N_STREAMS, 2 * SLAB_STATES), F32),
            pltpu.VMEM((D_MODEL, D_IN), BF16),
            pltpu.SemaphoreType.DMA((W_IN_RING,)),
        ],
        compiler_params=pltpu.CompilerParams(
            dimension_semantics=("arbitrary",), vmem_limit_bytes=VMEM_LIMIT),
        name="mixer",
    )(x, *consts, w_out)


def _outproj_kernel(oin_ref, x_ref, mod_ref, gpost_ref, w_out_ref, y_ref, *, n_seq):
    o = _dot(oin_ref[...], w_out_ref[...])
    r = o * _rms_scale(o)
    for b in range(n_seq):
        gain = gpost_ref[...] * mod_ref[b:b + 1, 2 * D_MODEL:3 * D_MODEL]
        y_ref[b] = x_ref[b] + r[b * CHUNK:(b + 1) * CHUNK, :] * gain


def _outproj(oin, x, mod, g_post, w_out_bf):
    n_seq, seq, _ = x.shape
    rows = n_seq * CHUNK
    return pl.pallas_call(
        functools.partial(_outproj_kernel, n_seq=n_seq),
        grid=(seq // CHUNK,),
        in_specs=[
            pl.BlockSpec((rows, D_A + D_B), lambda i: (i, 0)),
            pl.BlockSpec((n_seq, CHUNK, D_MODEL), lambda i: (0, i, 0)),
            _const_spec(mod.shape), _const_spec(g_post.shape), _const_spec(w_out_bf.shape),
        ],
        out_specs=pl.BlockSpec((n_seq, CHUNK, D_MODEL), lambda i: (0, i, 0)),
        out_shape=jax.ShapeDtypeStruct(x.shape, F32),
        compiler_params=pltpu.CompilerParams(
            dimension_semantics=("arbitrary",), vmem_limit_bytes=VMEM_LIMIT),
        name="outproj",
    )(oin, x, mod, g_post, w_out_bf)


def _sample_kernel(x2_ref, mod_ref, proj_ref, lng_ref, lnb_ref, ws_ref, bst_ref,
                   lam_re_ref, lam_im_ref, bmat_ref, cboth_ref, gmat_ref, dskip_ref, bglu_ref,
                   h0t_re_ref, h0t_im_ref, gpost_ref, w_out_ref,
                   y2_ref, v2_ref, hst_re_ref, hst_im_ref, oin_sc):
    n = proj_ref.shape[0]
    x_tiles = D_MODEL // LANES
    v_tiles = D_A // LANES
    w_u, w_v, w_z, w_xb, w_zb = 0, D_A, 2 * D_A, 3 * D_A, 3 * D_A + D_B

    v = _layernorm(proj_ref[:, w_v:w_v + D_A], lng_ref[...], lnb_ref[...])
    for q in range(v_tiles):
        v2_ref[pl.ds(q, n, stride=v_tiles), :] = v[:, q * LANES:(q + 1) * LANES]
    for hd in range(N_HEADS_A):
        cols = slice(hd * HEAD_A, (hd + 1) * HEAD_A)
        mix = ws_ref[hd, 0:1, 0:1] * v[:, cols] + bst_ref[0:1, hd:hd + 1]
        u = proj_ref[:, w_u + hd * HEAD_A:w_u + (hd + 1) * HEAD_A]
        z = proj_ref[:, w_z + hd * HEAD_A:w_z + (hd + 1) * HEAD_A]
        oin_sc[:, cols] = (u * mix * _silu(z)).astype(BF16)

    for j in range(N_SLABS):
        cols = slice(j * LANES, (j + 1) * LANES)
        st = slice(j * SLAB_STATES, (j + 1) * SLAB_STATES)
        xb = proj_ref[:, w_xb + j * LANES:w_xb + (j + 1) * LANES]
        zb = proj_ref[:, w_zb + j * LANES:w_zb + (j + 1) * LANES]
        bu = _dot(xb.astype(BF16), bmat_ref[j])
        lr = lam_re_ref[j:j + 1, :]
        li = lam_im_ref[j:j + 1, :]
        h0r = h0t_re_ref[st, :].T
        h0i = h0t_im_ref[st, :].T
        hr = lr * h0r - li * h0i + bu[:, 0:SLAB_STATES]
        hi = lr * h0i + li * h0r + bu[:, SLAB_STATES:2 * SLAB_STATES]
        hst_re_ref[st, :] = hr.T
        hst_im_ref[st, :] = hi.T
        k, half = j % HALF_SLABS, j // HALF_SLABS
        c_j = cboth_ref[k, :, half * LANES:(half + 1) * LANES]
        y = _dot(jnp.concatenate([hr, hi], axis=1).astype(BF16), c_j)
        y = y + dskip_ref[:, cols] * xb
        g = _dot(y.astype(BF16), gmat_ref[j])
        val = g[:, 0:LANES] + bglu_ref[0:1, cols]
        gt = g[:, LANES:2 * LANES] + bglu_ref[1:2, cols]
        oin_sc[:, D_A + j * LANES:D_A + (j + 1) * LANES] = (
            val * jax.nn.sigmoid(gt) * _silu(zb)).astype(BF16)

    o = _dot(oin_sc[...], w_out_ref[...])
    gate = mod_ref[:, 2 * D_MODEL:3 * D_MODEL]
    r = gate * _rms(o, gpost_ref[...])
    for q in range(x_tiles):
        cols = slice(q * LANES, (q + 1) * LANES)
        y2_ref[pl.ds(q, n, stride=x_tiles), :] = x2_ref[pl.ds(q, n, stride=x_tiles), :] + r[:, cols]


def _sample(x2, mod, proj, ln_g, ln_b, w_s, bst, lam_re, lam_im, bmat, cboth, gmat,
            d_skip, b_glu_rows, h0t_re, h0t_im, g_post, w_out_bf):
    n = proj.shape[0]
    n_state = N_GROUPS_B * P_STATE
    out_shapes = ((n * D_MODEL // LANES, LANES), (n * D_A // LANES, LANES), (n_state, n), (n_state, n))
    return pl.pallas_call(
        _sample_kernel,
        out_shape=tuple(jax.ShapeDtypeStruct(shp, F32) for shp in out_shapes),
        scratch_shapes=[pltpu.VMEM((n, D_A + D_B), BF16)],
        compiler_params=pltpu.CompilerParams(vmem_limit_bytes=VMEM_LIMIT),
        name="sample",
    )(x2, mod, proj, ln_g, ln_b, w_s, bst, lam_re, lam_im, bmat, cboth, gmat,
      d_skip, b_glu_rows, h0t_re, h0t_im, g_post, w_out_bf)


def kernel(x_prompt, x_sample, c_prompt, c_sample, state_b_re, state_b_im, w_c, b_c, g_pre, w_in,
           ln_v_g, ln_v_b, w_s, b_s, a_re, a_im, log_dt, b_re, b_im, c_re, c_im, d_skip, w_glu, b_glu,
           w_out, g_post):
    n_p = x_prompt.shape[0]
    n_s = x_sample.shape[0]
    assert x_sample.shape[1] == 1 and n_s % SUBLANES == 0

    row = lambda v: v.reshape(1, -1)
    x2_sample = x_sample.reshape(n_s * D_MODEL // LANES, LANES)
    prep_args, prep_out_shape = _s5_prep_operands(a_re, a_im, log_dt, b_re, b_im, c_re, c_im, w_glu, b_glu, b_s)
    (mod_s, mod_p, h_sample, bmat, cboth, gmat, lam_re, lam_im, b_glu_rows, bst) = _adaln(
        c_sample, c_prompt, w_c, b_c, x2_sample, row(g_pre), prep_args, prep_out_shape)

    s5 = (w_s, bst, lam_re, lam_im, bmat, cboth, gmat, row(d_skip), b_glu_rows)
    oin, st_re, st_im, w_out_bf, proj_s = _mixer(
        x_prompt, mod_p, row(g_pre), w_in, row(ln_v_g), row(ln_v_b), *s5, h_sample, w_out)
    y_prompt = _outproj(oin, x_prompt, mod_p, row(g_post), w_out_bf)

    to_rows = lambda st: st.transpose(1, 2, 0).reshape(N_GROUPS_B * P_STATE, n_s)
    from_rows = lambda st: st.reshape(N_GROUPS_B, P_STATE, n_s).transpose(2, 0, 1)
    y_s, v_s, hs_re, hs_im = _sample(
        x2_sample, mod_s, proj_s, row(ln_v_g), row(ln_v_b), *s5,
        to_rows(state_b_re), to_rows(state_b_im), row(g_post), w_out_bf)

    return (y_prompt, y_s.reshape(n_s, 1, D_MODEL), v_s.reshape(n_s, 1, D_A),
            st_re.reshape(n_p, N_GROUPS_B, P_STATE), st_im.reshape(n_p, N_GROUPS_B, P_STATE),
            from_rows(hs_re), from_rows(hs_im))
```

```python
import functools

import jax
import jax.numpy as jnp
from jax import lax
from jax.experimental import pallas as pl
from jax.experimental.pallas import tpu as pltpu

F32 = jnp.float32
BF16 = jnp.bfloat16

EPS = 1e-6
D_MODEL = 2048
D_A = 1024
D_B = 1024
D_IN = 3 * D_A + 2 * D_B
CHUNK = 128
HEAD_A = 128
N_HEADS_A = D_A // HEAD_A
GROUP_B = 16
N_GROUPS_B = D_B // GROUP_B
P_STATE = 64

LANES = 128
SUBLANES = 8
SLAB_GROUPS = LANES // GROUP_B
N_SLABS = N_GROUPS_B // SLAB_GROUPS
SLAB_STATES = SLAB_GROUPS * P_STATE
SLAB_TILES = SLAB_STATES // LANES
N_STREAMS = SUBLANES
HALF_SLABS = N_SLABS // 2
STREAM_PITCH = CHUNK + SUBLANES
SCAN_PIECES = 4
PIECE = CHUNK // SCAN_PIECES
V7X_VMEM_BYTES = 64 * 1024 * 1024
VMEM_LIMIT = V7X_VMEM_BYTES - 1024 * 1024
SAMPLE_PROJ_COLS = 1024
W_IN_RING = 6


def _silu(x):
    return x * jax.nn.sigmoid(x)


def _rms_scale(x):
    return lax.rsqrt(jnp.mean(x * x, axis=-1, keepdims=True) + EPS)


def _rms(x, g):
    return x * _rms_scale(x) * g


def _layernorm(x, g, b):
    mu = jnp.mean(x, axis=-1, keepdims=True)
    xc = x - mu
    return xc * lax.rsqrt(jnp.mean(xc * xc, axis=-1, keepdims=True) + EPS) * g + b


def _dot(a, b):
    return jnp.dot(a, b, preferred_element_type=F32)


def _discretize(a_re, a_im, log_dt):
    dt = jnp.exp(log_dt)
    mag = jnp.exp(dt * a_re)
    abar_re = mag * jnp.cos(dt * a_im)
    abar_im = mag * jnp.sin(dt * a_im)
    return abar_re, abar_im


def _nt_dot(a, b):
    return lax.dot_general(a, b, (((1,), (1,)), ((), ())), preferred_element_type=F32)


def _bf16_terms(x):
    hi = x.astype(BF16)
    r1 = x - hi.astype(F32)
    mid = r1.astype(BF16)
    lo = (r1 - mid.astype(F32)).astype(BF16)
    return hi, mid, lo


def _prep_kernel(a8_ref, bt_re_ref, bt_im_ref, c_re_ref, c_im_ref, wglu_ref, bglu_t_ref, bs_ref,
                 bmat_ref, cboth_ref, gmat_ref, lam_re_ref, lam_im_ref, bglu_row_ref, bst_ref):
    a_re = a8_ref[0:N_SLABS, :]
    a_im = a8_ref[N_SLABS:2 * N_SLABS, :]
    abar_re, abar_im = _discretize(a_re, a_im, a8_ref[2 * N_SLABS:3 * N_SLABS, :])
    lam_re_ref[...] = abar_re
    lam_im_ref[...] = abar_im
    num_re = abar_re - 1.0
    num_im = abar_im
    den = a_re * a_re + a_im * a_im
    coef_re = (num_re * a_re + num_im * a_im) / den
    coef_im = (num_im * a_re - num_re * a_im) / den

    def rep_matrix(k, n, period, offset=0):
        row = lax.broadcasted_iota(jnp.int32, (k, n), 0)
        col = lax.broadcasted_iota(jnp.int32, (k, n), 1)
        return jnp.where((col & (period - 1)) + offset == row, 1.0, 0.0).astype(BF16)

    def rep_rows(m, k, period):
        row = lax.broadcasted_iota(jnp.int32, (m, k), 0)
        col = lax.broadcasted_iota(jnp.int32, (m, k), 1)
        return jnp.where((row & (period - 1)) == col, 1.0, 0.0).astype(BF16)

    def block_mask(m, n, row_shift, col_shift):
        row = lax.broadcasted_iota(jnp.int32, (m, n), 0)
        col = lax.broadcasted_iota(jnp.int32, (m, n), 1)
        return (row >> row_shift) == (col >> col_shift)

    rep_state = rep_matrix(P_STATE, SLAB_STATES, P_STATE)
    rows_state = rep_rows(SLAB_STATES, P_STATE, P_STATE)
    rep_val = rep_matrix(2 * GROUP_B, LANES, GROUP_B)
    rep_gate = rep_matrix(2 * GROUP_B, LANES, GROUP_B, GROUP_B)
    mask_b = block_mask(LANES, SLAB_STATES, 4, 6)
    mask_c = block_mask(SLAB_STATES, LANES, 6, 4)
    mask_g = block_mask(LANES, LANES, 4, 4)

    def spread(x, rep):
        return sum(_dot(term, rep) for term in _bf16_terms(x))

    for j in range(N_SLABS):
        crows = slice(j * LANES, (j + 1) * LANES)
        bt_re = spread(bt_re_ref[crows, :], rep_state)
        bt_im = spread(bt_im_ref[crows, :], rep_state)
        cr = coef_re[j:j + 1, :]
        ci = coef_im[j:j + 1, :]
        bmat_ref[j, :, 0:SLAB_STATES] = jnp.where(mask_b, cr * bt_re - ci * bt_im, 0.0).astype(BF16)
        bmat_ref[j, :, SLAB_STATES:2 * SLAB_STATES] = jnp.where(
            mask_b, cr * bt_im + ci * bt_re, 0.0).astype(BF16)

        k, half = j % HALF_SLABS, j // HALF_SLABS
        ccols = slice(half * LANES, (half + 1) * LANES)
        ct_re = _nt_dot(rows_state, c_re_ref[crows, :].astype(BF16))
        ct_im = _nt_dot(rows_state, c_im_ref[crows, :].astype(BF16))
        cboth_ref[k, 0:SLAB_STATES, ccols] = jnp.where(mask_c, ct_re, 0.0).astype(BF16)
        cboth_ref[k, SLAB_STATES:2 * SLAB_STATES, ccols] = jnp.where(mask_c, -ct_im, 0.0).astype(BF16)

        w = wglu_ref[crows, :].astype(BF16)
        gmat_ref[j, :, 0:LANES] = jnp.where(mask_g, _dot(w, rep_val), 0.0).astype(BF16)
        gmat_ref[j, :, LANES:2 * LANES] = jnp.where(mask_g, _dot(w, rep_gate), 0.0).astype(BF16)

    row = lax.broadcasted_iota(jnp.int32, (2 * GROUP_B, D_B), 0)
    col = lax.broadcasted_iota(jnp.int32, (2 * GROUP_B, D_B), 1)
    by_group = spread(bglu_t_ref[...], jnp.where(block_mask(N_GROUPS_B, D_B, 0, 4), 1.0, 0.0).astype(BF16))
    for i in range(2):
        mine = row == (col & (GROUP_B - 1)) + i * GROUP_B
        bglu_row_ref[i:i + 1, :] = jnp.sum(jnp.where(mine, by_group, 0.0), axis=0, keepdims=True)

    eye = rep_rows(CHUNK, CHUNK, CHUNK)
    bst_ref[...] = sum(_nt_dot(eye, term) for term in _bf16_terms(bs_ref[...]))


def _s5_prep_operands(a_re, a_im, log_dt, b_re, b_im, c_re, c_im, w_glu, b_glu, b_s):
    g, p, c = N_GROUPS_B, P_STATE, GROUP_B
    a8 = jnp.stack([a_re, a_im, jnp.broadcast_to(log_dt[:, None], (g, p))]).reshape(3 * N_SLABS, SLAB_STATES)
    args = (a8, b_re.transpose(0, 2, 1).reshape(g * c, p), b_im.transpose(0, 2, 1).reshape(g * c, p),
            c_re.reshape(g * c, p), c_im.reshape(g * c, p), w_glu.reshape(g * c, 2 * c), b_glu.T, b_s)
    out_shape = (
        jax.ShapeDtypeStruct((N_SLABS, LANES, 2 * SLAB_STATES), BF16),
        jax.ShapeDtypeStruct((HALF_SLABS, 2 * SLAB_STATES, 2 * LANES), BF16),
        jax.ShapeDtypeStruct((N_SLABS, LANES, 2 * LANES), BF16),
        jax.ShapeDtypeStruct((N_SLABS, SLAB_STATES), F32),
        jax.ShapeDtypeStruct((N_SLABS, SLAB_STATES), F32),
        jax.ShapeDtypeStruct((2, D_B), F32),
        jax.ShapeDtypeStruct((CHUNK, N_HEADS_A), F32),
    )
    return args, out_shape


ADALN_ROWS = 512
ADALN_PROMPT_ROWS = 2 * SUBLANES


N_PREP_IN, N_PREP_OUT = 8, 7


def _adaln_kernel(cs_ref, cp_ref, w_ref, b_ref, x2_ref, gpre_ref, *refs):
    prep_in = refs[:N_PREP_IN]
    os_ref, op_ref, hs_ref = refs[N_PREP_IN:N_PREP_IN + 3]
    prep_out = refs[N_PREP_IN + 3:]
    assert len(prep_out) == N_PREP_OUT
    k = pl.program_id(0)

    @pl.when(k == 0)
    def _():
        os_ref[...] = jnp.broadcast_to(b_ref[...], os_ref.shape)
        op_ref[...] = jnp.broadcast_to(b_ref[...], op_ref.shape)
        _prep_kernel(*prep_in, *prep_out)

    cols = pl.ds(pl.multiple_of(k * ADALN_ROWS, ADALN_ROWS), ADALN_ROWS)
    w = w_ref[...].astype(BF16)
    os_ref[...] += _dot(_silu(cs_ref[:, cols]).astype(BF16), w)
    n_p = cp_ref.shape[0]
    cp = jnp.concatenate([cp_ref[:, cols], jnp.zeros((ADALN_PROMPT_ROWS - n_p, ADALN_ROWS), F32)], axis=0)
    op_ref[...] += _dot(_silu(cp).astype(BF16), w)

    @pl.when(k == pl.num_programs(0) - 1)
    def _():
        x_tiles = D_MODEL // LANES
        n = hs_ref.shape[0]
        x = jnp.concatenate([x2_ref[pl.ds(q, n, stride=x_tiles), :] for q in range(x_tiles)], axis=1)
        shift = os_ref[:, 0:D_MODEL]
        scale = os_ref[:, D_MODEL:2 * D_MODEL]
        hs_ref[...] = (_rms(x, gpre_ref[...]) * (1.0 + scale) + shift).astype(BF16)


def _adaln(c_sample, c_prompt, w_c, b_c, x2_sample, g_pre, prep_args, prep_out_shape):
    n_s, n_p = c_sample.shape[0], c_prompt.shape[0]
    assert n_p <= ADALN_PROMPT_ROWS and D_MODEL % ADALN_ROWS == 0
    assert len(prep_args) == N_PREP_IN and len(prep_out_shape) == N_PREP_OUT
    whole = lambda a: pl.BlockSpec(a.shape, lambda k, nd=len(a.shape): (0,) * nd)
    return pl.pallas_call(
        _adaln_kernel,
        grid=(D_MODEL // ADALN_ROWS,),
        in_specs=[
            pl.BlockSpec((n_s, D_MODEL), lambda k: (0, 0)),
            pl.BlockSpec((n_p, D_MODEL), lambda k: (0, 0)),
            pl.BlockSpec((ADALN_ROWS, 3 * D_MODEL), lambda k: (k, 0)),
            pl.BlockSpec((1, 3 * D_MODEL), lambda k: (0, 0)),
            pl.BlockSpec(x2_sample.shape, lambda k: (0, 0)),
            pl.BlockSpec(g_pre.shape, lambda k: (0, 0)),
        ] + [whole(a) for a in prep_args],
        out_specs=(pl.BlockSpec((n_s, 3 * D_MODEL), lambda k: (0, 0)),
                   pl.BlockSpec((ADALN_PROMPT_ROWS, 3 * D_MODEL), lambda k: (0, 0)),
                   pl.BlockSpec((n_s, D_MODEL), lambda k: (0, 0))) + tuple(whole(a) for a in prep_out_shape),
        out_shape=(jax.ShapeDtypeStruct((n_s, 3 * D_MODEL), F32),
                   jax.ShapeDtypeStruct((ADALN_PROMPT_ROWS, 3 * D_MODEL), F32),
                   jax.ShapeDtypeStruct((n_s, D_MODEL), BF16)) + tuple(prep_out_shape),
        compiler_params=pltpu.CompilerParams(
            dimension_semantics=("arbitrary",), vmem_limit_bytes=VMEM_LIMIT),
        name="adaln",
    )(c_sample, c_prompt, w_c, b_c.reshape(1, -1), x2_sample, g_pre, *prep_args)


def _mixer_kernel(x_ref, mod_ref, gpre_ref, w_in_hbm, lng_ref, lnb_ref, ws_ref, bst_ref,
                  lam_re_ref, lam_im_ref, bmat_ref, cboth_ref, gmat_ref, dskip_ref, bglu_ref, hs_ref, wout_ref,
                  oin_ref, st_re_ref, st_im_ref, wout_bf_ref, ps_ref,
                  h_sc, a_sc, b_sc, c_sc, v_sc, xb_sc, bu_sc, y2_sc, st_sc, w_in_ref, w_sem, *, n_seq):
    step = pl.program_id(0)

    @pl.when(step == 0)
    def _():
        st_sc[...] = jnp.zeros_like(st_sc)

        t_rows, t_cols = a_sc.shape[0] // 2, a_sc.shape[1]
        stage = [buf.at[pl.ds(i * t_rows, t_rows), :] for buf in (a_sc, b_sc, c_sc) for i in range(2)]
        assert len(stage) == W_IN_RING
        tiles = [(r, c) for c in range(D_IN // t_cols) for r in range(D_MODEL // t_rows)]

        def tile_copy(n):
            r, c = tiles[n]
            src = w_in_hbm.at[pl.ds(r * t_rows, t_rows), pl.ds(c * t_cols, t_cols)]
            return pltpu.make_async_copy(src, stage[n % len(stage)], w_sem.at[n % len(stage)])

        for n in range(len(stage)):
            tile_copy(n).start()
        for n, (r, c) in enumerate(tiles):
            tile_copy(n).wait()
            w_in_ref[r * t_rows:(r + 1) * t_rows, c * t_cols:(c + 1) * t_cols] = (
                stage[n % len(stage)][...].astype(BF16))
            if n + len(stage) < len(tiles):
                tile_copy(n + len(stage)).start()

    wout_bf_ref[...] = wout_ref[...].astype(BF16)

    @pl.when(step < D_IN // SAMPLE_PROJ_COLS)
    def _():
        cols = pl.ds(pl.multiple_of(step * SAMPLE_PROJ_COLS, SAMPLE_PROJ_COLS), SAMPLE_PROJ_COLS)
        ps_ref[...] = _dot(hs_ref[...], w_in_ref[:, cols])

    w_u, w_v, w_z, w_xb, w_zb = 0, D_A, 2 * D_A, 3 * D_A, 3 * D_A + D_B

    for b in range(n_seq):
        x = x_ref[b]
        shift = mod_ref[b:b + 1, 0:D_MODEL]
        gain = gpre_ref[...] * (1.0 + mod_ref[b:b + 1, D_MODEL:2 * D_MODEL])
        rws = slice(b * CHUNK, (b + 1) * CHUNK)
        h_sc[rws, :] = (x * _rms_scale(x) * gain + shift).astype(BF16)
        b_sc[rws, :] = _dot(h_sc[rws, :], w_in_ref[:, w_xb:w_xb + D_B])
        xb_sc[rws, :] = b_sc[rws, :].astype(BF16)

    h = h_sc[...]

    tril = (lax.broadcasted_iota(jnp.int32, (CHUNK, CHUNK), 0)
            >= lax.broadcasted_iota(jnp.int32, (CHUNK, CHUNK), 1))
    n_fill = HALF_SLABS * SCAN_PIECES // 4
    fill_cols = D_A // n_fill

    def fill_u(q):
        cols = slice(q * fill_cols, (q + 1) * fill_cols)
        a_sc[:, cols] = _dot(h, w_in_ref[:, w_u + q * fill_cols:w_u + (q + 1) * fill_cols])

    def fill_z(q):
        cols = slice(q * fill_cols, (q + 1) * fill_cols)
        z = _dot(h, w_in_ref[:, w_z + q * fill_cols:w_z + (q + 1) * fill_cols])
        a_sc[:, cols] = a_sc[:, cols] * _silu(z)

    def fill_mix(q):
        heads_per_fill = N_HEADS_A // n_fill
        for hd in range(q * heads_per_fill, (q + 1) * heads_per_fill):
            cols = slice(hd * HEAD_A, (hd + 1) * HEAD_A)
            w_t = jnp.where(tril, ws_ref[hd], 0.0).astype(BF16)
            v_h = jnp.concatenate([v_sc[b * CHUNK:(b + 1) * CHUNK, cols] for b in range(n_seq)], axis=1)
            mix = _dot(w_t, v_h) + bst_ref[:, hd:hd + 1]
            for b in range(n_seq):
                rws = slice(b * CHUNK, (b + 1) * CHUNK)
                oin_ref[rws, cols] = (a_sc[rws, cols] * mix[:, b * HEAD_A:(b + 1) * HEAD_A]).astype(BF16)

    def fill_zb(q):
        cols = slice(q * fill_cols, (q + 1) * fill_cols)
        a_sc[:, cols] = _silu(_dot(h, w_in_ref[:, w_zb + q * fill_cols:w_zb + (q + 1) * fill_cols]))

    fillers = [functools.partial(f, q) for f, q in (
        (fill_u, 0), (fill_z, 0), (fill_u, 1), (fill_mix, 0), (fill_z, 1), (fill_u, 2), (fill_z, 2), (fill_mix, 1),
        (fill_u, 3), (fill_z, 3), (fill_zb, 0), (fill_mix, 2), (fill_zb, 1), (fill_zb, 2), (fill_mix, 3), (fill_zb, 3))]
    assert len(fillers) == HALF_SLABS * SCAN_PIECES

    first_half = lax.broadcasted_iota(jnp.int32, (N_STREAMS, LANES), 0) < n_seq

    def bu_piece(k, piece):
        t0 = piece * PIECE
        for half in range(2):
            j = half * HALF_SLABS + k
            lhs = jnp.concatenate(
                [xb_sc[b * CHUNK + t0:b * CHUNK + t0 + PIECE, j * LANES:(j + 1) * LANES]
                 for b in range(n_seq)], axis=0)
            bu = _dot(lhs, bmat_ref[j])
            for b in range(n_seq):
                r0 = (half * n_seq + b) * STREAM_PITCH + t0
                for l in range(2 * SLAB_TILES):
                    bu_sc[l, r0:r0 + PIECE, :] = bu[b * PIECE:(b + 1) * PIECE, l * LANES:(l + 1) * LANES]

    def scan_piece(k, piece, hr, hi, lam_r, lam_i):
        rows = []
        for t in range(piece * PIECE, (piece + 1) * PIECE):
            new_r, new_i = [], []
            for l in range(SLAB_TILES):
                br = bu_sc[l, pl.ds(t, N_STREAMS, stride=STREAM_PITCH), :]
                bi = bu_sc[SLAB_TILES + l, pl.ds(t, N_STREAMS, stride=STREAM_PITCH), :]
                new_r.append(lam_r[l] * hr[l] - lam_i[l] * hi[l] + br)
                new_i.append(lam_r[l] * hi[l] + lam_i[l] * hr[l] + bi)
            hr, hi = new_r, new_i
            rows.append(jnp.concatenate(hr + hi, axis=1))
        return hr, hi, jnp.concatenate(rows, axis=0)

    def c_piece(k, piece, hh):
        t0 = piece * PIECE
        rows = slice(t0 * N_STREAMS, (t0 + PIECE) * N_STREAMS)
        y2 = _dot(hh.astype(BF16), cboth_ref[k])
        y2_sc[0, rows, :] = y2[:, 0:LANES]
        y2_sc[1, rows, :] = y2[:, LANES:2 * LANES]
        for half in range(2):
            j = half * HALF_SLABS + k
            for b in range(n_seq):
                s = half * n_seq + b
                c_sc[b * CHUNK + t0:b * CHUNK + t0 + PIECE, j * LANES:(j + 1) * LANES] = (
                    y2_sc[half, pl.ds(t0 * N_STREAMS + s, PIECE, stride=N_STREAMS), :])

    for piece in range(SCAN_PIECES):
        bu_piece(0, piece)
    c_sc[...] = _dot(h, w_in_ref[:, w_v:w_v + D_A])
    v_sc[...] = _layernorm(c_sc[...], lng_ref[...], lnb_ref[...]).astype(BF16)
    for k in range(HALF_SLABS):
        tiles = [slice(l * LANES, (l + 1) * LANES) for l in range(SLAB_TILES)]
        lam_r = [jnp.where(first_half, lam_re_ref[k:k + 1, t], lam_re_ref[HALF_SLABS + k:HALF_SLABS + k + 1, t])
                 for t in tiles]
        lam_i = [jnp.where(first_half, lam_im_ref[k:k + 1, t], lam_im_ref[HALF_SLABS + k:HALF_SLABS + k + 1, t])
                 for t in tiles]
        hr = [st_sc[k, :, l * LANES:(l + 1) * LANES] for l in range(SLAB_TILES)]
        hi = [st_sc[k, :, (SLAB_TILES + l) * LANES:(SLAB_TILES + l + 1) * LANES] for l in range(SLAB_TILES)]
        for piece in range(SCAN_PIECES):
            hr, hi, hh = scan_piece(k, piece, hr, hi, lam_r, lam_i)
            c_piece(k, piece, hh)
            if k + 1 < HALF_SLABS:
                bu_piece(k + 1, piece)
            fillers[k * SCAN_PIECES + piece]()
        st_sc[k] = jnp.concatenate(hr + hi, axis=1)

    for k in range(HALF_SLABS):
        for half in range(2):
            j = half * HALF_SLABS + k
            for b in range(n_seq):
                s = half * n_seq + b
                st_re_ref[b, j:j + 1, :] = st_sc[k, s:s + 1, 0:SLAB_STATES]
                st_im_ref[b, j:j + 1, :] = st_sc[k, s:s + 1, SLAB_STATES:2 * SLAB_STATES]

    for j in range(N_SLABS):
        cols = slice(j * LANES, (j + 1) * LANES)
        y = c_sc[:, cols] + dskip_ref[:, cols] * b_sc[:, cols]
        g = _dot(y.astype(BF16), gmat_ref[j])
        val = g[:, 0:LANES] + bglu_ref[0:1, cols]
        gate = g[:, LANES:2 * LANES] + bglu_ref[1:2, cols]
        oin_ref[:, D_A + j * LANES:D_A + (j + 1) * LANES] = (
            val * jax.nn.sigmoid(gate) * a_sc[:, cols]).astype(BF16)


def _const_spec(shape):
    zeros = (0,) * len(shape)
    return pl.BlockSpec(shape, lambda i: zeros, pipeline_mode=pl.Buffered(1))


def _mixer(x, mod, g_pre, w_in, ln_g, ln_b, w_s, bst, lam_re, lam_im, bmat, cboth, gmat,
           d_skip, b_glu_rows, h_sample, w_out):
    n_seq, seq, _ = x.shape
    assert 2 * n_seq == N_STREAMS and seq % CHUNK == 0
    rows = n_seq * CHUNK
    n_steps = seq // CHUNK
    wout_rows = w_out.shape[0] // n_steps
    assert wout_rows * n_steps == w_out.shape[0] and wout_rows % (2 * SUBLANES) == 0
    state_shape = (n_seq, N_SLABS, SLAB_STATES)
    n_sample = h_sample.shape[0]
    n_proj = D_IN // SAMPLE_PROJ_COLS
    assert n_proj <= n_steps
    consts = (mod, g_pre, w_in, ln_g, ln_b, w_s, bst, lam_re, lam_im, bmat, cboth, gmat,
              d_skip, b_glu_rows, h_sample)
    assert w_in.shape == (D_MODEL, D_IN) and D_MODEL % rows == 0 and D_IN % D_A == 0
    return pl.pallas_call(
        functools.partial(_mixer_kernel, n_seq=n_seq),
        grid=(n_steps,),
        in_specs=[pl.BlockSpec((n_seq, CHUNK, D_MODEL), lambda i: (0, i, 0))]
        + [pl.BlockSpec(memory_space=pl.ANY) if c is w_in else _const_spec(c.shape) for c in consts]
        + [pl.BlockSpec((wout_rows, w_out.shape[1]), lambda i: (i, 0))],
        out_specs=(
            pl.BlockSpec((rows, D_A + D_B), lambda i: (i, 0)),
            pl.BlockSpec(state_shape, lambda i: (0, 0, 0)),
            pl.BlockSpec(state_shape, lambda i: (0, 0, 0)),
            pl.BlockSpec((wout_rows, w_out.shape[1]), lambda i: (i, 0)),
            pl.BlockSpec((n_sample, SAMPLE_PROJ_COLS), lambda i: (0, jnp.minimum(i, n_proj - 1))),
        ),
        out_shape=(
            jax.ShapeDtypeStruct((n_steps * rows, D_A + D_B), BF16),
            jax.ShapeDtypeStruct(state_shape, F32),
            jax.ShapeDtypeStruct(state_shape, F32),
            jax.ShapeDtypeStruct(w_out.shape, BF16),
            jax.ShapeDtypeStruct((n_sample, D_IN), F32),
        ),
        scratch_shapes=[
            pltpu.VMEM((rows, D_MODEL), BF16),
            pltpu.VMEM((rows, D_A), F32),
            pltpu.VMEM((rows, D_A), F32),
            pltpu.VMEM((rows, D_A), F32),
            pltpu.VMEM((rows, D_A), BF16),
            pltpu.VMEM((rows, D_B), BF16),
            pltpu.VMEM((2 * SLAB_TILES, N_STREAMS * STREAM_PITCH, LANES), F32),
            pltpu.VMEM((2, N_STREAMS * CHUNK, LANES), F32),
            pltpu.VMEM((HALF_SLABS, N_STREAMS, 2 * SLAB_STATES), F32),
            pltpu.VMEM((D_MODEL, D_IN), BF16),
            pltpu.SemaphoreType.DMA((W_IN_RING,)),
        ],
        compiler_params=pltpu.CompilerParams(
            dimension_semantics=("arbitrary",), vmem_limit_bytes=VMEM_LIMIT),
        name="mixer",
    )(x, *consts, w_out)


def _outproj_kernel(oin_ref, x_ref, mod_ref, gpost_ref, w_out_ref, oin_s_ref, x2_ref, mod_s_ref,
                    y_ref, y2_ref, *, n_seq, n_tiles):
    i = pl.program_id(0)

    @pl.when(i < n_tiles)
    def _():
        o = _dot(oin_ref[...], w_out_ref[...])
        r = o * _rms_scale(o)
        for b in range(n_seq):
            gain = gpost_ref[...] * mod_ref[b:b + 1, 2 * D_MODEL:3 * D_MODEL]
            y_ref[b] = x_ref[b] + r[b * CHUNK:(b + 1) * CHUNK, :] * gain

    @pl.when(i == n_tiles)
    def _():
        n = oin_s_ref.shape[0]
        x_tiles = D_MODEL // LANES
        o = _dot(oin_s_ref[...], w_out_ref[...])
        r = mod_s_ref[:, 2 * D_MODEL:3 * D_MODEL] * _rms(o, gpost_ref[...])
        for q in range(x_tiles):
            cols = slice(q * LANES, (q + 1) * LANES)
            y2_ref[pl.ds(q, n, stride=x_tiles), :] = x2_ref[pl.ds(q, n, stride=x_tiles), :] + r[:, cols]


def _outproj(oin, x, mod, g_post, w_out_bf, oin_sample, x2_sample, mod_sample):
    n_seq, seq, _ = x.shape
    rows = n_seq * CHUNK
    n_tiles = seq // CHUNK
    tile = lambda i: jnp.minimum(i, n_tiles - 1)
    return pl.pallas_call(
        functools.partial(_outproj_kernel, n_seq=n_seq, n_tiles=n_tiles),
        grid=(n_tiles + 1,),
        in_specs=[
            pl.BlockSpec((rows, D_A + D_B), lambda i: (tile(i), 0)),
            pl.BlockSpec((n_seq, CHUNK, D_MODEL), lambda i: (0, tile(i), 0)),
            _const_spec(mod.shape), _const_spec(g_post.shape), _const_spec(w_out_bf.shape),
            _const_spec(oin_sample.shape), _const_spec(x2_sample.shape), _const_spec(mod_sample.shape),
        ],
        out_specs=(pl.BlockSpec((n_seq, CHUNK, D_MODEL), lambda i: (0, tile(i), 0)),
                   pl.BlockSpec(x2_sample.shape, lambda i: (0, 0))),
        out_shape=(jax.ShapeDtypeStruct(x.shape, F32), jax.ShapeDtypeStruct(x2_sample.shape, F32)),
        compiler_params=pltpu.CompilerParams(
            dimension_semantics=("arbitrary",), vmem_limit_bytes=VMEM_LIMIT),
        name="outproj",
    )(oin, x, mod, g_post, w_out_bf, oin_sample, x2_sample, mod_sample)


def _sample_kernel(proj_ref, lng_ref, lnb_ref, ws_ref, bst_ref,
                   lam_re_ref, lam_im_ref, bmat_ref, cboth_ref, gmat_ref, dskip_ref, bglu_ref,
                   h0t_re_ref, h0t_im_ref,
                   oin_sc, v2_ref, hst_re_ref, hst_im_ref):
    n = proj_ref.shape[0]
    v_tiles = D_A // LANES
    w_u, w_v, w_z, w_xb, w_zb = 0, D_A, 2 * D_A, 3 * D_A, 3 * D_A + D_B

    v = _layernorm(proj_ref[:, w_v:w_v + D_A], lng_ref[...], lnb_ref[...])
    for q in range(v_tiles):
        v2_ref[pl.ds(q, n, stride=v_tiles), :] = v[:, q * LANES:(q + 1) * LANES]
    for hd in range(N_HEADS_A):
        cols = slice(hd * HEAD_A, (hd + 1) * HEAD_A)
        mix = ws_ref[hd, 0:1, 0:1] * v[:, cols] + bst_ref[0:1, hd:hd + 1]
        u = proj_ref[:, w_u + hd * HEAD_A:w_u + (hd + 1) * HEAD_A]
        z = proj_ref[:, w_z + hd * HEAD_A:w_z + (hd + 1) * HEAD_A]
        oin_sc[:, cols] = (u * mix * _silu(z)).astype(BF16)

    for j in range(N_SLABS):
        cols = slice(j * LANES, (j + 1) * LANES)
        st = slice(j * SLAB_STATES, (j + 1) * SLAB_STATES)
        xb = proj_ref[:, w_xb + j * LANES:w_xb + (j + 1) * LANES]
        zb = proj_ref[:, w_zb + j * LANES:w_zb + (j + 1) * LANES]
        bu = _dot(xb.astype(BF16), bmat_ref[j])
        lr = lam_re_ref[j:j + 1, :]
        li = lam_im_ref[j:j + 1, :]
        h0r = h0t_re_ref[st, :].T
        h0i = h0t_im_ref[st, :].T
        hr = lr * h0r - li * h0i + bu[:, 0:SLAB_STATES]
        hi = lr * h0i + li * h0r + bu[:, SLAB_STATES:2 * SLAB_STATES]
        hst_re_ref[st, :] = hr.T
        hst_im_ref[st, :] = hi.T
        k, half = j % HALF_SLABS, j // HALF_SLABS
        c_j = cboth_ref[k, :, half * LANES:(half + 1) * LANES]
        y = _dot(jnp.concatenate([hr, hi], axis=1).astype(BF16), c_j)
        y = y + dskip_ref[:, cols] * xb
        g = _dot(y.astype(BF16), gmat_ref[j])
        val = g[:, 0:LANES] + bglu_ref[0:1, cols]
        gt = g[:, LANES:2 * LANES] + bglu_ref[1:2, cols]
        oin_sc[:, D_A + j * LANES:D_A + (j + 1) * LANES] = (
            val * jax.nn.sigmoid(gt) * _silu(zb)).astype(BF16)


def _sample(proj, ln_g, ln_b, w_s, bst, lam_re, lam_im, bmat, cboth, gmat,
            d_skip, b_glu_rows, h0t_re, h0t_im):
    n = proj.shape[0]
    n_state = N_GROUPS_B * P_STATE
    return pl.pallas_call(
        _sample_kernel,
        out_shape=(jax.ShapeDtypeStruct((n, D_A + D_B), BF16),
                   jax.ShapeDtypeStruct((n * D_A // LANES, LANES), F32),
                   jax.ShapeDtypeStruct((n_state, n), F32),
                   jax.ShapeDtypeStruct((n_state, n), F32)),
        compiler_params=pltpu.CompilerParams(vmem_limit_bytes=VMEM_LIMIT),
        name="sample",
    )(proj, ln_g, ln_b, w_s, bst, lam_re, lam_im, bmat, cboth, gmat,
      d_skip, b_glu_rows, h0t_re, h0t_im)


def kernel(x_prompt, x_sample, c_prompt, c_sample, state_b_re, state_b_im, w_c, b_c, g_pre, w_in,
           ln_v_g, ln_v_b, w_s, b_s, a_re, a_im, log_dt, b_re, b_im, c_re, c_im, d_skip, w_glu, b_glu,
           w_out, g_post):
    n_p = x_prompt.shape[0]
    n_s = x_sample.shape[0]
    assert x_sample.shape[1] == 1 and n_s % SUBLANES == 0

    row = lambda v: v.reshape(1, -1)
    x2_sample = x_sample.reshape(n_s * D_MODEL // LANES, LANES)
    prep_args, prep_out_shape = _s5_prep_operands(a_re, a_im, log_dt, b_re, b_im, c_re, c_im, w_glu, b_glu, b_s)
    (mod_s, mod_p, h_sample, bmat, cboth, gmat, lam_re, lam_im, b_glu_rows, bst) = _adaln(
        c_sample, c_prompt, w_c, b_c, x2_sample, row(g_pre), prep_args, prep_out_shape)

    s5 = (w_s, bst, lam_re, lam_im, bmat, cboth, gmat, row(d_skip), b_glu_rows)
    oin, st_re, st_im, w_out_bf, proj_s = _mixer(
        x_prompt, mod_p, row(g_pre), w_in, row(ln_v_g), row(ln_v_b), *s5, h_sample, w_out)
    to_rows = lambda st: st.transpose(1, 2, 0).reshape(N_GROUPS_B * P_STATE, n_s)
    from_rows = lambda st: st.reshape(N_GROUPS_B, P_STATE, n_s).transpose(2, 0, 1)
    oin_s, v_s, hs_re, hs_im = _sample(
        proj_s, row(ln_v_g), row(ln_v_b), *s5, to_rows(state_b_re), to_rows(state_b_im))
    y_prompt, y_s = _outproj(oin, x_prompt, mod_p, row(g_post), w_out_bf, oin_s, x2_sample, mod_s)

    return (y_prompt, y_s.reshape(n_s, 1, D_MODEL), v_s.reshape(n_s, 1, D_A),
            st_re.reshape(n_p, N_GROUPS_B, P_STATE), st_im.reshape(n_p, N_GROUPS_B, P_STATE),
            from_rows(hs_re), from_rows(hs_im))
```

```python
import functools

import jax
import jax.numpy as jnp
from jax import lax
from jax.experimental import pallas as pl
from jax.experimental.pallas import tpu as pltpu

F32 = jnp.float32
BF16 = jnp.bfloat16

EPS = 1e-6
D_MODEL = 2048
D_A = 1024
D_B = 1024
D_IN = 3 * D_A + 2 * D_B
CHUNK = 128
HEAD_A = 128
N_HEADS_A = D_A // HEAD_A
GROUP_B = 16
N_GROUPS_B = D_B // GROUP_B
P_STATE = 64

LANES = 128
SUBLANES = 8
SLAB_GROUPS = LANES // GROUP_B
N_SLABS = N_GROUPS_B // SLAB_GROUPS
SLAB_STATES = SLAB_GROUPS * P_STATE
SLAB_TILES = SLAB_STATES // LANES
N_STREAMS = SUBLANES
HALF_SLABS = N_SLABS // 2
STREAM_PITCH = CHUNK + SUBLANES
SCAN_PIECES = 4
PIECE = CHUNK // SCAN_PIECES
V7X_VMEM_BYTES = 64 * 1024 * 1024
VMEM_LIMIT = V7X_VMEM_BYTES - 1024 * 1024
SAMPLE_PROJ_COLS = 1024
W_IN_RING = 6
W_OUT_PARTS = 4


def _silu(x):
    return x * jax.nn.sigmoid(x)


def _rms_scale(x):
    return lax.rsqrt(jnp.mean(x * x, axis=-1, keepdims=True) + EPS)


def _rms(x, g):
    return x * _rms_scale(x) * g


def _layernorm(x, g, b):
    mu = jnp.mean(x, axis=-1, keepdims=True)
    xc = x - mu
    return xc * lax.rsqrt(jnp.mean(xc * xc, axis=-1, keepdims=True) + EPS) * g + b


def _dot(a, b):
    return jnp.dot(a, b, preferred_element_type=F32)


def _discretize(a_re, a_im, log_dt):
    dt = jnp.exp(log_dt)
    mag = jnp.exp(dt * a_re)
    abar_re = mag * jnp.cos(dt * a_im)
    abar_im = mag * jnp.sin(dt * a_im)
    return abar_re, abar_im


def _nt_dot(a, b):
    return lax.dot_general(a, b, (((1,), (1,)), ((), ())), preferred_element_type=F32)


def _bf16_terms(x):
    hi = x.astype(BF16)
    r1 = x - hi.astype(F32)
    mid = r1.astype(BF16)
    lo = (r1 - mid.astype(F32)).astype(BF16)
    return hi, mid, lo


def _prep_kernel(a8_ref, bt_re_ref, bt_im_ref, c_re_ref, c_im_ref, wglu_ref, bglu_t_ref, bs_ref,
                 bmat_ref, cboth_ref, gmat_ref, lam_re_ref, lam_im_ref, bglu_row_ref, bst_ref):
    a_re = a8_ref[0:N_SLABS, :]
    a_im = a8_ref[N_SLABS:2 * N_SLABS, :]
    abar_re, abar_im = _discretize(a_re, a_im, a8_ref[2 * N_SLABS:3 * N_SLABS, :])
    lam_re_ref[...] = abar_re
    lam_im_ref[...] = abar_im
    num_re = abar_re - 1.0
    num_im = abar_im
    den = a_re * a_re + a_im * a_im
    coef_re = (num_re * a_re + num_im * a_im) / den
    coef_im = (num_im * a_re - num_re * a_im) / den

    def rep_matrix(k, n, period, offset=0):
        row = lax.broadcasted_iota(jnp.int32, (k, n), 0)
        col = lax.broadcasted_iota(jnp.int32, (k, n), 1)
        return jnp.where((col & (period - 1)) + offset == row, 1.0, 0.0).astype(BF16)

    def rep_rows(m, k, period):
        row = lax.broadcasted_iota(jnp.int32, (m, k), 0)
        col = lax.broadcasted_iota(jnp.int32, (m, k), 1)
        return jnp.where((row & (period - 1)) == col, 1.0, 0.0).astype(BF16)

    def block_mask(m, n, row_shift, col_shift):
        row = lax.broadcasted_iota(jnp.int32, (m, n), 0)
        col = lax.broadcasted_iota(jnp.int32, (m, n), 1)
        return (row >> row_shift) == (col >> col_shift)

    rep_state = rep_matrix(P_STATE, SLAB_STATES, P_STATE)
    rows_state = rep_rows(SLAB_STATES, P_STATE, P_STATE)
    rep_val = rep_matrix(2 * GROUP_B, LANES, GROUP_B)
    rep_gate = rep_matrix(2 * GROUP_B, LANES, GROUP_B, GROUP_B)
    mask_b = block_mask(LANES, SLAB_STATES, 4, 6)
    mask_c = block_mask(SLAB_STATES, LANES, 6, 4)
    mask_g = block_mask(LANES, LANES, 4, 4)

    def spread(x, rep):
        return sum(_dot(term, rep) for term in _bf16_terms(x))

    for j in range(N_SLABS):
        crows = slice(j * LANES, (j + 1) * LANES)
        bt_re = spread(bt_re_ref[crows, :], rep_state)
        bt_im = spread(bt_im_ref[crows, :], rep_state)
        cr = coef_re[j:j + 1, :]
        ci = coef_im[j:j + 1, :]
        bmat_ref[j, :, 0:SLAB_STATES] = jnp.where(mask_b, cr * bt_re - ci * bt_im, 0.0).astype(BF16)
        bmat_ref[j, :, SLAB_STATES:2 * SLAB_STATES] = jnp.where(
            mask_b, cr * bt_im + ci * bt_re, 0.0).astype(BF16)

        k, half = j % HALF_SLABS, j // HALF_SLABS
        ccols = slice(half * LANES, (half + 1) * LANES)
        ct_re = _nt_dot(rows_state, c_re_ref[crows, :].astype(BF16))
        ct_im = _nt_dot(rows_state, c_im_ref[crows, :].astype(BF16))
        cboth_ref[k, 0:SLAB_STATES, ccols] = jnp.where(mask_c, ct_re, 0.0).astype(BF16)
        cboth_ref[k, SLAB_STATES:2 * SLAB_STATES, ccols] = jnp.where(mask_c, -ct_im, 0.0).astype(BF16)

        w = wglu_ref[crows, :].astype(BF16)
        gmat_ref[j, :, 0:LANES] = jnp.where(mask_g, _dot(w, rep_val), 0.0).astype(BF16)
        gmat_ref[j, :, LANES:2 * LANES] = jnp.where(mask_g, _dot(w, rep_gate), 0.0).astype(BF16)

    row = lax.broadcasted_iota(jnp.int32, (2 * GROUP_B, D_B), 0)
    col = lax.broadcasted_iota(jnp.int32, (2 * GROUP_B, D_B), 1)
    by_group = spread(bglu_t_ref[...], jnp.where(block_mask(N_GROUPS_B, D_B, 0, 4), 1.0, 0.0).astype(BF16))
    for i in range(2):
        mine = row == (col & (GROUP_B - 1)) + i * GROUP_B
        bglu_row_ref[i:i + 1, :] = jnp.sum(jnp.where(mine, by_group, 0.0), axis=0, keepdims=True)

    eye = rep_rows(CHUNK, CHUNK, CHUNK)
    bst_ref[...] = sum(_nt_dot(eye, term) for term in _bf16_terms(bs_ref[...]))


def _s5_prep_operands(a_re, a_im, log_dt, b_re, b_im, c_re, c_im, w_glu, b_glu, b_s):
    g, p, c = N_GROUPS_B, P_STATE, GROUP_B
    a8 = jnp.stack([a_re, a_im, jnp.broadcast_to(log_dt[:, None], (g, p))]).reshape(3 * N_SLABS, SLAB_STATES)
    args = (a8, b_re.transpose(0, 2, 1).reshape(g * c, p), b_im.transpose(0, 2, 1).reshape(g * c, p),
            c_re.reshape(g * c, p), c_im.reshape(g * c, p), w_glu.reshape(g * c, 2 * c), b_glu.T, b_s)
    out_shape = (
        jax.ShapeDtypeStruct((N_SLABS, LANES, 2 * SLAB_STATES), BF16),
        jax.ShapeDtypeStruct((HALF_SLABS, 2 * SLAB_STATES, 2 * LANES), BF16),
        jax.ShapeDtypeStruct((N_SLABS, LANES, 2 * LANES), BF16),
        jax.ShapeDtypeStruct((N_SLABS, SLAB_STATES), F32),
        jax.ShapeDtypeStruct((N_SLABS, SLAB_STATES), F32),
        jax.ShapeDtypeStruct((2, D_B), F32),
        jax.ShapeDtypeStruct((CHUNK, N_HEADS_A), F32),
    )
    return args, out_shape


ADALN_ROWS = 512
ADALN_PROMPT_ROWS = 2 * SUBLANES


N_PREP_IN, N_PREP_OUT = 8, 7


def _adaln_kernel(cs_ref, cp_ref, w_ref, b_ref, x2_ref, gpre_ref, *refs):
    prep_in = refs[:N_PREP_IN]
    os_ref, op_ref, hs_ref = refs[N_PREP_IN:N_PREP_IN + 3]
    prep_out = refs[N_PREP_IN + 3:]
    assert len(prep_out) == N_PREP_OUT
    k = pl.program_id(0)

    @pl.when(k == 0)
    def _():
        os_ref[...] = jnp.broadcast_to(b_ref[...], os_ref.shape)
        op_ref[...] = jnp.broadcast_to(b_ref[...], op_ref.shape)
        _prep_kernel(*prep_in, *prep_out)

    cols = pl.ds(pl.multiple_of(k * ADALN_ROWS, ADALN_ROWS), ADALN_ROWS)
    w = w_ref[...].astype(BF16)
    os_ref[...] += _dot(_silu(cs_ref[:, cols]).astype(BF16), w)
    n_p = cp_ref.shape[0]
    cp = jnp.concatenate([cp_ref[:, cols], jnp.zeros((ADALN_PROMPT_ROWS - n_p, ADALN_ROWS), F32)], axis=0)
    op_ref[...] += _dot(_silu(cp).astype(BF16), w)

    @pl.when(k == pl.num_programs(0) - 1)
    def _():
        x_tiles = D_MODEL // LANES
        n = hs_ref.shape[0]
        x = jnp.concatenate([x2_ref[pl.ds(q, n, stride=x_tiles), :] for q in range(x_tiles)], axis=1)
        shift = os_ref[:, 0:D_MODEL]
        scale = os_ref[:, D_MODEL:2 * D_MODEL]
        hs_ref[...] = (_rms(x, gpre_ref[...]) * (1.0 + scale) + shift).astype(BF16)


def _adaln(c_sample, c_prompt, w_c, b_c, x2_sample, g_pre, prep_args, prep_out_shape):
    n_s, n_p = c_sample.shape[0], c_prompt.shape[0]
    assert n_p <= ADALN_PROMPT_ROWS and D_MODEL % ADALN_ROWS == 0
    assert len(prep_args) == N_PREP_IN and len(prep_out_shape) == N_PREP_OUT
    whole = lambda a: pl.BlockSpec(a.shape, lambda k, nd=len(a.shape): (0,) * nd)
    return pl.pallas_call(
        _adaln_kernel,
        grid=(D_MODEL // ADALN_ROWS,),
        in_specs=[
            pl.BlockSpec((n_s, D_MODEL), lambda k: (0, 0)),
            pl.BlockSpec((n_p, D_MODEL), lambda k: (0, 0)),
            pl.BlockSpec((ADALN_ROWS, 3 * D_MODEL), lambda k: (k, 0)),
            pl.BlockSpec((1, 3 * D_MODEL), lambda k: (0, 0)),
            pl.BlockSpec(x2_sample.shape, lambda k: (0, 0)),
            pl.BlockSpec(g_pre.shape, lambda k: (0, 0)),
        ] + [whole(a) for a in prep_args],
        out_specs=(pl.BlockSpec((n_s, 3 * D_MODEL), lambda k: (0, 0)),
                   pl.BlockSpec((ADALN_PROMPT_ROWS, 3 * D_MODEL), lambda k: (0, 0)),
                   pl.BlockSpec((n_s, D_MODEL), lambda k: (0, 0))) + tuple(whole(a) for a in prep_out_shape),
        out_shape=(jax.ShapeDtypeStruct((n_s, 3 * D_MODEL), F32),
                   jax.ShapeDtypeStruct((ADALN_PROMPT_ROWS, 3 * D_MODEL), F32),
                   jax.ShapeDtypeStruct((n_s, D_MODEL), BF16)) + tuple(prep_out_shape),
        compiler_params=pltpu.CompilerParams(
            dimension_semantics=("arbitrary",), vmem_limit_bytes=VMEM_LIMIT),
        name="adaln",
    )(c_sample, c_prompt, w_c, b_c.reshape(1, -1), x2_sample, g_pre, *prep_args)


def _mixer_kernel(x_ref, mod_ref, gpre_ref, w_in_hbm, lng_ref, lnb_ref, ws_ref, bst_ref,
                  lam_re_ref, lam_im_ref, bmat_ref, cboth_ref, gmat_ref, dskip_ref, bglu_ref, hs_ref, wout_ref,
                  oin_ref, st_re_ref, st_im_ref, wout_bf_ref, ps_ref,
                  h_sc, a_sc, b_sc, c_sc, v_sc, xb_sc, bu_sc, y2_sc, st_sc, w_in_ref, w_sem, *, n_seq):
    step = pl.program_id(0)

    @pl.when(step == 0)
    def _():
        st_sc[...] = jnp.zeros_like(st_sc)

        t_rows, t_cols = a_sc.shape[0] // 2, a_sc.shape[1]
        stage = [buf.at[pl.ds(i * t_rows, t_rows), :] for buf in (a_sc, b_sc, c_sc) for i in range(2)]
        assert len(stage) == W_IN_RING
        tiles = [(r, c) for c in range(D_IN // t_cols) for r in range(D_MODEL // t_rows)]

        def tile_copy(n):
            r, c = tiles[n]
            src = w_in_hbm.at[pl.ds(r * t_rows, t_rows), pl.ds(c * t_cols, t_cols)]
            return pltpu.make_async_copy(src, stage[n % len(stage)], w_sem.at[n % len(stage)])

        for n in range(len(stage)):
            tile_copy(n).start()
        for n, (r, c) in enumerate(tiles):
            tile_copy(n).wait()
            w_in_ref[r * t_rows:(r + 1) * t_rows, c * t_cols:(c + 1) * t_cols] = (
                stage[n % len(stage)][...].astype(BF16))
            if n + len(stage) < len(tiles):
                tile_copy(n + len(stage)).start()

    wout_bf_ref[...] = wout_ref[...].astype(BF16)

    @pl.when(step < D_IN // SAMPLE_PROJ_COLS)
    def _():
        cols = pl.ds(pl.multiple_of(step * SAMPLE_PROJ_COLS, SAMPLE_PROJ_COLS), SAMPLE_PROJ_COLS)
        ps_ref[...] = _dot(hs_ref[...], w_in_ref[:, cols])

    w_u, w_v, w_z, w_xb, w_zb = 0, D_A, 2 * D_A, 3 * D_A, 3 * D_A + D_B

    for b in range(n_seq):
        x = x_ref[b]
        shift = mod_ref[b:b + 1, 0:D_MODEL]
        gain = gpre_ref[...] * (1.0 + mod_ref[b:b + 1, D_MODEL:2 * D_MODEL])
        rws = slice(b * CHUNK, (b + 1) * CHUNK)
        h_sc[rws, :] = (x * _rms_scale(x) * gain + shift).astype(BF16)
        b_sc[rws, :] = _dot(h_sc[rws, :], w_in_ref[:, w_xb:w_xb + D_B])
        xb_sc[rws, :] = b_sc[rws, :].astype(BF16)

    h = h_sc[...]

    tril = (lax.broadcasted_iota(jnp.int32, (CHUNK, CHUNK), 0)
            >= lax.broadcasted_iota(jnp.int32, (CHUNK, CHUNK), 1))
    n_fill = HALF_SLABS * SCAN_PIECES // 4
    fill_cols = D_A // n_fill

    def fill_u(q):
        cols = slice(q * fill_cols, (q + 1) * fill_cols)
        a_sc[:, cols] = _dot(h, w_in_ref[:, w_u + q * fill_cols:w_u + (q + 1) * fill_cols])

    def fill_z(q):
        cols = slice(q * fill_cols, (q + 1) * fill_cols)
        z = _dot(h, w_in_ref[:, w_z + q * fill_cols:w_z + (q + 1) * fill_cols])
        a_sc[:, cols] = a_sc[:, cols] * _silu(z)

    def fill_mix(q):
        heads_per_fill = N_HEADS_A // n_fill
        for hd in range(q * heads_per_fill, (q + 1) * heads_per_fill):
            cols = slice(hd * HEAD_A, (hd + 1) * HEAD_A)
            w_t = jnp.where(tril, ws_ref[hd], 0.0).astype(BF16)
            v_h = jnp.concatenate([v_sc[b * CHUNK:(b + 1) * CHUNK, cols] for b in range(n_seq)], axis=1)
            mix = _dot(w_t, v_h) + bst_ref[:, hd:hd + 1]
            for b in range(n_seq):
                rws = slice(b * CHUNK, (b + 1) * CHUNK)
                oin_ref[rws, cols] = (a_sc[rws, cols] * mix[:, b * HEAD_A:(b + 1) * HEAD_A]).astype(BF16)

    def fill_zb(q):
        cols = slice(q * fill_cols, (q + 1) * fill_cols)
        a_sc[:, cols] = _silu(_dot(h, w_in_ref[:, w_zb + q * fill_cols:w_zb + (q + 1) * fill_cols]))

    fillers = [functools.partial(f, q) for f, q in (
        (fill_u, 0), (fill_z, 0), (fill_u, 1), (fill_mix, 0), (fill_z, 1), (fill_u, 2), (fill_z, 2), (fill_mix, 1),
        (fill_u, 3), (fill_z, 3), (fill_zb, 0), (fill_mix, 2), (fill_zb, 1), (fill_zb, 2), (fill_mix, 3), (fill_zb, 3))]
    assert len(fillers) == HALF_SLABS * SCAN_PIECES

    first_half = lax.broadcasted_iota(jnp.int32, (N_STREAMS, LANES), 0) < n_seq

    def bu_piece(k, piece):
        t0 = piece * PIECE
        for half in range(2):
            j = half * HALF_SLABS + k
            lhs = jnp.concatenate(
                [xb_sc[b * CHUNK + t0:b * CHUNK + t0 + PIECE, j * LANES:(j + 1) * LANES]
                 for b in range(n_seq)], axis=0)
            bu = _dot(lhs, bmat_ref[j])
            for b in range(n_seq):
                r0 = (half * n_seq + b) * STREAM_PITCH + t0
                for l in range(2 * SLAB_TILES):
                    bu_sc[l, r0:r0 + PIECE, :] = bu[b * PIECE:(b + 1) * PIECE, l * LANES:(l + 1) * LANES]

    def scan_piece(k, piece, hr, hi, lam_r, lam_i):
        rows = []
        for t in range(piece * PIECE, (piece + 1) * PIECE):
            new_r, new_i = [], []
            for l in range(SLAB_TILES):
                br = bu_sc[l, pl.ds(t, N_STREAMS, stride=STREAM_PITCH), :]
                bi = bu_sc[SLAB_TILES + l, pl.ds(t, N_STREAMS, stride=STREAM_PITCH), :]
                new_r.append(lam_r[l] * hr[l] - lam_i[l] * hi[l] + br)
                new_i.append(lam_r[l] * hi[l] + lam_i[l] * hr[l] + bi)
            hr, hi = new_r, new_i
            rows.append(jnp.concatenate(hr + hi, axis=1))
        return hr, hi, jnp.concatenate(rows, axis=0)

    def c_piece(k, piece, hh):
        t0 = piece * PIECE
        rows = slice(t0 * N_STREAMS, (t0 + PIECE) * N_STREAMS)
        y2 = _dot(hh.astype(BF16), cboth_ref[k])
        y2_sc[0, rows, :] = y2[:, 0:LANES]
        y2_sc[1, rows, :] = y2[:, LANES:2 * LANES]
        for half in range(2):
            j = half * HALF_SLABS + k
            for b in range(n_seq):
                s = half * n_seq + b
                c_sc[b * CHUNK + t0:b * CHUNK + t0 + PIECE, j * LANES:(j + 1) * LANES] = (
                    y2_sc[half, pl.ds(t0 * N_STREAMS + s, PIECE, stride=N_STREAMS), :])

    for piece in range(SCAN_PIECES):
        bu_piece(0, piece)
    c_sc[...] = _dot(h, w_in_ref[:, w_v:w_v + D_A])
    v_sc[...] = _layernorm(c_sc[...], lng_ref[...], lnb_ref[...]).astype(BF16)
    for k in range(HALF_SLABS):
        tiles = [slice(l * LANES, (l + 1) * LANES) for l in range(SLAB_TILES)]
        lam_r = [jnp.where(first_half, lam_re_ref[k:k + 1, t], lam_re_ref[HALF_SLABS + k:HALF_SLABS + k + 1, t])
                 for t in tiles]
        lam_i = [jnp.where(first_half, lam_im_ref[k:k + 1, t], lam_im_ref[HALF_SLABS + k:HALF_SLABS + k + 1, t])
                 for t in tiles]
        hr = [st_sc[k, :, l * LANES:(l + 1) * LANES] for l in range(SLAB_TILES)]
        hi = [st_sc[k, :, (SLAB_TILES + l) * LANES:(SLAB_TILES + l + 1) * LANES] for l in range(SLAB_TILES)]
        for piece in range(SCAN_PIECES):
            hr, hi, hh = scan_piece(k, piece, hr, hi, lam_r, lam_i)
            c_piece(k, piece, hh)
            if k + 1 < HALF_SLABS:
                bu_piece(k + 1, piece)
            fillers[k * SCAN_PIECES + piece]()
        st_sc[k] = jnp.concatenate(hr + hi, axis=1)

    for k in range(HALF_SLABS):
        for half in range(2):
            j = half * HALF_SLABS + k
            for b in range(n_seq):
                s = half * n_seq + b
                st_re_ref[b, j:j + 1, :] = st_sc[k, s:s + 1, 0:SLAB_STATES]
                st_im_ref[b, j:j + 1, :] = st_sc[k, s:s + 1, SLAB_STATES:2 * SLAB_STATES]

    for j in range(N_SLABS):
        cols = slice(j * LANES, (j + 1) * LANES)
        y = c_sc[:, cols] + dskip_ref[:, cols] * b_sc[:, cols]
        g = _dot(y.astype(BF16), gmat_ref[j])
        val = g[:, 0:LANES] + bglu_ref[0:1, cols]
        gate = g[:, LANES:2 * LANES] + bglu_ref[1:2, cols]
        oin_ref[:, D_A + j * LANES:D_A + (j + 1) * LANES] = (
            val * jax.nn.sigmoid(gate) * a_sc[:, cols]).astype(BF16)


def _const_spec(shape):
    zeros = (0,) * len(shape)
    return pl.BlockSpec(shape, lambda i: zeros, pipeline_mode=pl.Buffered(1))


def _mixer(x, mod, g_pre, w_in, ln_g, ln_b, w_s, bst, lam_re, lam_im, bmat, cboth, gmat,
           d_skip, b_glu_rows, h_sample, w_out):
    n_seq, seq, _ = x.shape
    assert 2 * n_seq == N_STREAMS and seq % CHUNK == 0
    rows = n_seq * CHUNK
    n_steps = seq // CHUNK
    wout_rows = w_out.shape[0] // n_steps
    assert wout_rows * n_steps == w_out.shape[0] and wout_rows % (2 * SUBLANES) == 0
    state_shape = (n_seq, N_SLABS, SLAB_STATES)
    n_sample = h_sample.shape[0]
    n_proj = D_IN // SAMPLE_PROJ_COLS
    assert n_proj <= n_steps
    consts = (mod, g_pre, w_in, ln_g, ln_b, w_s, bst, lam_re, lam_im, bmat, cboth, gmat,
              d_skip, b_glu_rows, h_sample)
    assert w_in.shape == (D_MODEL, D_IN) and D_MODEL % rows == 0 and D_IN % D_A == 0
    return pl.pallas_call(
        functools.partial(_mixer_kernel, n_seq=n_seq),
        grid=(n_steps,),
        in_specs=[pl.BlockSpec((n_seq, CHUNK, D_MODEL), lambda i: (0, i, 0))]
        + [pl.BlockSpec(memory_space=pl.ANY) if c is w_in else _const_spec(c.shape) for c in consts]
        + [pl.BlockSpec((wout_rows, w_out.shape[1]), lambda i: (i, 0))],
        out_specs=(
            pl.BlockSpec((rows, D_A + D_B), lambda i: (i, 0)),
            pl.BlockSpec(state_shape, lambda i: (0, 0, 0)),
            pl.BlockSpec(state_shape, lambda i: (0, 0, 0)),
            pl.BlockSpec((wout_rows, w_out.shape[1]), lambda i: (i, 0)),
            pl.BlockSpec((n_sample, SAMPLE_PROJ_COLS), lambda i: (0, jnp.minimum(i, n_proj - 1))),
        ),
        out_shape=(
            jax.ShapeDtypeStruct((n_steps * rows, D_A + D_B), BF16),
            jax.ShapeDtypeStruct(state_shape, F32),
            jax.ShapeDtypeStruct(state_shape, F32),
            jax.ShapeDtypeStruct(w_out.shape, BF16),
            jax.ShapeDtypeStruct((n_sample, D_IN), F32),
        ),
        scratch_shapes=[
            pltpu.VMEM((rows, D_MODEL), BF16),
            pltpu.VMEM((rows, D_A), F32),
            pltpu.VMEM((rows, D_A), F32),
            pltpu.VMEM((rows, D_A), F32),
            pltpu.VMEM((rows, D_A), BF16),
            pltpu.VMEM((rows, D_B), BF16),
            pltpu.VMEM((2 * SLAB_TILES, N_STREAMS * STREAM_PITCH, LANES), F32),
            pltpu.VMEM((2, N_STREAMS * CHUNK, LANES), F32),
            pltpu.VMEM((HALF_SLABS, N_STREAMS, 2 * SLAB_STATES), F32),
            pltpu.VMEM((D_MODEL, D_IN), BF16),
            pltpu.SemaphoreType.DMA((W_IN_RING,)),
        ],
        compiler_params=pltpu.CompilerParams(
            dimension_semantics=("arbitrary",), vmem_limit_bytes=VMEM_LIMIT),
        name="mixer",
    )(x, *consts, w_out)


def _outproj_kernel(oin_ref, x_ref, mod_ref, gpost_ref, w_out_hbm, y_ref, w_out_ref, w_sem, *, n_seq):
    i = pl.program_id(0)
    part = (D_A + D_B) // W_OUT_PARTS

    def finish(o):
        r = o * _rms_scale(o)
        for b in range(n_seq):
            gain = gpost_ref[...] * mod_ref[b:b + 1, 2 * D_MODEL:3 * D_MODEL]
            y_ref[b] = x_ref[b] + r[b * CHUNK:(b + 1) * CHUNK, :] * gain

    @pl.when(i == 0)
    def _():
        copies = [pltpu.make_async_copy(w_out_hbm.at[pl.ds(k * part, part)],
                                        w_out_ref.at[pl.ds(k * part, part)], w_sem.at[k])
                  for k in range(W_OUT_PARTS)]
        for copy in copies:
            copy.start()
        o = None
        for k, copy in enumerate(copies):
            copy.wait()
            p = _dot(oin_ref[:, k * part:(k + 1) * part], w_out_ref[k * part:(k + 1) * part, :])
            o = p if o is None else o + p
        finish(o)

    @pl.when(i > 0)
    def _():
        finish(_dot(oin_ref[...], w_out_ref[...]))


def _outproj(oin, x, mod, g_post, w_out_bf):
    n_seq, seq, _ = x.shape
    rows = n_seq * CHUNK
    return pl.pallas_call(
        functools.partial(_outproj_kernel, n_seq=n_seq),
        grid=(seq // CHUNK,),
        in_specs=[
            pl.BlockSpec((rows, D_A + D_B), lambda i: (i, 0)),
            pl.BlockSpec((n_seq, CHUNK, D_MODEL), lambda i: (0, i, 0)),
            _const_spec(mod.shape), _const_spec(g_post.shape), pl.BlockSpec(memory_space=pl.ANY),
        ],
        out_specs=pl.BlockSpec((n_seq, CHUNK, D_MODEL), lambda i: (0, i, 0)),
        out_shape=jax.ShapeDtypeStruct(x.shape, F32),
        scratch_shapes=[pltpu.VMEM(w_out_bf.shape, BF16), pltpu.SemaphoreType.DMA((W_OUT_PARTS,))],
        compiler_params=pltpu.CompilerParams(
            dimension_semantics=("arbitrary",), vmem_limit_bytes=VMEM_LIMIT),
        name="outproj",
    )(oin, x, mod, g_post, w_out_bf)


def _sample_kernel(x2_ref, mod_ref, proj_ref, lng_ref, lnb_ref, ws_ref, bst_ref,
                   lam_re_ref, lam_im_ref, bmat_ref, cboth_ref, gmat_ref, dskip_ref, bglu_ref,
                   h0t_re_ref, h0t_im_ref, gpost_ref, w_out_ref,
                   y2_ref, v2_ref, hst_re_ref, hst_im_ref, oin_sc):
    n = proj_ref.shape[0]
    x_tiles = D_MODEL // LANES
    v_tiles = D_A // LANES
    w_u, w_v, w_z, w_xb, w_zb = 0, D_A, 2 * D_A, 3 * D_A, 3 * D_A + D_B

    v = _layernorm(proj_ref[:, w_v:w_v + D_A], lng_ref[...], lnb_ref[...])
    for q in range(v_tiles):
        v2_ref[pl.ds(q, n, stride=v_tiles), :] = v[:, q * LANES:(q + 1) * LANES]
    for hd in range(N_HEADS_A):
        cols = slice(hd * HEAD_A, (hd + 1) * HEAD_A)
        mix = ws_ref[hd, 0:1, 0:1] * v[:, cols] + bst_ref[0:1, hd:hd + 1]
        u = proj_ref[:, w_u + hd * HEAD_A:w_u + (hd + 1) * HEAD_A]
        z = proj_ref[:, w_z + hd * HEAD_A:w_z + (hd + 1) * HEAD_A]
        oin_sc[:, cols] = (u * mix * _silu(z)).astype(BF16)

    for j in range(N_SLABS):
        cols = slice(j * LANES, (j + 1) * LANES)
        st = slice(j * SLAB_STATES, (j + 1) * SLAB_STATES)
        xb = proj_ref[:, w_xb + j * LANES:w_xb + (j + 1) * LANES]
        zb = proj_ref[:, w_zb + j * LANES:w_zb + (j + 1) * LANES]
        bu = _dot(xb.astype(BF16), bmat_ref[j])
        lr = lam_re_ref[j:j + 1, :]
        li = lam_im_ref[j:j + 1, :]
        h0r = h0t_re_ref[st, :].T
        h0i = h0t_im_ref[st, :].T
        hr = lr * h0r - li * h0i + bu[:, 0:SLAB_STATES]
        hi = lr * h0i + li * h0r + bu[:, SLAB_STATES:2 * SLAB_STATES]
        hst_re_ref[st, :] = hr.T
        hst_im_ref[st, :] = hi.T
        k, half = j % HALF_SLABS, j // HALF_SLABS
        c_j = cboth_ref[k, :, half * LANES:(half + 1) * LANES]
        y = _dot(jnp.concatenate([hr, hi], axis=1).astype(BF16), c_j)
        y = y + dskip_ref[:, cols] * xb
        g = _dot(y.astype(BF16), gmat_ref[j])
        val = g[:, 0:LANES] + bglu_ref[0:1, cols]
        gt = g[:, LANES:2 * LANES] + bglu_ref[1:2, cols]
        oin_sc[:, D_A + j * LANES:D_A + (j + 1) * LANES] = (
            val * jax.nn.sigmoid(gt) * _silu(zb)).astype(BF16)

    o = _dot(oin_sc[...], w_out_ref[...])
    gate = mod_ref[:, 2 * D_MODEL:3 * D_MODEL]
    r = gate * _rms(o, gpost_ref[...])
    for q in range(x_tiles):
        cols = slice(q * LANES, (q + 1) * LANES)
        y2_ref[pl.ds(q, n, stride=x_tiles), :] = x2_ref[pl.ds(q, n, stride=x_tiles), :] + r[:, cols]


def _sample(x2, mod, proj, ln_g, ln_b, w_s, bst, lam_re, lam_im, bmat, cboth, gmat,
            d_skip, b_glu_rows, h0t_re, h0t_im, g_post, w_out_bf):
    n = proj.shape[0]
    n_state = N_GROUPS_B * P_STATE
    out_shapes = ((n * D_MODEL // LANES, LANES), (n * D_A // LANES, LANES), (n_state, n), (n_state, n))
    return pl.pallas_call(
        _sample_kernel,
        out_shape=tuple(jax.ShapeDtypeStruct(shp, F32) for shp in out_shapes),
        scratch_shapes=[pltpu.VMEM((n, D_A + D_B), BF16)],
        compiler_params=pltpu.CompilerParams(vmem_limit_bytes=VMEM_LIMIT),
        name="sample",
    )(x2, mod, proj, ln_g, ln_b, w_s, bst, lam_re, lam_im, bmat, cboth, gmat,
      d_skip, b_glu_rows, h0t_re, h0t_im, g_post, w_out_bf)


def kernel(x_prompt, x_sample, c_prompt, c_sample, state_b_re, state_b_im, w_c, b_c, g_pre, w_in,
           ln_v_g, ln_v_b, w_s, b_s, a_re, a_im, log_dt, b_re, b_im, c_re, c_im, d_skip, w_glu, b_glu,
           w_out, g_post):
    n_p = x_prompt.shape[0]
    n_s = x_sample.shape[0]
    assert x_sample.shape[1] == 1 and n_s % SUBLANES == 0

    row = lambda v: v.reshape(1, -1)
    x2_sample = x_sample.reshape(n_s * D_MODEL // LANES, LANES)
    prep_args, prep_out_shape = _s5_prep_operands(a_re, a_im, log_dt, b_re, b_im, c_re, c_im, w_glu, b_glu, b_s)
    (mod_s, mod_p, h_sample, bmat, cboth, gmat, lam_re, lam_im, b_glu_rows, bst) = _adaln(
        c_sample, c_prompt, w_c, b_c, x2_sample, row(g_pre), prep_args, prep_out_shape)

    s5 = (w_s, bst, lam_re, lam_im, bmat, cboth, gmat, row(d_skip), b_glu_rows)
    oin, st_re, st_im, w_out_bf, proj_s = _mixer(
        x_prompt, mod_p, row(g_pre), w_in, row(ln_v_g), row(ln_v_b), *s5, h_sample, w_out)
    y_prompt = _outproj(oin, x_prompt, mod_p, row(g_post), w_out_bf)

    to_rows = lambda st: st.transpose(1, 2, 0).reshape(N_GROUPS_B * P_STATE, n_s)
    from_rows = lambda st: st.reshape(N_GROUPS_B, P_STATE, n_s).transpose(2, 0, 1)
    y_s, v_s, hs_re, hs_im = _sample(
        x2_sample, mod_s, proj_s, row(ln_v_g), row(ln_v_b), *s5,
        to_rows(state_b_re), to_rows(state_b_im), row(g_post), w_out_bf)

    return (y_prompt, y_s.reshape(n_s, 1, D_MODEL), v_s.reshape(n_s, 1, D_A),
            st_re.reshape(n_p, N_GROUPS_B, P_STATE), st_im.reshape(n_p, N_GROUPS_B, P_STATE),
            from_rows(hs_re), from_rows(hs_im))
```

```python
import functools

import jax
import jax.numpy as jnp
from jax import lax
from jax.experimental import pallas as pl
from jax.experimental.pallas import tpu as pltpu

F32 = jnp.float32
BF16 = jnp.bfloat16

EPS = 1e-6
D_MODEL = 2048
D_A = 1024
D_B = 1024
D_IN = 3 * D_A + 2 * D_B
CHUNK = 128
HEAD_A = 128
N_HEADS_A = D_A // HEAD_A
GROUP_B = 16
N_GROUPS_B = D_B // GROUP_B
P_STATE = 64

LANES = 128
SUBLANES = 8
SLAB_GROUPS = LANES // GROUP_B
N_SLABS = N_GROUPS_B // SLAB_GROUPS
SLAB_STATES = SLAB_GROUPS * P_STATE
SLAB_TILES = SLAB_STATES // LANES
N_STREAMS = SUBLANES
HALF_SLABS = N_SLABS // 2
STREAM_PITCH = CHUNK + SUBLANES
SCAN_PIECES = 4
PIECE = CHUNK // SCAN_PIECES
V7X_VMEM_BYTES = 64 * 1024 * 1024
VMEM_LIMIT = V7X_VMEM_BYTES - 1024 * 1024
SAMPLE_PROJ_COLS = 1024
W_IN_RING = 6


def _silu(x):
    return x * jax.nn.sigmoid(x)


def _rms_scale(x):
    return lax.rsqrt(jnp.mean(x * x, axis=-1, keepdims=True) + EPS)


def _rms(x, g):
    return x * _rms_scale(x) * g


def _layernorm(x, g, b):
    mu = jnp.mean(x, axis=-1, keepdims=True)
    xc = x - mu
    return xc * lax.rsqrt(jnp.mean(xc * xc, axis=-1, keepdims=True) + EPS) * g + b


def _dot(a, b):
    return jnp.dot(a, b, preferred_element_type=F32)


def _discretize(a_re, a_im, log_dt):
    dt = jnp.exp(log_dt)
    mag = jnp.exp(dt * a_re)
    abar_re = mag * jnp.cos(dt * a_im)
    abar_im = mag * jnp.sin(dt * a_im)
    return abar_re, abar_im


def _nt_dot(a, b):
    return lax.dot_general(a, b, (((1,), (1,)), ((), ())), preferred_element_type=F32)


def _bf16_terms(x):
    hi = x.astype(BF16)
    r1 = x - hi.astype(F32)
    mid = r1.astype(BF16)
    lo = (r1 - mid.astype(F32)).astype(BF16)
    return hi, mid, lo


def _prep_kernel(a8_ref, bt_re_ref, bt_im_ref, c_re_ref, c_im_ref, wglu_ref, bglu_t_ref, bs_ref,
                 bmat_ref, cboth_ref, gmat_ref, lam_re_ref, lam_im_ref, bglu_row_ref, bst_ref):
    a_re = a8_ref[0:N_SLABS, :]
    a_im = a8_ref[N_SLABS:2 * N_SLABS, :]
    abar_re, abar_im = _discretize(a_re, a_im, a8_ref[2 * N_SLABS:3 * N_SLABS, :])
    lam_re_ref[...] = abar_re
    lam_im_ref[...] = abar_im
    num_re = abar_re - 1.0
    num_im = abar_im
    den = a_re * a_re + a_im * a_im
    coef_re = (num_re * a_re + num_im * a_im) / den
    coef_im = (num_im * a_re - num_re * a_im) / den

    def rep_matrix(k, n, period, offset=0):
        row = lax.broadcasted_iota(jnp.int32, (k, n), 0)
        col = lax.broadcasted_iota(jnp.int32, (k, n), 1)
        return jnp.where((col & (period - 1)) + offset == row, 1.0, 0.0).astype(BF16)

    def rep_rows(m, k, period):
        row = lax.broadcasted_iota(jnp.int32, (m, k), 0)
        col = lax.broadcasted_iota(jnp.int32, (m, k), 1)
        return jnp.where((row & (period - 1)) == col, 1.0, 0.0).astype(BF16)

    def block_mask(m, n, row_shift, col_shift):
        row = lax.broadcasted_iota(jnp.int32, (m, n), 0)
        col = lax.broadcasted_iota(jnp.int32, (m, n), 1)
        return (row >> row_shift) == (col >> col_shift)

    rep_state = rep_matrix(P_STATE, SLAB_STATES, P_STATE)
    rows_state = rep_rows(SLAB_STATES, P_STATE, P_STATE)
    rep_val = rep_matrix(2 * GROUP_B, LANES, GROUP_B)
    rep_gate = rep_matrix(2 * GROUP_B, LANES, GROUP_B, GROUP_B)
    mask_b = block_mask(LANES, SLAB_STATES, 4, 6)
    mask_c = block_mask(SLAB_STATES, LANES, 6, 4)
    mask_g = block_mask(LANES, LANES, 4, 4)

    def spread(x, rep):
        return sum(_dot(term, rep) for term in _bf16_terms(x))

    for j in range(N_SLABS):
        crows = slice(j * LANES, (j + 1) * LANES)
        bt_re = spread(bt_re_ref[crows, :], rep_state)
        bt_im = spread(bt_im_ref[crows, :], rep_state)
        cr = coef_re[j:j + 1, :]
        ci = coef_im[j:j + 1, :]
        bmat_ref[j, :, 0:SLAB_STATES] = jnp.where(mask_b, cr * bt_re - ci * bt_im, 0.0).astype(BF16)
        bmat_ref[j, :, SLAB_STATES:2 * SLAB_STATES] = jnp.where(
            mask_b, cr * bt_im + ci * bt_re, 0.0).astype(BF16)

        k, half = j % HALF_SLABS, j // HALF_SLABS
        ccols = slice(half * LANES, (half + 1) * LANES)
        ct_re = _nt_dot(rows_state, c_re_ref[crows, :].astype(BF16))
        ct_im = _nt_dot(rows_state, c_im_ref[crows, :].astype(BF16))
        cboth_ref[k, 0:SLAB_STATES, ccols] = jnp.where(mask_c, ct_re, 0.0).astype(BF16)
        cboth_ref[k, SLAB_STATES:2 * SLAB_STATES, ccols] = jnp.where(mask_c, -ct_im, 0.0).astype(BF16)

        w = wglu_ref[crows, :].astype(BF16)
        gmat_ref[j, :, 0:LANES] = jnp.where(mask_g, _dot(w, rep_val), 0.0).astype(BF16)
        gmat_ref[j, :, LANES:2 * LANES] = jnp.where(mask_g, _dot(w, rep_gate), 0.0).astype(BF16)

    row = lax.broadcasted_iota(jnp.int32, (2 * GROUP_B, D_B), 0)
    col = lax.broadcasted_iota(jnp.int32, (2 * GROUP_B, D_B), 1)
    by_group = spread(bglu_t_ref[...], jnp.where(block_mask(N_GROUPS_B, D_B, 0, 4), 1.0, 0.0).astype(BF16))
    for i in range(2):
        mine = row == (col & (GROUP_B - 1)) + i * GROUP_B
        bglu_row_ref[i:i + 1, :] = jnp.sum(jnp.where(mine, by_group, 0.0), axis=0, keepdims=True)

    eye = rep_rows(CHUNK, CHUNK, CHUNK)
    bst_ref[...] = sum(_nt_dot(eye, term) for term in _bf16_terms(bs_ref[...]))


def _s5_prep_operands(a_re, a_im, log_dt, b_re, b_im, c_re, c_im, w_glu, b_glu, b_s):
    g, p, c = N_GROUPS_B, P_STATE, GROUP_B
    a8 = jnp.stack([a_re, a_im, jnp.broadcast_to(log_dt[:, None], (g, p))]).reshape(3 * N_SLABS, SLAB_STATES)
    args = (a8, b_re.transpose(0, 2, 1).reshape(g * c, p), b_im.transpose(0, 2, 1).reshape(g * c, p),
            c_re.reshape(g * c, p), c_im.reshape(g * c, p), w_glu.reshape(g * c, 2 * c), b_glu.T, b_s)
    out_shape = (
        jax.ShapeDtypeStruct((N_SLABS, LANES, 2 * SLAB_STATES), BF16),
        jax.ShapeDtypeStruct((HALF_SLABS, 2 * SLAB_STATES, 2 * LANES), BF16),
        jax.ShapeDtypeStruct((N_SLABS, LANES, 2 * LANES), BF16),
        jax.ShapeDtypeStruct((N_SLABS, SLAB_STATES), F32),
        jax.ShapeDtypeStruct((N_SLABS, SLAB_STATES), F32),
        jax.ShapeDtypeStruct((2, D_B), F32),
        jax.ShapeDtypeStruct((CHUNK, N_HEADS_A), F32),
    )
    return args, out_shape


ADALN_ROWS = 512
ADALN_PROMPT_ROWS = 2 * SUBLANES


N_PREP_IN, N_PREP_OUT = 8, 7


def _adaln_kernel(cs_ref, cp_ref, w_ref, b_ref, x2_ref, gpre_ref, *refs):
    prep_in = refs[:N_PREP_IN]
    os_ref, op_ref, hs_ref = refs[N_PREP_IN:N_PREP_IN + 3]
    prep_out = refs[N_PREP_IN + 3:]
    assert len(prep_out) == N_PREP_OUT
    k = pl.program_id(0)

    @pl.when(k == 0)
    def _():
        os_ref[...] = jnp.broadcast_to(b_ref[...], os_ref.shape)
        op_ref[...] = jnp.broadcast_to(b_ref[...], op_ref.shape)
        _prep_kernel(*prep_in, *prep_out)

    cols = pl.ds(pl.multiple_of(k * ADALN_ROWS, ADALN_ROWS), ADALN_ROWS)
    w = w_ref[...].astype(BF16)
    os_ref[...] += _dot(_silu(cs_ref[:, cols]).astype(BF16), w)
    n_p = cp_ref.shape[0]
    cp = jnp.concatenate([cp_ref[:, cols], jnp.zeros((ADALN_PROMPT_ROWS - n_p, ADALN_ROWS), F32)], axis=0)
    op_ref[...] += _dot(_silu(cp).astype(BF16), w)

    @pl.when(k == pl.num_programs(0) - 1)
    def _():
        x_tiles = D_MODEL // LANES
        n = hs_ref.shape[0]
        x = jnp.concatenate([x2_ref[pl.ds(q, n, stride=x_tiles), :] for q in range(x_tiles)], axis=1)
        shift = os_ref[:, 0:D_MODEL]
        scale = os_ref[:, D_MODEL:2 * D_MODEL]
        hs_ref[...] = (_rms(x, gpre_ref[...]) * (1.0 + scale) + shift).astype(BF16)


def _adaln(c_sample, c_prompt, w_c, b_c, x2_sample, g_pre, prep_args, prep_out_shape):
    n_s, n_p = c_sample.shape[0], c_prompt.shape[0]
    assert n_p <= ADALN_PROMPT_ROWS and D_MODEL % ADALN_ROWS == 0
    assert len(prep_args) == N_PREP_IN and len(prep_out_shape) == N_PREP_OUT
    whole = lambda a: pl.BlockSpec(a.shape, lambda k, nd=len(a.shape): (0,) * nd)
    return pl.pallas_call(
        _adaln_kernel,
        grid=(D_MODEL // ADALN_ROWS,),
        in_specs=[
            pl.BlockSpec((n_s, D_MODEL), lambda k: (0, 0)),
            pl.BlockSpec((n_p, D_MODEL), lambda k: (0, 0)),
            pl.BlockSpec((ADALN_ROWS, 3 * D_MODEL), lambda k: (k, 0)),
            pl.BlockSpec((1, 3 * D_MODEL), lambda k: (0, 0)),
            pl.BlockSpec(x2_sample.shape, lambda k: (0, 0)),
            pl.BlockSpec(g_pre.shape, lambda k: (0, 0)),
        ] + [whole(a) for a in prep_args],
        out_specs=(pl.BlockSpec((n_s, 3 * D_MODEL), lambda k: (0, 0)),
                   pl.BlockSpec((ADALN_PROMPT_ROWS, 3 * D_MODEL), lambda k: (0, 0)),
                   pl.BlockSpec((n_s, D_MODEL), lambda k: (0, 0))) + tuple(whole(a) for a in prep_out_shape),
        out_shape=(jax.ShapeDtypeStruct((n_s, 3 * D_MODEL), F32),
                   jax.ShapeDtypeStruct((ADALN_PROMPT_ROWS, 3 * D_MODEL), F32),
                   jax.ShapeDtypeStruct((n_s, D_MODEL), BF16)) + tuple(prep_out_shape),
        compiler_params=pltpu.CompilerParams(
            dimension_semantics=("arbitrary",), vmem_limit_bytes=VMEM_LIMIT),
        name="adaln",
    )(c_sample, c_prompt, w_c, b_c.reshape(1, -1), x2_sample, g_pre, *prep_args)


def _mixer_kernel(x_ref, mod_ref, gpre_ref, w_in_hbm, lng_ref, lnb_ref, ws_ref, bst_ref,
                  lam_re_ref, lam_im_ref, bmat_ref, cboth_ref, gmat_ref, dskip_ref, bglu_ref, hs_ref, wout_ref,
                  oin_ref, st_re_ref, st_im_ref, wout_bf_ref, ps_ref,
                  h_sc, a_sc, b_sc, c_sc, v_sc, xb_sc, bu_sc, y2_sc, st_sc, w_in_ref, w_sem, *, n_seq):
    step = pl.program_id(0)

    @pl.when(step == 0)
    def _():
        st_sc[...] = jnp.zeros_like(st_sc)

        t_rows, t_cols = a_sc.shape[0] // 2, a_sc.shape[1]
        stage = [buf.at[pl.ds(i * t_rows, t_rows), :] for buf in (a_sc, b_sc, c_sc) for i in range(2)]
        assert len(stage) == W_IN_RING
        tiles = [(r, c) for c in range(D_IN // t_cols) for r in range(D_MODEL // t_rows)]

        def tile_copy(n):
            r, c = tiles[n]
            src = w_in_hbm.at[pl.ds(r * t_rows, t_rows), pl.ds(c * t_cols, t_cols)]
            return pltpu.make_async_copy(src, stage[n % len(stage)], w_sem.at[n % len(stage)])

        for n in range(len(stage)):
            tile_copy(n).start(priority=n % 2)
        for n, (r, c) in enumerate(tiles):
            tile_copy(n).wait()
            w_in_ref[r * t_rows:(r + 1) * t_rows, c * t_cols:(c + 1) * t_cols] = (
                stage[n % len(stage)][...].astype(BF16))
            if n + len(stage) < len(tiles):
                tile_copy(n + len(stage)).start(priority=n % 2)

    wout_bf_ref[...] = wout_ref[...].astype(BF16)

    @pl.when(step < D_IN // SAMPLE_PROJ_COLS)
    def _():
        cols = pl.ds(pl.multiple_of(step * SAMPLE_PROJ_COLS, SAMPLE_PROJ_COLS), SAMPLE_PROJ_COLS)
        ps_ref[...] = _dot(hs_ref[...], w_in_ref[:, cols])

    w_u, w_v, w_z, w_xb, w_zb = 0, D_A, 2 * D_A, 3 * D_A, 3 * D_A + D_B

    for b in range(n_seq):
        x = x_ref[b]
        shift = mod_ref[b:b + 1, 0:D_MODEL]
        gain = gpre_ref[...] * (1.0 + mod_ref[b:b + 1, D_MODEL:2 * D_MODEL])
        rws = slice(b * CHUNK, (b + 1) * CHUNK)
        h_sc[rws, :] = (x * _rms_scale(x) * gain + shift).astype(BF16)
        b_sc[rws, :] = _dot(h_sc[rws, :], w_in_ref[:, w_xb:w_xb + D_B])
        xb_sc[rws, :] = b_sc[rws, :].astype(BF16)

    h = h_sc[...]

    tril = (lax.broadcasted_iota(jnp.int32, (CHUNK, CHUNK), 0)
            >= lax.broadcasted_iota(jnp.int32, (CHUNK, CHUNK), 1))
    n_fill = HALF_SLABS * SCAN_PIECES // 4
    fill_cols = D_A // n_fill

    def fill_u(q):
        cols = slice(q * fill_cols, (q + 1) * fill_cols)
        a_sc[:, cols] = _dot(h, w_in_ref[:, w_u + q * fill_cols:w_u + (q + 1) * fill_cols])

    def fill_z(q):
        cols = slice(q * fill_cols, (q + 1) * fill_cols)
        z = _dot(h, w_in_ref[:, w_z + q * fill_cols:w_z + (q + 1) * fill_cols])
        a_sc[:, cols] = a_sc[:, cols] * _silu(z)

    def fill_mix(q):
        heads_per_fill = N_HEADS_A // n_fill
        for hd in range(q * heads_per_fill, (q + 1) * heads_per_fill):
            cols = slice(hd * HEAD_A, (hd + 1) * HEAD_A)
            w_t = jnp.where(tril, ws_ref[hd], 0.0).astype(BF16)
            v_h = jnp.concatenate([v_sc[b * CHUNK:(b + 1) * CHUNK, cols] for b in range(n_seq)], axis=1)
            mix = _dot(w_t, v_h) + bst_ref[:, hd:hd + 1]
            for b in range(n_seq):
                rws = slice(b * CHUNK, (b + 1) * CHUNK)
                oin_ref[rws, cols] = (a_sc[rws, cols] * mix[:, b * HEAD_A:(b + 1) * HEAD_A]).astype(BF16)

    def fill_zb(q):
        cols = slice(q * fill_cols, (q + 1) * fill_cols)
        a_sc[:, cols] = _silu(_dot(h, w_in_ref[:, w_zb + q * fill_cols:w_zb + (q + 1) * fill_cols]))

    fillers = [functools.partial(f, q) for f, q in (
        (fill_u, 0), (fill_z, 0), (fill_u, 1), (fill_mix, 0), (fill_z, 1), (fill_u, 2), (fill_z, 2), (fill_mix, 1),
        (fill_u, 3), (fill_z, 3), (fill_zb, 0), (fill_mix, 2), (fill_zb, 1), (fill_zb, 2), (fill_mix, 3), (fill_zb, 3))]
    assert len(fillers) == HALF_SLABS * SCAN_PIECES

    first_half = lax.broadcasted_iota(jnp.int32, (N_STREAMS, LANES), 0) < n_seq

    def bu_piece(k, piece):
        t0 = piece * PIECE
        for half in range(2):
            j = half * HALF_SLABS + k
            lhs = jnp.concatenate(
                [xb_sc[b * CHUNK + t0:b * CHUNK + t0 + PIECE, j * LANES:(j + 1) * LANES]
                 for b in range(n_seq)], axis=0)
            bu = _dot(lhs, bmat_ref[j])
            for b in range(n_seq):
                r0 = (half * n_seq + b) * STREAM_PITCH + t0
                for l in range(2 * SLAB_TILES):
                    bu_sc[l, r0:r0 + PIECE, :] = bu[b * PIECE:(b + 1) * PIECE, l * LANES:(l + 1) * LANES]

    def scan_piece(k, piece, hr, hi, lam_r, lam_i):
        rows = []
        for t in range(piece * PIECE, (piece + 1) * PIECE):
            new_r, new_i = [], []
            for l in range(SLAB_TILES):
                br = bu_sc[l, pl.ds(t, N_STREAMS, stride=STREAM_PITCH), :]
                bi = bu_sc[SLAB_TILES + l, pl.ds(t, N_STREAMS, stride=STREAM_PITCH), :]
                new_r.append(lam_r[l] * hr[l] - lam_i[l] * hi[l] + br)
                new_i.append(lam_r[l] * hi[l] + lam_i[l] * hr[l] + bi)
            hr, hi = new_r, new_i
            rows.append(jnp.concatenate(hr + hi, axis=1))
        return hr, hi, jnp.concatenate(rows, axis=0)

    def c_piece(k, piece, hh):
        t0 = piece * PIECE
        rows = slice(t0 * N_STREAMS, (t0 + PIECE) * N_STREAMS)
        y2 = _dot(hh.astype(BF16), cboth_ref[k])
        y2_sc[0, rows, :] = y2[:, 0:LANES]
        y2_sc[1, rows, :] = y2[:, LANES:2 * LANES]
        for half in range(2):
            j = half * HALF_SLABS + k
            for b in range(n_seq):
                s = half * n_seq + b
                c_sc[b * CHUNK + t0:b * CHUNK + t0 + PIECE, j * LANES:(j + 1) * LANES] = (
                    y2_sc[half, pl.ds(t0 * N_STREAMS + s, PIECE, stride=N_STREAMS), :])

    for piece in range(SCAN_PIECES):
        bu_piece(0, piece)
    c_sc[...] = _dot(h, w_in_ref[:, w_v:w_v + D_A])
    v_sc[...] = _layernorm(c_sc[...], lng_ref[...], lnb_ref[...]).astype(BF16)
    for k in range(HALF_SLABS):
        tiles = [slice(l * LANES, (l + 1) * LANES) for l in range(SLAB_TILES)]
        lam_r = [jnp.where(first_half, lam_re_ref[k:k + 1, t], lam_re_ref[HALF_SLABS + k:HALF_SLABS + k + 1, t])
                 for t in tiles]
        lam_i = [jnp.where(first_half, lam_im_ref[k:k + 1, t], lam_im_ref[HALF_SLABS + k:HALF_SLABS + k + 1, t])
                 for t in tiles]
        hr = [st_sc[k, :, l * LANES:(l + 1) * LANES] for l in range(SLAB_TILES)]
        hi = [st_sc[k, :, (SLAB_TILES + l) * LANES:(SLAB_TILES + l + 1) * LANES] for l in range(SLAB_TILES)]
        for piece in range(SCAN_PIECES):
            hr, hi, hh = scan_piece(k, piece, hr, hi, lam_r, lam_i)
            c_piece(k, piece, hh)
            if k + 1 < HALF_SLABS:
                bu_piece(k + 1, piece)
            fillers[k * SCAN_PIECES + piece]()
        st_sc[k] = jnp.concatenate(hr + hi, axis=1)

    for k in range(HALF_SLABS):
        for half in range(2):
            j = half * HALF_SLABS + k
            for b in range(n_seq):
                s = half * n_seq + b
                st_re_ref[b, j:j + 1, :] = st_sc[k, s:s + 1, 0:SLAB_STATES]
                st_im_ref[b, j:j + 1, :] = st_sc[k, s:s + 1, SLAB_STATES:2 * SLAB_STATES]

    for j in range(N_SLABS):
        cols = slice(j * LANES, (j + 1) * LANES)
        y = c_sc[:, cols] + dskip_ref[:, cols] * b_sc[:, cols]
        g = _dot(y.astype(BF16), gmat_ref[j])
        val = g[:, 0:LANES] + bglu_ref[0:1, cols]
        gate = g[:, LANES:2 * LANES] + bglu_ref[1:2, cols]
        oin_ref[:, D_A + j * LANES:D_A + (j + 1) * LANES] = (
            val * jax.nn.sigmoid(gate) * a_sc[:, cols]).astype(BF16)


def _const_spec(shape):
    zeros = (0,) * len(shape)
    return pl.BlockSpec(shape, lambda i: zeros, pipeline_mode=pl.Buffered(1))


def _mixer(x, mod, g_pre, w_in, ln_g, ln_b, w_s, bst, lam_re, lam_im, bmat, cboth, gmat,
           d_skip, b_glu_rows, h_sample, w_out):
    n_seq, seq, _ = x.shape
    assert 2 * n_seq == N_STREAMS and seq % CHUNK == 0
    rows = n_seq * CHUNK
    n_steps = seq // CHUNK
    wout_rows = w_out.shape[0] // n_steps
    assert wout_rows * n_steps == w_out.shape[0] and wout_rows % (2 * SUBLANES) == 0
    state_shape = (n_seq, N_SLABS, SLAB_STATES)
    n_sample = h_sample.shape[0]
    n_proj = D_IN // SAMPLE_PROJ_COLS
    assert n_proj <= n_steps
    consts = (mod, g_pre, w_in, ln_g, ln_b, w_s, bst, lam_re, lam_im, bmat, cboth, gmat,
              d_skip, b_glu_rows, h_sample)
    assert w_in.shape == (D_MODEL, D_IN) and D_MODEL % rows == 0 and D_IN % D_A == 0
    return pl.pallas_call(
        functools.partial(_mixer_kernel, n_seq=n_seq),
        grid=(n_steps,),
        in_specs=[pl.BlockSpec((n_seq, CHUNK, D_MODEL), lambda i: (0, i, 0))]
        + [pl.BlockSpec(memory_space=pl.ANY) if c is w_in else _const_spec(c.shape) for c in consts]
        + [pl.BlockSpec((wout_rows, w_out.shape[1]), lambda i: (i, 0))],
        out_specs=(
            pl.BlockSpec((rows, D_A + D_B), lambda i: (i, 0)),
            pl.BlockSpec(state_shape, lambda i: (0, 0, 0)),
            pl.BlockSpec(state_shape, lambda i: (0, 0, 0)),
            pl.BlockSpec((wout_rows, w_out.shape[1]), lambda i: (i, 0)),
            pl.BlockSpec((n_sample, SAMPLE_PROJ_COLS), lambda i: (0, jnp.minimum(i, n_proj - 1))),
        ),
        out_shape=(
            jax.ShapeDtypeStruct((n_steps * rows, D_A + D_B), BF16),
            jax.ShapeDtypeStruct(state_shape, F32),
            jax.ShapeDtypeStruct(state_shape, F32),
            jax.ShapeDtypeStruct(w_out.shape, BF16),
            jax.ShapeDtypeStruct((n_sample, D_IN), F32),
        ),
        scratch_shapes=[
            pltpu.VMEM((rows, D_MODEL), BF16),
            pltpu.VMEM((rows, D_A), F32),
            pltpu.VMEM((rows, D_A), F32),
            pltpu.VMEM((rows, D_A), F32),
            pltpu.VMEM((rows, D_A), BF16),
            pltpu.VMEM((rows, D_B), BF16),
            pltpu.VMEM((2 * SLAB_TILES, N_STREAMS * STREAM_PITCH, LANES), F32),
            pltpu.VMEM((2, N_STREAMS * CHUNK, LANES), F32),
            pltpu.VMEM((HALF_SLABS, N_STREAMS, 2 * SLAB_STATES), F32),
            pltpu.VMEM((D_MODEL, D_IN), BF16),
            pltpu.SemaphoreType.DMA((W_IN_RING,)),
        ],
        compiler_params=pltpu.CompilerParams(
            dimension_semantics=("arbitrary",), vmem_limit_bytes=VMEM_LIMIT),
        name="mixer",
    )(x, *consts, w_out)


def _outproj_kernel(oin_ref, x_ref, mod_ref, gpost_ref, w_out_ref, y_ref, *, n_seq):
    o = _dot(oin_ref[...], w_out_ref[...])
    r = o * _rms_scale(o)
    for b in range(n_seq):
        gain = gpost_ref[...] * mod_ref[b:b + 1, 2 * D_MODEL:3 * D_MODEL]
        y_ref[b] = x_ref[b] + r[b * CHUNK:(b + 1) * CHUNK, :] * gain


def _outproj(oin, x, mod, g_post, w_out_bf):
    n_seq, seq, _ = x.shape
    rows = n_seq * CHUNK
    return pl.pallas_call(
        functools.partial(_outproj_kernel, n_seq=n_seq),
        grid=(seq // CHUNK,),
        in_specs=[
            pl.BlockSpec((rows, D_A + D_B), lambda i: (i, 0)),
            pl.BlockSpec((n_seq, CHUNK, D_MODEL), lambda i: (0, i, 0)),
            _const_spec(mod.shape), _const_spec(g_post.shape), _const_spec(w_out_bf.shape),
        ],
        out_specs=pl.BlockSpec((n_seq, CHUNK, D_MODEL), lambda i: (0, i, 0)),
        out_shape=jax.ShapeDtypeStruct(x.shape, F32),
        compiler_params=pltpu.CompilerParams(
            dimension_semantics=("arbitrary",), vmem_limit_bytes=VMEM_LIMIT),
        name="outproj",
    )(oin, x, mod, g_post, w_out_bf)


def _sample_kernel(x2_ref, mod_ref, proj_ref, lng_ref, lnb_ref, ws_ref, bst_ref,
                   lam_re_ref, lam_im_ref, bmat_ref, cboth_ref, gmat_ref, dskip_ref, bglu_ref,
                   h0t_re_ref, h0t_im_ref, gpost_ref, w_out_ref,
                   y2_ref, v2_ref, hst_re_ref, hst_im_ref, oin_sc):
    n = proj_ref.shape[0]
    x_tiles = D_MODEL // LANES
    v_tiles = D_A // LANES
    w_u, w_v, w_z, w_xb, w_zb = 0, D_A, 2 * D_A, 3 * D_A, 3 * D_A + D_B

    v = _layernorm(proj_ref[:, w_v:w_v + D_A], lng_ref[...], lnb_ref[...])
    for q in range(v_tiles):
        v2_ref[pl.ds(q, n, stride=v_tiles), :] = v[:, q * LANES:(q + 1) * LANES]
    for hd in range(N_HEADS_A):
        cols = slice(hd * HEAD_A, (hd + 1) * HEAD_A)
        mix = ws_ref[hd, 0:1, 0:1] * v[:, cols] + bst_ref[0:1, hd:hd + 1]
        u = proj_ref[:, w_u + hd * HEAD_A:w_u + (hd + 1) * HEAD_A]
        z = proj_ref[:, w_z + hd * HEAD_A:w_z + (hd + 1) * HEAD_A]
        oin_sc[:, cols] = (u * mix * _silu(z)).astype(BF16)

    for j in range(N_SLABS):
        cols = slice(j * LANES, (j + 1) * LANES)
        st = slice(j * SLAB_STATES, (j + 1) * SLAB_STATES)
        xb = proj_ref[:, w_xb + j * LANES:w_xb + (j + 1) * LANES]
        zb = proj_ref[:, w_zb + j * LANES:w_zb + (j + 1) * LANES]
        bu = _dot(xb.astype(BF16), bmat_ref[j])
        lr = lam_re_ref[j:j + 1, :]
        li = lam_im_ref[j:j + 1, :]
        h0r = h0t_re_ref[st, :].T
        h0i = h0t_im_ref[st, :].T
        hr = lr * h0r - li * h0i + bu[:, 0:SLAB_STATES]
        hi = lr * h0i + li * h0r + bu[:, SLAB_STATES:2 * SLAB_STATES]
        hst_re_ref[st, :] = hr.T
        hst_im_ref[st, :] = hi.T
        k, half = j % HALF_SLABS, j // HALF_SLABS
        c_j = cboth_ref[k, :, half * LANES:(half + 1) * LANES]
        y = _dot(jnp.concatenate([hr, hi], axis=1).astype(BF16), c_j)
        y = y + dskip_ref[:, cols] * xb
        g = _dot(y.astype(BF16), gmat_ref[j])
        val = g[:, 0:LANES] + bglu_ref[0:1, cols]
        gt = g[:, LANES:2 * LANES] + bglu_ref[1:2, cols]
        oin_sc[:, D_A + j * LANES:D_A + (j + 1) * LANES] = (
            val * jax.nn.sigmoid(gt) * _silu(zb)).astype(BF16)

    o = _dot(oin_sc[...], w_out_ref[...])
    gate = mod_ref[:, 2 * D_MODEL:3 * D_MODEL]
    r = gate * _rms(o, gpost_ref[...])
    for q in range(x_tiles):
        cols = slice(q * LANES, (q + 1) * LANES)
        y2_ref[pl.ds(q, n, stride=x_tiles), :] = x2_ref[pl.ds(q, n, stride=x_tiles), :] + r[:, cols]


def _sample(x2, mod, proj, ln_g, ln_b, w_s, bst, lam_re, lam_im, bmat, cboth, gmat,
            d_skip, b_glu_rows, h0t_re, h0t_im, g_post, w_out_bf):
    n = proj.shape[0]
    n_state = N_GROUPS_B * P_STATE
    out_shapes = ((n * D_MODEL // LANES, LANES), (n * D_A // LANES, LANES), (n_state, n), (n_state, n))
    return pl.pallas_call(
        _sample_kernel,
        out_shape=tuple(jax.ShapeDtypeStruct(shp, F32) for shp in out_shapes),
        scratch_shapes=[pltpu.VMEM((n, D_A + D_B), BF16)],
        compiler_params=pltpu.CompilerParams(vmem_limit_bytes=VMEM_LIMIT),
        name="sample",
    )(x2, mod, proj, ln_g, ln_b, w_s, bst, lam_re, lam_im, bmat, cboth, gmat,
      d_skip, b_glu_rows, h0t_re, h0t_im, g_post, w_out_bf)


def kernel(x_prompt, x_sample, c_prompt, c_sample, state_b_re, state_b_im, w_c, b_c, g_pre, w_in,
           ln_v_g, ln_v_b, w_s, b_s, a_re, a_im, log_dt, b_re, b_im, c_re, c_im, d_skip, w_glu, b_glu,
           w_out, g_post):
    n_p = x_prompt.shape[0]
    n_s = x_sample.shape[0]
    assert x_sample.shape[1] == 1 and n_s % SUBLANES == 0

    row = lambda v: v.reshape(1, -1)
    x2_sample = x_sample.reshape(n_s * D_MODEL // LANES, LANES)
    prep_args, prep_out_shape = _s5_prep_operands(a_re, a_im, log_dt, b_re, b_im, c_re, c_im, w_glu, b_glu, b_s)
    (mod_s, mod_p, h_sample, bmat, cboth, gmat, lam_re, lam_im, b_glu_rows, bst) = _adaln(
        c_sample, c_prompt, w_c, b_c, x2_sample, row(g_pre), prep_args, prep_out_shape)

    s5 = (w_s, bst, lam_re, lam_im, bmat, cboth, gmat, row(d_skip), b_glu_rows)
    oin, st_re, st_im, w_out_bf, proj_s = _mixer(
        x_prompt, mod_p, row(g_pre), w_in, row(ln_v_g), row(ln_v_b), *s5, h_sample, w_out)
    y_prompt = _outproj(oin, x_prompt, mod_p, row(g_post), w_out_bf)

    to_rows = lambda st: st.transpose(1, 2, 0).reshape(N_GROUPS_B * P_STATE, n_s)
    from_rows = lambda st: st.reshape(N_GROUPS_B, P_STATE, n_s).transpose(2, 0, 1)
    y_s, v_s, hs_re, hs_im = _sample(
        x2_sample, mod_s, proj_s, row(ln_v_g), row(ln_v_b), *s5,
        to_rows(state_b_re), to_rows(state_b_im), row(g_post), w_out_bf)

    return (y_prompt, y_s.reshape(n_s, 1, D_MODEL), v_s.reshape(n_s, 1, D_A),
            st_re.reshape(n_p, N_GROUPS_B, P_STATE), st_im.reshape(n_p, N_GROUPS_B, P_STATE),
            from_rows(hs_re), from_rows(hs_im))
```
